```python
import jax, jax.numpy as jnp
from jax import lax
import numpy as np

D_MODEL = 2048
BATCH = 1
SEQ = 8192
DEPTH = 1

A_HEAD_DIM = 128
A_WIDTH = D_MODEL // 2
A_HEADS = A_WIDTH // A_HEAD_DIM
A_CHUNK = 64
B_HEAD_DIM = 128
B_WIDTH = D_MODEL // 2
B_HEADS = B_WIDTH // B_HEAD_DIM
Q_BLOCK = 128
D_FF = -(-(8 * D_MODEL) // (3 * 256)) * 256
N_IN = 4 * A_WIDTH + 3 * B_WIDTH + B_HEADS + 2 * D_MODEL
RMS_EPS = 1e-6

kernel_name = "hgrn2_fox_gated_parallel_sandwich_block"


def _in_split_points():
    sizes = [A_WIDTH] * 4 + [B_WIDTH] * 3 + [B_HEADS, D_MODEL, D_MODEL]
    return [int(v) for v in np.cumsum(sizes)[:-1]]


def rms_norm(x, w):
    xf = x.astype(jnp.float32)
    y = xf * lax.rsqrt(jnp.mean(xf * xf, axis=-1, keepdims=True) + RMS_EPS)
    return (y * w.astype(jnp.float32)).astype(x.dtype)


def hgrn2_mixer(q, f_logit, i, g, lb, norm_w):
    B, S, _ = q.shape
    H, D, C = A_HEADS, A_HEAD_DIM, A_CHUNK
    z = f_logit.astype(jnp.float32)
    lb = lb.astype(jnp.float32)
    log_f = jnp.log(lb + (1.0 - lb) * jax.nn.sigmoid(z))
    k = (1.0 - lb) * jax.nn.sigmoid(-z)

    def to_chunks(t):
        return t.astype(jnp.float32).reshape(B, S // C, C, H, D).transpose(1, 0, 3, 2, 4)

    causal = jnp.tril(jnp.ones((C, C), dtype=bool))[None, None, :, :, None]

    def step(state, inp):
        qc, kc, vc, gc = inp
        b = jnp.cumsum(gc, axis=2)
        o_inter = jnp.einsum('bhtk,bhkv->bhtv', qc * jnp.exp(b), state)
        diff = jnp.where(causal, b[:, :, :, None, :] - b[:, :, None, :, :], -jnp.inf)
        scores = jnp.einsum('bhtsk,bhsk->bhts', qc[:, :, :, None, :] * jnp.exp(diff), kc)
        o_intra = jnp.einsum('bhts,bhsv->bhtv', scores, vc)
        b_last = b[:, :, -1:, :]
        k_dec = kc * jnp.exp(b_last - b)
        new_state = jnp.exp(b_last[:, :, 0, :])[..., None] * state + jnp.einsum('bhsk,bhsv->bhkv', k_dec, vc)
        return new_state, o_inter + o_intra

    state0 = jnp.zeros((B, H, D, D), jnp.float32)
    _, o = lax.scan(step, state0, (to_chunks(q), to_chunks(k), to_chunks(i), to_chunks(log_f)))
    o = o.transpose(1, 0, 3, 2, 4).reshape(B, S, H, D)
    o = rms_norm(o, norm_w) * jax.nn.silu(g.astype(jnp.float32).reshape(B, S, H, D))
    return o.reshape(B, S, A_WIDTH).astype(q.dtype)


def fox_mixer(q, k, v, f_logit):
    B, S, _ = q.shape
    H, D, Qb = B_HEADS, B_HEAD_DIM, Q_BLOCK
    n_blk = S // Qb
    qh = q.reshape(B, S, H, D).transpose(0, 2, 1, 3) * (D ** -0.5)
    kh = k.reshape(B, S, H, D).transpose(0, 2, 1, 3)
    vh = v.reshape(B, S, H, D).transpose(0, 2, 1, 3)
    cum = jnp.cumsum(jax.nn.log_sigmoid(f_logit.astype(jnp.float32)), axis=1).transpose(0, 2, 1)
    q_blocks = qh.reshape(B, H, n_blk, Qb, D).transpose(2, 0, 1, 3, 4)
    c_blocks = cum.reshape(B, H, n_blk, Qb).transpose(2, 0, 1, 3)
    pos_k = jnp.arange(S)

    def block(args):
        idx, q_blk, c_blk = args
        pos_q = idx * Qb + jnp.arange(Qb)
        s = jnp.einsum('bhqd,bhkd->bhqk', q_blk, kh).astype(jnp.float32)
        s = s + c_blk[..., None] - cum[:, :, None, :]
        s = jnp.where((pos_q[:, None] >= pos_k[None, :])[None, None], s, -jnp.inf)
        p = jax.nn.softmax(s, axis=-1)
        return jnp.einsum('bhqk,bhkd->bhqd', p.astype(vh.dtype), vh)

    out = lax.map(block, (jnp.arange(n_blk), q_blocks, c_blocks))
    return out.transpose(1, 0, 3, 2, 4).reshape(B, S, B_WIDTH)


def setup_inputs(seed: int = 0) -> dict:
    key = jax.random.key(seed)
    ks = jax.random.split(key, 14)
    f32 = jnp.float32

    def dense(k, fan_in, fan_out):
        return jax.random.normal(k, (DEPTH, fan_in, fan_out), f32) * fan_in ** -0.5

    def gain(k, n):
        return 1.0 + 0.02 * jax.random.normal(k, (DEPTH, n), f32)

    return {
        "x": jax.random.normal(ks[0], (BATCH, SEQ, D_MODEL), f32),
        "w_in": dense(ks[1], D_MODEL, N_IN),
        "b_fox_f": 0.1 * jax.random.normal(ks[2], (DEPTH, B_HEADS), f32),
        "hgrn_lb_logits": jax.random.normal(ks[3], (DEPTH + 1, A_WIDTH), f32),
        "hgrn_norm_w": gain(ks[4], A_HEAD_DIM),
        "w_up_a": dense(ks[5], A_WIDTH, D_MODEL),
        "w_up_b": dense(ks[6], B_WIDTH, D_MODEL),
        "w_o": dense(ks[7], D_MODEL, D_MODEL),
        "norm_mix_pre": gain(ks[8], D_MODEL),
        "norm_mix_post": gain(ks[9], D_MODEL),
        "norm_ffn_pre": gain(ks[10], D_MODEL),
        "norm_ffn_post": gain(ks[11], D_MODEL),
        "w_ffn_in": dense(ks[12], D_MODEL, 2 * D_FF),
        "w_ffn_down": dense(ks[13], D_FF, D_MODEL),
    }


def reference(x, w_in, b_fox_f, hgrn_lb_logits, hgrn_norm_w, w_up_a, w_up_b, w_o,
              norm_mix_pre, norm_mix_post, norm_ffn_pre, norm_ffn_post, w_ffn_in, w_ffn_down):
    split_points = _in_split_points()
    lb_table = jnp.cumsum(jax.nn.softmax(hgrn_lb_logits.astype(jnp.float32), axis=0), axis=0)
    for l in range(DEPTH):
        h = rms_norm(x, norm_mix_pre[l])
        proj = h @ w_in[l]
        a_q, a_f, a_i, a_g, b_q, b_k, b_v, b_f, g_a, g_b = jnp.split(proj, split_points, axis=-1)
        y_a = hgrn2_mixer(a_q, a_f, a_i, a_g, lb_table[l], hgrn_norm_w[l]) @ w_up_a[l]
        y_b = fox_mixer(b_q, b_k, b_v, b_f + b_fox_f[l]) @ w_up_b[l]
        merged = jax.nn.sigmoid(g_a) * y_a + jax.nn.sigmoid(g_b) * y_b
        x = x + rms_norm(merged @ w_o[l], norm_mix_post[l])
        h = rms_norm(x, norm_ffn_pre[l])
        gate, up = jnp.split(h @ w_ffn_in[l], 2, axis=-1)
        x = x + rms_norm((jax.nn.silu(gate) * up) @ w_ffn_down[l], norm_ffn_post[l])
    return x
```

```python
import functools

import numpy as np
import jax
import jax.numpy as jnp
from jax import lax
from jax.experimental import pallas as pl
from jax.experimental.pallas import tpu as pltpu

F32 = jnp.float32
BF16 = jnp.bfloat16

D_MODEL = 2048
SEQ = 8192
HEAD_DIM = 128
N_HEADS = 8
WIDTH = N_HEADS * HEAD_DIM
D_FF = 5632
RMS_EPS = 1e-6
N_MAIN = 4 * WIDTH + 2 * D_MODEL + 3 * WIDTH

VMEM_LIMIT_BYTES = 56 * 1024 * 1024

BLK_AQ, BLK_AF, BLK_AI, BLK_AG, BLK_GA, BLK_GB, BLK_BQ, BLK_BK, BLK_BV = 0, 1, 2, 3, 4, 6, 8, 9, 10
HEADS_PER_BLK = WIDTH // HEAD_DIM

IN_TM, IN_TN = 512, 1024
HG_TS, HG_C = 1024, 64
FOX_T = 512
MERGE_TM = 256
FFN_TM, FFN_TF = 512, 512


def _params(*sem):
    return pltpu.CompilerParams(dimension_semantics=sem, vmem_limit_bytes=VMEM_LIMIT_BYTES)


def _dot_nt(a, b):
    return lax.dot_general(a, b, (((1,), (1,)), ((), ())), preferred_element_type=F32)


def _dot_tn(a, b):
    return lax.dot_general(a, b, (((0,), (0,)), ((), ())), preferred_element_type=F32)


def _log_sigmoid(x):
    return jnp.minimum(x, 0.0) - jnp.log(1.0 + jnp.exp(-jnp.abs(x)))


def _sigmoid(x):
    return 1.0 / (1.0 + jnp.exp(-x))


def _inproj_kernel(x_ref, nw_ref, w_ref, wf_ref, bf_ref, proj_ref, af_ref, cum_ref, h_ref, carry_ref):
    i = pl.program_id(0)
    j = pl.program_id(1)
    tm = x_ref.shape[0]

    @pl.when(j == 0)
    def _():
        x = x_ref[...]
        ms = jnp.mean(x * x, axis=-1, keepdims=True)
        hb = (x * lax.rsqrt(ms + RMS_EPS) * nw_ref[...]).astype(BF16)
        h_ref[...] = hb
        ls = _log_sigmoid(_dot_nt(wf_ref[...], hb) + bf_ref[...])
        lane = lax.broadcasted_iota(jnp.int32, ls.shape, 1)
        d = 1
        while d < tm:
            ls = ls + jnp.where(lane >= d, pltpu.roll(ls, d, 1), 0.0)
            d *= 2

        @pl.when(i == 0)
        def _():
            carry_ref[...] = jnp.zeros_like(carry_ref)

        cum = ls + carry_ref[:, 0:1]
        cum_ref[...] = cum
        carry_ref[...] = jnp.broadcast_to(cum[:, tm - 1:tm], carry_ref.shape)

    acc = jnp.dot(h_ref[...], w_ref[...], preferred_element_type=F32)
    proj_ref[...] = acc.astype(BF16)

    @pl.when(j == BLK_AF)
    def _():
        af_ref[...] = acc


def _inproj(x2, norm_w, w_main, wf_t, b_fox):
    tm, tn = IN_TM, IN_TN
    grid = (SEQ // tm, N_MAIN // tn)
    return pl.pallas_call(
        _inproj_kernel,
        grid=grid,
        in_specs=[
            pl.BlockSpec((tm, D_MODEL), lambda i, j: (i, 0)),
            pl.BlockSpec((1, D_MODEL), lambda i, j: (0, 0)),
            pl.BlockSpec((D_MODEL, tn), lambda i, j: (0, j)),
            pl.BlockSpec((N_HEADS, D_MODEL), lambda i, j: (0, 0)),
            pl.BlockSpec((N_HEADS, 1), lambda i, j: (0, 0)),
        ],
        out_specs=[
            pl.BlockSpec((tm, tn), lambda i, j: (i, j)),
            pl.BlockSpec((tm, WIDTH), lambda i, j: (i, 0)),
            pl.BlockSpec((N_HEADS, tm), lambda i, j: (0, i)),
        ],
        out_shape=[
            jax.ShapeDtypeStruct((SEQ, N_MAIN), BF16),
            jax.ShapeDtypeStruct((SEQ, WIDTH), F32),
            jax.ShapeDtypeStruct((N_HEADS, SEQ), F32),
        ],
        scratch_shapes=[pltpu.VMEM((tm, D_MODEL), BF16), pltpu.VMEM((N_HEADS, 128), F32)],
        compiler_params=_params("arbitrary", "arbitrary"),
        name="inproj",
    )(x2, norm_w, w_main, wf_t, b_fox)


def _hgrn_constants(c):
    n_lvl = int(np.log2(c))
    t = np.arange(c)[:, None]
    j = np.arange(c)[None, :]
    blocks = [(j <= t), (j > t)]
    level = np.full((c, c), -1, np.int32)
    level[np.arange(c), np.arange(c)] = 0
    for l in range(n_lvl):
        b = 2 << l
        mid = (t // b) * b + b // 2 - 1
        second = (t % b) >= b // 2
        m = np.where(second, (j > mid) & (j <= t), (j > t) & (j <= mid))
        blocks.append(m)
        s = np.arange(c)[None, :]
        own = (t // b == s // b) & second & ((s % b) < b // 2)
        level[own] = l + 1
    sums = np.concatenate(blocks, axis=0).astype(np.float32)
    sums2 = np.concatenate([sums, sums], axis=1)
    return jnp.asarray(sums2, BF16), jnp.asarray(level), n_lvl


def _hgrn_kernel(q_ref, z_ref, v_ref, g_ref, lbl_ref, nw_ref, sums_ref, lvl_ref, y_ref, st_ref, *, c, n_lvl):
    @pl.when(pl.program_id(1) == 0)
    def _():
        st_ref[...] = jnp.zeros_like(st_ref)

    logits = lbl_ref[...]
    ex = jnp.exp(logits - jnp.max(logits, axis=0, keepdims=True))
    lb = ex[0:1, :] / jnp.sum(ex, axis=0, keepdims=True)
    one_m_lb = 1.0 - lb
    nw = nw_ref[...]
    sums = sums_ref[...]
    lvl = lvl_ref[...]
    n_chunks = q_ref.shape[0] // c

    def chunk(ci, carry):
        r = pl.ds(pl.multiple_of(ci * c, c), c)
        q = q_ref[r, :].astype(F32)
        z = z_ref[r, :]
        v = v_ref[r, :]
        e = jnp.exp(-jnp.abs(z))
        inv = 1.0 / (1.0 + e)
        pos = z >= 0.0
        sig = jnp.where(pos, inv, e * inv)
        sig_n = jnp.where(pos, e * inv, inv)
        g = jnp.log(lb + one_m_lb * sig)
        k = one_m_lb * sig_n
        g_hi = g.astype(BF16)
        g_lo = (g - g_hi.astype(F32)).astype(BF16)
        expo = jnp.dot(sums, jnp.concatenate([g_hi, g_lo], axis=0), preferred_element_type=F32)
        dec = jnp.exp(expo)
        q_in = (q * dec[0:c]).astype(BF16)
        k_out = (k * dec[c:2 * c]).astype(BF16)
        scores = jnp.where(lvl == 0, _dot_nt(q.astype(BF16), k.astype(BF16)), 0.0)
        for l in range(n_lvl):
            d_l = dec[(2 + l) * c:(3 + l) * c]
            s_l = _dot_nt((q * d_l).astype(BF16), (k * d_l).astype(BF16))
            scores = jnp.where(lvl == l + 1, s_l, scores)
        st = st_ref[...]
        o = _dot_nt(q_in, st.astype(BF16)) + jnp.dot(scores.astype(BF16), v, preferred_element_type=F32)
        st_ref[...] = st * dec[c - 1:c] + _dot_tn(v, k_out)
        o = o * lax.rsqrt(jnp.mean(o * o, axis=-1, keepdims=True) + RMS_EPS) * nw
        gt = g_ref[r, :].astype(F32)
        y_ref[r, :] = (o * gt * _sigmoid(gt)).astype(y_ref.dtype)
        return carry

    lax.fori_loop(0, n_chunks, chunk, 0)


def _hgrn(proj, a_f, lb_logits, norm_w):
    ts, c = HG_TS, HG_C
    sums, lvl, n_lvl = _hgrn_constants(c)
    hb = HEADS_PER_BLK
    grid = (N_HEADS, SEQ // ts)
    return pl.pallas_call(
        functools.partial(_hgrn_kernel, c=c, n_lvl=n_lvl),
        grid=grid,
        in_specs=[
            pl.BlockSpec((ts, HEAD_DIM), lambda h, i: (i, BLK_AQ * hb + h)),
            pl.BlockSpec((ts, HEAD_DIM), lambda h, i: (i, h)),
            pl.BlockSpec((ts, HEAD_DIM), lambda h, i: (i, BLK_AI * hb + h)),
            pl.BlockSpec((ts, HEAD_DIM), lambda h, i: (i, BLK_AG * hb + h)),
            pl.BlockSpec((lb_logits.shape[0], HEAD_DIM), lambda h, i: (0, h)),
            pl.BlockSpec((1, HEAD_DIM), lambda h, i: (0, 0)),
            pl.BlockSpec(sums.shape, lambda h, i: (0, 0)),
            pl.BlockSpec(lvl.shape, lambda h, i: (0, 0)),
        ],
        out_specs=pl.BlockSpec((ts, HEAD_DIM), lambda h, i: (i, h)),
        out_shape=jax.ShapeDtypeStruct((SEQ, WIDTH), BF16),
        scratch_shapes=[pltpu.VMEM((HEAD_DIM, HEAD_DIM), F32)],
        compiler_params=_params("arbitrary", "arbitrary"),
        name="hgrn",
    )(proj, a_f, proj, proj, lb_logits, norm_w, sums, lvl)


def _fox_kernel(q_ref, k_ref, v_ref, cum_ref, o_ref, m_ref, l_ref, acc_ref):
    h = pl.program_id(0)
    i = pl.program_id(1)
    j = pl.program_id(2)
    t = q_ref.shape[0]

    @pl.when(j == 0)
    def _():
        m_ref[...] = jnp.full_like(m_ref, -jnp.inf)
        l_ref[...] = jnp.zeros_like(l_ref)
        acc_ref[...] = jnp.zeros_like(acc_ref)

    def step(masked):
        s = _dot_nt(q_ref[...], k_ref[...]) - cum_ref[pl.ds(h, 1), :]
        if masked:
            row = lax.broadcasted_iota(jnp.int32, s.shape, 0)
            col = lax.broadcasted_iota(jnp.int32, s.shape, 1)
            s = jnp.where(row >= col, s, -jnp.inf)
        m_prev = m_ref[...]
        m_new = jnp.maximum(m_prev, jnp.max(s, axis=-1, keepdims=True))
        p = jnp.exp(s - m_new)
        alpha = jnp.exp(m_prev - m_new)
        l_ref[...] = alpha * l_ref[...] + jnp.sum(p, axis=-1, keepdims=True)
        acc_ref[...] = alpha * acc_ref[...] + jnp.dot(p.astype(BF16), v_ref[...], preferred_element_type=F32)
        m_ref[...] = m_new

    @pl.when(j < i)
    def _():
        step(False)

    @pl.when(j == i)
    def _():
        step(True)
        o_ref[...] = (acc_ref[...] / l_ref[...]).astype(o_ref.dtype)


def _fox(proj, cum):
    t = FOX_T
    n = SEQ // t
    hb = HEADS_PER_BLK
    return pl.pallas_call(
        _fox_kernel,
        grid=(N_HEADS, n, n),
        in_specs=[
            pl.BlockSpec((t, HEAD_DIM), lambda h, i, j: (i, BLK_BQ * hb + h)),
            pl.BlockSpec((t, HEAD_DIM), lambda h, i, j: (jnp.minimum(j, i), BLK_BK * hb + h)),
            pl.BlockSpec((t, HEAD_DIM), lambda h, i, j: (jnp.minimum(j, i), BLK_BV * hb + h)),
            pl.BlockSpec((N_HEADS, t), lambda h, i, j: (0, jnp.minimum(j, i))),
        ],
        out_specs=pl.BlockSpec((t, HEAD_DIM), lambda h, i, j: (i, h)),
        out_shape=jax.ShapeDtypeStruct((SEQ, WIDTH), BF16),
        scratch_shapes=[
            pltpu.VMEM((t, 1), F32),
            pltpu.VMEM((t, 1), F32),
            pltpu.VMEM((t, HEAD_DIM), F32),
        ],
        compiler_params=_params("arbitrary", "arbitrary", "arbitrary"),
        name="fox",
    )(proj, proj, proj, cum)


def _merge_kernel(oa_ref, ob_ref, ga_ref, gb_ref, wa_ref, wb_ref, wo_ref, x_ref, nw_ref, out_ref):
    ya = jnp.dot(oa_ref[...], wa_ref[...], preferred_element_type=F32)
    yb = jnp.dot(ob_ref[...], wb_ref[...], preferred_element_type=F32)
    merged = _sigmoid(ga_ref[...].astype(F32)) * ya + _sigmoid(gb_ref[...].astype(F32)) * yb
    u = jnp.dot(merged.astype(BF16), wo_ref[...], preferred_element_type=F32)
    u = u * lax.rsqrt(jnp.mean(u * u, axis=-1, keepdims=True) + RMS_EPS) * nw_ref[...]
    out_ref[...] = x_ref[...] + u


def _merge(y_a, y_b, proj, w_up_a, w_up_b, w_o, x2, norm_w):
    tm = MERGE_TM
    once = pl.Buffered(1)
    return pl.pallas_call(
        _merge_kernel,
        grid=(SEQ // tm,),
        in_specs=[
            pl.BlockSpec((tm, WIDTH), lambda i: (i, 0)),
            pl.BlockSpec((tm, WIDTH), lambda i: (i, 0)),
            pl.BlockSpec((tm, D_MODEL), lambda i: (i, BLK_GA // 2)),
            pl.BlockSpec((tm, D_MODEL), lambda i: (i, BLK_GB // 2)),
            pl.BlockSpec((WIDTH, D_MODEL), lambda i: (0, 0), pipeline_mode=once),
            pl.BlockSpec((WIDTH, D_MODEL), lambda i: (0, 0), pipeline_mode=once),
            pl.BlockSpec((D_MODEL, D_MODEL), lambda i: (0, 0), pipeline_mode=once),
            pl.BlockSpec((tm, D_MODEL), lambda i: (i, 0)),
            pl.BlockSpec((1, D_MODEL), lambda i: (0, 0)),
        ],
        out_specs=pl.BlockSpec((tm, D_MODEL), lambda i: (i, 0)),
        out_shape=jax.ShapeDtypeStruct((SEQ, D_MODEL), F32),
        compiler_params=_params("arbitrary"),
        name="merge",
    )(y_a, y_b, proj, proj, w_up_a, w_up_b, w_o, x2, norm_w)


def _ffn_kernel(x_ref, npre_ref, npost_ref, wg_ref, wu_ref, wd_ref, out_ref, h_ref):
    f = pl.program_id(1)

    @pl.when(f == 0)
    def _():
        x = x_ref[...]
        ms = jnp.mean(x * x, axis=-1, keepdims=True)
        h_ref[...] = (x * lax.rsqrt(ms + RMS_EPS) * npre_ref[...]).astype(BF16)

    h = h_ref[...]
    gate = jnp.dot(h, wg_ref[...], preferred_element_type=F32)
    up = jnp.dot(h, wu_ref[...], preferred_element_type=F32)
    act = (gate * _sigmoid(gate) * up).astype(BF16)
    part = jnp.dot(act, wd_ref[...], preferred_element_type=F32)

    @pl.when(f == 0)
    def _():
        out_ref[...] = part

    @pl.when(f > 0)
    def _():
        out_ref[...] += part

    @pl.when(f == pl.num_programs(1) - 1)
    def _():
        u = out_ref[...]
        u = u * lax.rsqrt(jnp.mean(u * u, axis=-1, keepdims=True) + RMS_EPS) * npost_ref[...]
        out_ref[...] = x_ref[...] + u


def _ffn(x1, norm_pre, norm_post, w_in, w_down):
    tm, tf = FFN_TM, FFN_TF
    nf = D_FF // tf
    return pl.pallas_call(
        _ffn_kernel,
        grid=(SEQ // tm, nf),
        in_specs=[
            pl.BlockSpec((tm, D_MODEL), lambda i, f: (i, 0)),
            pl.BlockSpec((1, D_MODEL), lambda i, f: (0, 0)),
            pl.BlockSpec((1, D_MODEL), lambda i, f: (0, 0)),
            pl.BlockSpec((D_MODEL, tf), lambda i, f: (0, f)),
            pl.BlockSpec((D_MODEL, tf), lambda i, f: (0, f + nf)),
            pl.BlockSpec((tf, D_MODEL), lambda i, f: (f, 0)),
        ],
        out_specs=pl.BlockSpec((tm, D_MODEL), lambda i, f: (i, 0)),
        out_shape=jax.ShapeDtypeStruct((SEQ, D_MODEL), F32),
        scratch_shapes=[pltpu.VMEM((tm, D_MODEL), BF16)],
        compiler_params=_params("arbitrary", "arbitrary"),
        name="ffn",
    )(x1, norm_pre, norm_post, w_in, w_in, w_down)


def kernel(x, w_in, b_fox_f, hgrn_lb_logits, hgrn_norm_w, w_up_a, w_up_b, w_o, norm_mix_pre,
           norm_mix_post, norm_ffn_pre, norm_ffn_post, w_ffn_in, w_ffn_down):
    assert x.shape == (1, SEQ, D_MODEL) and w_in.shape[0] == 1
    w = w_in[0]
    o_bq, o_bf, o_ga = 4 * WIDTH, 7 * WIDTH, 7 * WIDTH + N_HEADS
    scale = HEAD_DIM ** -0.5
    w_main = jnp.concatenate(
        [w[:, :o_bq], w[:, o_ga:], w[:, o_bq:o_bq + WIDTH] * scale, w[:, o_bq + WIDTH:o_bf]], axis=1
    ).astype(BF16)
    wf_t = w[:, o_bf:o_ga].T.astype(BF16)
    x2 = x[0]

    proj, a_f, cum = _inproj(x2, norm_mix_pre, w_main, wf_t, b_fox_f.reshape(N_HEADS, 1))
    y_a = _hgrn(proj, a_f, hgrn_lb_logits, hgrn_norm_w)
    y_b = _fox(proj, cum)
    x1 = _merge(y_a, y_b, proj, w_up_a[0].astype(BF16), w_up_b[0].astype(BF16), w_o[0].astype(BF16),
                x2, norm_mix_post)
    out = _ffn(x1, norm_ffn_pre, norm_ffn_post, w_ffn_in[0].astype(BF16), w_ffn_down[0].astype(BF16))
    return out[None]
```

```python
import functools

import numpy as np
import jax
import jax.numpy as jnp
from jax import lax
from jax.experimental import pallas as pl
from jax.experimental.pallas import tpu as pltpu

F32 = jnp.float32
BF16 = jnp.bfloat16

D_MODEL = 2048
SEQ = 8192
HEAD_DIM = 128
N_HEADS = 8
WIDTH = N_HEADS * HEAD_DIM
D_FF = 5632
RMS_EPS = 1e-6
N_MAIN = 4 * WIDTH + 2 * D_MODEL + 3 * WIDTH

VMEM_LIMIT_BYTES = 56 * 1024 * 1024

BLK_AQ, BLK_AF, BLK_AI, BLK_AG, BLK_GA, BLK_GB, BLK_BQ, BLK_BK, BLK_BV = 0, 1, 2, 3, 4, 6, 8, 9, 10
HEADS_PER_BLK = WIDTH // HEAD_DIM

IN_TM, IN_TN = 512, 1024
HG_TS, HG_C = 1024, 64
FOX_TQ, FOX_TK = 512, 256
LOG2E = 1.4426950408889634
FOX_SKIP_LOG2 = 160.0
MERGE_TM = 256
FFN_TM, FFN_TF = 512, 512


def _params(*sem):
    return pltpu.CompilerParams(dimension_semantics=sem, vmem_limit_bytes=VMEM_LIMIT_BYTES)


def _dot_nt(a, b):
    return lax.dot_general(a, b, (((1,), (1,)), ((), ())), preferred_element_type=F32)


def _dot_tn(a, b):
    return lax.dot_general(a, b, (((0,), (0,)), ((), ())), preferred_element_type=F32)


def _log_sigmoid(x):
    return jnp.minimum(x, 0.0) - jnp.log(1.0 + jnp.exp(-jnp.abs(x)))


def _sigmoid(x):
    return 1.0 / (1.0 + jnp.exp(-x))


def _split3(x):
    p0 = x.astype(BF16)
    r1 = x - p0.astype(F32)
    p1 = r1.astype(BF16)
    p2 = (r1 - p1.astype(F32)).astype(BF16)
    return p0, p1, p2


def _inproj_kernel(x_ref, nw_ref, w_ref, wf_ref, bf_ref, tri_ref, proj_ref, af_ref, cx_ref, nc_ref,
                   h_ref, carry_ref):
    i = pl.program_id(0)
    j = pl.program_id(1)
    tm = x_ref.shape[0]

    @pl.when(j == 0)
    def _():
        x = x_ref[...]
        ms = jnp.mean(x * x, axis=-1, keepdims=True)
        hb = (x * lax.rsqrt(ms + RMS_EPS) * nw_ref[...]).astype(BF16)
        h_ref[...] = hb
        nls = _log_sigmoid(jnp.dot(hb, wf_ref[...], preferred_element_type=F32) + bf_ref[...]) * (-LOG2E)
        tri = tri_ref[...]
        loc = sum(jnp.dot(tri, p, preferred_element_type=F32) for p in _split3(nls))

        @pl.when(i == 0)
        def _():
            carry_ref[...] = jnp.zeros_like(carry_ref)

        nc = loc + carry_ref[...]
        nc_ref[...] = nc
        carry_ref[...] = nc[tm - 1:tm, :]
        c0, c1, c2 = _split3(nc)
        lane = lax.broadcasted_iota(jnp.int32, nc.shape, 1)
        parts = jnp.where(lane < N_HEADS, c0.astype(F32),
                          jnp.where(lane < 2 * N_HEADS, pltpu.roll(c1.astype(F32), N_HEADS, 1),
                                    jnp.where(lane < 3 * N_HEADS, pltpu.roll(c2.astype(F32), 2 * N_HEADS, 1),
                                              0.0)))
        cx_ref[...] = parts.astype(BF16)

    acc = jnp.dot(h_ref[...], w_ref[...], preferred_element_type=F32)
    proj_ref[...] = acc.astype(BF16)

    @pl.when(j == BLK_AF)
    def _():
        af_ref[...] = acc


def _inproj(x2, norm_w, w_main, wf_pad, bf_row):
    tm, tn = IN_TM, IN_TN
    grid = (SEQ // tm, N_MAIN // tn)
    tri = jnp.asarray(np.tril(np.ones((tm, tm), np.float32)), BF16)
    return pl.pallas_call(
        _inproj_kernel,
        grid=grid,
        in_specs=[
            pl.BlockSpec((tm, D_MODEL), lambda i, j: (i, 0)),
            pl.BlockSpec((1, D_MODEL), lambda i, j: (0, 0)),
            pl.BlockSpec((D_MODEL, tn), lambda i, j: (0, j)),
            pl.BlockSpec((D_MODEL, 128), lambda i, j: (0, 0)),
            pl.BlockSpec((1, 128), lambda i, j: (0, 0)),
            pl.BlockSpec((tm, tm), lambda i, j: (0, 0)),
        ],
        out_specs=[
            pl.BlockSpec((tm, tn), lambda i, j: (i, j)),
            pl.BlockSpec((tm, WIDTH), lambda i, j: (i, 0)),
            pl.BlockSpec((tm, 128), lambda i, j: (i, 0)),
            pl.BlockSpec((tm, 128), lambda i, j: (i, 0)),
        ],
        out_shape=[
            jax.ShapeDtypeStruct((SEQ, N_MAIN), BF16),
            jax.ShapeDtypeStruct((SEQ, WIDTH), F32),
            jax.ShapeDtypeStruct((SEQ, 128), BF16),
            jax.ShapeDtypeStruct((SEQ, 128), F32),
        ],
        scratch_shapes=[pltpu.VMEM((tm, D_MODEL), BF16), pltpu.VMEM((1, 128), F32)],
        compiler_params=_params("arbitrary", "arbitrary"),
        name="inproj",
    )(x2, norm_w, w_main, wf_pad, bf_row, tri)


def _hgrn_constants(c):
    n_lvl = int(np.log2(c))
    t = np.arange(c)[:, None]
    j = np.arange(c)[None, :]
    blocks = [(j <= t), (j > t)]
    level = np.full((c, c), -1, np.int32)
    level[np.arange(c), np.arange(c)] = 0
    for l in range(n_lvl):
        b = 2 << l
        mid = (t // b) * b + b // 2 - 1
        second = (t % b) >= b // 2
        m = np.where(second, (j > mid) & (j <= t), (j > t) & (j <= mid))
        blocks.append(m)
        s = np.arange(c)[None, :]
        own = (t // b == s // b) & second & ((s % b) < b // 2)
        level[own] = l + 1
    sums = np.concatenate(blocks, axis=0).astype(np.float32)
    sums2 = np.concatenate([sums, sums], axis=1)
    return jnp.asarray(sums2, BF16), jnp.asarray(level), n_lvl


def _hgrn_kernel(q_ref, z_ref, v_ref, g_ref, lbl_ref, nw_ref, sums_ref, lvl_ref, y_ref, st_ref, *, c, n_lvl):
    @pl.when(pl.program_id(1) == 0)
    def _():
        st_ref[...] = jnp.zeros_like(st_ref)

    logits = lbl_ref[...]
    ex = jnp.exp(logits - jnp.max(logits, axis=0, keepdims=True))
    lb = ex[0:1, :] / jnp.sum(ex, axis=0, keepdims=True)
    one_m_lb = 1.0 - lb
    nw = nw_ref[...]
    sums = sums_ref[...]
    lvl = lvl_ref[...]
    n_chunks = q_ref.shape[0] // c

    def chunk(ci, carry):
        r = pl.ds(pl.multiple_of(ci * c, c), c)
        q = q_ref[r, :].astype(F32)
        z = z_ref[r, :]
        v = v_ref[r, :]
        e = jnp.exp(-jnp.abs(z))
        inv = 1.0 / (1.0 + e)
        pos = z >= 0.0
        sig = jnp.where(pos, inv, e * inv)
        sig_n = jnp.where(pos, e * inv, inv)
        g = jnp.log(lb + one_m_lb * sig)
        k = one_m_lb * sig_n
        g_hi = g.astype(BF16)
        g_lo = (g - g_hi.astype(F32)).astype(BF16)
        expo = jnp.dot(sums, jnp.concatenate([g_hi, g_lo], axis=0), preferred_element_type=F32)
        dec = jnp.exp(expo)
        q_in = (q * dec[0:c]).astype(BF16)
        k_out = (k * dec[c:2 * c]).astype(BF16)
        scores = jnp.where(lvl == 0, _dot_nt(q.astype(BF16), k.astype(BF16)), 0.0)
        for l in range(n_lvl):
            d_l = dec[(2 + l) * c:(3 + l) * c]
            s_l = _dot_nt((q * d_l).astype(BF16), (k * d_l).astype(BF16))
            scores = jnp.where(lvl == l + 1, s_l, scores)
        st = st_ref[...]
        o = _dot_nt(q_in, st.astype(BF16)) + jnp.dot(scores.astype(BF16), v, preferred_element_type=F32)
        st_ref[...] = st * dec[c - 1:c] + _dot_tn(v, k_out)
        o = o * lax.rsqrt(jnp.mean(o * o, axis=-1, keepdims=True) + RMS_EPS) * nw
        gt = g_ref[r, :].astype(F32)
        y_ref[r, :] = (o * gt * _sigmoid(gt)).astype(y_ref.dtype)
        return carry

    lax.fori_loop(0, n_chunks, chunk, 0)


def _hgrn(proj, a_f, lb_logits, norm_w):
    ts, c = HG_TS, HG_C
    sums, lvl, n_lvl = _hgrn_constants(c)
    hb = HEADS_PER_BLK
    grid = (N_HEADS, SEQ // ts)
    return pl.pallas_call(
        functools.partial(_hgrn_kernel, c=c, n_lvl=n_lvl),
        grid=grid,
        in_specs=[
            pl.BlockSpec((ts, HEAD_DIM), lambda h, i: (i, BLK_AQ * hb + h)),
            pl.BlockSpec((ts, HEAD_DIM), lambda h, i: (i, h)),
            pl.BlockSpec((ts, HEAD_DIM), lambda h, i: (i, BLK_AI * hb + h)),
            pl.BlockSpec((ts, HEAD_DIM), lambda h, i: (i, BLK_AG * hb + h)),
            pl.BlockSpec((lb_logits.shape[0], HEAD_DIM), lambda h, i: (0, h)),
            pl.BlockSpec((1, HEAD_DIM), lambda h, i: (0, 0)),
            pl.BlockSpec(sums.shape, lambda h, i: (0, 0)),
            pl.BlockSpec(lvl.shape, lambda h, i: (0, 0)),
        ],
        out_specs=pl.BlockSpec((ts, HEAD_DIM), lambda h, i: (i, h)),
        out_shape=jax.ShapeDtypeStruct((SEQ, WIDTH), BF16),
        scratch_shapes=[pltpu.VMEM((HEAD_DIM, HEAD_DIM), F32)],
        compiler_params=_params("arbitrary", "arbitrary"),
        name="hgrn",
    )(proj, a_f, proj, proj, lb_logits, norm_w, sums, lvl)


def _fox_kernel(q_ref, k_ref, v_ref, cx_ref, nc_ref, o_ref, vt_ref, kn_ref, m_ref, acc_ref, *, tk):
    h = pl.program_id(0)
    i = pl.program_id(1)
    tq = q_ref.shape[0]
    s_len = k_ref.shape[0]
    n_sub = tq // tk
    n_kt = s_len // tk
    vrows = vt_ref.shape[1]

    @pl.when(i == 0)
    def _():
        ones_row = jnp.where(lax.broadcasted_iota(jnp.int32, (vrows - HEAD_DIM, tk), 0) == 0, 1.0, 0.0)
        kn = jnp.zeros((1, 1), F32)
        for r in range(n_kt):
            rows = slice(r * tk, (r + 1) * tk)
            vt_ref[r, 0:HEAD_DIM, :] = v_ref[rows, :].astype(F32).T.astype(BF16)
            vt_ref[r, HEAD_DIM:vrows, :] = ones_row.astype(BF16)
            kf = k_ref[rows, :].astype(F32)
            kn = jnp.maximum(kn, jnp.max(jnp.sum(kf * kf, axis=1, keepdims=True), axis=0, keepdims=True))
        kn_ref[...] = jnp.broadcast_to(kn, kn_ref.shape)

    qf = q_ref[...].astype(F32)
    qn = jnp.max(jnp.sum(qf * qf, axis=1, keepdims=True), axis=0, keepdims=True)
    rsel = lax.broadcasted_iota(jnp.int32, (HEAD_DIM, tq), 0)
    sel = jnp.where((rsel < 3 * N_HEADS) & ((rsel & (N_HEADS - 1)) == h), 1.0, 0.0).astype(BF16)
    qa = jnp.concatenate([qf.T.astype(BF16), sel], axis=0)

    bound = 2.0 * jnp.sqrt(qn * kn_ref[0:1, 0:1])
    thr = nc_ref[pl.ds(pl.multiple_of(i * tq, tq), 1), :] - bound - FOX_SKIP_LOG2
    ends = nc_ref[pl.ds(tk - 1, n_kt, stride=tk), :]
    lane = lax.broadcasted_iota(jnp.int32, ends.shape, 1)
    r_lo = jnp.sum(jnp.where((ends < thr) & (lane == h), 1, 0))

    m_ref[...] = jnp.full_like(m_ref, -jnp.inf)
    acc_ref[...] = jnp.zeros_like(acc_ref)

    def step(tiles):
        ss = []
        for r, mask_off in tiles:
            ks = pl.ds(pl.multiple_of(r * tk, tk), tk)
            ka = jnp.concatenate([k_ref[ks, :], cx_ref[ks, :]], axis=1)
            s = jnp.dot(ka, qa, preferred_element_type=F32)
            if mask_off is not None:
                kid = lax.broadcasted_iota(jnp.int32, s.shape, 0) + mask_off
                qid = lax.broadcasted_iota(jnp.int32, s.shape, 1)
                s = jnp.where(kid <= qid, s, -jnp.inf)
            ss.append(s)
        m_prev = m_ref[...]
        m_new = m_prev
        for s in ss:
            m_new = jnp.maximum(m_new, jnp.max(s, axis=0, keepdims=True))
        pv = None
        for (r, _), s in zip(tiles, ss):
            d = jnp.dot(vt_ref[r], jnp.exp2(s - m_new).astype(BF16), preferred_element_type=F32)
            pv = d if pv is None else pv + d
        acc_ref[...] = jnp.exp2(m_prev - m_new) * acc_ref[...] + pv
        m_ref[...] = m_new

    def off_diag(r, carry):
        step([(r, None)])
        return carry

    diag = [(i * n_sub + rr, rr * tk) for rr in range(n_sub)]

    @pl.when(i == 0)
    def _():
        step(diag)

    @pl.when(i > 0)
    def _():
        step([(i * n_sub - 1, None)] + diag)
        lax.fori_loop(r_lo, i * n_sub - 1, off_diag, 0)

    acc = acc_ref[...]
    o_ref[...] = (acc[0:HEAD_DIM] / acc[HEAD_DIM:HEAD_DIM + 1]).T.astype(o_ref.dtype)


def _fox(proj, cx, nc):
    tq, tk = FOX_TQ, FOX_TK
    hb = HEADS_PER_BLK
    vrows = HEAD_DIM + 16
    return pl.pallas_call(
        functools.partial(_fox_kernel, tk=tk),
        grid=(N_HEADS, SEQ // tq),
        in_specs=[
            pl.BlockSpec((tq, HEAD_DIM), lambda h, i: (i, BLK_BQ * hb + h)),
            pl.BlockSpec((SEQ, HEAD_DIM), lambda h, i: (0, BLK_BK * hb + h)),
            pl.BlockSpec((SEQ, HEAD_DIM), lambda h, i: (0, BLK_BV * hb + h)),
            pl.BlockSpec((SEQ, 128), lambda h, i: (0, 0)),
            pl.BlockSpec((SEQ, 128), lambda h, i: (0, 0)),
        ],
        out_specs=pl.BlockSpec((tq, HEAD_DIM), lambda h, i: (i, h)),
        out_shape=jax.ShapeDtypeStruct((SEQ, WIDTH), BF16),
        scratch_shapes=[
            pltpu.VMEM((SEQ // tk, vrows, tk), BF16),
            pltpu.VMEM((8, 128), F32),
            pltpu.VMEM((1, tq), F32),
            pltpu.VMEM((vrows, tq), F32),
        ],
        compiler_params=_params("arbitrary", "arbitrary"),
        name="fox",
    )(proj, proj, proj, cx, nc)


def _merge_kernel(oa_ref, ob_ref, ga_ref, gb_ref, wa_ref, wb_ref, wo_ref, x_ref, nw_ref, out_ref):
    ya = jnp.dot(oa_ref[...], wa_ref[...], preferred_element_type=F32)
    yb = jnp.dot(ob_ref[...], wb_ref[...], preferred_element_type=F32)
    merged = _sigmoid(ga_ref[...].astype(F32)) * ya + _sigmoid(gb_ref[...].astype(F32)) * yb
    u = jnp.dot(merged.astype(BF16), wo_ref[...], preferred_element_type=F32)
    u = u * lax.rsqrt(jnp.mean(u * u, axis=-1, keepdims=True) + RMS_EPS) * nw_ref[...]
    out_ref[...] = x_ref[...] + u


def _merge(y_a, y_b, proj, w_up_a, w_up_b, w_o, x2, norm_w):
    tm = MERGE_TM
    once = pl.Buffered(1)
    return pl.pallas_call(
        _merge_kernel,
        grid=(SEQ // tm,),
        in_specs=[
            pl.BlockSpec((tm, WIDTH), lambda i: (i, 0)),
            pl.BlockSpec((tm, WIDTH), lambda i: (i, 0)),
            pl.BlockSpec((tm, D_MODEL), lambda i: (i, BLK_GA // 2)),
            pl.BlockSpec((tm, D_MODEL), lambda i: (i, BLK_GB // 2)),
            pl.BlockSpec((WIDTH, D_MODEL), lambda i: (0, 0), pipeline_mode=once),
            pl.BlockSpec((WIDTH, D_MODEL), lambda i: (0, 0), pipeline_mode=once),
            pl.BlockSpec((D_MODEL, D_MODEL), lambda i: (0, 0), pipeline_mode=once),
            pl.BlockSpec((tm, D_MODEL), lambda i: (i, 0)),
            pl.BlockSpec((1, D_MODEL), lambda i: (0, 0)),
        ],
        out_specs=pl.BlockSpec((tm, D_MODEL), lambda i: (i, 0)),
        out_shape=jax.ShapeDtypeStruct((SEQ, D_MODEL), F32),
        compiler_params=_params("arbitrary"),
        name="merge",
    )(y_a, y_b, proj, proj, w_up_a, w_up_b, w_o, x2, norm_w)


def _ffn_kernel(x_ref, npre_ref, npost_ref, wg_ref, wu_ref, wd_ref, out_ref, h_ref):
    f = pl.program_id(1)

    @pl.when(f == 0)
    def _():
        x = x_ref[...]
        ms = jnp.mean(x * x, axis=-1, keepdims=True)
        h_ref[...] = (x * lax.rsqrt(ms + RMS_EPS) * npre_ref[...]).astype(BF16)

    h = h_ref[...]
    gate = jnp.dot(h, wg_ref[...], preferred_element_type=F32)
    up = jnp.dot(h, wu_ref[...], preferred_element_type=F32)
    act = (gate * _sigmoid(gate) * up).astype(BF16)
    part = jnp.dot(act, wd_ref[...], preferred_element_type=F32)

    @pl.when(f == 0)
    def _():
        out_ref[...] = part

    @pl.when(f > 0)
    def _():
        out_ref[...] += part

    @pl.when(f == pl.num_programs(1) - 1)
    def _():
        u = out_ref[...]
        u = u * lax.rsqrt(jnp.mean(u * u, axis=-1, keepdims=True) + RMS_EPS) * npost_ref[...]
        out_ref[...] = x_ref[...] + u


def _ffn(x1, norm_pre, norm_post, w_in, w_down):
    tm, tf = FFN_TM, FFN_TF
    nf = D_FF // tf
    return pl.pallas_call(
        _ffn_kernel,
        grid=(SEQ // tm, nf),
        in_specs=[
            pl.BlockSpec((tm, D_MODEL), lambda i, f: (i, 0)),
            pl.BlockSpec((1, D_MODEL), lambda i, f: (0, 0)),
            pl.BlockSpec((1, D_MODEL), lambda i, f: (0, 0)),
            pl.BlockSpec((D_MODEL, tf), lambda i, f: (0, f)),
            pl.BlockSpec((D_MODEL, tf), lambda i, f: (0, f + nf)),
            pl.BlockSpec((tf, D_MODEL), lambda i, f: (f, 0)),
        ],
        out_specs=pl.BlockSpec((tm, D_MODEL), lambda i, f: (i, 0)),
        out_shape=jax.ShapeDtypeStruct((SEQ, D_MODEL), F32),
        scratch_shapes=[pltpu.VMEM((tm, D_MODEL), BF16)],
        compiler_params=_params("arbitrary", "arbitrary"),
        name="ffn",
    )(x1, norm_pre, norm_post, w_in, w_in, w_down)


def kernel(x, w_in, b_fox_f, hgrn_lb_logits, hgrn_norm_w, w_up_a, w_up_b, w_o, norm_mix_pre,
           norm_mix_post, norm_ffn_pre, norm_ffn_post, w_ffn_in, w_ffn_down):
    assert x.shape == (1, SEQ, D_MODEL) and w_in.shape[0] == 1
    w = w_in[0]
    o_bq, o_bf, o_ga = 4 * WIDTH, 7 * WIDTH, 7 * WIDTH + N_HEADS
    scale = LOG2E * HEAD_DIM ** -0.5
    w_main = jnp.concatenate(
        [w[:, :o_bq], w[:, o_ga:], w[:, o_bq:o_bq + WIDTH] * scale, w[:, o_bq + WIDTH:o_bf]], axis=1
    ).astype(BF16)
    wf_pad = jnp.pad(w[:, o_bf:o_ga], ((0, 0), (0, 128 - N_HEADS))).astype(BF16)
    bf_row = jnp.pad(b_fox_f.reshape(1, N_HEADS), ((0, 0), (0, 128 - N_HEADS)))
    x2 = x[0]

    proj, a_f, cx, nc = _inproj(x2, norm_mix_pre, w_main, wf_pad, bf_row)
    y_a = _hgrn(proj, a_f, hgrn_lb_logits, hgrn_norm_w)
    y_b = _fox(proj, cx, nc)
    x1 = _merge(y_a, y_b, proj, w_up_a[0].astype(BF16), w_up_b[0].astype(BF16), w_o[0].astype(BF16),
                x2, norm_mix_post)
    out = _ffn(x1, norm_ffn_pre, norm_ffn_post, w_ffn_in[0].astype(BF16), w_ffn_down[0].astype(BF16))
    return out[None]
```

```python
import functools

import numpy as np
import jax
import jax.numpy as jnp
from jax import lax
from jax.experimental import pallas as pl
from jax.experimental.pallas import tpu as pltpu

F32 = jnp.float32
BF16 = jnp.bfloat16

D_MODEL = 2048
SEQ = 8192
HEAD_DIM = 128
N_HEADS = 8
WIDTH = N_HEADS * HEAD_DIM
D_FF = 5632
RMS_EPS = 1e-6
N_MAIN = 4 * WIDTH + 2 * D_MODEL + 3 * WIDTH

VMEM_LIMIT_BYTES = 56 * 1024 * 1024

BLK_AQ, BLK_AF, BLK_AI, BLK_AG, BLK_GA, BLK_GB, BLK_BQ, BLK_BK, BLK_BV = 0, 1, 2, 3, 4, 6, 8, 9, 10
HEADS_PER_BLK = WIDTH // HEAD_DIM

IN_TM, IN_TN = 512, 1024
IN_NA = 7
IN_SRC_BQ = 4
HG_TS, HG_C, HG_HEADS = 1024, 64, 8
FOX_TQ, FOX_TK = 512, 256
LOG2E = 1.4426950408889634
FOX_SKIP_LOG2 = 160.0
MERGE_TM = 256
FFN_TM, FFN_TF = 512, 512


def _params(*sem):
    return pltpu.CompilerParams(dimension_semantics=sem, vmem_limit_bytes=VMEM_LIMIT_BYTES)


def _dot_nt(a, b):
    return lax.dot_general(a, b, (((1,), (1,)), ((), ())), preferred_element_type=F32)


def _dot_tn(a, b):
    return lax.dot_general(a, b, (((0,), (0,)), ((), ())), preferred_element_type=F32)


def _log_sigmoid(x):
    return jnp.minimum(x, 0.0) - jnp.log(1.0 + jnp.exp(-jnp.abs(x)))


def _sigmoid(x):
    return 1.0 / (1.0 + jnp.exp(-x))


def _split3(x):
    p0 = x.astype(BF16)
    r1 = x - p0.astype(F32)
    p1 = r1.astype(BF16)
    p2 = (r1 - p1.astype(F32)).astype(BF16)
    return p0, p1, p2


def _inproj_kernel(x_ref, nw_ref, wa_ref, wg_ref, wf_ref, bf_ref, tri_ref, proj_ref, af_ref, cx_ref, nc_ref,
                   h_ref, carry_ref):
    i = pl.program_id(0)
    j = pl.program_id(1)
    tm = x_ref.shape[0]

    @pl.when(j == 0)
    def _():
        x = x_ref[...]
        ms = jnp.mean(x * x, axis=-1, keepdims=True)
        hb = (x * lax.rsqrt(ms + RMS_EPS) * nw_ref[...]).astype(BF16)
        h_ref[...] = hb
        nls = _log_sigmoid(jnp.dot(hb, wf_ref[...], preferred_element_type=F32) + bf_ref[...]) * (-LOG2E)
        tri = tri_ref[...]
        loc = sum(jnp.dot(tri, p, preferred_element_type=F32) for p in _split3(nls))

        @pl.when(i == 0)
        def _():
            carry_ref[...] = jnp.zeros_like(carry_ref)

        nc = loc + carry_ref[...]
        nc_ref[...] = nc
        carry_ref[...] = nc[tm - 1:tm, :]
        c0, c1, c2 = _split3(nc)
        lane = lax.broadcasted_iota(jnp.int32, nc.shape, 1)
        parts = jnp.where(lane < N_HEADS, c0.astype(F32),
                          jnp.where(lane < 2 * N_HEADS, pltpu.roll(c1.astype(F32), N_HEADS, 1),
                                    jnp.where(lane < 3 * N_HEADS, pltpu.roll(c2.astype(F32), 2 * N_HEADS, 1),
                                              0.0)))
        cx_ref[...] = parts.astype(BF16)

    @pl.when(j < IN_NA)
    def _():
        acc = jnp.dot(h_ref[...], wa_ref[...], preferred_element_type=F32)
        acc = acc * jnp.where(j == IN_SRC_BQ, LOG2E * HEAD_DIM ** -0.5, 1.0)
        proj_ref[...] = acc.astype(BF16)

        @pl.when(j == BLK_AF)
        def _():
            af_ref[...] = acc

    @pl.when(j >= IN_NA)
    def _():
        proj_ref[...] = jnp.dot(h_ref[...], wg_ref[...], preferred_element_type=F32).astype(BF16)


def _inproj_out_block(j):
    return jnp.where(j < IN_SRC_BQ, j, jnp.where(j < IN_NA, j + (BLK_BQ - IN_SRC_BQ), j - IN_NA + BLK_GA))


def _inproj(x2, norm_w, w_a, w_g, wf_pad, bf_row):
    tm, tn = IN_TM, IN_TN
    grid = (SEQ // tm, N_MAIN // tn)
    tri = jnp.asarray(np.tril(np.ones((tm, tm), np.float32)), BF16)
    return pl.pallas_call(
        _inproj_kernel,
        grid=grid,
        in_specs=[
            pl.BlockSpec((tm, D_MODEL), lambda i, j: (i, 0)),
            pl.BlockSpec((1, D_MODEL), lambda i, j: (0, 0)),
            pl.BlockSpec((D_MODEL, tn), lambda i, j: (0, jnp.minimum(j, IN_NA - 1))),
            pl.BlockSpec((D_MODEL, tn), lambda i, j: (0, jnp.maximum(j - IN_NA, 0))),
            pl.BlockSpec((D_MODEL, 128), lambda i, j: (0, 0)),
            pl.BlockSpec((1, 128), lambda i, j: (0, 0)),
            pl.BlockSpec((tm, tm), lambda i, j: (0, 0)),
        ],
        out_specs=[
            pl.BlockSpec((tm, tn), lambda i, j: (i, _inproj_out_block(j))),
            pl.BlockSpec((tm, WIDTH), lambda i, j: (i, 0)),
            pl.BlockSpec((tm, 128), lambda i, j: (i, 0)),
            pl.BlockSpec((tm, 128), lambda i, j: (i, 0)),
        ],
        out_shape=[
            jax.ShapeDtypeStruct((SEQ, N_MAIN), BF16),
            jax.ShapeDtypeStruct((SEQ, WIDTH), F32),
            jax.ShapeDtypeStruct((SEQ, 128), BF16),
            jax.ShapeDtypeStruct((SEQ, 128), F32),
        ],
        scratch_shapes=[pltpu.VMEM((tm, D_MODEL), BF16), pltpu.VMEM((1, 128), F32)],
        compiler_params=_params("arbitrary", "arbitrary"),
        name="inproj",
    )(x2, norm_w, w_a, w_g, wf_pad, bf_row, tri)


def _hgrn_constants(c):
    n_lvl = int(np.log2(c))
    t = np.arange(c)[:, None]
    j = np.arange(c)[None, :]
    blocks = [(j <= t), (j > t)]
    level = np.full((c, c), -1, np.int32)
    level[np.arange(c), np.arange(c)] = 0
    for l in range(n_lvl):
        b = 2 << l
        mid = (t // b) * b + b // 2 - 1
        second = (t % b) >= b // 2
        m = np.where(second, (j > mid) & (j <= t), (j > t) & (j <= mid))
        blocks.append(m)
        s = np.arange(c)[None, :]
        own = (t // b == s // b) & second & ((s % b) < b // 2)
        level[own] = l + 1
    sums = np.concatenate(blocks, axis=0).astype(np.float32)
    sums2 = np.concatenate([sums, sums], axis=1)
    return jnp.asarray(sums2, BF16), jnp.asarray(level), n_lvl


def _hgrn_kernel(q_ref, z_ref, v_ref, g_ref, lbl_ref, nw_ref, sums_ref, lvl_ref, y_ref, st_ref, *, c, n_lvl):
    n_heads = st_ref.shape[0]
    d = HEAD_DIM

    @pl.when(pl.program_id(1) == 0)
    def _():
        st_ref[...] = jnp.zeros_like(st_ref)

    logits = lbl_ref[...]
    ex = jnp.exp(logits - jnp.max(logits, axis=0, keepdims=True))
    lb = ex[0:1, :] / jnp.sum(ex, axis=0, keepdims=True)
    one_m_lb = 1.0 - lb
    nw = nw_ref[...]
    sums = sums_ref[...]
    lvl = lvl_ref[...]
    n_chunks = q_ref.shape[0] // c

    def chunk(ci, carry):
        r = pl.ds(pl.multiple_of(ci * c, c), c)
        z = z_ref[r, :]
        e = jnp.exp(-jnp.abs(z))
        inv = 1.0 / (1.0 + e)
        pos = z >= 0.0
        sig = jnp.where(pos, inv, e * inv)
        sig_n = jnp.where(pos, e * inv, inv)
        g = jnp.log2(lb + one_m_lb * sig)
        k_all = one_m_lb * sig_n
        g_hi = g.astype(BF16)
        g_lo = (g - g_hi.astype(F32)).astype(BF16)
        expo = jnp.dot(sums, jnp.concatenate([g_hi, g_lo], axis=0), preferred_element_type=F32)
        dec_all = jnp.exp2(expo)
        heads = [slice(hh * d, (hh + 1) * d) for hh in range(n_heads)]
        scores = []
        for cols in heads:
            q = q_ref[r, cols].astype(F32)
            k = k_all[:, cols]
            sc = jnp.where(lvl == 0, _dot_nt(q.astype(BF16), k.astype(BF16)), 0.0)
            for l in range(n_lvl):
                d_l = dec_all[(2 + l) * c:(3 + l) * c, cols]
                s_l = _dot_nt((q * d_l).astype(BF16), (k * d_l).astype(BF16))
                sc = jnp.where(lvl == l + 1, s_l, sc)
            scores.append(sc.astype(BF16))
        outs = []
        for hh, cols in enumerate(heads):
            q = q_ref[r, cols].astype(F32)
            v = v_ref[r, cols]
            q_in = (q * dec_all[0:c, cols]).astype(BF16)
            k_out = (k_all[:, cols] * dec_all[c:2 * c, cols]).astype(BF16)
            st = st_ref[hh]
            outs.append(_dot_nt(q_in, st.astype(BF16)) + jnp.dot(scores[hh], v, preferred_element_type=F32))
            st_ref[hh] = st * dec_all[c - 1:c, cols] + _dot_tn(v, k_out)
        for o, cols in zip(outs, heads):
            o = o * lax.rsqrt(jnp.mean(o * o, axis=-1, keepdims=True) + RMS_EPS) * nw
            gt = g_ref[r, cols].astype(F32)
            y_ref[r, cols] = (o * gt * _sigmoid(gt)).astype(y_ref.dtype)
        return carry

    lax.fori_loop(0, n_chunks, chunk, 0)


def _hgrn(proj, a_f, lb_logits, norm_w):
    ts, c, hg = HG_TS, HG_C, HG_HEADS
    sums, lvl, n_lvl = _hgrn_constants(c)
    w = hg * HEAD_DIM
    per_blk = WIDTH // w
    grid = (N_HEADS // hg, SEQ // ts)
    return pl.pallas_call(
        functools.partial(_hgrn_kernel, c=c, n_lvl=n_lvl),
        grid=grid,
        in_specs=[
            pl.BlockSpec((ts, w), lambda h, i: (i, BLK_AQ * per_blk + h)),
            pl.BlockSpec((ts, w), lambda h, i: (i, h)),
            pl.BlockSpec((ts, w), lambda h, i: (i, BLK_AI * per_blk + h)),
            pl.BlockSpec((ts, w), lambda h, i: (i, BLK_AG * per_blk + h)),
            pl.BlockSpec((lb_logits.shape[0], w), lambda h, i: (0, h)),
            pl.BlockSpec((1, HEAD_DIM), lambda h, i: (0, 0)),
            pl.BlockSpec(sums.shape, lambda h, i: (0, 0)),
            pl.BlockSpec(lvl.shape, lambda h, i: (0, 0)),
        ],
        out_specs=pl.BlockSpec((ts, w), lambda h, i: (i, h)),
        out_shape=jax.ShapeDtypeStruct((SEQ, WIDTH), BF16),
        scratch_shapes=[pltpu.VMEM((hg, HEAD_DIM, HEAD_DIM), F32)],
        compiler_params=_params("arbitrary", "arbitrary"),
        name="hgrn",
    )(proj, a_f, proj, proj, lb_logits, norm_w, sums, lvl)


def _fox_kernel(q_ref, k_ref, v_ref, cx_ref, nc_ref, o_ref, vt_ref, kn_ref, m_ref, acc_ref, *, tk):
    h = pl.program_id(0)
    i = pl.program_id(1)
    tq = q_ref.shape[0]
    s_len = k_ref.shape[0]
    n_sub = tq // tk
    n_kt = s_len // tk
    vrows = vt_ref.shape[1]

    @pl.when(i == 0)
    def _():
        ones_row = jnp.where(lax.broadcasted_iota(jnp.int32, (vrows - HEAD_DIM, tk), 0) == 0, 1.0, 0.0)
        kn = jnp.zeros((1, 1), F32)
        for r in range(n_kt):
            rows = slice(r * tk, (r + 1) * tk)
            vt_ref[r, 0:HEAD_DIM, :] = v_ref[rows, :].astype(F32).T.astype(BF16)
            vt_ref[r, HEAD_DIM:vrows, :] = ones_row.astype(BF16)
            kf = k_ref[rows, :].astype(F32)
            kn = jnp.maximum(kn, jnp.max(jnp.sum(kf * kf, axis=1, keepdims=True), axis=0, keepdims=True))
        kn_ref[...] = jnp.broadcast_to(kn, kn_ref.shape)

    qf = q_ref[...].astype(F32)
    qn = jnp.max(jnp.sum(qf * qf, axis=1, keepdims=True), axis=0, keepdims=True)
    rsel = lax.broadcasted_iota(jnp.int32, (HEAD_DIM, tq), 0)
    sel = jnp.where((rsel < 3 * N_HEADS) & ((rsel & (N_HEADS - 1)) == h), 1.0, 0.0).astype(BF16)
    qa = jnp.concatenate([qf.T.astype(BF16), sel], axis=0)

    bound = 2.0 * jnp.sqrt(qn * kn_ref[0:1, 0:1])
    thr = nc_ref[pl.ds(pl.multiple_of(i * tq, tq), 1), :] - bound - FOX_SKIP_LOG2
    ends = nc_ref[pl.ds(tk - 1, n_kt, stride=tk), :]
    lane = lax.broadcasted_iota(jnp.int32, ends.shape, 1)
    r_lo = jnp.sum(jnp.where((ends < thr) & (lane == h), 1, 0))

    m_ref[...] = jnp.full_like(m_ref, -jnp.inf)
    acc_ref[...] = jnp.zeros_like(acc_ref)

    def step(tiles):
        ss = []
        for r, mask_off in tiles:
            ks = pl.ds(pl.multiple_of(r * tk, tk), tk)
            ka = jnp.concatenate([k_ref[ks, :], cx_ref[ks, :]], axis=1)
            s = jnp.dot(ka, qa, preferred_element_type=F32)
            if mask_off is not None:
                kid = lax.broadcasted_iota(jnp.int32, s.shape, 0) + mask_off
                qid = lax.broadcasted_iota(jnp.int32, s.shape, 1)
                s = jnp.where(kid <= qid, s, -jnp.inf)
            ss.append(s)
        m_prev = m_ref[...]
        m_new = m_prev
        for s in ss:
            m_new = jnp.maximum(m_new, jnp.max(s, axis=0, keepdims=True))
        pv = None
        for (r, _), s in zip(tiles, ss):
            d = jnp.dot(vt_ref[r], jnp.exp2(s - m_new).astype(BF16), preferred_element_type=F32)
            pv = d if pv is None else pv + d
        acc_ref[...] = jnp.exp2(m_prev - m_new) * acc_ref[...] + pv
        m_ref[...] = m_new

    def off_diag(r, carry):
        step([(r, None)])
        return carry

    diag = [(i * n_sub + rr, rr * tk) for rr in range(n_sub)]

    @pl.when(i == 0)
    def _():
        step(diag)

    @pl.when(i > 0)
    def _():
        step([(i * n_sub - 1, None)] + diag)
        lax.fori_loop(r_lo, i * n_sub - 1, off_diag, 0)

    acc = acc_ref[...]
    o_ref[...] = (acc[0:HEAD_DIM] / acc[HEAD_DIM:HEAD_DIM + 1]).T.astype(o_ref.dtype)


def _fox(proj, cx, nc):
    tq, tk = FOX_TQ, FOX_TK
    hb = HEADS_PER_BLK
    vrows = HEAD_DIM + 16
    return pl.pallas_call(
        functools.partial(_fox_kernel, tk=tk),
        grid=(N_HEADS, SEQ // tq),
        in_specs=[
            pl.BlockSpec((tq, HEAD_DIM), lambda h, i: (i, BLK_BQ * hb + h)),
            pl.BlockSpec((SEQ, HEAD_DIM), lambda h, i: (0, BLK_BK * hb + h)),
            pl.BlockSpec((SEQ, HEAD_DIM), lambda h, i: (0, BLK_BV * hb + h)),
            pl.BlockSpec((SEQ, 128), lambda h, i: (0, 0)),
            pl.BlockSpec((SEQ, 128), lambda h, i: (0, 0)),
        ],
        out_specs=pl.BlockSpec((tq, HEAD_DIM), lambda h, i: (i, h)),
        out_shape=jax.ShapeDtypeStruct((SEQ, WIDTH), BF16),
        scratch_shapes=[
            pltpu.VMEM((SEQ // tk, vrows, tk), BF16),
            pltpu.VMEM((8, 128), F32),
            pltpu.VMEM((1, tq), F32),
            pltpu.VMEM((vrows, tq), F32),
        ],
        compiler_params=_params("arbitrary", "arbitrary"),
        name="fox",
    )(proj, proj, proj, cx, nc)


def _merge_kernel(oa_ref, ob_ref, ga_ref, gb_ref, wa_ref, wb_ref, wo_ref, x_ref, nw_ref, out_ref):
    ya = jnp.dot(oa_ref[...], wa_ref[...], preferred_element_type=F32)
    yb = jnp.dot(ob_ref[...], wb_ref[...], preferred_element_type=F32)
    merged = _sigmoid(ga_ref[...].astype(F32)) * ya + _sigmoid(gb_ref[...].astype(F32)) * yb
    u = jnp.dot(merged.astype(BF16), wo_ref[...], preferred_element_type=F32)
    u = u * lax.rsqrt(jnp.mean(u * u, axis=-1, keepdims=True) + RMS_EPS) * nw_ref[...]
    out_ref[...] = x_ref[...] + u


def _merge(y_a, y_b, proj, w_up_a, w_up_b, w_o, x2, norm_w):
    tm = MERGE_TM
    once = pl.Buffered(1)
    return pl.pallas_call(
        _merge_kernel,
        grid=(SEQ // tm,),
        in_specs=[
            pl.BlockSpec((tm, WIDTH), lambda i: (i, 0)),
            pl.BlockSpec((tm, WIDTH), lambda i: (i, 0)),
            pl.BlockSpec((tm, D_MODEL), lambda i: (i, BLK_GA // 2)),
            pl.BlockSpec((tm, D_MODEL), lambda i: (i, BLK_GB // 2)),
            pl.BlockSpec((WIDTH, D_MODEL), lambda i: (0, 0), pipeline_mode=once),
            pl.BlockSpec((WIDTH, D_MODEL), lambda i: (0, 0), pipeline_mode=once),
            pl.BlockSpec((D_MODEL, D_MODEL), lambda i: (0, 0), pipeline_mode=once),
            pl.BlockSpec((tm, D_MODEL), lambda i: (i, 0)),
            pl.BlockSpec((1, D_MODEL), lambda i: (0, 0)),
        ],
        out_specs=pl.BlockSpec((tm, D_MODEL), lambda i: (i, 0)),
        out_shape=jax.ShapeDtypeStruct((SEQ, D_MODEL), F32),
        compiler_params=_params("arbitrary"),
        name="merge",
    )(y_a, y_b, proj, proj, w_up_a, w_up_b, w_o, x2, norm_w)


def _ffn_kernel(x_ref, npre_ref, npost_ref, wg_ref, wu_ref, wd_ref, out_ref, h_ref):
    f = pl.program_id(1)

    @pl.when(f == 0)
    def _():
        x = x_ref[...]
        ms = jnp.mean(x * x, axis=-1, keepdims=True)
        h_ref[...] = (x * lax.rsqrt(ms + RMS_EPS) * npre_ref[...]).astype(BF16)

    h = h_ref[...]
    gate = jnp.dot(h, wg_ref[...], preferred_element_type=F32)
    up = jnp.dot(h, wu_ref[...], preferred_element_type=F32)
    act = (gate * _sigmoid(gate) * up).astype(BF16)
    part = jnp.dot(act, wd_ref[...], preferred_element_type=F32)

    @pl.when(f == 0)
    def _():
        out_ref[...] = part

    @pl.when(f > 0)
    def _():
        out_ref[...] += part

    @pl.when(f == pl.num_programs(1) - 1)
    def _():
        u = out_ref[...]
        u = u * lax.rsqrt(jnp.mean(u * u, axis=-1, keepdims=True) + RMS_EPS) * npost_ref[...]
        out_ref[...] = x_ref[...] + u


def _ffn(x1, norm_pre, norm_post, w_in, w_down):
    tm, tf = FFN_TM, FFN_TF
    nf = D_FF // tf
    return pl.pallas_call(
        _ffn_kernel,
        grid=(SEQ // tm, nf),
        in_specs=[
            pl.BlockSpec((tm, D_MODEL), lambda i, f: (i, 0)),
            pl.BlockSpec((1, D_MODEL), lambda i, f: (0, 0)),
            pl.BlockSpec((1, D_MODEL), lambda i, f: (0, 0)),
            pl.BlockSpec((D_MODEL, tf), lambda i, f: (0, f)),
            pl.BlockSpec((D_MODEL, tf), lambda i, f: (0, f + nf)),
            pl.BlockSpec((tf, D_MODEL), lambda i, f: (f, 0)),
        ],
        out_specs=pl.BlockSpec((tm, D_MODEL), lambda i, f: (i, 0)),
        out_shape=jax.ShapeDtypeStruct((SEQ, D_MODEL), F32),
        scratch_shapes=[pltpu.VMEM((tm, D_MODEL), BF16)],
        compiler_params=_params("arbitrary", "arbitrary"),
        name="ffn",
    )(x1, norm_pre, norm_post, w_in, w_in, w_down)


def kernel(x, w_in, b_fox_f, hgrn_lb_logits, hgrn_norm_w, w_up_a, w_up_b, w_o, norm_mix_pre,
           norm_mix_post, norm_ffn_pre, norm_ffn_post, w_ffn_in, w_ffn_down):
    assert x.shape == (1, SEQ, D_MODEL) and w_in.shape[0] == 1
    w = w_in[0]
    o_bf, o_ga = IN_NA * WIDTH, IN_NA * WIDTH + N_HEADS
    w_a = w[:, :o_bf].astype(BF16)
    w_g = w[:, o_ga:].astype(BF16)
    wf_pad = jnp.pad(w[:, o_bf:o_ga], ((0, 0), (0, 128 - N_HEADS))).astype(BF16)
    bf_row = jnp.pad(b_fox_f.reshape(1, N_HEADS), ((0, 0), (0, 128 - N_HEADS)))
    x2 = x[0]

    proj, a_f, cx, nc = _inproj(x2, norm_mix_pre, w_a, w_g, wf_pad, bf_row)
    y_a = _hgrn(proj, a_f, hgrn_lb_logits, hgrn_norm_w)
    y_b = _fox(proj, cx, nc)
    x1 = _merge(y_a, y_b, proj, w_up_a[0].astype(BF16), w_up_b[0].astype(BF16), w_o[0].astype(BF16),
                x2, norm_mix_post)
    out = _ffn(x1, norm_ffn_pre, norm_ffn_post, w_ffn_in[0].astype(BF16), w_ffn_down[0].astype(BF16))
    return out[None]
```

```python
import functools

import numpy as np
import jax
import jax.numpy as jnp
from jax import lax
from jax.experimental import pallas as pl
from jax.experimental.pallas import tpu as pltpu

F32 = jnp.float32
BF16 = jnp.bfloat16

D_MODEL = 2048
SEQ = 8192
HEAD_DIM = 128
N_HEADS = 8
WIDTH = N_HEADS * HEAD_DIM
D_FF = 5632
RMS_EPS = 1e-6
N_MAIN = 4 * WIDTH + 2 * D_MODEL + 3 * WIDTH

VMEM_LIMIT_BYTES = 56 * 1024 * 1024

BLK_AQ, BLK_AF, BLK_AI, BLK_AG, BLK_GA, BLK_GB, BLK_BQ, BLK_BK, BLK_BV = 0, 1, 2, 3, 4, 6, 8, 9, 10
HEADS_PER_BLK = WIDTH // HEAD_DIM

IN_TM, IN_TN = 1024, 512
IN_NA = 7
IN_SRC_BQ = 4
HG_TS, HG_C, HG_HEADS = 1024, 64, 8
FOX_TQ, FOX_TK = 512, 256
LOG2E = 1.4426950408889634
FOX_SKIP_LOG2 = 160.0
MERGE_TM = 256
FFN_TM, FFN_TF = 512, 512


def _params(*sem):
    return pltpu.CompilerParams(dimension_semantics=sem, vmem_limit_bytes=VMEM_LIMIT_BYTES)


def _dot_nt(a, b):
    return lax.dot_general(a, b, (((1,), (1,)), ((), ())), preferred_element_type=F32)


def _dot_tn(a, b):
    return lax.dot_general(a, b, (((0,), (0,)), ((), ())), preferred_element_type=F32)


def _log_sigmoid(x):
    return jnp.minimum(x, 0.0) - jnp.log(1.0 + jnp.exp(-jnp.abs(x)))


def _sigmoid(x):
    return 1.0 / (1.0 + jnp.exp(-x))


def _split3(x):
    p0 = x.astype(BF16)
    r1 = x - p0.astype(F32)
    p1 = r1.astype(BF16)
    p2 = (r1 - p1.astype(F32)).astype(BF16)
    return p0, p1, p2


def _inproj_kernel(x_ref, nw_ref, wa_ref, wg_ref, wf_ref, bf_ref, tri_ref, proj_ref, af_ref, cx_ref, nc_ref,
                   h_ref, carry_ref):
    i = pl.program_id(0)
    j = pl.program_id(1)
    tm = x_ref.shape[0]

    @pl.when(j == 0)
    def _():
        x = x_ref[...]
        ms = jnp.mean(x * x, axis=-1, keepdims=True)
        hb = (x * lax.rsqrt(ms + RMS_EPS) * nw_ref[...]).astype(BF16)
        h_ref[...] = hb
        nls = _log_sigmoid(jnp.dot(hb, wf_ref[...], preferred_element_type=F32) + bf_ref[...]) * (-LOG2E)
        tri = tri_ref[...]
        loc = sum(jnp.dot(tri, p, preferred_element_type=F32) for p in _split3(nls))

        @pl.when(i == 0)
        def _():
            carry_ref[...] = jnp.zeros_like(carry_ref)

        nc = loc + carry_ref[...]
        nc_ref[...] = nc
        carry_ref[...] = nc[tm - 1:tm, :]
        c0, c1, c2 = _split3(nc)
        lane = lax.broadcasted_iota(jnp.int32, nc.shape, 1)
        parts = jnp.where(lane < N_HEADS, c0.astype(F32),
                          jnp.where(lane < 2 * N_HEADS, pltpu.roll(c1.astype(F32), N_HEADS, 1),
                                    jnp.where(lane < 3 * N_HEADS, pltpu.roll(c2.astype(F32), 2 * N_HEADS, 1),
                                              0.0)))
        cx_ref[...] = parts.astype(BF16)

    nb = WIDTH // proj_ref.shape[1]
    grp = j // nb

    @pl.when(grp < IN_NA)
    def _():
        acc = jnp.dot(h_ref[...], wa_ref[...], preferred_element_type=F32)
        acc = acc * jnp.where(grp == IN_SRC_BQ, LOG2E * HEAD_DIM ** -0.5, 1.0)
        proj_ref[...] = acc.astype(BF16)

        @pl.when(grp == BLK_AF)
        def _():
            af_ref[...] = acc

    @pl.when(grp >= IN_NA)
    def _():
        proj_ref[...] = jnp.dot(h_ref[...], wg_ref[...], preferred_element_type=F32).astype(BF16)


def _inproj_out_block(j, nb):
    g = j // nb
    g_out = jnp.where(g < IN_SRC_BQ, g, jnp.where(g < IN_NA, g + (BLK_BQ - IN_SRC_BQ), g - IN_NA + BLK_GA))
    return g_out * nb + j % nb


def _inproj(x2, norm_w, w_a, w_g, wf_pad, bf_row):
    tm, tn = IN_TM, IN_TN
    nb = WIDTH // tn
    n_a = IN_NA * nb
    grid = (SEQ // tm, N_MAIN // tn)
    tri = jnp.asarray(np.tril(np.ones((tm, tm), np.float32)), BF16)
    return pl.pallas_call(
        _inproj_kernel,
        grid=grid,
        in_specs=[
            pl.BlockSpec((tm, D_MODEL), lambda i, j: (i, 0)),
            pl.BlockSpec((1, D_MODEL), lambda i, j: (0, 0)),
            pl.BlockSpec((D_MODEL, tn), lambda i, j: (0, jnp.minimum(j, n_a - 1))),
            pl.BlockSpec((D_MODEL, tn), lambda i, j: (0, jnp.maximum(j - n_a, 0))),
            pl.BlockSpec((D_MODEL, 128), lambda i, j: (0, 0)),
            pl.BlockSpec((1, 128), lambda i, j: (0, 0)),
            pl.BlockSpec((tm, tm), lambda i, j: (0, 0)),
        ],
        out_specs=[
            pl.BlockSpec((tm, tn), lambda i, j: (i, _inproj_out_block(j, nb))),
            pl.BlockSpec((tm, tn), lambda i, j: (i, jnp.clip(j - BLK_AF * nb, 0, nb - 1))),
            pl.BlockSpec((tm, 128), lambda i, j: (i, 0)),
            pl.BlockSpec((tm, 128), lambda i, j: (i, 0)),
        ],
        out_shape=[
            jax.ShapeDtypeStruct((SEQ, N_MAIN), BF16),
            jax.ShapeDtypeStruct((SEQ, WIDTH), F32),
            jax.ShapeDtypeStruct((SEQ, 128), BF16),
            jax.ShapeDtypeStruct((SEQ, 128), F32),
        ],
        scratch_shapes=[pltpu.VMEM((tm, D_MODEL), BF16), pltpu.VMEM((1, 128), F32)],
        compiler_params=_params("arbitrary", "arbitrary"),
        name="inproj",
    )(x2, norm_w, w_a, w_g, wf_pad, bf_row, tri)


def _hgrn_constants(c):
    n_lvl = int(np.log2(c))
    t = np.arange(c)[:, None]
    j = np.arange(c)[None, :]
    blocks = [(j <= t), (j > t)]
    level = np.full((c, c), -1, np.int32)
    level[np.arange(c), np.arange(c)] = 0
    for l in range(n_lvl):
        b = 2 << l
        mid = (t // b) * b + b // 2 - 1
        second = (t % b) >= b // 2
        m = np.where(second, (j > mid) & (j <= t), (j > t) & (j <= mid))
        blocks.append(m)
        s = np.arange(c)[None, :]
        own = (t // b == s // b) & second & ((s % b) < b // 2)
        level[own] = l + 1
    sums = np.concatenate(blocks, axis=0).astype(np.float32)
    sums2 = np.concatenate([sums, sums], axis=1)
    return jnp.asarray(sums2, BF16), jnp.asarray(level), n_lvl


def _hgrn_kernel(q_ref, z_ref, v_ref, g_ref, lbl_ref, nw_ref, sums_ref, lvl_ref, y_ref, st_ref, *, c, n_lvl):
    n_heads = st_ref.shape[0]
    d = HEAD_DIM

    @pl.when(pl.program_id(1) == 0)
    def _():
        st_ref[...] = jnp.zeros_like(st_ref)

    logits = lbl_ref[...]
    ex = jnp.exp(logits - jnp.max(logits, axis=0, keepdims=True))
    lb = ex[0:1, :] / jnp.sum(ex, axis=0, keepdims=True)
    one_m_lb = 1.0 - lb
    nw = nw_ref[...]
    sums = sums_ref[...]
    lvl = lvl_ref[...]
    n_chunks = q_ref.shape[0] // c

    def chunk(ci, carry):
        r = pl.ds(pl.multiple_of(ci * c, c), c)
        z = z_ref[r, :]
        e = jnp.exp(-jnp.abs(z))
        inv = 1.0 / (1.0 + e)
        pos = z >= 0.0
        sig = jnp.where(pos, inv, e * inv)
        sig_n = jnp.where(pos, e * inv, inv)
        g = jnp.log2(lb + one_m_lb * sig)
        k_all = one_m_lb * sig_n
        g_hi = g.astype(BF16)
        g_lo = (g - g_hi.astype(F32)).astype(BF16)
        expo = jnp.dot(sums, jnp.concatenate([g_hi, g_lo], axis=0), preferred_element_type=F32)
        dec_all = jnp.exp2(expo)
        heads = [slice(hh * d, (hh + 1) * d) for hh in range(n_heads)]
        scores = []
        for cols in heads:
            q = q_ref[r, cols].astype(F32)
            k = k_all[:, cols]
            sc = jnp.where(lvl == 0, _dot_nt(q.astype(BF16), k.astype(BF16)), 0.0)
            for l in range(n_lvl):
                d_l = dec_all[(2 + l) * c:(3 + l) * c, cols]
                s_l = _dot_nt((q * d_l).astype(BF16), (k * d_l).astype(BF16))
                sc = jnp.where(lvl == l + 1, s_l, sc)
            scores.append(sc.astype(BF16))
        outs = []
        for hh, cols in enumerate(heads):
            q = q_ref[r, cols].astype(F32)
            v = v_ref[r, cols]
            q_in = (q * dec_all[0:c, cols]).astype(BF16)
            k_out = (k_all[:, cols] * dec_all[c:2 * c, cols]).astype(BF16)
            st = st_ref[hh]
            outs.append(_dot_nt(q_in, st.astype(BF16)) + jnp.dot(scores[hh], v, preferred_element_type=F32))
            st_ref[hh] = st * dec_all[c - 1:c, cols] + _dot_tn(v, k_out)
        for o, cols in zip(outs, heads):
            o = o * lax.rsqrt(jnp.mean(o * o, axis=-1, keepdims=True) + RMS_EPS) * nw
            gt = g_ref[r, cols].astype(F32)
            y_ref[r, cols] = (o * gt * _sigmoid(gt)).astype(y_ref.dtype)
        return carry

    lax.fori_loop(0, n_chunks, chunk, 0)


def _hgrn(proj, a_f, lb_logits, norm_w):
    ts, c, hg = HG_TS, HG_C, HG_HEADS
    sums, lvl, n_lvl = _hgrn_constants(c)
    w = hg * HEAD_DIM
    per_blk = WIDTH // w
    grid = (N_HEADS // hg, SEQ // ts)
    return pl.pallas_call(
        functools.partial(_hgrn_kernel, c=c, n_lvl=n_lvl),
        grid=grid,
        in_specs=[
            pl.BlockSpec((ts, w), lambda h, i: (i, BLK_AQ * per_blk + h)),
            pl.BlockSpec((ts, w), lambda h, i: (i, h)),
            pl.BlockSpec((ts, w), lambda h, i: (i, BLK_AI * per_blk + h)),
            pl.BlockSpec((ts, w), lambda h, i: (i, BLK_AG * per_blk + h)),
            pl.BlockSpec((lb_logits.shape[0], w), lambda h, i: (0, h)),
            pl.BlockSpec((1, HEAD_DIM), lambda h, i: (0, 0)),
            pl.BlockSpec(sums.shape, lambda h, i: (0, 0)),
            pl.BlockSpec(lvl.shape, lambda h, i: (0, 0)),
        ],
        out_specs=pl.BlockSpec((ts, w), lambda h, i: (i, h)),
        out_shape=jax.ShapeDtypeStruct((SEQ, WIDTH), BF16),
        scratch_shapes=[pltpu.VMEM((hg, HEAD_DIM, HEAD_DIM), F32)],
        compiler_params=_params("arbitrary", "arbitrary"),
        name="hgrn",
    )(proj, a_f, proj, proj, lb_logits, norm_w, sums, lvl)


def _fox_kernel(q_ref, k_ref, v_ref, cx_ref, nc_ref, o_ref, vt_ref, kn_ref, m_ref, acc_ref, *, tk):
    h = pl.program_id(0)
    i = pl.program_id(1)
    tq = q_ref.shape[0]
    s_len = k_ref.shape[0]
    n_sub = tq // tk
    n_kt = s_len // tk
    vrows = vt_ref.shape[1]

    @pl.when(i == 0)
    def _():
        ones_row = jnp.where(lax.broadcasted_iota(jnp.int32, (vrows - HEAD_DIM, tk), 0) == 0, 1.0, 0.0)
        kn = jnp.zeros((1, 1), F32)
        for r in range(n_kt):
            rows = slice(r * tk, (r + 1) * tk)
            vt_ref[r, 0:HEAD_DIM, :] = v_ref[rows, :].astype(F32).T.astype(BF16)
            vt_ref[r, HEAD_DIM:vrows, :] = ones_row.astype(BF16)
            kf = k_ref[rows, :].astype(F32)
            kn = jnp.maximum(kn, jnp.max(jnp.sum(kf * kf, axis=1, keepdims=True), axis=0, keepdims=True))
        kn_ref[...] = jnp.broadcast_to(kn, kn_ref.shape)

    qf = q_ref[...].astype(F32)
    qn = jnp.max(jnp.sum(qf * qf, axis=1, keepdims=True), axis=0, keepdims=True)
    rsel = lax.broadcasted_iota(jnp.int32, (HEAD_DIM, tq), 0)
    sel = jnp.where((rsel < 3 * N_HEADS) & ((rsel & (N_HEADS - 1)) == h), 1.0, 0.0).astype(BF16)
    qa = jnp.concatenate([qf.T.astype(BF16), sel], axis=0)

    bound = 2.0 * jnp.sqrt(qn * kn_ref[0:1, 0:1])
    thr = nc_ref[pl.ds(pl.multiple_of(i * tq, tq), 1), :] - bound - FOX_SKIP_LOG2
    ends = nc_ref[pl.ds(tk - 1, n_kt, stride=tk), :]
    lane = lax.broadcasted_iota(jnp.int32, ends.shape, 1)
    r_lo = jnp.sum(jnp.where((ends < thr) & (lane == h), 1, 0))

    m_ref[...] = jnp.full_like(m_ref, -jnp.inf)
    acc_ref[...] = jnp.zeros_like(acc_ref)

    def step(tiles):
        ss = []
        for r, mask_off in tiles:
            ks = pl.ds(pl.multiple_of(r * tk, tk), tk)
            ka = jnp.concatenate([k_ref[ks, :], cx_ref[ks, :]], axis=1)
            s = jnp.dot(ka, qa, preferred_element_type=F32)
            if mask_off is not None:
                kid = lax.broadcasted_iota(jnp.int32, s.shape, 0) + mask_off
                qid = lax.broadcasted_iota(jnp.int32, s.shape, 1)
                s = jnp.where(kid <= qid, s, -jnp.inf)
            ss.append(s)
        m_prev = m_ref[...]
        m_new = m_prev
        for s in ss:
            m_new = jnp.maximum(m_new, jnp.max(s, axis=0, keepdims=True))
        pv = None
        for (r, _), s in zip(tiles, ss):
            d = jnp.dot(vt_ref[r], jnp.exp2(s - m_new).astype(BF16), preferred_element_type=F32)
            pv = d if pv is None else pv + d
        acc_ref[...] = jnp.exp2(m_prev - m_new) * acc_ref[...] + pv
        m_ref[...] = m_new

    def off_diag(r, carry):
        step([(r, None)])
        return carry

    diag = [(i * n_sub + rr, rr * tk) for rr in range(n_sub)]

    @pl.when(i == 0)
    def _():
        step(diag)

    @pl.when(i > 0)
    def _():
        step([(i * n_sub - 1, None)] + diag)
        lax.fori_loop(r_lo, i * n_sub - 1, off_diag, 0)

    acc = acc_ref[...]
    o_ref[...] = (acc[0:HEAD_DIM] / acc[HEAD_DIM:HEAD_DIM + 1]).T.astype(o_ref.dtype)


def _fox(proj, cx, nc):
    tq, tk = FOX_TQ, FOX_TK
    hb = HEADS_PER_BLK
    vrows = HEAD_DIM + 16
    return pl.pallas_call(
        functools.partial(_fox_kernel, tk=tk),
        grid=(N_HEADS, SEQ // tq),
        in_specs=[
            pl.BlockSpec((tq, HEAD_DIM), lambda h, i: (i, BLK_BQ * hb + h)),
            pl.BlockSpec((SEQ, HEAD_DIM), lambda h, i: (0, BLK_BK * hb + h)),
            pl.BlockSpec((SEQ, HEAD_DIM), lambda h, i: (0, BLK_BV * hb + h)),
            pl.BlockSpec((SEQ, 128), lambda h, i: (0, 0)),
            pl.BlockSpec((SEQ, 128), lambda h, i: (0, 0)),
        ],
        out_specs=pl.BlockSpec((tq, HEAD_DIM), lambda h, i: (i, h)),
        out_shape=jax.ShapeDtypeStruct((SEQ, WIDTH), BF16),
        scratch_shapes=[
            pltpu.VMEM((SEQ // tk, vrows, tk), BF16),
            pltpu.VMEM((8, 128), F32),
            pltpu.VMEM((1, tq), F32),
            pltpu.VMEM((vrows, tq), F32),
        ],
        compiler_params=_params("arbitrary", "arbitrary"),
        name="fox",
    )(proj, proj, proj, cx, nc)


def _merge_kernel(oa_ref, ob_ref, ga_ref, gb_ref, wa_ref, wb_ref, wo_ref, x_ref, nw_ref, out_ref):
    ya = jnp.dot(oa_ref[...], wa_ref[...], preferred_element_type=F32)
    yb = jnp.dot(ob_ref[...], wb_ref[...], preferred_element_type=F32)
    merged = _sigmoid(ga_ref[...].astype(F32)) * ya + _sigmoid(gb_ref[...].astype(F32)) * yb
    u = jnp.dot(merged.astype(BF16), wo_ref[...], preferred_element_type=F32)
    u = u * lax.rsqrt(jnp.mean(u * u, axis=-1, keepdims=True) + RMS_EPS) * nw_ref[...]
    out_ref[...] = x_ref[...] + u


def _merge(y_a, y_b, proj, w_up_a, w_up_b, w_o, x2, norm_w):
    tm = MERGE_TM
    once = pl.Buffered(1)
    return pl.pallas_call(
        _merge_kernel,
        grid=(SEQ // tm,),
        in_specs=[
            pl.BlockSpec((tm, WIDTH), lambda i: (i, 0)),
            pl.BlockSpec((tm, WIDTH), lambda i: (i, 0)),
            pl.BlockSpec((tm, D_MODEL), lambda i: (i, BLK_GA // 2)),
            pl.BlockSpec((tm, D_MODEL), lambda i: (i, BLK_GB // 2)),
            pl.BlockSpec((WIDTH, D_MODEL), lambda i: (0, 0), pipeline_mode=once),
            pl.BlockSpec((WIDTH, D_MODEL), lambda i: (0, 0), pipeline_mode=once),
            pl.BlockSpec((D_MODEL, D_MODEL), lambda i: (0, 0), pipeline_mode=once),
            pl.BlockSpec((tm, D_MODEL), lambda i: (i, 0)),
            pl.BlockSpec((1, D_MODEL), lambda i: (0, 0)),
        ],
        out_specs=pl.BlockSpec((tm, D_MODEL), lambda i: (i, 0)),
        out_shape=jax.ShapeDtypeStruct((SEQ, D_MODEL), F32),
        compiler_params=_params("arbitrary"),
        name="merge",
    )(y_a, y_b, proj, proj, w_up_a, w_up_b, w_o, x2, norm_w)


def _ffn_kernel(x_ref, npre_ref, npost_ref, wg_ref, wu_ref, wd_ref, out_ref, h_ref):
    f = pl.program_id(1)

    @pl.when(f == 0)
    def _():
        x = x_ref[...]
        ms = jnp.mean(x * x, axis=-1, keepdims=True)
        h_ref[...] = (x * lax.rsqrt(ms + RMS_EPS) * npre_ref[...]).astype(BF16)
        out_ref[...] = jnp.zeros_like(out_ref)

    h = h_ref[...]
    gate = jnp.dot(h, wg_ref[...], preferred_element_type=F32)
    up = jnp.dot(h, wu_ref[...], preferred_element_type=F32)
    act = (gate * _sigmoid(gate) * up).astype(BF16)
    out_ref[...] += jnp.dot(act, wd_ref[...], preferred_element_type=F32)

    @pl.when(f == pl.num_programs(1) - 1)
    def _():
        u = out_ref[...]
        u = u * lax.rsqrt(jnp.mean(u * u, axis=-1, keepdims=True) + RMS_EPS) * npost_ref[...]
        out_ref[...] = x_ref[...] + u


def _ffn(x1, norm_pre, norm_post, w_in, w_down):
    tm, tf = FFN_TM, FFN_TF
    nf = D_FF // tf
    return pl.pallas_call(
        _ffn_kernel,
        grid=(SEQ // tm, nf),
        in_specs=[
            pl.BlockSpec((tm, D_MODEL), lambda i, f: (i, 0)),
            pl.BlockSpec((1, D_MODEL), lambda i, f: (0, 0)),
            pl.BlockSpec((1, D_MODEL), lambda i, f: (0, 0)),
            pl.BlockSpec((D_MODEL, tf), lambda i, f: (0, f)),
            pl.BlockSpec((D_MODEL, tf), lambda i, f: (0, f + nf)),
            pl.BlockSpec((tf, D_MODEL), lambda i, f: (f, 0)),
        ],
        out_specs=pl.BlockSpec((tm, D_MODEL), lambda i, f: (i, 0)),
        out_shape=jax.ShapeDtypeStruct((SEQ, D_MODEL), F32),
        scratch_shapes=[pltpu.VMEM((tm, D_MODEL), BF16)],
        compiler_params=_params("arbitrary", "arbitrary"),
        name="ffn",
    )(x1, norm_pre, norm_post, w_in, w_in, w_down)


def kernel(x, w_in, b_fox_f, hgrn_lb_logits, hgrn_norm_w, w_up_a, w_up_b, w_o, norm_mix_pre,
           norm_mix_post, norm_ffn_pre, norm_ffn_post, w_ffn_in, w_ffn_down):
    assert x.shape == (1, SEQ, D_MODEL) and w_in.shape[0] == 1
    w = w_in[0]
    o_bf, o_ga = IN_NA * WIDTH, IN_NA * WIDTH + N_HEADS
    w_a = w[:, :o_bf].astype(BF16)
    w_g = w[:, o_ga:].astype(BF16)
    wf_pad = jnp.pad(w[:, o_bf:o_ga], ((0, 0), (0, 128 - N_HEADS))).astype(BF16)
    bf_row = jnp.pad(b_fox_f.reshape(1, N_HEADS), ((0, 0), (0, 128 - N_HEADS)))
    x2 = x[0]

    proj, a_f, cx, nc = _inproj(x2, norm_mix_pre, w_a, w_g, wf_pad, bf_row)
    y_a = _hgrn(proj, a_f, hgrn_lb_logits, hgrn_norm_w)
    y_b = _fox(proj, cx, nc)
    x1 = _merge(y_a, y_b, proj, w_up_a[0].astype(BF16), w_up_b[0].astype(BF16), w_o[0].astype(BF16),
                x2, norm_mix_post)
    out = _ffn(x1, norm_ffn_pre, norm_ffn_post, w_ffn_in[0].astype(BF16), w_ffn_down[0].astype(BF16))
    return out[None]
```

```python
import functools

import numpy as np
import jax
import jax.numpy as jnp
from jax import lax
from jax.experimental import pallas as pl
from jax.experimental.pallas import tpu as pltpu

F32 = jnp.float32
BF16 = jnp.bfloat16

D_MODEL = 2048
SEQ = 8192
HEAD_DIM = 128
N_HEADS = 8
WIDTH = N_HEADS * HEAD_DIM
D_FF = 5632
RMS_EPS = 1e-6
N_PROJ = 3 * WIDTH + 2 * D_MODEL + 3 * WIDTH

VMEM_LIMIT_BYTES = 56 * 1024 * 1024

SRC_AQ, SRC_AF, SRC_AI, SRC_AG, SRC_BQ, SRC_BK, SRC_BV = 0, 1, 2, 3, 4, 5, 6
IN_NA = 7
BLK_AQ, BLK_AI, BLK_GA, BLK_GB, BLK_AG, BLK_BQ, BLK_BK, BLK_BV = 0, 1, 2, 4, 6, 7, 8, 9
HEADS_PER_BLK = WIDTH // HEAD_DIM

PRE_TM = 512
IN_TM, IN_TN = 2048, 512
HG_TS, HG_C, HG_HEADS = 1024, 64, 8
FOX_TQ, FOX_TK = 512, 256
LOG2E = 1.4426950408889634
FOX_SKIP_LOG2 = 160.0
MERGE_TM = 256
FFN_TM, FFN_TF = 512, 512


def _params(*sem):
    return pltpu.CompilerParams(dimension_semantics=sem, vmem_limit_bytes=VMEM_LIMIT_BYTES)


def _dot_nt(a, b):
    return lax.dot_general(a, b, (((1,), (1,)), ((), ())), preferred_element_type=F32)


def _dot_tn(a, b):
    return lax.dot_general(a, b, (((0,), (0,)), ((), ())), preferred_element_type=F32)


def _log_sigmoid(x):
    return jnp.minimum(x, 0.0) - jnp.log(1.0 + jnp.exp(-jnp.abs(x)))


def _sigmoid(x):
    return 1.0 / (1.0 + jnp.exp(-x))


def _split3(x):
    p0 = x.astype(BF16)
    r1 = x - p0.astype(F32)
    p1 = r1.astype(BF16)
    p2 = (r1 - p1.astype(F32)).astype(BF16)
    return p0, p1, p2


def _prenorm_kernel(x_ref, nw_ref, waf_ref, wf_ref, bf_ref, tri_ref, h_ref, af_ref, cx_ref, nc_ref,
                    wafb_ref, carry_ref):
    i = pl.program_id(0)
    tm = x_ref.shape[0]

    @pl.when(i == 0)
    def _():
        wafb_ref[...] = waf_ref[...].astype(BF16)
        carry_ref[...] = jnp.zeros_like(carry_ref)

    x = x_ref[...]
    ms = jnp.mean(x * x, axis=-1, keepdims=True)
    hb = (x * lax.rsqrt(ms + RMS_EPS) * nw_ref[...]).astype(BF16)
    h_ref[...] = hb
    af_ref[...] = jnp.dot(hb, wafb_ref[...], preferred_element_type=F32)
    logit = jnp.dot(hb, wf_ref[...].astype(BF16), preferred_element_type=F32) + bf_ref[...]
    nls = _log_sigmoid(logit) * (-LOG2E)
    tri = tri_ref[...]
    loc = sum(jnp.dot(tri, p, preferred_element_type=F32) for p in _split3(nls))
    nc = loc + carry_ref[...]
    nc_ref[...] = nc
    carry_ref[...] = nc[tm - 1:tm, :]
    c0, c1, c2 = _split3(nc)
    lane = lax.broadcasted_iota(jnp.int32, nc.shape, 1)
    parts = jnp.where(lane < N_HEADS, c0.astype(F32),
                      jnp.where(lane < 2 * N_HEADS, pltpu.roll(c1.astype(F32), N_HEADS, 1),
                                jnp.where(lane < 3 * N_HEADS, pltpu.roll(c2.astype(F32), 2 * N_HEADS, 1),
                                          0.0)))
    cx_ref[...] = parts.astype(BF16)


def _prenorm(x2, norm_w, w, bf_row):
    tm = PRE_TM
    tri = jnp.asarray(np.tril(np.ones((tm, tm), np.float32)), BF16)
    once = pl.Buffered(1)
    return pl.pallas_call(
        _prenorm_kernel,
        grid=(SEQ // tm,),
        in_specs=[
            pl.BlockSpec((tm, D_MODEL), lambda i: (i, 0)),
            pl.BlockSpec((1, D_MODEL), lambda i: (0, 0)),
            pl.BlockSpec((D_MODEL, WIDTH), lambda i: (0, SRC_AF), pipeline_mode=once),
            pl.BlockSpec((D_MODEL, 128), lambda i: (0, IN_NA * WIDTH // 128), pipeline_mode=once),
            pl.BlockSpec((1, 128), lambda i: (0, 0)),
            pl.BlockSpec((tm, tm), lambda i: (0, 0), pipeline_mode=once),
        ],
        out_specs=[
            pl.BlockSpec((tm, D_MODEL), lambda i: (i, 0)),
            pl.BlockSpec((tm, WIDTH), lambda i: (i, 0)),
            pl.BlockSpec((tm, 128), lambda i: (i, 0)),
            pl.BlockSpec((tm, 128), lambda i: (i, 0)),
        ],
        out_shape=[
            jax.ShapeDtypeStruct((SEQ, D_MODEL), BF16),
            jax.ShapeDtypeStruct((SEQ, WIDTH), F32),
            jax.ShapeDtypeStruct((SEQ, 128), BF16),
            jax.ShapeDtypeStruct((SEQ, 128), F32),
        ],
        scratch_shapes=[pltpu.VMEM((D_MODEL, WIDTH), BF16), pltpu.VMEM((1, 128), F32)],
        compiler_params=_params("arbitrary"),
        name="prenorm",
    )(x2, norm_w, w, w, bf_row, tri)


def _inproj_kernel(h_ref, wa_ref, wb_ref, proj_ref, wc_ref):
    j = pl.program_id(1)
    i = pl.program_id(2)
    tm, tn = proj_ref.shape
    nb = WIDTH // tn
    src = _inproj_src_block(j, nb)

    @pl.when(i == 0)
    def _():
        @pl.when(src < IN_NA * nb)
        def _():
            scale = jnp.where(src // nb == SRC_BQ, LOG2E * HEAD_DIM ** -0.5, 1.0)
            wc_ref[...] = (wa_ref[...] * scale).astype(BF16)

        @pl.when(src >= IN_NA * nb)
        def _():
            wc_ref[...] = jnp.concatenate([wa_ref[:, N_HEADS:], wb_ref[:, :N_HEADS]], axis=1).astype(BF16)

    rows = pl.ds(pl.multiple_of(i * tm, tm), tm)
    proj_ref[...] = jnp.dot(h_ref[rows, :], wc_ref[...], preferred_element_type=F32).astype(BF16)


def _inproj_src_block(j, nb):
    return jnp.where(j < SRC_AF * nb, j, j + nb)


def _inproj_out_block(j, nb):
    src = _inproj_src_block(j, nb)
    g = src // nb
    g_out = jnp.int32(BLK_AQ)
    for g_src, g_dst in ((SRC_AI, BLK_AI), (SRC_AG, BLK_AG), (SRC_BQ, BLK_BQ), (SRC_BK, BLK_BK), (SRC_BV, BLK_BV)):
        g_out = jnp.where(g == g_src, g_dst, g_out)
    return jnp.where(g < IN_NA, g_out * nb + src % nb, src - IN_NA * nb + BLK_GA * nb)


def _inproj(h, w):
    tm, tn = IN_TM, IN_TN
    nb = WIDTH // tn
    half = SEQ // 2
    n_i = half // tm
    first_gate = IN_NA * nb
    grid = (2, N_PROJ // tn, n_i)
    return pl.pallas_call(
        _inproj_kernel,
        grid=grid,
        in_specs=[
            pl.BlockSpec((half, D_MODEL), lambda s, j, i: (s, 0), pipeline_mode=pl.Buffered(1)),
            pl.BlockSpec((D_MODEL, tn), lambda s, j, i: (0, _inproj_src_block(j, nb))),
            pl.BlockSpec((D_MODEL, tn), lambda s, j, i: (0, jnp.maximum(_inproj_src_block(j, nb), first_gate) + 1)),
        ],
        out_specs=pl.BlockSpec((tm, tn), lambda s, j, i: (s * n_i + i, _inproj_out_block(j, nb))),
        out_shape=jax.ShapeDtypeStruct((SEQ, N_PROJ), BF16),
        scratch_shapes=[pltpu.VMEM((D_MODEL, tn), BF16)],
        compiler_params=_params("arbitrary", "arbitrary", "arbitrary"),
        name="inproj",
    )(h, w, w)


def _hgrn_constants(c):
    n_lvl = int(np.log2(c))
    t = np.arange(c)[:, None]
    j = np.arange(c)[None, :]
    blocks = [(j <= t), (j > t)]
    level = np.full((c, c), -1, np.int32)
    level[np.arange(c), np.arange(c)] = 0
    for l in range(n_lvl):
        b = 2 << l
        mid = (t // b) * b + b // 2 - 1
        second = (t % b) >= b // 2
        m = np.where(second, (j > mid) & (j <= t), (j > t) & (j <= mid))
        blocks.append(m)
        s = np.arange(c)[None, :]
        own = (t // b == s // b) & second & ((s % b) < b // 2)
        level[own] = l + 1
    sums = np.concatenate(blocks, axis=0).astype(np.float32)
    sums2 = np.concatenate([sums, sums], axis=1)
    return jnp.asarray(sums2, BF16), jnp.asarray(level), n_lvl


def _hgrn_kernel(q_ref, z_ref, v_ref, g_ref, lbl_ref, nw_ref, sums_ref, lvl_ref, y_ref, st_ref, *, c, n_lvl):
    n_heads = st_ref.shape[0]
    d = HEAD_DIM

    @pl.when(pl.program_id(1) == 0)
    def _():
        st_ref[...] = jnp.zeros_like(st_ref)

    logits = lbl_ref[...]
    ex = jnp.exp(logits - jnp.max(logits, axis=0, keepdims=True))
    lb = ex[0:1, :] / jnp.sum(ex, axis=0, keepdims=True)
    one_m_lb = 1.0 - lb
    nw = nw_ref[...]
    sums = sums_ref[...]
    lvl = lvl_ref[...]
    n_chunks = q_ref.shape[0] // c

    def chunk(ci, carry):
        r = pl.ds(pl.multiple_of(ci * c, c), c)
        z = z_ref[r, :]
        e = jnp.exp(-jnp.abs(z))
        inv = 1.0 / (1.0 + e)
        pos = z >= 0.0
        sig = jnp.where(pos, inv, e * inv)
        sig_n = jnp.where(pos, e * inv, inv)
        g = jnp.log2(lb + one_m_lb * sig)
        k_all = one_m_lb * sig_n
        g_hi = g.astype(BF16)
        g_lo = (g - g_hi.astype(F32)).astype(BF16)
        expo = jnp.dot(sums, jnp.concatenate([g_hi, g_lo], axis=0), preferred_element_type=F32)
        dec_all = jnp.exp2(expo)
        heads = [slice(hh * d, (hh + 1) * d) for hh in range(n_heads)]
        scores = []
        for cols in heads:
            q = q_ref[r, cols].astype(F32)
            k = k_all[:, cols]
            sc = jnp.where(lvl == 0, _dot_nt(q.astype(BF16), k.astype(BF16)), 0.0)
            for l in range(n_lvl):
                d_l = dec_all[(2 + l) * c:(3 + l) * c, cols]
                s_l = _dot_nt((q * d_l).astype(BF16), (k * d_l).astype(BF16))
                sc = jnp.where(lvl == l + 1, s_l, sc)
            scores.append(sc.astype(BF16))
        outs = []
        for hh, cols in enumerate(heads):
            q = q_ref[r, cols].astype(F32)
            v = v_ref[r, cols]
            q_in = (q * dec_all[0:c, cols]).astype(BF16)
            k_out = (k_all[:, cols] * dec_all[c:2 * c, cols]).astype(BF16)
            st = st_ref[hh]
            outs.append(_dot_nt(q_in, st.astype(BF16)) + jnp.dot(scores[hh], v, preferred_element_type=F32))
            st_ref[hh] = st * dec_all[c - 1:c, cols] + _dot_tn(v, k_out)
        for o, cols in zip(outs, heads):
            o = o * lax.rsqrt(jnp.mean(o * o, axis=-1, keepdims=True) + RMS_EPS) * nw
            gt = g_ref[r, cols].astype(F32)
            y_ref[r, cols] = (o * gt * _sigmoid(gt)).astype(y_ref.dtype)
        return carry

    lax.fori_loop(0, n_chunks, chunk, 0)


def _hgrn(proj, a_f, lb_logits, norm_w):
    ts, c, hg = HG_TS, HG_C, HG_HEADS
    sums, lvl, n_lvl = _hgrn_constants(c)
    w = hg * HEAD_DIM
    per_blk = WIDTH // w
    grid = (N_HEADS // hg, SEQ // ts)
    return pl.pallas_call(
        functools.partial(_hgrn_kernel, c=c, n_lvl=n_lvl),
        grid=grid,
        in_specs=[
            pl.BlockSpec((ts, w), lambda h, i: (i, BLK_AQ * per_blk + h)),
            pl.BlockSpec((ts, w), lambda h, i: (i, h)),
            pl.BlockSpec((ts, w), lambda h, i: (i, BLK_AI * per_blk + h)),
            pl.BlockSpec((ts, w), lambda h, i: (i, BLK_AG * per_blk + h)),
            pl.BlockSpec((lb_logits.shape[0], w), lambda h, i: (0, h)),
            pl.BlockSpec((1, HEAD_DIM), lambda h, i: (0, 0)),
            pl.BlockSpec(sums.shape, lambda h, i: (0, 0)),
            pl.BlockSpec(lvl.shape, lambda h, i: (0, 0)),
        ],
        out_specs=pl.BlockSpec((ts, w), lambda h, i: (i, h)),
        out_shape=jax.ShapeDtypeStruct((SEQ, WIDTH), BF16),
        scratch_shapes=[pltpu.VMEM((hg, HEAD_DIM, HEAD_DIM), F32)],
        compiler_params=_params("arbitrary", "arbitrary"),
        name="hgrn",
    )(proj, a_f, proj, proj, lb_logits, norm_w, sums, lvl)


def _fox_kernel(q_ref, k_ref, v_ref, cx_ref, nc_ref, o_ref, vt_ref, kn_ref, m_ref, acc_ref, *, tk):
    h = pl.program_id(0)
    i = pl.program_id(1)
    tq = q_ref.shape[0]
    s_len = k_ref.shape[0]
    n_sub = tq // tk
    n_kt = s_len // tk
    vrows = vt_ref.shape[1]

    @pl.when(i == 0)
    def _():
        ones_row = jnp.where(lax.broadcasted_iota(jnp.int32, (vrows - HEAD_DIM, tk), 0) == 0, 1.0, 0.0)
        kn = jnp.zeros((1, 1), F32)
        for r in range(n_kt):
            rows = slice(r * tk, (r + 1) * tk)
            vt_ref[r, 0:HEAD_DIM, :] = v_ref[rows, :].astype(F32).T.astype(BF16)
            vt_ref[r, HEAD_DIM:vrows, :] = ones_row.astype(BF16)
            kf = k_ref[rows, :].astype(F32)
            kn = jnp.maximum(kn, jnp.max(jnp.sum(kf * kf, axis=1, keepdims=True), axis=0, keepdims=True))
        kn_ref[...] = jnp.broadcast_to(kn, kn_ref.shape)

    qf = q_ref[...].astype(F32)
    qn = jnp.max(jnp.sum(qf * qf, axis=1, keepdims=True), axis=0, keepdims=True)
    rsel = lax.broadcasted_iota(jnp.int32, (HEAD_DIM, tq), 0)
    sel = jnp.where((rsel < 3 * N_HEADS) & ((rsel & (N_HEADS - 1)) == h), 1.0, 0.0).astype(BF16)
    qa = jnp.concatenate([qf.T.astype(BF16), sel], axis=0)

    bound = 2.0 * jnp.sqrt(qn * kn_ref[0:1, 0:1])
    thr = nc_ref[pl.ds(pl.multiple_of(i * tq, tq), 1), :] - bound - FOX_SKIP_LOG2
    ends = nc_ref[pl.ds(tk - 1, n_kt, stride=tk), :]
    lane = lax.broadcasted_iota(jnp.int32, ends.shape, 1)
    r_lo = jnp.sum(jnp.where((ends < thr) & (lane == h), 1, 0))

    m_ref[...] = jnp.full_like(m_ref, -jnp.inf)
    acc_ref[...] = jnp.zeros_like(acc_ref)

    def step(tiles):
        ss = []
        for r, mask_off in tiles:
            ks = pl.ds(pl.multiple_of(r * tk, tk), tk)
            ka = jnp.concatenate([k_ref[ks, :], cx_ref[ks, :]], axis=1)
            s = jnp.dot(ka, qa, preferred_element_type=F32)
            if mask_off is not None:
                kid = lax.broadcasted_iota(jnp.int32, s.shape, 0) + mask_off
                qid = lax.broadcasted_iota(jnp.int32, s.shape, 1)
                s = jnp.where(kid <= qid, s, -jnp.inf)
            ss.append(s)
        m_prev = m_ref[...]
        m_new = m_prev
        for s in ss:
            m_new = jnp.maximum(m_new, jnp.max(s, axis=0, keepdims=True))
        pv = None
        for (r, _), s in zip(tiles, ss):
            d = jnp.dot(vt_ref[r], jnp.exp2(s - m_new).astype(BF16), preferred_element_type=F32)
            pv = d if pv is None else pv + d
        acc_ref[...] = jnp.exp2(m_prev - m_new) * acc_ref[...] + pv
        m_ref[...] = m_new

    def off_diag(r, carry):
        step([(r, None)])
        return carry

    diag = [(i * n_sub + rr, rr * tk) for rr in range(n_sub)]

    @pl.when(i == 0)
    def _():
        step(diag)

    @pl.when(i > 0)
    def _():
        step([(i * n_sub - 1, None)] + diag)
        lax.fori_loop(r_lo, i * n_sub - 1, off_diag, 0)

    acc = acc_ref[...]
    o_ref[...] = (acc[0:HEAD_DIM] / acc[HEAD_DIM:HEAD_DIM + 1]).T.astype(o_ref.dtype)


def _fox(proj, cx, nc):
    tq, tk = FOX_TQ, FOX_TK
    hb = HEADS_PER_BLK
    vrows = HEAD_DIM + 16
    return pl.pallas_call(
        functools.partial(_fox_kernel, tk=tk),
        grid=(N_HEADS, SEQ // tq),
        in_specs=[
            pl.BlockSpec((tq, HEAD_DIM), lambda h, i: (i, BLK_BQ * hb + h)),
            pl.BlockSpec((SEQ, HEAD_DIM), lambda h, i: (0, BLK_BK * hb + h)),
            pl.BlockSpec((SEQ, HEAD_DIM), lambda h, i: (0, BLK_BV * hb + h)),
            pl.BlockSpec((SEQ, 128), lambda h, i: (0, 0)),
            pl.BlockSpec((SEQ, 128), lambda h, i: (0, 0)),
        ],
        out_specs=pl.BlockSpec((tq, HEAD_DIM), lambda h, i: (i, h)),
        out_shape=jax.ShapeDtypeStruct((SEQ, WIDTH), BF16),
        scratch_shapes=[
            pltpu.VMEM((SEQ // tk, vrows, tk), BF16),
            pltpu.VMEM((8, 128), F32),
            pltpu.VMEM((1, tq), F32),
            pltpu.VMEM((vrows, tq), F32),
        ],
        compiler_params=_params("arbitrary", "arbitrary"),
        name="fox",
    )(proj, proj, proj, cx, nc)


def _merge_kernel(oa_ref, ob_ref, ga_ref, gb_ref, wa_ref, wb_ref, wo_ref, x_ref, nw_ref, out_ref):
    ya = jnp.dot(oa_ref[...], wa_ref[...], preferred_element_type=F32)
    yb = jnp.dot(ob_ref[...], wb_ref[...], preferred_element_type=F32)
    merged = _sigmoid(ga_ref[...].astype(F32)) * ya + _sigmoid(gb_ref[...].astype(F32)) * yb
    u = jnp.dot(merged.astype(BF16), wo_ref[...], preferred_element_type=F32)
    u = u * lax.rsqrt(jnp.mean(u * u, axis=-1, keepdims=True) + RMS_EPS) * nw_ref[...]
    out_ref[...] = x_ref[...] + u


def _merge(y_a, y_b, proj, w_up_a, w_up_b, w_o, x2, norm_w):
    tm = MERGE_TM
    once = pl.Buffered(1)
    return pl.pallas_call(
        _merge_kernel,
        grid=(SEQ // tm,),
        in_specs=[
            pl.BlockSpec((tm, WIDTH), lambda i: (i, 0)),
            pl.BlockSpec((tm, WIDTH), lambda i: (i, 0)),
            pl.BlockSpec((tm, D_MODEL), lambda i: (i, BLK_GA // 2)),
            pl.BlockSpec((tm, D_MODEL), lambda i: (i, BLK_GB // 2)),
            pl.BlockSpec((WIDTH, D_MODEL), lambda i: (0, 0), pipeline_mode=once),
            pl.BlockSpec((WIDTH, D_MODEL), lambda i: (0, 0), pipeline_mode=once),
            pl.BlockSpec((D_MODEL, D_MODEL), lambda i: (0, 0), pipeline_mode=once),
            pl.BlockSpec((tm, D_MODEL), lambda i: (i, 0)),
            pl.BlockSpec((1, D_MODEL), lambda i: (0, 0)),
        ],
        out_specs=pl.BlockSpec((tm, D_MODEL), lambda i: (i, 0)),
        out_shape=jax.ShapeDtypeStruct((SEQ, D_MODEL), F32),
        compiler_params=_params("arbitrary"),
        name="merge",
    )(y_a, y_b, proj, proj, w_up_a, w_up_b, w_o, x2, norm_w)


def _ffn_kernel(x_ref, npre_ref, npost_ref, wg_ref, wu_ref, wd_ref, out_ref, h_ref):
    f = pl.program_id(1)

    @pl.when(f == 0)
    def _():
        x = x_ref[...]
        ms = jnp.mean(x * x, axis=-1, keepdims=True)
        h_ref[...] = (x * lax.rsqrt(ms + RMS_EPS) * npre_ref[...]).astype(BF16)
        out_ref[...] = jnp.zeros_like(out_ref)

    h = h_ref[...]
    gate = jnp.dot(h, wg_ref[...], preferred_element_type=F32)
    up = jnp.dot(h, wu_ref[...], preferred_element_type=F32)
    act = (gate * _sigmoid(gate) * up).astype(BF16)
    out_ref[...] += jnp.dot(act, wd_ref[...], preferred_element_type=F32)

    @pl.when(f == pl.num_programs(1) - 1)
    def _():
        u = out_ref[...]
        u = u * lax.rsqrt(jnp.mean(u * u, axis=-1, keepdims=True) + RMS_EPS) * npost_ref[...]
        out_ref[...] = x_ref[...] + u


def _ffn(x1, norm_pre, norm_post, w_in, w_down):
    tm, tf = FFN_TM, FFN_TF
    nf = D_FF // tf
    return pl.pallas_call(
        _ffn_kernel,
        grid=(SEQ // tm, nf),
        in_specs=[
            pl.BlockSpec((tm, D_MODEL), lambda i, f: (i, 0)),
            pl.BlockSpec((1, D_MODEL), lambda i, f: (0, 0)),
            pl.BlockSpec((1, D_MODEL), lambda i, f: (0, 0)),
            pl.BlockSpec((D_MODEL, tf), lambda i, f: (0, f)),
            pl.BlockSpec((D_MODEL, tf), lambda i, f: (0, f + nf)),
            pl.BlockSpec((tf, D_MODEL), lambda i, f: (f, 0)),
        ],
        out_specs=pl.BlockSpec((tm, D_MODEL), lambda i, f: (i, 0)),
        out_shape=jax.ShapeDtypeStruct((SEQ, D_MODEL), F32),
        scratch_shapes=[pltpu.VMEM((tm, D_MODEL), BF16)],
        compiler_params=_params("arbitrary", "arbitrary"),
        name="ffn",
    )(x1, norm_pre, norm_post, w_in, w_in, w_down)


def kernel(x, w_in, b_fox_f, hgrn_lb_logits, hgrn_norm_w, w_up_a, w_up_b, w_o, norm_mix_pre,
           norm_mix_post, norm_ffn_pre, norm_ffn_post, w_ffn_in, w_ffn_down):
    assert x.shape == (1, SEQ, D_MODEL) and w_in.shape[0] == 1
    w = w_in[0]
    bf_row = jnp.pad(b_fox_f.reshape(1, N_HEADS), ((0, 0), (0, 128 - N_HEADS)))
    x2 = x[0]

    h, a_f, cx, nc = _prenorm(x2, norm_mix_pre, w, bf_row)
    proj = _inproj(h, w)
    y_a = _hgrn(proj, a_f, hgrn_lb_logits, hgrn_norm_w)
    y_b = _fox(proj, cx, nc)
    x1 = _merge(y_a, y_b, proj, w_up_a[0].astype(BF16), w_up_b[0].astype(BF16), w_o[0].astype(BF16),
                x2, norm_mix_post)
    out = _ffn(x1, norm_ffn_pre, norm_ffn_post, w_ffn_in[0].astype(BF16), w_ffn_down[0].astype(BF16))
    return out[None]
```

```python
import functools

import numpy as np
import jax
import jax.numpy as jnp
from jax import lax
from jax.experimental import pallas as pl
from jax.experimental.pallas import tpu as pltpu

F32 = jnp.float32
BF16 = jnp.bfloat16

D_MODEL = 2048
SEQ = 8192
HEAD_DIM = 128
N_HEADS = 8
WIDTH = N_HEADS * HEAD_DIM
D_FF = 5632
RMS_EPS = 1e-6
N_PROJ = 3 * WIDTH + 2 * D_MODEL + 3 * WIDTH

VMEM_LIMIT_BYTES = 56 * 1024 * 1024

SRC_AQ, SRC_AF, SRC_AI, SRC_AG, SRC_BQ, SRC_BK, SRC_BV = 0, 1, 2, 3, 4, 5, 6
IN_NA = 7
BLK_AQ, BLK_AI, BLK_GA, BLK_GB, BLK_AG, BLK_BQ, BLK_BK, BLK_BV = 0, 1, 2, 4, 6, 7, 8, 9
HEADS_PER_BLK = WIDTH // HEAD_DIM

PRE_TM = 512
IN_TM, IN_TN = 2048, 512
HG_TS, HG_C, HG_HEADS = 1024, 64, 8
FOX_TQ, FOX_TK = 512, 256
LOG2E = 1.4426950408889634
FOX_SKIP_LOG2 = 160.0
MERGE_TM = 256
FFN_TM, FFN_TF = 512, 512


def _params(*sem):
    return pltpu.CompilerParams(dimension_semantics=sem, vmem_limit_bytes=VMEM_LIMIT_BYTES)


def _dot_nt(a, b):
    return lax.dot_general(a, b, (((1,), (1,)), ((), ())), preferred_element_type=F32)


def _dot_tn(a, b):
    return lax.dot_general(a, b, (((0,), (0,)), ((), ())), preferred_element_type=F32)


def _log_sigmoid(x):
    return jnp.minimum(x, 0.0) - jnp.log(1.0 + jnp.exp(-jnp.abs(x)))


def _sigmoid(x):
    return 1.0 / (1.0 + jnp.exp(-x))


def _split3(x):
    p0 = x.astype(BF16)
    r1 = x - p0.astype(F32)
    p1 = r1.astype(BF16)
    p2 = (r1 - p1.astype(F32)).astype(BF16)
    return p0, p1, p2


def _prenorm_kernel(x_ref, nw_ref, waf_ref, wf_ref, bf_ref, tri_ref, h_ref, af_ref, cx_ref, nc_ref,
                    wafb_ref, carry_ref):
    i = pl.program_id(0)
    tm = x_ref.shape[0]

    @pl.when(i == 0)
    def _():
        wafb_ref[...] = waf_ref[...].T.astype(BF16)
        carry_ref[...] = jnp.zeros_like(carry_ref)

    x = x_ref[...]
    ms = jnp.mean(x * x, axis=-1, keepdims=True)
    hb = (x * lax.rsqrt(ms + RMS_EPS) * nw_ref[...]).astype(BF16)
    h_ref[...] = hb
    af_ref[...] = jnp.dot(hb, wafb_ref[...], preferred_element_type=F32)
    wf = jnp.concatenate([wf_ref[...], jnp.zeros((128 - N_HEADS, D_MODEL), F32)], axis=0).astype(BF16)
    logit = _dot_nt(hb, wf) + bf_ref[...]
    nls = _log_sigmoid(logit) * (-LOG2E)
    tri = tri_ref[...]
    loc = sum(jnp.dot(tri, p, preferred_element_type=F32) for p in _split3(nls))
    nc = loc + carry_ref[...]
    nc_ref[...] = nc
    carry_ref[...] = nc[tm - 1:tm, :]
    c0, c1, c2 = _split3(nc)
    lane = lax.broadcasted_iota(jnp.int32, nc.shape, 1)
    parts = jnp.where(lane < N_HEADS, c0.astype(F32),
                      jnp.where(lane < 2 * N_HEADS, pltpu.roll(c1.astype(F32), N_HEADS, 1),
                                jnp.where(lane < 3 * N_HEADS, pltpu.roll(c2.astype(F32), 2 * N_HEADS, 1),
                                          0.0)))
    cx_ref[...] = parts.astype(BF16)


def _prenorm(x2, norm_w, w, bf_row):
    tm = PRE_TM
    tri = jnp.asarray(np.tril(np.ones((tm, tm), np.float32)), BF16)
    once = pl.Buffered(1)
    return pl.pallas_call(
        _prenorm_kernel,
        grid=(SEQ // tm,),
        in_specs=[
            pl.BlockSpec((tm, D_MODEL), lambda i: (i, 0)),
            pl.BlockSpec((1, D_MODEL), lambda i: (0, 0)),
            pl.BlockSpec((WIDTH, D_MODEL), lambda i: (SRC_AF, 0), pipeline_mode=once),
            pl.BlockSpec((N_HEADS, D_MODEL), lambda i: (IN_NA * WIDTH // N_HEADS, 0), pipeline_mode=once),
            pl.BlockSpec((1, 128), lambda i: (0, 0)),
            pl.BlockSpec((tm, tm), lambda i: (0, 0), pipeline_mode=once),
        ],
        out_specs=[
            pl.BlockSpec((tm, D_MODEL), lambda i: (i, 0)),
            pl.BlockSpec((tm, WIDTH), lambda i: (i, 0)),
            pl.BlockSpec((tm, 128), lambda i: (i, 0)),
            pl.BlockSpec((tm, 128), lambda i: (i, 0)),
        ],
        out_shape=[
            jax.ShapeDtypeStruct((SEQ, D_MODEL), BF16),
            jax.ShapeDtypeStruct((SEQ, WIDTH), F32),
            jax.ShapeDtypeStruct((SEQ, 128), BF16),
            jax.ShapeDtypeStruct((SEQ, 128), F32),
        ],
        scratch_shapes=[pltpu.VMEM((D_MODEL, WIDTH), BF16), pltpu.VMEM((1, 128), F32)],
        compiler_params=_params("arbitrary"),
        name="prenorm",
    )(x2, norm_w, w, w, bf_row, tri)


def _inproj_kernel(h_ref, wa_ref, wb_ref, proj_ref, wc_ref):
    j = pl.program_id(1)
    i = pl.program_id(2)
    tm, tn = proj_ref.shape
    nb = WIDTH // tn
    src = _inproj_src_block(j, nb)

    @pl.when(i == 0)
    def _():
        @pl.when(src < IN_NA * nb)
        def _():
            scale = jnp.where(src // nb == SRC_BQ, LOG2E * HEAD_DIM ** -0.5, 1.0)
            wc_ref[...] = (wa_ref[...] * scale).T.astype(BF16)

        @pl.when(src >= IN_NA * nb)
        def _():
            wc_ref[...] = jnp.concatenate([wa_ref[N_HEADS:, :], wb_ref[:N_HEADS, :]], axis=0).T.astype(BF16)

    rows = pl.ds(pl.multiple_of(i * tm, tm), tm)
    proj_ref[...] = jnp.dot(h_ref[rows, :], wc_ref[...], preferred_element_type=F32).astype(BF16)


def _inproj_src_block(j, nb):
    return jnp.where(j < SRC_AF * nb, j, j + nb)


def _inproj_out_block(j, nb):
    src = _inproj_src_block(j, nb)
    g = src // nb
    g_out = jnp.int32(BLK_AQ)
    for g_src, g_dst in ((SRC_AI, BLK_AI), (SRC_AG, BLK_AG), (SRC_BQ, BLK_BQ), (SRC_BK, BLK_BK), (SRC_BV, BLK_BV)):
        g_out = jnp.where(g == g_src, g_dst, g_out)
    return jnp.where(g < IN_NA, g_out * nb + src % nb, src - IN_NA * nb + BLK_GA * nb)


def _inproj(h, w):
    tm, tn = IN_TM, IN_TN
    nb = WIDTH // tn
    half = SEQ // 2
    n_i = half // tm
    first_gate = IN_NA * nb
    grid = (2, N_PROJ // tn, n_i)
    return pl.pallas_call(
        _inproj_kernel,
        grid=grid,
        in_specs=[
            pl.BlockSpec((half, D_MODEL), lambda s, j, i: (s, 0), pipeline_mode=pl.Buffered(1)),
            pl.BlockSpec((tn, D_MODEL), lambda s, j, i: (_inproj_src_block(j, nb), 0)),
            pl.BlockSpec((tn, D_MODEL), lambda s, j, i: (jnp.maximum(_inproj_src_block(j, nb), first_gate) + 1, 0)),
        ],
        out_specs=pl.BlockSpec((tm, tn), lambda s, j, i: (s * n_i + i, _inproj_out_block(j, nb))),
        out_shape=jax.ShapeDtypeStruct((SEQ, N_PROJ), BF16),
        scratch_shapes=[pltpu.VMEM((D_MODEL, tn), BF16)],
        compiler_params=_params("arbitrary", "arbitrary", "arbitrary"),
        name="inproj",
    )(h, w, w)


def _hgrn_constants(c):
    n_lvl = int(np.log2(c))
    t = np.arange(c)[:, None]
    j = np.arange(c)[None, :]
    blocks = [(j <= t), (j > t)]
    level = np.full((c, c), -1, np.int32)
    level[np.arange(c), np.arange(c)] = 0
    for l in range(n_lvl):
        b = 2 << l
        mid = (t // b) * b + b // 2 - 1
        second = (t % b) >= b // 2
        m = np.where(second, (j > mid) & (j <= t), (j > t) & (j <= mid))
        blocks.append(m)
        s = np.arange(c)[None, :]
        own = (t // b == s // b) & second & ((s % b) < b // 2)
        level[own] = l + 1
    sums = np.concatenate(blocks, axis=0).astype(np.float32)
    sums2 = np.concatenate([sums, sums], axis=1)
    return jnp.asarray(sums2, BF16), jnp.asarray(level), n_lvl


def _hgrn_kernel(q_ref, z_ref, v_ref, g_ref, lbl_ref, nw_ref, sums_ref, lvl_ref, y_ref, st_ref, *, c, n_lvl):
    n_heads = st_ref.shape[0]
    d = HEAD_DIM

    @pl.when(pl.program_id(1) == 0)
    def _():
        st_ref[...] = jnp.zeros_like(st_ref)

    logits = lbl_ref[...]
    ex = jnp.exp(logits - jnp.max(logits, axis=0, keepdims=True))
    lb = ex[0:1, :] / jnp.sum(ex, axis=0, keepdims=True)
    one_m_lb = 1.0 - lb
    nw = nw_ref[...]
    sums = sums_ref[...]
    lvl = lvl_ref[...]
    n_chunks = q_ref.shape[0] // c

    def chunk(ci, carry):
        r = pl.ds(pl.multiple_of(ci * c, c), c)
        z = z_ref[r, :]
        e = jnp.exp(-jnp.abs(z))
        inv = 1.0 / (1.0 + e)
        pos = z >= 0.0
        sig = jnp.where(pos, inv, e * inv)
        sig_n = jnp.where(pos, e * inv, inv)
        g = jnp.log2(lb + one_m_lb * sig)
        k_all = one_m_lb * sig_n
        g_hi = g.astype(BF16)
        g_lo = (g - g_hi.astype(F32)).astype(BF16)
        expo = jnp.dot(sums, jnp.concatenate([g_hi, g_lo], axis=0), preferred_element_type=F32)
        dec_all = jnp.exp2(expo)
        heads = [slice(hh * d, (hh + 1) * d) for hh in range(n_heads)]
        scores = []
        for cols in heads:
            q = q_ref[r, cols].astype(F32)
            k = k_all[:, cols]
            sc = jnp.where(lvl == 0, _dot_nt(q.astype(BF16), k.astype(BF16)), 0.0)
            for l in range(n_lvl):
                d_l = dec_all[(2 + l) * c:(3 + l) * c, cols]
                s_l = _dot_nt((q * d_l).astype(BF16), (k * d_l).astype(BF16))
                sc = jnp.where(lvl == l + 1, s_l, sc)
            scores.append(sc.astype(BF16))
        outs = []
        for hh, cols in enumerate(heads):
            q = q_ref[r, cols].astype(F32)
            v = v_ref[r, cols]
            q_in = (q * dec_all[0:c, cols]).astype(BF16)
            k_out = (k_all[:, cols] * dec_all[c:2 * c, cols]).astype(BF16)
            st = st_ref[hh]
            outs.append(_dot_nt(q_in, st.astype(BF16)) + jnp.dot(scores[hh], v, preferred_element_type=F32))
            st_ref[hh] = st * dec_all[c - 1:c, cols] + _dot_tn(v, k_out)
        for o, cols in zip(outs, heads):
            o = o * lax.rsqrt(jnp.mean(o * o, axis=-1, keepdims=True) + RMS_EPS) * nw
            gt = g_ref[r, cols].astype(F32)
            y_ref[r, cols] = (o * gt * _sigmoid(gt)).astype(y_ref.dtype)
        return carry

    lax.fori_loop(0, n_chunks, chunk, 0)


def _hgrn(proj, a_f, lb_logits, norm_w):
    ts, c, hg = HG_TS, HG_C, HG_HEADS
    sums, lvl, n_lvl = _hgrn_constants(c)
    w = hg * HEAD_DIM
    per_blk = WIDTH // w
    grid = (N_HEADS // hg, SEQ // ts)
    return pl.pallas_call(
        functools.partial(_hgrn_kernel, c=c, n_lvl=n_lvl),
        grid=grid,
        in_specs=[
            pl.BlockSpec((ts, w), lambda h, i: (i, BLK_AQ * per_blk + h)),
            pl.BlockSpec((ts, w), lambda h, i: (i, h)),
            pl.BlockSpec((ts, w), lambda h, i: (i, BLK_AI * per_blk + h)),
            pl.BlockSpec((ts, w), lambda h, i: (i, BLK_AG * per_blk + h)),
            pl.BlockSpec((lb_logits.shape[0], w), lambda h, i: (0, h)),
            pl.BlockSpec((1, HEAD_DIM), lambda h, i: (0, 0)),
            pl.BlockSpec(sums.shape, lambda h, i: (0, 0)),
            pl.BlockSpec(lvl.shape, lambda h, i: (0, 0)),
        ],
        out_specs=pl.BlockSpec((ts, w), lambda h, i: (i, h)),
        out_shape=jax.ShapeDtypeStruct((SEQ, WIDTH), BF16),
        scratch_shapes=[pltpu.VMEM((hg, HEAD_DIM, HEAD_DIM), F32)],
        compiler_params=_params("arbitrary", "arbitrary"),
        name="hgrn",
    )(proj, a_f, proj, proj, lb_logits, norm_w, sums, lvl)


def _fox_kernel(q_ref, k_ref, v_ref, cx_ref, nc_ref, o_ref, vt_ref, kn_ref, m_ref, acc_ref, *, tk):
    h = pl.program_id(0)
    i = pl.program_id(1)
    tq = q_ref.shape[0]
    s_len = k_ref.shape[0]
    n_sub = tq // tk
    n_kt = s_len // tk
    vrows = vt_ref.shape[1]

    @pl.when(i == 0)
    def _():
        ones_row = jnp.where(lax.broadcasted_iota(jnp.int32, (vrows - HEAD_DIM, tk), 0) == 0, 1.0, 0.0)
        kn = jnp.zeros((1, 1), F32)
        for r in range(n_kt):
            rows = slice(r * tk, (r + 1) * tk)
            vt_ref[r, 0:HEAD_DIM, :] = v_ref[rows, :].astype(F32).T.astype(BF16)
            vt_ref[r, HEAD_DIM:vrows, :] = ones_row.astype(BF16)
            kf = k_ref[rows, :].astype(F32)
            kn = jnp.maximum(kn, jnp.max(jnp.sum(kf * kf, axis=1, keepdims=True), axis=0, keepdims=True))
        kn_ref[...] = jnp.broadcast_to(kn, kn_ref.shape)

    qf = q_ref[...].astype(F32)
    qn = jnp.max(jnp.sum(qf * qf, axis=1, keepdims=True), axis=0, keepdims=True)
    rsel = lax.broadcasted_iota(jnp.int32, (HEAD_DIM, tq), 0)
    sel = jnp.where((rsel < 3 * N_HEADS) & ((rsel & (N_HEADS - 1)) == h), 1.0, 0.0).astype(BF16)
    qa = jnp.concatenate([qf.T.astype(BF16), sel], axis=0)

    bound = 2.0 * jnp.sqrt(qn * kn_ref[0:1, 0:1])
    thr = nc_ref[pl.ds(pl.multiple_of(i * tq, tq), 1), :] - bound - FOX_SKIP_LOG2
    ends = nc_ref[pl.ds(tk - 1, n_kt, stride=tk), :]
    lane = lax.broadcasted_iota(jnp.int32, ends.shape, 1)
    r_lo = jnp.sum(jnp.where((ends < thr) & (lane == h), 1, 0))

    m_ref[...] = jnp.full_like(m_ref, -jnp.inf)
    acc_ref[...] = jnp.zeros_like(acc_ref)

    def step(tiles):
        ss = []
        for r, mask_off in tiles:
            ks = pl.ds(pl.multiple_of(r * tk, tk), tk)
            ka = jnp.concatenate([k_ref[ks, :], cx_ref[ks, :]], axis=1)
            s = jnp.dot(ka, qa, preferred_element_type=F32)
            if mask_off is not None:
                kid = lax.broadcasted_iota(jnp.int32, s.shape, 0) + mask_off
                qid = lax.broadcasted_iota(jnp.int32, s.shape, 1)
                s = jnp.where(kid <= qid, s, -jnp.inf)
            ss.append(s)
        m_prev = m_ref[...]
        m_new = m_prev
        for s in ss:
            m_new = jnp.maximum(m_new, jnp.max(s, axis=0, keepdims=True))
        pv = None
        for (r, _), s in zip(tiles, ss):
            d = jnp.dot(vt_ref[r], jnp.exp2(s - m_new).astype(BF16), preferred_element_type=F32)
            pv = d if pv is None else pv + d
        acc_ref[...] = jnp.exp2(m_prev - m_new) * acc_ref[...] + pv
        m_ref[...] = m_new

    def off_diag(r, carry):
        step([(r, None)])
        return carry

    diag = [(i * n_sub + rr, rr * tk) for rr in range(n_sub)]

    @pl.when(i == 0)
    def _():
        step(diag)

    @pl.when(i > 0)
    def _():
        step([(i * n_sub - 1, None)] + diag)
        lax.fori_loop(r_lo, i * n_sub - 1, off_diag, 0)

    acc = acc_ref[...]
    o_ref[...] = (acc[0:HEAD_DIM] / acc[HEAD_DIM:HEAD_DIM + 1]).T.astype(o_ref.dtype)


def _fox(proj, cx, nc):
    tq, tk = FOX_TQ, FOX_TK
    hb = HEADS_PER_BLK
    vrows = HEAD_DIM + 16
    return pl.pallas_call(
        functools.partial(_fox_kernel, tk=tk),
        grid=(N_HEADS, SEQ // tq),
        in_specs=[
            pl.BlockSpec((tq, HEAD_DIM), lambda h, i: (i, BLK_BQ * hb + h)),
            pl.BlockSpec((SEQ, HEAD_DIM), lambda h, i: (0, BLK_BK * hb + h)),
            pl.BlockSpec((SEQ, HEAD_DIM), lambda h, i: (0, BLK_BV * hb + h)),
            pl.BlockSpec((SEQ, 128), lambda h, i: (0, 0)),
            pl.BlockSpec((SEQ, 128), lambda h, i: (0, 0)),
        ],
        out_specs=pl.BlockSpec((tq, HEAD_DIM), lambda h, i: (i, h)),
        out_shape=jax.ShapeDtypeStruct((SEQ, WIDTH), BF16),
        scratch_shapes=[
            pltpu.VMEM((SEQ // tk, vrows, tk), BF16),
            pltpu.VMEM((8, 128), F32),
            pltpu.VMEM((1, tq), F32),
            pltpu.VMEM((vrows, tq), F32),
        ],
        compiler_params=_params("arbitrary", "arbitrary"),
        name="fox",
    )(proj, proj, proj, cx, nc)


def _merge_kernel(oa_ref, ob_ref, ga_ref, gb_ref, wa_ref, wb_ref, wo_ref, x_ref, nw_ref, out_ref):
    ya = jnp.dot(oa_ref[...], wa_ref[...], preferred_element_type=F32)
    yb = jnp.dot(ob_ref[...], wb_ref[...], preferred_element_type=F32)
    merged = _sigmoid(ga_ref[...].astype(F32)) * ya + _sigmoid(gb_ref[...].astype(F32)) * yb
    u = jnp.dot(merged.astype(BF16), wo_ref[...], preferred_element_type=F32)
    u = u * lax.rsqrt(jnp.mean(u * u, axis=-1, keepdims=True) + RMS_EPS) * nw_ref[...]
    out_ref[...] = x_ref[...] + u


def _merge(y_a, y_b, proj, w_up_a, w_up_b, w_o, x2, norm_w):
    tm = MERGE_TM
    once = pl.Buffered(1)
    return pl.pallas_call(
        _merge_kernel,
        grid=(SEQ // tm,),
        in_specs=[
            pl.BlockSpec((tm, WIDTH), lambda i: (i, 0)),
            pl.BlockSpec((tm, WIDTH), lambda i: (i, 0)),
            pl.BlockSpec((tm, D_MODEL), lambda i: (i, BLK_GA // 2)),
            pl.BlockSpec((tm, D_MODEL), lambda i: (i, BLK_GB // 2)),
            pl.BlockSpec((WIDTH, D_MODEL), lambda i: (0, 0), pipeline_mode=once),
            pl.BlockSpec((WIDTH, D_MODEL), lambda i: (0, 0), pipeline_mode=once),
            pl.BlockSpec((D_MODEL, D_MODEL), lambda i: (0, 0), pipeline_mode=once),
            pl.BlockSpec((tm, D_MODEL), lambda i: (i, 0)),
            pl.BlockSpec((1, D_MODEL), lambda i: (0, 0)),
        ],
        out_specs=pl.BlockSpec((tm, D_MODEL), lambda i: (i, 0)),
        out_shape=jax.ShapeDtypeStruct((SEQ, D_MODEL), F32),
        compiler_params=_params("arbitrary"),
        name="merge",
    )(y_a, y_b, proj, proj, w_up_a, w_up_b, w_o, x2, norm_w)


def _ffn_kernel(x_ref, npre_ref, npost_ref, wg_ref, wu_ref, wd_ref, out_ref, h_ref):
    f = pl.program_id(1)

    @pl.when(f == 0)
    def _():
        x = x_ref[...]
        ms = jnp.mean(x * x, axis=-1, keepdims=True)
        h_ref[...] = (x * lax.rsqrt(ms + RMS_EPS) * npre_ref[...]).astype(BF16)
        out_ref[...] = jnp.zeros_like(out_ref)

    h = h_ref[...]
    gate = jnp.dot(h, wg_ref[...], preferred_element_type=F32)
    up = jnp.dot(h, wu_ref[...], preferred_element_type=F32)
    act = (gate * _sigmoid(gate) * up).astype(BF16)
    out_ref[...] += jnp.dot(act, wd_ref[...], preferred_element_type=F32)

    @pl.when(f == pl.num_programs(1) - 1)
    def _():
        u = out_ref[...]
        u = u * lax.rsqrt(jnp.mean(u * u, axis=-1, keepdims=True) + RMS_EPS) * npost_ref[...]
        out_ref[...] = x_ref[...] + u


def _ffn(x1, norm_pre, norm_post, w_in, w_down):
    tm, tf = FFN_TM, FFN_TF
    nf = D_FF // tf
    return pl.pallas_call(
        _ffn_kernel,
        grid=(SEQ // tm, nf),
        in_specs=[
            pl.BlockSpec((tm, D_MODEL), lambda i, f: (i, 0)),
            pl.BlockSpec((1, D_MODEL), lambda i, f: (0, 0)),
            pl.BlockSpec((1, D_MODEL), lambda i, f: (0, 0)),
            pl.BlockSpec((D_MODEL, tf), lambda i, f: (0, f)),
            pl.BlockSpec((D_MODEL, tf), lambda i, f: (0, f + nf)),
            pl.BlockSpec((tf, D_MODEL), lambda i, f: (f, 0)),
        ],
        out_specs=pl.BlockSpec((tm, D_MODEL), lambda i, f: (i, 0)),
        out_shape=jax.ShapeDtypeStruct((SEQ, D_MODEL), F32),
        scratch_shapes=[pltpu.VMEM((tm, D_MODEL), BF16)],
        compiler_params=_params("arbitrary", "arbitrary"),
        name="ffn",
    )(x1, norm_pre, norm_post, w_in, w_in, w_down)


def kernel(x, w_in, b_fox_f, hgrn_lb_logits, hgrn_norm_w, w_up_a, w_up_b, w_o, norm_mix_pre,
           norm_mix_post, norm_ffn_pre, norm_ffn_post, w_ffn_in, w_ffn_down):
    assert x.shape == (1, SEQ, D_MODEL) and w_in.shape[0] == 1
    w_t = w_in[0].T
    bf_row = jnp.pad(b_fox_f.reshape(1, N_HEADS), ((0, 0), (0, 128 - N_HEADS)))
    x2 = x[0]

    h, a_f, cx, nc = _prenorm(x2, norm_mix_pre, w_t, bf_row)
    proj = _inproj(h, w_t)
    y_a = _hgrn(proj, a_f, hgrn_lb_logits, hgrn_norm_w)
    y_b = _fox(proj, cx, nc)
    x1 = _merge(y_a, y_b, proj, w_up_a[0].astype(BF16), w_up_b[0].astype(BF16), w_o[0].astype(BF16),
                x2, norm_mix_post)
    out = _ffn(x1, norm_ffn_pre, norm_ffn_post, w_ffn_in[0].astype(BF16), w_ffn_down[0].astype(BF16))
    return out[None]
```

```python
import functools

import numpy as np
import jax
import jax.numpy as jnp
from jax import lax
from jax.experimental import pallas as pl
from jax.experimental.pallas import tpu as pltpu

F32 = jnp.float32
BF16 = jnp.bfloat16

D_MODEL = 2048
SEQ = 8192
HEAD_DIM = 128
N_HEADS = 8
WIDTH = N_HEADS * HEAD_DIM
D_FF = 5632
RMS_EPS = 1e-6
N_PROJ = 3 * WIDTH + 2 * D_MODEL + 3 * WIDTH

VMEM_LIMIT_BYTES = 56 * 1024 * 1024

SRC_AQ, SRC_AF, SRC_AI, SRC_AG, SRC_BQ, SRC_BK, SRC_BV = 0, 1, 2, 3, 4, 5, 6
IN_NA = 7
BLK_AQ, BLK_AI, BLK_GA, BLK_GB, BLK_AG, BLK_BQ, BLK_BK, BLK_BV = 0, 1, 2, 4, 6, 7, 8, 9
HEADS_PER_BLK = WIDTH // HEAD_DIM

PRE_TM = 512
IN_TM, IN_TN = 2048, 512
HG_TS, HG_C, HG_HEADS = 1024, 64, 8
FOX_TQ, FOX_TK, FOX_HEADS = 512, 256, 2
LOG2E = 1.4426950408889634
FOX_SKIP_LOG2 = 160.0
FOX_NORM_MARGIN = 1.01
MERGE_TM = 256
FFN_TM, FFN_TF = 1024, 512


def _params(*sem):
    return pltpu.CompilerParams(dimension_semantics=sem, vmem_limit_bytes=VMEM_LIMIT_BYTES)


def _dot_nt(a, b):
    return lax.dot_general(a, b, (((1,), (1,)), ((), ())), preferred_element_type=F32)


def _dot_tn(a, b):
    return lax.dot_general(a, b, (((0,), (0,)), ((), ())), preferred_element_type=F32)


def _log_sigmoid(x):
    return jnp.minimum(x, 0.0) - jnp.log(1.0 + jnp.exp(-jnp.abs(x)))


def _sigmoid(x):
    return 1.0 / (1.0 + jnp.exp(-x))


def _split3(x):
    p0 = x.astype(BF16)
    r1 = x - p0.astype(F32)
    p1 = r1.astype(BF16)
    p2 = (r1 - p1.astype(F32)).astype(BF16)
    return p0, p1, p2


def _prenorm_kernel(x_ref, nw_ref, waf_ref, wf_ref, bf_ref, tri_ref, h_ref, af_ref, cx_ref, nc_ref,
                    wafb_ref, carry_ref):
    i = pl.program_id(0)
    tm = x_ref.shape[0]

    @pl.when(i == 0)
    def _():
        wafb_ref[...] = waf_ref[...].T.astype(BF16)
        carry_ref[...] = jnp.zeros_like(carry_ref)

    x = x_ref[...]
    ms = jnp.mean(x * x, axis=-1, keepdims=True)
    hb = (x * lax.rsqrt(ms + RMS_EPS) * nw_ref[...]).astype(BF16)
    h_ref[...] = hb
    af_ref[...] = jnp.dot(hb, wafb_ref[...], preferred_element_type=F32)
    wf = jnp.concatenate([wf_ref[...], jnp.zeros((128 - N_HEADS, D_MODEL), F32)], axis=0).astype(BF16)
    logit = _dot_nt(hb, wf) + bf_ref[...]
    nls = _log_sigmoid(logit) * (-LOG2E)
    tri = tri_ref[...]
    loc = sum(jnp.dot(tri, p, preferred_element_type=F32) for p in _split3(nls))
    nc = loc + carry_ref[...]
    nc_ref[...] = nc
    carry_ref[...] = nc[tm - 1:tm, :]
    c0, c1, c2 = _split3(nc)
    lane = lax.broadcasted_iota(jnp.int32, nc.shape, 1)
    parts = jnp.where(lane < N_HEADS, c0.astype(F32),
                      jnp.where(lane < 2 * N_HEADS, pltpu.roll(c1.astype(F32), N_HEADS, 1),
                                jnp.where(lane < 3 * N_HEADS, pltpu.roll(c2.astype(F32), 2 * N_HEADS, 1),
                                          0.0)))
    cx_ref[...] = parts.astype(BF16)


def _prenorm(x2, norm_w, w, bf_row):
    tm = PRE_TM
    tri = jnp.asarray(np.tril(np.ones((tm, tm), np.float32)), BF16)
    once = pl.Buffered(1)
    return pl.pallas_call(
        _prenorm_kernel,
        grid=(SEQ // tm,),
        in_specs=[
            pl.BlockSpec((tm, D_MODEL), lambda i: (i, 0)),
            pl.BlockSpec((1, D_MODEL), lambda i: (0, 0)),
            pl.BlockSpec((WIDTH, D_MODEL), lambda i: (SRC_AF, 0), pipeline_mode=once),
            pl.BlockSpec((N_HEADS, D_MODEL), lambda i: (IN_NA * WIDTH // N_HEADS, 0), pipeline_mode=once),
            pl.BlockSpec((1, 128), lambda i: (0, 0)),
            pl.BlockSpec((tm, tm), lambda i: (0, 0), pipeline_mode=once),
        ],
        out_specs=[
            pl.BlockSpec((tm, D_MODEL), lambda i: (i, 0)),
            pl.BlockSpec((tm, WIDTH), lambda i: (i, 0)),
            pl.BlockSpec((tm, 128), lambda i: (i, 0)),
            pl.BlockSpec((tm, 128), lambda i: (i, 0)),
        ],
        out_shape=[
            jax.ShapeDtypeStruct((SEQ, D_MODEL), BF16),
            jax.ShapeDtypeStruct((SEQ, WIDTH), F32),
            jax.ShapeDtypeStruct((SEQ, 128), BF16),
            jax.ShapeDtypeStruct((SEQ, 128), F32),
        ],
        scratch_shapes=[pltpu.VMEM((D_MODEL, WIDTH), BF16), pltpu.VMEM((1, 128), F32)],
        compiler_params=_params("arbitrary"),
        name="prenorm",
    )(x2, norm_w, w, w, bf_row, tri)


def _inproj_kernel(h_ref, wa_ref, wb_ref, proj_ref, wc_ref):
    j = pl.program_id(1)
    i = pl.program_id(2)
    tm, tn = proj_ref.shape
    nb = WIDTH // tn
    src = _inproj_src_block(j, nb)

    @pl.when(i == 0)
    def _():
        @pl.when(src < IN_NA * nb)
        def _():
            scale = jnp.where(src // nb == SRC_BQ, LOG2E * HEAD_DIM ** -0.5, 1.0)
            wc_ref[...] = (wa_ref[...] * scale).T.astype(BF16)

        @pl.when(src >= IN_NA * nb)
        def _():
            wc_ref[...] = jnp.concatenate([wa_ref[N_HEADS:, :], wb_ref[:N_HEADS, :]], axis=0).T.astype(BF16)

    rows = pl.ds(pl.multiple_of(i * tm, tm), tm)
    proj_ref[...] = jnp.dot(h_ref[rows, :], wc_ref[...], preferred_element_type=F32).astype(BF16)


def _inproj_src_block(j, nb):
    return jnp.where(j < SRC_AF * nb, j, j + nb)


def _inproj_out_block(j, nb):
    src = _inproj_src_block(j, nb)
    g = src // nb
    g_out = jnp.int32(BLK_AQ)
    for g_src, g_dst in ((SRC_AI, BLK_AI), (SRC_AG, BLK_AG), (SRC_BQ, BLK_BQ), (SRC_BK, BLK_BK), (SRC_BV, BLK_BV)):
        g_out = jnp.where(g == g_src, g_dst, g_out)
    return jnp.where(g < IN_NA, g_out * nb + src % nb, src - IN_NA * nb + BLK_GA * nb)


def _inproj(h, w):
    tm, tn = IN_TM, IN_TN
    nb = WIDTH // tn
    half = SEQ // 2
    n_i = half // tm
    first_gate = IN_NA * nb
    grid = (2, N_PROJ // tn, n_i)
    return pl.pallas_call(
        _inproj_kernel,
        grid=grid,
        in_specs=[
            pl.BlockSpec((half, D_MODEL), lambda s, j, i: (s, 0), pipeline_mode=pl.Buffered(1)),
            pl.BlockSpec((tn, D_MODEL), lambda s, j, i: (_inproj_src_block(j, nb), 0)),
            pl.BlockSpec((tn, D_MODEL), lambda s, j, i: (jnp.maximum(_inproj_src_block(j, nb), first_gate) + 1, 0)),
        ],
        out_specs=pl.BlockSpec((tm, tn), lambda s, j, i: (s * n_i + i, _inproj_out_block(j, nb))),
        out_shape=jax.ShapeDtypeStruct((SEQ, N_PROJ), BF16),
        scratch_shapes=[pltpu.VMEM((D_MODEL, tn), BF16)],
        compiler_params=_params("arbitrary", "arbitrary", "arbitrary"),
        name="inproj",
    )(h, w, w)


def _hgrn_constants(c):
    n_lvl = int(np.log2(c))
    t = np.arange(c)[:, None]
    j = np.arange(c)[None, :]
    blocks = [(j <= t), (j > t)]
    level = np.full((c, c), -1, np.int32)
    level[np.arange(c), np.arange(c)] = 0
    for l in range(n_lvl):
        b = 2 << l
        mid = (t // b) * b + b // 2 - 1
        second = (t % b) >= b // 2
        m = np.where(second, (j > mid) & (j <= t), (j > t) & (j <= mid))
        blocks.append(m)
        s = np.arange(c)[None, :]
        own = (t // b == s // b) & second & ((s % b) < b // 2)
        level[own] = l + 1
    sums = np.concatenate(blocks, axis=0).astype(np.float32)
    sums2 = np.concatenate([sums, sums], axis=1)
    return jnp.asarray(sums2, BF16), jnp.asarray(level), n_lvl


def _hgrn_kernel(q_ref, z_ref, v_ref, g_ref, lbl_ref, nw_ref, sums_ref, lvl_ref, y_ref, st_ref, *, c, n_lvl):
    n_heads = st_ref.shape[0]
    d = HEAD_DIM

    @pl.when(pl.program_id(1) == 0)
    def _():
        st_ref[...] = jnp.zeros_like(st_ref)

    logits = lbl_ref[...]
    ex = jnp.exp(logits - jnp.max(logits, axis=0, keepdims=True))
    lb = ex[0:1, :] / jnp.sum(ex, axis=0, keepdims=True)
    one_m_lb = 1.0 - lb
    nw = nw_ref[...]
    sums = sums_ref[...]
    lvl = lvl_ref[...]
    n_chunks = q_ref.shape[0] // c

    def chunk(ci, carry):
        r = pl.ds(pl.multiple_of(ci * c, c), c)
        z = z_ref[r, :]
        e = jnp.exp(-jnp.abs(z))
        inv = 1.0 / (1.0 + e)
        pos = z >= 0.0
        sig = jnp.where(pos, inv, e * inv)
        sig_n = jnp.where(pos, e * inv, inv)
        g = jnp.log2(lb + one_m_lb * sig)
        k_all = one_m_lb * sig_n
        g_hi = g.astype(BF16)
        g_lo = (g - g_hi.astype(F32)).astype(BF16)
        expo = jnp.dot(sums, jnp.concatenate([g_hi, g_lo], axis=0), preferred_element_type=F32)
        dec_all = jnp.exp2(expo)
        heads = [slice(hh * d, (hh + 1) * d) for hh in range(n_heads)]
        scores = []
        for cols in heads:
            q = q_ref[r, cols].astype(F32)
            k = k_all[:, cols]
            sc = jnp.where(lvl == 0, _dot_nt(q.astype(BF16), k.astype(BF16)), 0.0)
            for l in range(n_lvl):
                d_l = dec_all[(2 + l) * c:(3 + l) * c, cols]
                s_l = _dot_nt((q * d_l).astype(BF16), (k * d_l).astype(BF16))
                sc = jnp.where(lvl == l + 1, s_l, sc)
            scores.append(sc.astype(BF16))
        outs = []
        for hh, cols in enumerate(heads):
            q = q_ref[r, cols].astype(F32)
            v = v_ref[r, cols]
            q_in = (q * dec_all[0:c, cols]).astype(BF16)
            k_out = (k_all[:, cols] * dec_all[c:2 * c, cols]).astype(BF16)
            st = st_ref[hh]
            outs.append(_dot_nt(q_in, st.astype(BF16)) + jnp.dot(scores[hh], v, preferred_element_type=F32))
            st_ref[hh] = st * dec_all[c - 1:c, cols] + _dot_tn(v, k_out)
        for o, cols in zip(outs, heads):
            o = o * lax.rsqrt(jnp.mean(o * o, axis=-1, keepdims=True) + RMS_EPS) * nw
            gt = g_ref[r, cols].astype(F32)
            y_ref[r, cols] = (o * gt * _sigmoid(gt)).astype(y_ref.dtype)
        return carry

    lax.fori_loop(0, n_chunks, chunk, 0)


def _hgrn(proj, a_f, lb_logits, norm_w):
    ts, c, hg = HG_TS, HG_C, HG_HEADS
    sums, lvl, n_lvl = _hgrn_constants(c)
    w = hg * HEAD_DIM
    per_blk = WIDTH // w
    grid = (N_HEADS // hg, SEQ // ts)
    return pl.pallas_call(
        functools.partial(_hgrn_kernel, c=c, n_lvl=n_lvl),
        grid=grid,
        in_specs=[
            pl.BlockSpec((ts, w), lambda h, i: (i, BLK_AQ * per_blk + h)),
            pl.BlockSpec((ts, w), lambda h, i: (i, h)),
            pl.BlockSpec((ts, w), lambda h, i: (i, BLK_AI * per_blk + h)),
            pl.BlockSpec((ts, w), lambda h, i: (i, BLK_AG * per_blk + h)),
            pl.BlockSpec((lb_logits.shape[0], w), lambda h, i: (0, h)),
            pl.BlockSpec((1, HEAD_DIM), lambda h, i: (0, 0)),
            pl.BlockSpec(sums.shape, lambda h, i: (0, 0)),
            pl.BlockSpec(lvl.shape, lambda h, i: (0, 0)),
        ],
        out_specs=pl.BlockSpec((ts, w), lambda h, i: (i, h)),
        out_shape=jax.ShapeDtypeStruct((SEQ, WIDTH), BF16),
        scratch_shapes=[pltpu.VMEM((hg, HEAD_DIM, HEAD_DIM), F32)],
        compiler_params=_params("arbitrary", "arbitrary"),
        name="hgrn",
    )(proj, a_f, proj, proj, lb_logits, norm_w, sums, lvl)


def _fox_kernel(q_ref, k_ref, v_ref, cx_ref, nc_ref, o_ref, vt_ref, nrm_ref, m_ref, acc_ref, *, tq, tk):
    g = pl.program_id(0)
    i = pl.program_id(1)
    n_heads = vt_ref.shape[0]
    d = HEAD_DIM
    s_len = k_ref.shape[0]
    n_sub = tq // tk
    n_kt = s_len // tk
    vrows = vt_ref.shape[2]
    heads = [(hh, slice(hh * d, (hh + 1) * d)) for hh in range(n_heads)]

    @pl.when(i == 0)
    def _():
        ones_row = jnp.where(lax.broadcasted_iota(jnp.int32, (vrows - d, tk), 0) == 0, 1.0, 0.0)
        wg = n_heads * d
        same_head = (lax.broadcasted_iota(jnp.int32, (wg, wg), 0) // d
                     == lax.broadcasted_iota(jnp.int32, (wg, wg), 1) // d)
        ones_blk = jnp.where(same_head, 1.0, 0.0).astype(BF16)

        def sq_norms(x):
            xf = x.astype(F32)
            return jnp.dot((xf * xf).astype(BF16), ones_blk, preferred_element_type=F32)

        kn = jnp.zeros((tk, wg), F32)
        qn = jnp.zeros((tk, wg), F32)
        for r in range(n_kt):
            rows = slice(r * tk, (r + 1) * tk)
            for hh, cols in heads:
                vt_ref[hh, r, 0:d, :] = v_ref[rows, cols].astype(F32).T.astype(BF16)
                vt_ref[hh, r, d:vrows, :] = ones_row.astype(BF16)
            kn = jnp.maximum(kn, sq_norms(k_ref[rows, :]))
            qn = jnp.maximum(qn, sq_norms(q_ref[rows, :]))
        nrm = 2.0 * FOX_NORM_MARGIN * jnp.sqrt(jnp.max(qn, axis=0, keepdims=True)
                                               * jnp.max(kn, axis=0, keepdims=True))
        for hh, cols in heads:
            nrm_ref[hh] = jnp.broadcast_to(nrm[:, cols], nrm_ref.shape[1:])

    q_rows = pl.ds(pl.multiple_of(i * tq, tq), tq)
    nc_q0 = nc_ref[pl.ds(pl.multiple_of(i * tq, tq), 1), :]
    ends = nc_ref[pl.ds(tk - 1, n_kt, stride=tk), :]
    lane = lax.broadcasted_iota(jnp.int32, ends.shape, 1)
    rsel = lax.broadcasted_iota(jnp.int32, (d, tq), 0)
    qa, r_lo = [], []
    for hh, cols in heads:
        h = g * n_heads + hh
        thr = nc_q0 - nrm_ref[hh, 0:1, :] - FOX_SKIP_LOG2
        r_lo.append(jnp.sum(jnp.where((ends < thr) & (lane == h), 1, 0)))
        sel = jnp.where((rsel < 3 * N_HEADS) & ((rsel & (N_HEADS - 1)) == h), 1.0, 0.0).astype(BF16)
        qa.append(jnp.concatenate([q_ref[q_rows, cols].astype(F32).T.astype(BF16), sel], axis=0))

    def scores(hh, cols, r, mask_off):
        ks = pl.ds(pl.multiple_of(r * tk, tk), tk)
        ka = jnp.concatenate([k_ref[ks, cols], cx_ref[ks, :]], axis=1)
        s = jnp.dot(ka, qa[hh], preferred_element_type=F32)
        if mask_off is not None:
            kid = lax.broadcasted_iota(jnp.int32, s.shape, 0) + mask_off
            qid = lax.broadcasted_iota(jnp.int32, s.shape, 1)
            s = jnp.where(kid <= qid, s, -jnp.inf)
        return s

    def first_step(tiles):
        ss = [[scores(hh, cols, r, off) for r, off in tiles] for hh, cols in heads]
        ms = []
        for hh, _ in heads:
            m = jnp.max(ss[hh][0], axis=0, keepdims=True)
            for s in ss[hh][1:]:
                m = jnp.maximum(m, jnp.max(s, axis=0, keepdims=True))
            ms.append(m)
        for hh, _ in heads:
            pv = None
            for (r, _), s in zip(tiles, ss[hh]):
                p = jnp.exp2(s - ms[hh]).astype(BF16)
                dd = jnp.dot(vt_ref[hh, r], p, preferred_element_type=F32)
                pv = dd if pv is None else pv + dd
            acc_ref[hh] = pv
            m_ref[hh] = ms[hh]

    def later_step(hh, cols, r):
        s = scores(hh, cols, r, None)
        m_prev = m_ref[hh]
        m_new = jnp.maximum(m_prev, jnp.max(s, axis=0, keepdims=True))
        pv = jnp.dot(vt_ref[hh, r], jnp.exp2(s - m_new).astype(BF16), preferred_element_type=F32)
        acc_ref[hh] = jnp.exp2(m_prev - m_new) * acc_ref[hh] + pv
        m_ref[hh] = m_new

    diag = [(i * n_sub + rr, rr * tk) for rr in range(n_sub)]

    @pl.when(i == 0)
    def _():
        first_step(diag)

    @pl.when(i > 0)
    def _():
        first_step([(i * n_sub - 1, None)] + diag)
        for hh, cols in heads:
            def off_diag(r, carry, hh=hh, cols=cols):
                later_step(hh, cols, r)
                return carry
            lax.fori_loop(r_lo[hh], i * n_sub - 1, off_diag, 0)

    for hh, cols in heads:
        acc = acc_ref[hh]
        o_ref[:, cols] = (acc[0:d] / acc[d:d + 1]).T.astype(o_ref.dtype)


def _fox(proj, cx, nc):
    tq, tk, hg = FOX_TQ, FOX_TK, FOX_HEADS
    w = hg * HEAD_DIM
    per_blk = WIDTH // w
    vrows = HEAD_DIM + 16
    return pl.pallas_call(
        functools.partial(_fox_kernel, tq=tq, tk=tk),
        grid=(N_HEADS // hg, SEQ // tq),
        in_specs=[
            pl.BlockSpec((SEQ, w), lambda g, i: (0, BLK_BQ * per_blk + g)),
            pl.BlockSpec((SEQ, w), lambda g, i: (0, BLK_BK * per_blk + g)),
            pl.BlockSpec((SEQ, w), lambda g, i: (0, BLK_BV * per_blk + g)),
            pl.BlockSpec((SEQ, 128), lambda g, i: (0, 0)),
            pl.BlockSpec((SEQ, 128), lambda g, i: (0, 0)),
        ],
        out_specs=pl.BlockSpec((tq, w), lambda g, i: (i, g)),
        out_shape=jax.ShapeDtypeStruct((SEQ, WIDTH), BF16),
        scratch_shapes=[
            pltpu.VMEM((hg, SEQ // tk, vrows, tk), BF16),
            pltpu.VMEM((hg, 8, 128), F32),
            pltpu.VMEM((hg, 1, tq), F32),
            pltpu.VMEM((hg, vrows, tq), F32),
        ],
        compiler_params=_params("arbitrary", "arbitrary"),
        name="fox",
    )(proj, proj, proj, cx, nc)


def _merge_kernel(oa_ref, ob_ref, ga_ref, gb_ref, wa_ref, wb_ref, wo_ref, x_ref, nw_ref, out_ref):
    ya = jnp.dot(oa_ref[...], wa_ref[...], preferred_element_type=F32)
    yb = jnp.dot(ob_ref[...], wb_ref[...], preferred_element_type=F32)
    merged = _sigmoid(ga_ref[...].astype(F32)) * ya + _sigmoid(gb_ref[...].astype(F32)) * yb
    u = jnp.dot(merged.astype(BF16), wo_ref[...], preferred_element_type=F32)
    u = u * lax.rsqrt(jnp.mean(u * u, axis=-1, keepdims=True) + RMS_EPS) * nw_ref[...]
    out_ref[...] = x_ref[...] + u


def _merge(y_a, y_b, proj, w_up_a, w_up_b, w_o, x2, norm_w):
    tm = MERGE_TM
    once = pl.Buffered(1)
    return pl.pallas_call(
        _merge_kernel,
        grid=(SEQ // tm,),
        in_specs=[
            pl.BlockSpec((tm, WIDTH), lambda i: (i, 0)),
            pl.BlockSpec((tm, WIDTH), lambda i: (i, 0)),
            pl.BlockSpec((tm, D_MODEL), lambda i: (i, BLK_GA // 2)),
            pl.BlockSpec((tm, D_MODEL), lambda i: (i, BLK_GB // 2)),
            pl.BlockSpec((WIDTH, D_MODEL), lambda i: (0, 0), pipeline_mode=once),
            pl.BlockSpec((WIDTH, D_MODEL), lambda i: (0, 0), pipeline_mode=once),
            pl.BlockSpec((D_MODEL, D_MODEL), lambda i: (0, 0), pipeline_mode=once),
            pl.BlockSpec((tm, D_MODEL), lambda i: (i, 0)),
            pl.BlockSpec((1, D_MODEL), lambda i: (0, 0)),
        ],
        out_specs=pl.BlockSpec((tm, D_MODEL), lambda i: (i, 0)),
        out_shape=jax.ShapeDtypeStruct((SEQ, D_MODEL), F32),
        compiler_params=_params("arbitrary"),
        name="merge",
    )(y_a, y_b, proj, proj, w_up_a, w_up_b, w_o, x2, norm_w)


def _ffn_kernel(x_ref, npre_ref, npost_ref, wg_ref, wu_ref, wd_ref, out_ref, h_ref):
    f = pl.program_id(1)

    @pl.when(f == 0)
    def _():
        x = x_ref[...]
        ms = jnp.mean(x * x, axis=-1, keepdims=True)
        h_ref[...] = (x * lax.rsqrt(ms + RMS_EPS) * npre_ref[...]).astype(BF16)
        out_ref[...] = jnp.zeros_like(out_ref)

    h = h_ref[...]
    gate = jnp.dot(h, wg_ref[...], preferred_element_type=F32)
    up = jnp.dot(h, wu_ref[...], preferred_element_type=F32)
    act = (gate * _sigmoid(gate) * up).astype(BF16)
    out_ref[...] += jnp.dot(act, wd_ref[...], preferred_element_type=F32)

    @pl.when(f == pl.num_programs(1) - 1)
    def _():
        u = out_ref[...]
        u = u * lax.rsqrt(jnp.mean(u * u, axis=-1, keepdims=True) + RMS_EPS) * npost_ref[...]
        out_ref[...] = x_ref[...] + u


def _ffn(x1, norm_pre, norm_post, w_in, w_down):
    tm, tf = FFN_TM, FFN_TF
    nf = D_FF // tf
    return pl.pallas_call(
        _ffn_kernel,
        grid=(SEQ // tm, nf),
        in_specs=[
            pl.BlockSpec((tm, D_MODEL), lambda i, f: (i, 0), pipeline_mode=pl.Buffered(1)),
            pl.BlockSpec((1, D_MODEL), lambda i, f: (0, 0)),
            pl.BlockSpec((1, D_MODEL), lambda i, f: (0, 0)),
            pl.BlockSpec((D_MODEL, tf), lambda i, f: (0, f)),
            pl.BlockSpec((D_MODEL, tf), lambda i, f: (0, f + nf)),
            pl.BlockSpec((tf, D_MODEL), lambda i, f: (f, 0)),
        ],
        out_specs=pl.BlockSpec((tm, D_MODEL), lambda i, f: (i, 0)),
        out_shape=jax.ShapeDtypeStruct((SEQ, D_MODEL), F32),
        scratch_shapes=[pltpu.VMEM((tm, D_MODEL), BF16)],
        compiler_params=_params("arbitrary", "arbitrary"),
        name="ffn",
    )(x1, norm_pre, norm_post, w_in, w_in, w_down)


def kernel(x, w_in, b_fox_f, hgrn_lb_logits, hgrn_norm_w, w_up_a, w_up_b, w_o, norm_mix_pre,
           norm_mix_post, norm_ffn_pre, norm_ffn_post, w_ffn_in, w_ffn_down):
    assert x.shape == (1, SEQ, D_MODEL) and w_in.shape[0] == 1
    w_t = w_in[0].T
    bf_row = jnp.pad(b_fox_f.reshape(1, N_HEADS), ((0, 0), (0, 128 - N_HEADS)))
    x2 = x[0]

    h, a_f, cx, nc = _prenorm(x2, norm_mix_pre, w_t, bf_row)
    proj = _inproj(h, w_t)
    y_a = _hgrn(proj, a_f, hgrn_lb_logits, hgrn_norm_w)
    y_b = _fox(proj, cx, nc)
    x1 = _merge(y_a, y_b, proj, w_up_a[0].astype(BF16), w_up_b[0].astype(BF16), w_o[0].astype(BF16),
                x2, norm_mix_post)
    out = _ffn(x1, norm_ffn_pre, norm_ffn_post, w_ffn_in[0].astype(BF16), w_ffn_down[0].astype(BF16))
    return out[None]
```

```python
import functools

import numpy as np
import jax
import jax.numpy as jnp
from jax import lax
from jax.experimental import pallas as pl
from jax.experimental.pallas import tpu as pltpu

F32 = jnp.float32
BF16 = jnp.bfloat16

D_MODEL = 2048
SEQ = 8192
HEAD_DIM = 128
N_HEADS = 8
WIDTH = N_HEADS * HEAD_DIM
D_FF = 5632
RMS_EPS = 1e-6
N_PROJ = 3 * WIDTH + 2 * D_MODEL + 3 * WIDTH

VMEM_LIMIT_BYTES = 56 * 1024 * 1024

SRC_AQ, SRC_AF, SRC_AI, SRC_AG, SRC_BQ, SRC_BK, SRC_BV = 0, 1, 2, 3, 4, 5, 6
IN_NA = 7
BLK_AQ, BLK_AI, BLK_GA, BLK_GB, BLK_AG, BLK_BQ, BLK_BK, BLK_BV = 0, 1, 2, 4, 6, 7, 8, 9
HEADS_PER_BLK = WIDTH // HEAD_DIM

PRE_TM = 512
IN_TM, IN_TN = 2048, 512
HG_TS, HG_C, HG_HEADS = 1024, 64, 8
FOX_TQ, FOX_TK, FOX_HEADS = 512, 256, 2
LOG2E = 1.4426950408889634
FOX_SKIP_LOG2 = 160.0
FOX_NORM_MARGIN = 1.01
MERGE_TM = 256
FFN_TM, FFN_TF = 1024, 256


def _params(*sem):
    return pltpu.CompilerParams(dimension_semantics=sem, vmem_limit_bytes=VMEM_LIMIT_BYTES)


def _dot_nt(a, b):
    return lax.dot_general(a, b, (((1,), (1,)), ((), ())), preferred_element_type=F32)


def _dot_tn(a, b):
    return lax.dot_general(a, b, (((0,), (0,)), ((), ())), preferred_element_type=F32)


def _log_sigmoid(x):
    return jnp.minimum(x, 0.0) - jnp.log(1.0 + jnp.exp(-jnp.abs(x)))


def _sigmoid(x):
    return 1.0 / (1.0 + jnp.exp(-x))


def _split3(x):
    p0 = x.astype(BF16)
    r1 = x - p0.astype(F32)
    p1 = r1.astype(BF16)
    p2 = (r1 - p1.astype(F32)).astype(BF16)
    return p0, p1, p2


def _prenorm_kernel(x_ref, nw_ref, waf_ref, wf_ref, bf_ref, tri_ref, h_ref, af_ref, cx_ref, nc_ref,
                    wafb_ref, carry_ref):
    i = pl.program_id(0)
    tm = x_ref.shape[0]

    @pl.when(i == 0)
    def _():
        wafb_ref[...] = waf_ref[...].T.astype(BF16)
        carry_ref[...] = jnp.zeros_like(carry_ref)

    x = x_ref[...]
    ms = jnp.mean(x * x, axis=-1, keepdims=True)
    hb = (x * lax.rsqrt(ms + RMS_EPS) * nw_ref[...]).astype(BF16)
    h_ref[...] = hb
    af_ref[...] = jnp.dot(hb, wafb_ref[...], preferred_element_type=F32)
    wf = jnp.concatenate([wf_ref[...], jnp.zeros((128 - N_HEADS, D_MODEL), F32)], axis=0).astype(BF16)
    logit = _dot_nt(hb, wf) + bf_ref[...]
    nls = _log_sigmoid(logit) * (-LOG2E)
    tri = tri_ref[...]
    loc = sum(jnp.dot(tri, p, preferred_element_type=F32) for p in _split3(nls))
    nc = loc + carry_ref[...]
    nc_ref[...] = nc
    carry_ref[...] = nc[tm - 1:tm, :]
    c0, c1, c2 = _split3(nc)
    lane = lax.broadcasted_iota(jnp.int32, nc.shape, 1)
    parts = jnp.where(lane < N_HEADS, c0.astype(F32),
                      jnp.where(lane < 2 * N_HEADS, pltpu.roll(c1.astype(F32), N_HEADS, 1),
                                jnp.where(lane < 3 * N_HEADS, pltpu.roll(c2.astype(F32), 2 * N_HEADS, 1),
                                          0.0)))
    cx_ref[...] = parts.astype(BF16)


def _prenorm(x2, norm_w, w, bf_row):
    tm = PRE_TM
    tri = jnp.asarray(np.tril(np.ones((tm, tm), np.float32)), BF16)
    once = pl.Buffered(1)
    return pl.pallas_call(
        _prenorm_kernel,
        grid=(SEQ // tm,),
        in_specs=[
            pl.BlockSpec((tm, D_MODEL), lambda i: (i, 0)),
            pl.BlockSpec((1, D_MODEL), lambda i: (0, 0)),
            pl.BlockSpec((WIDTH, D_MODEL), lambda i: (SRC_AF, 0), pipeline_mode=once),
            pl.BlockSpec((N_HEADS, D_MODEL), lambda i: (IN_NA * WIDTH // N_HEADS, 0), pipeline_mode=once),
            pl.BlockSpec((1, 128), lambda i: (0, 0)),
            pl.BlockSpec((tm, tm), lambda i: (0, 0), pipeline_mode=once),
        ],
        out_specs=[
            pl.BlockSpec((tm, D_MODEL), lambda i: (i, 0)),
            pl.BlockSpec((tm, WIDTH), lambda i: (i, 0)),
            pl.BlockSpec((tm, 128), lambda i: (i, 0)),
            pl.BlockSpec((tm, 128), lambda i: (i, 0)),
        ],
        out_shape=[
            jax.ShapeDtypeStruct((SEQ, D_MODEL), BF16),
            jax.ShapeDtypeStruct((SEQ, WIDTH), F32),
            jax.ShapeDtypeStruct((SEQ, 128), BF16),
            jax.ShapeDtypeStruct((SEQ, 128), F32),
        ],
        scratch_shapes=[pltpu.VMEM((D_MODEL, WIDTH), BF16), pltpu.VMEM((1, 128), F32)],
        compiler_params=_params("arbitrary"),
        name="prenorm",
    )(x2, norm_w, w, w, bf_row, tri)


def _inproj_kernel(h_ref, wa_ref, wb_ref, proj_ref, wc_ref):
    j = pl.program_id(1)
    i = pl.program_id(2)
    tm, tn = proj_ref.shape
    nb = WIDTH // tn
    src = _inproj_src_block(j, nb)

    @pl.when(i == 0)
    def _():
        @pl.when(src < IN_NA * nb)
        def _():
            scale = jnp.where(src // nb == SRC_BQ, LOG2E * HEAD_DIM ** -0.5, 1.0)
            wc_ref[...] = (wa_ref[...] * scale).T.astype(BF16)

        @pl.when(src >= IN_NA * nb)
        def _():
            wc_ref[...] = jnp.concatenate([wa_ref[N_HEADS:, :], wb_ref[:N_HEADS, :]], axis=0).T.astype(BF16)

    rows = pl.ds(pl.multiple_of(i * tm, tm), tm)
    proj_ref[...] = jnp.dot(h_ref[rows, :], wc_ref[...], preferred_element_type=F32).astype(BF16)


def _inproj_src_block(j, nb):
    return jnp.where(j < SRC_AF * nb, j, j + nb)


def _inproj_out_block(j, nb):
    src = _inproj_src_block(j, nb)
    g = src // nb
    g_out = jnp.int32(BLK_AQ)
    for g_src, g_dst in ((SRC_AI, BLK_AI), (SRC_AG, BLK_AG), (SRC_BQ, BLK_BQ), (SRC_BK, BLK_BK), (SRC_BV, BLK_BV)):
        g_out = jnp.where(g == g_src, g_dst, g_out)
    return jnp.where(g < IN_NA, g_out * nb + src % nb, src - IN_NA * nb + BLK_GA * nb)


def _inproj(h, w):
    tm, tn = IN_TM, IN_TN
    nb = WIDTH // tn
    half = SEQ // 2
    n_i = half // tm
    first_gate = IN_NA * nb
    grid = (2, N_PROJ // tn, n_i)
    return pl.pallas_call(
        _inproj_kernel,
        grid=grid,
        in_specs=[
            pl.BlockSpec((half, D_MODEL), lambda s, j, i: (s, 0), pipeline_mode=pl.Buffered(1)),
            pl.BlockSpec((tn, D_MODEL), lambda s, j, i: (_inproj_src_block(j, nb), 0)),
            pl.BlockSpec((tn, D_MODEL), lambda s, j, i: (jnp.maximum(_inproj_src_block(j, nb), first_gate) + 1, 0)),
        ],
        out_specs=pl.BlockSpec((tm, tn), lambda s, j, i: (s * n_i + i, _inproj_out_block(j, nb))),
        out_shape=jax.ShapeDtypeStruct((SEQ, N_PROJ), BF16),
        scratch_shapes=[pltpu.VMEM((D_MODEL, tn), BF16)],
        compiler_params=_params("arbitrary", "arbitrary", "arbitrary"),
        name="inproj",
    )(h, w, w)


def _hgrn_constants(c):
    n_lvl = int(np.log2(c))
    t = np.arange(c)[:, None]
    j = np.arange(c)[None, :]
    blocks = [(j <= t), (j > t)]
    level = np.full((c, c), -1, np.int32)
    level[np.arange(c), np.arange(c)] = 0
    for l in range(n_lvl):
        b = 2 << l
        mid = (t // b) * b + b // 2 - 1
        second = (t % b) >= b // 2
        m = np.where(second, (j > mid) & (j <= t), (j > t) & (j <= mid))
        blocks.append(m)
        s = np.arange(c)[None, :]
        own = (t // b == s // b) & second & ((s % b) < b // 2)
        level[own] = l + 1
    sums = np.concatenate(blocks, axis=0).astype(np.float32)
    sums2 = np.concatenate([sums, sums], axis=1)
    return jnp.asarray(sums2, BF16), jnp.asarray(level), n_lvl


def _hgrn_kernel(q_ref, z_ref, v_ref, g_ref, lbl_ref, nw_ref, sums_ref, lvl_ref, y_ref, st_ref, *, c, n_lvl):
    n_heads = st_ref.shape[0]
    d = HEAD_DIM

    @pl.when(pl.program_id(1) == 0)
    def _():
        st_ref[...] = jnp.zeros_like(st_ref)

    logits = lbl_ref[...]
    ex = jnp.exp(logits - jnp.max(logits, axis=0, keepdims=True))
    lb = ex[0:1, :] / jnp.sum(ex, axis=0, keepdims=True)
    one_m_lb = 1.0 - lb
    nw = nw_ref[...]
    sums = sums_ref[...]
    lvl = lvl_ref[...]
    n_chunks = q_ref.shape[0] // c

    def chunk(ci, carry):
        r = pl.ds(pl.multiple_of(ci * c, c), c)
        z = z_ref[r, :]
        e = jnp.exp(-jnp.abs(z))
        inv = 1.0 / (1.0 + e)
        pos = z >= 0.0
        sig = jnp.where(pos, inv, e * inv)
        sig_n = jnp.where(pos, e * inv, inv)
        g = jnp.log2(lb + one_m_lb * sig)
        k_all = one_m_lb * sig_n
        g_hi = g.astype(BF16)
        g_lo = (g - g_hi.astype(F32)).astype(BF16)
        expo = jnp.dot(sums, jnp.concatenate([g_hi, g_lo], axis=0), preferred_element_type=F32)
        dec_all = jnp.exp2(expo)
        heads = [slice(hh * d, (hh + 1) * d) for hh in range(n_heads)]
        scores = []
        for cols in heads:
            q = q_ref[r, cols].astype(F32)
            k = k_all[:, cols]
            sc = jnp.where(lvl == 0, _dot_nt(q.astype(BF16), k.astype(BF16)), 0.0)
            for l in range(n_lvl):
                d_l = dec_all[(2 + l) * c:(3 + l) * c, cols]
                s_l = _dot_nt((q * d_l).astype(BF16), (k * d_l).astype(BF16))
                sc = jnp.where(lvl == l + 1, s_l, sc)
            scores.append(sc.astype(BF16))
        outs = []
        for hh, cols in enumerate(heads):
            q = q_ref[r, cols].astype(F32)
            v = v_ref[r, cols]
            q_in = (q * dec_all[0:c, cols]).astype(BF16)
            k_out = (k_all[:, cols] * dec_all[c:2 * c, cols]).astype(BF16)
            st = st_ref[hh]
            outs.append(_dot_nt(q_in, st.astype(BF16)) + jnp.dot(scores[hh], v, preferred_element_type=F32))
            st_ref[hh] = st * dec_all[c - 1:c, cols] + _dot_tn(v, k_out)
        for o, cols in zip(outs, heads):
            o = o * lax.rsqrt(jnp.mean(o * o, axis=-1, keepdims=True) + RMS_EPS) * nw
            gt = g_ref[r, cols].astype(F32)
            y_ref[r, cols] = (o * gt * _sigmoid(gt)).astype(y_ref.dtype)
        return carry

    lax.fori_loop(0, n_chunks, chunk, 0)


def _hgrn(proj, a_f, lb_logits, norm_w):
    ts, c, hg = HG_TS, HG_C, HG_HEADS
    sums, lvl, n_lvl = _hgrn_constants(c)
    w = hg * HEAD_DIM
    per_blk = WIDTH // w
    grid = (N_HEADS // hg, SEQ // ts)
    return pl.pallas_call(
        functools.partial(_hgrn_kernel, c=c, n_lvl=n_lvl),
        grid=grid,
        in_specs=[
            pl.BlockSpec((ts, w), lambda h, i: (i, BLK_AQ * per_blk + h)),
            pl.BlockSpec((ts, w), lambda h, i: (i, h)),
            pl.BlockSpec((ts, w), lambda h, i: (i, BLK_AI * per_blk + h)),
            pl.BlockSpec((ts, w), lambda h, i: (i, BLK_AG * per_blk + h)),
            pl.BlockSpec((lb_logits.shape[0], w), lambda h, i: (0, h)),
            pl.BlockSpec((1, HEAD_DIM), lambda h, i: (0, 0)),
            pl.BlockSpec(sums.shape, lambda h, i: (0, 0)),
            pl.BlockSpec(lvl.shape, lambda h, i: (0, 0)),
        ],
        out_specs=pl.BlockSpec((ts, w), lambda h, i: (i, h)),
        out_shape=jax.ShapeDtypeStruct((SEQ, WIDTH), BF16),
        scratch_shapes=[pltpu.VMEM((hg, HEAD_DIM, HEAD_DIM), F32)],
        compiler_params=_params("arbitrary", "arbitrary"),
        name="hgrn",
    )(proj, a_f, proj, proj, lb_logits, norm_w, sums, lvl)


def _fox_kernel(q_ref, k_ref, v_ref, cx_ref, nc_ref, o_ref, vt_ref, nrm_ref, m_ref, acc_ref, *, tq, tk):
    g = pl.program_id(0)
    i = pl.program_id(1)
    n_heads = vt_ref.shape[0]
    d = HEAD_DIM
    s_len = k_ref.shape[0]
    n_sub = tq // tk
    n_kt = s_len // tk
    vrows = vt_ref.shape[2]
    heads = [(hh, slice(hh * d, (hh + 1) * d)) for hh in range(n_heads)]

    @pl.when(i == 0)
    def _():
        ones_row = jnp.where(lax.broadcasted_iota(jnp.int32, (vrows - d, tk), 0) == 0, 1.0, 0.0)
        wg = n_heads * d
        same_head = (lax.broadcasted_iota(jnp.int32, (wg, wg), 0) // d
                     == lax.broadcasted_iota(jnp.int32, (wg, wg), 1) // d)
        ones_blk = jnp.where(same_head, 1.0, 0.0).astype(BF16)

        def sq_norms(x):
            xf = x.astype(F32)
            return jnp.dot((xf * xf).astype(BF16), ones_blk, preferred_element_type=F32)

        kn = jnp.zeros((tk, wg), F32)
        qn = jnp.zeros((tk, wg), F32)
        for r in range(n_kt):
            rows = slice(r * tk, (r + 1) * tk)
            for hh, cols in heads:
                vt_ref[hh, r, 0:d, :] = v_ref[rows, cols].astype(F32).T.astype(BF16)
                vt_ref[hh, r, d:vrows, :] = ones_row.astype(BF16)
            kn = jnp.maximum(kn, sq_norms(k_ref[rows, :]))
            qn = jnp.maximum(qn, sq_norms(q_ref[rows, :]))
        nrm = 2.0 * FOX_NORM_MARGIN * jnp.sqrt(jnp.max(qn, axis=0, keepdims=True)
                                               * jnp.max(kn, axis=0, keepdims=True))
        for hh, cols in heads:
            nrm_ref[hh] = jnp.broadcast_to(nrm[:, cols], nrm_ref.shape[1:])

    q_rows = pl.ds(pl.multiple_of(i * tq, tq), tq)
    nc_q0 = nc_ref[pl.ds(pl.multiple_of(i * tq, tq), 1), :]
    ends = nc_ref[pl.ds(tk - 1, n_kt, stride=tk), :]
    lane = lax.broadcasted_iota(jnp.int32, ends.shape, 1)
    rsel = lax.broadcasted_iota(jnp.int32, (d, tq), 0)
    qa, r_lo = [], []
    for hh, cols in heads:
        h = g * n_heads + hh
        thr = nc_q0 - nrm_ref[hh, 0:1, :] - FOX_SKIP_LOG2
        r_lo.append(jnp.sum(jnp.where((ends < thr) & (lane == h), 1, 0)))
        sel = jnp.where((rsel < 3 * N_HEADS) & ((rsel & (N_HEADS - 1)) == h), 1.0, 0.0).astype(BF16)
        qa.append(jnp.concatenate([q_ref[q_rows, cols].astype(F32).T.astype(BF16), sel], axis=0))

    def scores(hh, cols, r, mask_off):
        ks = pl.ds(pl.multiple_of(r * tk, tk), tk)
        ka = jnp.concatenate([k_ref[ks, cols], cx_ref[ks, :]], axis=1)
        s = jnp.dot(ka, qa[hh], preferred_element_type=F32)
        if mask_off is not None:
            kid = lax.broadcasted_iota(jnp.int32, s.shape, 0) + mask_off
            qid = lax.broadcasted_iota(jnp.int32, s.shape, 1)
            s = jnp.where(kid <= qid, s, -jnp.inf)
        return s

    def first_step(tiles):
        ss = [[scores(hh, cols, r, off) for r, off in tiles] for hh, cols in heads]
        ms = []
        for hh, _ in heads:
            m = jnp.max(ss[hh][0], axis=0, keepdims=True)
            for s in ss[hh][1:]:
                m = jnp.maximum(m, jnp.max(s, axis=0, keepdims=True))
            ms.append(m)
        for hh, _ in heads:
            pv = None
            for (r, _), s in zip(tiles, ss[hh]):
                p = jnp.exp2(s - ms[hh]).astype(BF16)
                dd = jnp.dot(vt_ref[hh, r], p, preferred_element_type=F32)
                pv = dd if pv is None else pv + dd
            acc_ref[hh] = pv
            m_ref[hh] = ms[hh]

    def later_step(hh, cols, r):
        s = scores(hh, cols, r, None)
        m_prev = m_ref[hh]
        m_new = jnp.maximum(m_prev, jnp.max(s, axis=0, keepdims=True))
        pv = jnp.dot(vt_ref[hh, r], jnp.exp2(s - m_new).astype(BF16), preferred_element_type=F32)
        acc_ref[hh] = jnp.exp2(m_prev - m_new) * acc_ref[hh] + pv
        m_ref[hh] = m_new

    diag = [(i * n_sub + rr, rr * tk) for rr in range(n_sub)]

    @pl.when(i == 0)
    def _():
        first_step(diag)

    @pl.when(i > 0)
    def _():
        first_step([(i * n_sub - 1, None)] + diag)
        for hh, cols in heads:
            def off_diag(r, carry, hh=hh, cols=cols):
                later_step(hh, cols, r)
                return carry
            lax.fori_loop(r_lo[hh], i * n_sub - 1, off_diag, 0)

    for hh, cols in heads:
        acc = acc_ref[hh]
        o_ref[:, cols] = (acc[0:d] / acc[d:d + 1]).T.astype(o_ref.dtype)


def _fox(proj, cx, nc):
    tq, tk, hg = FOX_TQ, FOX_TK, FOX_HEADS
    w = hg * HEAD_DIM
    per_blk = WIDTH // w
    vrows = HEAD_DIM + 16
    return pl.pallas_call(
        functools.partial(_fox_kernel, tq=tq, tk=tk),
        grid=(N_HEADS // hg, SEQ // tq),
        in_specs=[
            pl.BlockSpec((SEQ, w), lambda g, i: (0, BLK_BQ * per_blk + g)),
            pl.BlockSpec((SEQ, w), lambda g, i: (0, BLK_BK * per_blk + g)),
            pl.BlockSpec((SEQ, w), lambda g, i: (0, BLK_BV * per_blk + g)),
            pl.BlockSpec((SEQ, 128), lambda g, i: (0, 0)),
            pl.BlockSpec((SEQ, 128), lambda g, i: (0, 0)),
        ],
        out_specs=pl.BlockSpec((tq, w), lambda g, i: (i, g)),
        out_shape=jax.ShapeDtypeStruct((SEQ, WIDTH), BF16),
        scratch_shapes=[
            pltpu.VMEM((hg, SEQ // tk, vrows, tk), BF16),
            pltpu.VMEM((hg, 8, 128), F32),
            pltpu.VMEM((hg, 1, tq), F32),
            pltpu.VMEM((hg, vrows, tq), F32),
        ],
        compiler_params=_params("arbitrary", "arbitrary"),
        name="fox",
    )(proj, proj, proj, cx, nc)


def _merge_kernel(oa_ref, ob_ref, ga_ref, gb_ref, wa_ref, wb_ref, wo_ref, x_ref, nw_ref, npre_ref,
                  out_ref, h2_ref):
    ya = jnp.dot(oa_ref[...], wa_ref[...], preferred_element_type=F32)
    yb = jnp.dot(ob_ref[...], wb_ref[...], preferred_element_type=F32)
    merged = _sigmoid(ga_ref[...].astype(F32)) * ya + _sigmoid(gb_ref[...].astype(F32)) * yb
    u = jnp.dot(merged.astype(BF16), wo_ref[...], preferred_element_type=F32)
    u = u * lax.rsqrt(jnp.mean(u * u, axis=-1, keepdims=True) + RMS_EPS) * nw_ref[...]
    x1 = x_ref[...] + u
    out_ref[...] = x1
    ms = jnp.mean(x1 * x1, axis=-1, keepdims=True)
    h2_ref[...] = (x1 * lax.rsqrt(ms + RMS_EPS) * npre_ref[...]).astype(BF16)


def _merge(y_a, y_b, proj, w_up_a, w_up_b, w_o, x2, norm_w, norm_ffn_pre):
    tm = MERGE_TM
    once = pl.Buffered(1)
    return pl.pallas_call(
        _merge_kernel,
        grid=(SEQ // tm,),
        in_specs=[
            pl.BlockSpec((tm, WIDTH), lambda i: (i, 0)),
            pl.BlockSpec((tm, WIDTH), lambda i: (i, 0)),
            pl.BlockSpec((tm, D_MODEL), lambda i: (i, BLK_GA // 2)),
            pl.BlockSpec((tm, D_MODEL), lambda i: (i, BLK_GB // 2)),
            pl.BlockSpec((WIDTH, D_MODEL), lambda i: (0, 0), pipeline_mode=once),
            pl.BlockSpec((WIDTH, D_MODEL), lambda i: (0, 0), pipeline_mode=once),
            pl.BlockSpec((D_MODEL, D_MODEL), lambda i: (0, 0), pipeline_mode=once),
            pl.BlockSpec((tm, D_MODEL), lambda i: (i, 0)),
            pl.BlockSpec((1, D_MODEL), lambda i: (0, 0)),
            pl.BlockSpec((1, D_MODEL), lambda i: (0, 0)),
        ],
        out_specs=[
            pl.BlockSpec((tm, D_MODEL), lambda i: (i, 0)),
            pl.BlockSpec((tm, D_MODEL), lambda i: (i, 0)),
        ],
        out_shape=[
            jax.ShapeDtypeStruct((SEQ, D_MODEL), F32),
            jax.ShapeDtypeStruct((SEQ, D_MODEL), BF16),
        ],
        compiler_params=_params("arbitrary"),
        name="merge",
    )(y_a, y_b, proj, proj, w_up_a, w_up_b, w_o, x2, norm_w, norm_ffn_pre)


def _ffn_residual_copy(x_hbm, xres_ref, sem, i):
    tm = xres_ref.shape[0]
    return pltpu.make_async_copy(x_hbm.at[pl.ds(pl.multiple_of(i * tm, tm), tm), :], xres_ref, sem)


def _ffn_kernel(h_ref, x_hbm, npost_ref, wg_ref, wu_ref, wd_ref, out_ref, xres_ref, sem):
    i = pl.program_id(0)
    f = pl.program_id(1)

    @pl.when(f == 0)
    def _():
        _ffn_residual_copy(x_hbm, xres_ref, sem, i).start()
        out_ref[...] = jnp.zeros_like(out_ref)

    h = h_ref[...]
    gate = jnp.dot(h, wg_ref[...].astype(BF16), preferred_element_type=F32)
    up = jnp.dot(h, wu_ref[...].astype(BF16), preferred_element_type=F32)
    act = (gate * _sigmoid(gate) * up).astype(BF16)
    out_ref[...] += jnp.dot(act, wd_ref[...].astype(BF16), preferred_element_type=F32)

    @pl.when(f == pl.num_programs(1) - 1)
    def _():
        _ffn_residual_copy(x_hbm, xres_ref, sem, i).wait()
        u = out_ref[...]
        u = u * lax.rsqrt(jnp.mean(u * u, axis=-1, keepdims=True) + RMS_EPS) * npost_ref[...]
        out_ref[...] = xres_ref[...] + u


def _ffn(h2, x1, norm_post, w_in, w_down):
    tm, tf = FFN_TM, FFN_TF
    nf = D_FF // tf
    return pl.pallas_call(
        _ffn_kernel,
        grid=(SEQ // tm, nf),
        in_specs=[
            pl.BlockSpec((tm, D_MODEL), lambda i, f: (i, 0)),
            pl.BlockSpec(memory_space=pl.ANY),
            pl.BlockSpec((1, D_MODEL), lambda i, f: (0, 0)),
            pl.BlockSpec((D_MODEL, tf), lambda i, f: (0, f)),
            pl.BlockSpec((D_MODEL, tf), lambda i, f: (0, f + nf)),
            pl.BlockSpec((tf, D_MODEL), lambda i, f: (f, 0)),
        ],
        out_specs=pl.BlockSpec((tm, D_MODEL), lambda i, f: (i, 0)),
        out_shape=jax.ShapeDtypeStruct((SEQ, D_MODEL), F32),
        scratch_shapes=[pltpu.VMEM((tm, D_MODEL), F32), pltpu.SemaphoreType.DMA(())],
        compiler_params=_params("arbitrary", "arbitrary"),
        name="ffn",
    )(h2, x1, norm_post, w_in, w_in, w_down)


def kernel(x, w_in, b_fox_f, hgrn_lb_logits, hgrn_norm_w, w_up_a, w_up_b, w_o, norm_mix_pre,
           norm_mix_post, norm_ffn_pre, norm_ffn_post, w_ffn_in, w_ffn_down):
    assert x.shape == (1, SEQ, D_MODEL) and w_in.shape[0] == 1
    w_t = w_in[0].T
    bf_row = jnp.pad(b_fox_f.reshape(1, N_HEADS), ((0, 0), (0, 128 - N_HEADS)))
    x2 = x[0]

    h, a_f, cx, nc = _prenorm(x2, norm_mix_pre, w_t, bf_row)
    proj = _inproj(h, w_t)
    y_a = _hgrn(proj, a_f, hgrn_lb_logits, hgrn_norm_w)
    y_b = _fox(proj, cx, nc)
    x1, h2 = _merge(y_a, y_b, proj, w_up_a[0].astype(BF16), w_up_b[0].astype(BF16), w_o[0].astype(BF16),
                    x2, norm_mix_post, norm_ffn_pre)
    out = _ffn(h2, x1, norm_ffn_post, w_ffn_in[0], w_ffn_down[0])
    return out[None]
```

```python
import functools

import numpy as np
import jax
import jax.numpy as jnp
from jax import lax
from jax.experimental import pallas as pl
from jax.experimental.pallas import tpu as pltpu

F32 = jnp.float32
BF16 = jnp.bfloat16

D_MODEL = 2048
SEQ = 8192
HEAD_DIM = 128
N_HEADS = 8
WIDTH = N_HEADS * HEAD_DIM
D_FF = 5632
RMS_EPS = 1e-6
N_PROJ = 3 * WIDTH + 2 * D_MODEL + 3 * WIDTH

VMEM_LIMIT_BYTES = 56 * 1024 * 1024

SRC_AQ, SRC_AF, SRC_AI, SRC_AG, SRC_BQ, SRC_BK, SRC_BV = 0, 1, 2, 3, 4, 5, 6
IN_NA = 7
BLK_AQ, BLK_AI, BLK_GA, BLK_GB, BLK_AG, BLK_BQ, BLK_BK, BLK_BV = 0, 1, 2, 4, 6, 7, 8, 9
HEADS_PER_BLK = WIDTH // HEAD_DIM

PRE_TM = 512
IN_TM, IN_TN = 2048, 512
HG_TS, HG_C, HG_HEADS, HG_SUB, HG_UNROLL = 1024, 64, 8, 1, 2
FOX_TQ, FOX_TK, FOX_HEADS = 512, 256, 2
LOG2E = 1.4426950408889634
FOX_SKIP_LOG2 = 160.0
FOX_NORM_MARGIN = 1.01
MERGE_TM = 256
FFN_TM, FFN_TF = 1024, 256


def _params(*sem):
    return pltpu.CompilerParams(dimension_semantics=sem, vmem_limit_bytes=VMEM_LIMIT_BYTES)


def _dot_nt(a, b):
    return lax.dot_general(a, b, (((1,), (1,)), ((), ())), preferred_element_type=F32)


def _dot_tn(a, b):
    return lax.dot_general(a, b, (((0,), (0,)), ((), ())), preferred_element_type=F32)


def _log_sigmoid(x):
    return jnp.minimum(x, 0.0) - jnp.log(1.0 + jnp.exp(-jnp.abs(x)))


def _sigmoid(x):
    return 1.0 / (1.0 + jnp.exp(-x))


def _split3(x):
    p0 = x.astype(BF16)
    r1 = x - p0.astype(F32)
    p1 = r1.astype(BF16)
    p2 = (r1 - p1.astype(F32)).astype(BF16)
    return p0, p1, p2


def _prenorm_kernel(x_ref, nw_ref, waf_ref, wf_ref, bf_ref, tri_ref, h_ref, af_ref, cx_ref, nc_ref,
                    wafb_ref, carry_ref):
    i = pl.program_id(0)
    tm = x_ref.shape[0]

    @pl.when(i == 0)
    def _():
        wafb_ref[...] = waf_ref[...].T.astype(BF16)
        carry_ref[...] = jnp.zeros_like(carry_ref)

    x = x_ref[...]
    ms = jnp.mean(x * x, axis=-1, keepdims=True)
    h_ref[...] = (x * lax.rsqrt(ms + RMS_EPS) * nw_ref[...]).astype(BF16)
    hb = h_ref[...]
    af_ref[...] = jnp.dot(hb, wafb_ref[...], preferred_element_type=F32)
    logit = _dot_nt(wf_ref[...].astype(BF16), hb) + bf_ref[...]
    nls = _log_sigmoid(logit) * (-LOG2E)
    tri = tri_ref[...]
    loc = sum(jnp.dot(p, tri, preferred_element_type=F32) for p in _split3(nls))
    nc = loc + carry_ref[:, 0:1]
    carry_ref[...] = jnp.broadcast_to(nc[:, tm - 1:tm], carry_ref.shape)
    pad = jnp.zeros((128 - N_HEADS, tm), F32)
    nc_ref[...] = jnp.concatenate([nc, pad], axis=0).T
    parts = [p.astype(F32) for p in _split3(nc)]
    pad3 = jnp.zeros((128 - 3 * N_HEADS, tm), F32)
    cx_ref[...] = jnp.concatenate(parts + [pad3], axis=0).T.astype(BF16)


def _prenorm(x2, norm_w, w, bf_col):
    tm = PRE_TM
    tri = jnp.asarray(np.triu(np.ones((tm, tm), np.float32)), BF16)
    once = pl.Buffered(1)
    return pl.pallas_call(
        _prenorm_kernel,
        grid=(SEQ // tm,),
        in_specs=[
            pl.BlockSpec((tm, D_MODEL), lambda i: (i, 0)),
            pl.BlockSpec((1, D_MODEL), lambda i: (0, 0)),
            pl.BlockSpec((WIDTH, D_MODEL), lambda i: (SRC_AF, 0), pipeline_mode=once),
            pl.BlockSpec((N_HEADS, D_MODEL), lambda i: (IN_NA * WIDTH // N_HEADS, 0), pipeline_mode=once),
            pl.BlockSpec((N_HEADS, 1), lambda i: (0, 0)),
            pl.BlockSpec((tm, tm), lambda i: (0, 0), pipeline_mode=once),
        ],
        out_specs=[
            pl.BlockSpec((tm, D_MODEL), lambda i: (i, 0)),
            pl.BlockSpec((tm, WIDTH), lambda i: (i, 0)),
            pl.BlockSpec((tm, 128), lambda i: (i, 0)),
            pl.BlockSpec((tm, 128), lambda i: (i, 0)),
        ],
        out_shape=[
            jax.ShapeDtypeStruct((SEQ, D_MODEL), BF16),
            jax.ShapeDtypeStruct((SEQ, WIDTH), F32),
            jax.ShapeDtypeStruct((SEQ, 128), BF16),
            jax.ShapeDtypeStruct((SEQ, 128), F32),
        ],
        scratch_shapes=[pltpu.VMEM((D_MODEL, WIDTH), BF16), pltpu.VMEM((N_HEADS, 128), F32)],
        compiler_params=_params("arbitrary"),
        name="prenorm",
    )(x2, norm_w, w, w, bf_col, tri)


def _inproj_kernel(h_ref, wa_ref, wb_ref, proj_ref, wc_ref):
    j = pl.program_id(1)
    i = pl.program_id(2)
    tm, tn = proj_ref.shape
    nb = WIDTH // tn
    src = _inproj_src_block(j, nb)

    @pl.when(i == 0)
    def _():
        @pl.when(src < IN_NA * nb)
        def _():
            scale = jnp.where(src // nb == SRC_BQ, LOG2E * HEAD_DIM ** -0.5, 1.0)
            wc_ref[...] = (wa_ref[...] * scale).T.astype(BF16)

        @pl.when(src >= IN_NA * nb)
        def _():
            wc_ref[...] = jnp.concatenate([wa_ref[N_HEADS:, :], wb_ref[:N_HEADS, :]], axis=0).T.astype(BF16)

    rows = pl.ds(pl.multiple_of(i * tm, tm), tm)
    proj_ref[...] = jnp.dot(h_ref[rows, :], wc_ref[...], preferred_element_type=F32).astype(BF16)


def _inproj_src_block(j, nb):
    return jnp.where(j < SRC_AF * nb, j, j + nb)


def _inproj_out_block(j, nb):
    src = _inproj_src_block(j, nb)
    g = src // nb
    g_out = jnp.int32(BLK_AQ)
    for g_src, g_dst in ((SRC_AI, BLK_AI), (SRC_AG, BLK_AG), (SRC_BQ, BLK_BQ), (SRC_BK, BLK_BK), (SRC_BV, BLK_BV)):
        g_out = jnp.where(g == g_src, g_dst, g_out)
    return jnp.where(g < IN_NA, g_out * nb + src % nb, src - IN_NA * nb + BLK_GA * nb)


def _inproj(h, w):
    tm, tn = IN_TM, IN_TN
    nb = WIDTH // tn
    half = SEQ // 2
    n_i = half // tm
    first_gate = IN_NA * nb
    grid = (2, N_PROJ // tn, n_i)
    return pl.pallas_call(
        _inproj_kernel,
        grid=grid,
        in_specs=[
            pl.BlockSpec((half, D_MODEL), lambda s, j, i: (s, 0), pipeline_mode=pl.Buffered(1)),
            pl.BlockSpec((tn, D_MODEL), lambda s, j, i: (_inproj_src_block(j, nb), 0)),
            pl.BlockSpec((tn, D_MODEL), lambda s, j, i: (jnp.maximum(_inproj_src_block(j, nb), first_gate) + 1, 0)),
        ],
        out_specs=pl.BlockSpec((tm, tn), lambda s, j, i: (s * n_i + i, _inproj_out_block(j, nb))),
        out_shape=jax.ShapeDtypeStruct((SEQ, N_PROJ), BF16),
        scratch_shapes=[pltpu.VMEM((D_MODEL, tn), BF16)],
        compiler_params=_params("arbitrary", "arbitrary", "arbitrary"),
        name="inproj",
    )(h, w, w)


def _hgrn_constants(c):
    n_lvl = int(np.log2(c))
    t = np.arange(c)[:, None]
    j = np.arange(c)[None, :]
    blocks = [(j <= t), (j > t)]
    level = np.full((c, c), -1, np.int32)
    level[np.arange(c), np.arange(c)] = 0
    for l in range(n_lvl):
        b = 2 << l
        mid = (t // b) * b + b // 2 - 1
        second = (t % b) >= b // 2
        m = np.where(second, (j > mid) & (j <= t), (j > t) & (j <= mid))
        blocks.append(m)
        s = np.arange(c)[None, :]
        own = (t // b == s // b) & second & ((s % b) < b // 2)
        level[own] = l + 1
    sums = np.concatenate(blocks, axis=0).astype(np.float32)
    sums2 = np.concatenate([sums, sums], axis=1)
    return jnp.asarray(sums2, BF16), jnp.asarray(level), n_lvl


def _hgrn_kernel(q_ref, z_ref, v_ref, g_ref, lbl_ref, nw_ref, sums_ref, lvl_ref, y_ref, st_ref, *,
                 c, n_lvl, n_sub):
    n_heads = st_ref.shape[0]
    d = HEAD_DIM

    @pl.when(pl.program_id(1) == 0)
    def _():
        st_ref[...] = jnp.zeros_like(st_ref)

    logits = lbl_ref[...]
    ex = jnp.exp(logits - jnp.max(logits, axis=0, keepdims=True))
    lb = ex[0:1, :] / jnp.sum(ex, axis=0, keepdims=True)
    one_m_lb = 1.0 - lb
    nw = nw_ref[...]
    sums = sums_ref[...]
    lvl = lvl_ref[...]
    n_chunks = q_ref.shape[0] // c

    heads = [slice(hh * d, (hh + 1) * d) for hh in range(n_heads)]

    def chunks(ci, carry):
        rows = [pl.ds(pl.multiple_of((ci * n_sub + s) * c, c), c) for s in range(n_sub)]
        k_all, dec_all = [], []
        for r in rows:
            z = z_ref[r, :]
            e = jnp.exp(-jnp.abs(z))
            inv = 1.0 / (1.0 + e)
            pos = z >= 0.0
            sig = jnp.where(pos, inv, e * inv)
            sig_n = jnp.where(pos, e * inv, inv)
            g = jnp.log2(lb + one_m_lb * sig)
            k_all.append(one_m_lb * sig_n)
            g_hi = g.astype(BF16)
            g_lo = (g - g_hi.astype(F32)).astype(BF16)
            expo = jnp.dot(sums, jnp.concatenate([g_hi, g_lo], axis=0), preferred_element_type=F32)
            dec_all.append(jnp.exp2(expo))
        scores = {}
        for s, r in enumerate(rows):
            for hh, cols in enumerate(heads):
                q = q_ref[r, cols].astype(F32)
                k = k_all[s][:, cols]
                sc = jnp.where(lvl == 0, _dot_nt(q.astype(BF16), k.astype(BF16)), 0.0)
                for l in range(n_lvl):
                    d_l = dec_all[s][(2 + l) * c:(3 + l) * c, cols]
                    s_l = _dot_nt((q * d_l).astype(BF16), (k * d_l).astype(BF16))
                    sc = jnp.where(lvl == l + 1, s_l, sc)
                scores[s, hh] = sc.astype(BF16)
        o_intra, inc, q_in = {}, {}, {}
        for s, r in enumerate(rows):
            for hh, cols in enumerate(heads):
                v = v_ref[r, cols]
                o_intra[s, hh] = jnp.dot(scores[s, hh], v, preferred_element_type=F32)
                k_out = (k_all[s][:, cols] * dec_all[s][c:2 * c, cols]).astype(BF16)
                inc[s, hh] = _dot_tn(v, k_out)
                q_in[s, hh] = (q_ref[r, cols].astype(F32) * dec_all[s][0:c, cols]).astype(BF16)
        for hh, cols in enumerate(heads):
            st = st_ref[hh]
            for s, r in enumerate(rows):
                o = o_intra[s, hh] + _dot_nt(q_in[s, hh], st.astype(BF16))
                st = st * dec_all[s][c - 1:c, cols] + inc[s, hh]
                o = o * lax.rsqrt(jnp.mean(o * o, axis=-1, keepdims=True) + RMS_EPS) * nw
                gt = g_ref[r, cols].astype(F32)
                y_ref[r, cols] = (o * gt * _sigmoid(gt)).astype(y_ref.dtype)
            st_ref[hh] = st
        return carry

    lax.fori_loop(0, n_chunks // n_sub, chunks, 0, unroll=HG_UNROLL)


def _hgrn(proj, a_f, lb_logits, norm_w):
    ts, c, hg = HG_TS, HG_C, HG_HEADS
    sums, lvl, n_lvl = _hgrn_constants(c)
    w = hg * HEAD_DIM
    per_blk = WIDTH // w
    grid = (N_HEADS // hg, SEQ // ts)
    return pl.pallas_call(
        functools.partial(_hgrn_kernel, c=c, n_lvl=n_lvl, n_sub=HG_SUB),
        grid=grid,
        in_specs=[
            pl.BlockSpec((ts, w), lambda h, i: (i, BLK_AQ * per_blk + h)),
            pl.BlockSpec((ts, w), lambda h, i: (i, h)),
            pl.BlockSpec((ts, w), lambda h, i: (i, BLK_AI * per_blk + h)),
            pl.BlockSpec((ts, w), lambda h, i: (i, BLK_AG * per_blk + h)),
            pl.BlockSpec((lb_logits.shape[0], w), lambda h, i: (0, h)),
            pl.BlockSpec((1, HEAD_DIM), lambda h, i: (0, 0)),
            pl.BlockSpec(sums.shape, lambda h, i: (0, 0)),
            pl.BlockSpec(lvl.shape, lambda h, i: (0, 0)),
        ],
        out_specs=pl.BlockSpec((ts, w), lambda h, i: (i, h)),
        out_shape=jax.ShapeDtypeStruct((SEQ, WIDTH), BF16),
        scratch_shapes=[pltpu.VMEM((hg, HEAD_DIM, HEAD_DIM), F32)],
        compiler_params=_params("arbitrary", "arbitrary"),
        name="hgrn",
    )(proj, a_f, proj, proj, lb_logits, norm_w, sums, lvl)


def _fox_kernel(q_ref, k_ref, v_ref, cx_ref, nc_ref, o_ref, vt_ref, nrm_ref, m_ref, acc_ref, *, tq, tk):
    g = pl.program_id(0)
    i = pl.program_id(1)
    n_heads = vt_ref.shape[0]
    d = HEAD_DIM
    s_len = k_ref.shape[0]
    n_sub = tq // tk
    n_kt = s_len // tk
    vrows = vt_ref.shape[2]
    heads = [(hh, slice(hh * d, (hh + 1) * d)) for hh in range(n_heads)]

    @pl.when(i == 0)
    def _():
        ones_row = jnp.where(lax.broadcasted_iota(jnp.int32, (vrows - d, tk), 0) == 0, 1.0, 0.0)
        wg = n_heads * d
        same_head = (lax.broadcasted_iota(jnp.int32, (wg, wg), 0) // d
                     == lax.broadcasted_iota(jnp.int32, (wg, wg), 1) // d)
        ones_blk = jnp.where(same_head, 1.0, 0.0).astype(BF16)

        def sq_norms(x):
            xf = x.astype(F32)
            return jnp.dot((xf * xf).astype(BF16), ones_blk, preferred_element_type=F32)

        kn = jnp.zeros((tk, wg), F32)
        qn = jnp.zeros((tk, wg), F32)
        for r in range(n_kt):
            rows = slice(r * tk, (r + 1) * tk)
            for hh, cols in heads:
                vt_ref[hh, r, 0:d, :] = v_ref[rows, cols].astype(F32).T.astype(BF16)
                vt_ref[hh, r, d:vrows, :] = ones_row.astype(BF16)
            kn = jnp.maximum(kn, sq_norms(k_ref[rows, :]))
            qn = jnp.maximum(qn, sq_norms(q_ref[rows, :]))
        nrm = 2.0 * FOX_NORM_MARGIN * jnp.sqrt(jnp.max(qn, axis=0, keepdims=True)
                                               * jnp.max(kn, axis=0, keepdims=True))
        for hh, cols in heads:
            nrm_ref[hh] = jnp.broadcast_to(nrm[:, cols], nrm_ref.shape[1:])

    q_rows = pl.ds(pl.multiple_of(i * tq, tq), tq)
    nc_q0 = nc_ref[pl.ds(pl.multiple_of(i * tq, tq), 1), :]
    ends = nc_ref[pl.ds(tk - 1, n_kt, stride=tk), :]
    lane = lax.broadcasted_iota(jnp.int32, ends.shape, 1)
    rsel = lax.broadcasted_iota(jnp.int32, (d, tq), 0)
    qa, r_lo = [], []
    for hh, cols in heads:
        h = g * n_heads + hh
        thr = nc_q0 - nrm_ref[hh, 0:1, :] - FOX_SKIP_LOG2
        r_lo.append(jnp.sum(jnp.where((ends < thr) & (lane == h), 1, 0)))
        sel = jnp.where((rsel < 3 * N_HEADS) & ((rsel & (N_HEADS - 1)) == h), 1.0, 0.0).astype(BF16)
        qa.append(jnp.concatenate([q_ref[q_rows, cols].astype(F32).T.astype(BF16), sel], axis=0))

    def scores(hh, cols, r, mask_off):
        ks = pl.ds(pl.multiple_of(r * tk, tk), tk)
        ka = jnp.concatenate([k_ref[ks, cols], cx_ref[ks, :]], axis=1)
        s = jnp.dot(ka, qa[hh], preferred_element_type=F32)
        if mask_off is not None:
            kid = lax.broadcasted_iota(jnp.int32, s.shape, 0) + mask_off
            qid = lax.broadcasted_iota(jnp.int32, s.shape, 1)
            s = jnp.where(kid <= qid, s, -jnp.inf)
        return s

    def first_step(tiles):
        ss = [[scores(hh, cols, r, off) for r, off in tiles] for hh, cols in heads]
        ms = []
        for hh, _ in heads:
            m = jnp.max(ss[hh][0], axis=0, keepdims=True)
            for s in ss[hh][1:]:
                m = jnp.maximum(m, jnp.max(s, axis=0, keepdims=True))
            ms.append(m)
        for hh, _ in heads:
            pv = None
            for (r, _), s in zip(tiles, ss[hh]):
                p = jnp.exp2(s - ms[hh]).astype(BF16)
                dd = jnp.dot(vt_ref[hh, r], p, preferred_element_type=F32)
                pv = dd if pv is None else pv + dd
            acc_ref[hh] = pv
            m_ref[hh] = ms[hh]

    def later_step(hh, cols, r):
        s = scores(hh, cols, r, None)
        m_prev = m_ref[hh]
        m_new = jnp.maximum(m_prev, jnp.max(s, axis=0, keepdims=True))
        pv = jnp.dot(vt_ref[hh, r], jnp.exp2(s - m_new).astype(BF16), preferred_element_type=F32)
        acc_ref[hh] = jnp.exp2(m_prev - m_new) * acc_ref[hh] + pv
        m_ref[hh] = m_new

    diag = [(i * n_sub + rr, rr * tk) for rr in range(n_sub)]

    @pl.when(i == 0)
    def _():
        first_step(diag)

    @pl.when(i > 0)
    def _():
        first_step([(i * n_sub - 1, None)] + diag)
        for hh, cols in heads:
            def off_diag(r, carry, hh=hh, cols=cols):
                later_step(hh, cols, r)
                return carry
            lax.fori_loop(r_lo[hh], i * n_sub - 1, off_diag, 0)

    for hh, cols in heads:
        acc = acc_ref[hh]
        o_ref[:, cols] = (acc[0:d] / acc[d:d + 1]).T.astype(o_ref.dtype)


def _fox(proj, cx, nc):
    tq, tk, hg = FOX_TQ, FOX_TK, FOX_HEADS
    w = hg * HEAD_DIM
    per_blk = WIDTH // w
    vrows = HEAD_DIM + 16
    return pl.pallas_call(
        functools.partial(_fox_kernel, tq=tq, tk=tk),
        grid=(N_HEADS // hg, SEQ // tq),
        in_specs=[
            pl.BlockSpec((SEQ, w), lambda g, i: (0, BLK_BQ * per_blk + g)),
            pl.BlockSpec((SEQ, w), lambda g, i: (0, BLK_BK * per_blk + g)),
            pl.BlockSpec((SEQ, w), lambda g, i: (0, BLK_BV * per_blk + g)),
            pl.BlockSpec((SEQ, 128), lambda g, i: (0, 0)),
            pl.BlockSpec((SEQ, 128), lambda g, i: (0, 0)),
        ],
        out_specs=pl.BlockSpec((tq, w), lambda g, i: (i, g)),
        out_shape=jax.ShapeDtypeStruct((SEQ, WIDTH), BF16),
        scratch_shapes=[
            pltpu.VMEM((hg, SEQ // tk, vrows, tk), BF16),
            pltpu.VMEM((hg, 8, 128), F32),
            pltpu.VMEM((hg, 1, tq), F32),
            pltpu.VMEM((hg, vrows, tq), F32),
        ],
        compiler_params=_params("arbitrary", "arbitrary"),
        name="fox",
    )(proj, proj, proj, cx, nc)


def _merge_kernel(oa_ref, ob_ref, ga_ref, gb_ref, wa_ref, wb_ref, wo_ref, x_ref, nw_ref, npre_ref,
                  out_ref, h2_ref):
    ya = jnp.dot(oa_ref[...], wa_ref[...], preferred_element_type=F32)
    yb = jnp.dot(ob_ref[...], wb_ref[...], preferred_element_type=F32)
    merged = _sigmoid(ga_ref[...].astype(F32)) * ya + _sigmoid(gb_ref[...].astype(F32)) * yb
    u = jnp.dot(merged.astype(BF16), wo_ref[...], preferred_element_type=F32)
    u = u * lax.rsqrt(jnp.mean(u * u, axis=-1, keepdims=True) + RMS_EPS) * nw_ref[...]
    x1 = x_ref[...] + u
    out_ref[...] = x1
    ms = jnp.mean(x1 * x1, axis=-1, keepdims=True)
    h2_ref[...] = (x1 * lax.rsqrt(ms + RMS_EPS) * npre_ref[...]).astype(BF16)


def _merge(y_a, y_b, proj, w_up_a, w_up_b, w_o, x2, norm_w, norm_ffn_pre):
    tm = MERGE_TM
    once = pl.Buffered(1)
    return pl.pallas_call(
        _merge_kernel,
        grid=(SEQ // tm,),
        in_specs=[
            pl.BlockSpec((tm, WIDTH), lambda i: (i, 0)),
            pl.BlockSpec((tm, WIDTH), lambda i: (i, 0)),
            pl.BlockSpec((tm, D_MODEL), lambda i: (i, BLK_GA // 2)),
            pl.BlockSpec((tm, D_MODEL), lambda i: (i, BLK_GB // 2)),
            pl.BlockSpec((WIDTH, D_MODEL), lambda i: (0, 0), pipeline_mode=once),
            pl.BlockSpec((WIDTH, D_MODEL), lambda i: (0, 0), pipeline_mode=once),
            pl.BlockSpec((D_MODEL, D_MODEL), lambda i: (0, 0), pipeline_mode=once),
            pl.BlockSpec((tm, D_MODEL), lambda i: (i, 0)),
            pl.BlockSpec((1, D_MODEL), lambda i: (0, 0)),
            pl.BlockSpec((1, D_MODEL), lambda i: (0, 0)),
        ],
        out_specs=[
            pl.BlockSpec((tm, D_MODEL), lambda i: (i, 0)),
            pl.BlockSpec((tm, D_MODEL), lambda i: (i, 0)),
        ],
        out_shape=[
            jax.ShapeDtypeStruct((SEQ, D_MODEL), F32),
            jax.ShapeDtypeStruct((SEQ, D_MODEL), BF16),
        ],
        compiler_params=_params("arbitrary"),
        name="merge",
    )(y_a, y_b, proj, proj, w_up_a, w_up_b, w_o, x2, norm_w, norm_ffn_pre)


def _ffn_residual_copy(x_hbm, xres_ref, sem, i):
    tm = xres_ref.shape[0]
    return pltpu.make_async_copy(x_hbm.at[pl.ds(pl.multiple_of(i * tm, tm), tm), :], xres_ref, sem)


def _ffn_kernel(h_ref, x_hbm, npost_ref, wg_ref, wu_ref, wd_ref, out_ref, xres_ref, sem):
    i = pl.program_id(0)
    f = pl.program_id(1)

    @pl.when(f == 0)
    def _():
        _ffn_residual_copy(x_hbm, xres_ref, sem, i).start()
        out_ref[...] = jnp.zeros_like(out_ref)

    h = h_ref[...]
    gate = jnp.dot(h, wg_ref[...].astype(BF16), preferred_element_type=F32)
    up = jnp.dot(h, wu_ref[...].astype(BF16), preferred_element_type=F32)
    act = (gate * _sigmoid(gate) * up).astype(BF16)
    out_ref[...] += jnp.dot(act, wd_ref[...].astype(BF16), preferred_element_type=F32)

    @pl.when(f == pl.num_programs(1) - 1)
    def _():
        _ffn_residual_copy(x_hbm, xres_ref, sem, i).wait()
        u = out_ref[...]
        u = u * lax.rsqrt(jnp.mean(u * u, axis=-1, keepdims=True) + RMS_EPS) * npost_ref[...]
        out_ref[...] = xres_ref[...] + u


def _ffn(h2, x1, norm_post, w_in, w_down):
    tm, tf = FFN_TM, FFN_TF
    nf = D_FF // tf
    return pl.pallas_call(
        _ffn_kernel,
        grid=(SEQ // tm, nf),
        in_specs=[
            pl.BlockSpec((tm, D_MODEL), lambda i, f: (i, 0)),
            pl.BlockSpec(memory_space=pl.ANY),
            pl.BlockSpec((1, D_MODEL), lambda i, f: (0, 0)),
            pl.BlockSpec((D_MODEL, tf), lambda i, f: (0, f)),
            pl.BlockSpec((D_MODEL, tf), lambda i, f: (0, f + nf)),
            pl.BlockSpec((tf, D_MODEL), lambda i, f: (f, 0)),
        ],
        out_specs=pl.BlockSpec((tm, D_MODEL), lambda i, f: (i, 0)),
        out_shape=jax.ShapeDtypeStruct((SEQ, D_MODEL), F32),
        scratch_shapes=[pltpu.VMEM((tm, D_MODEL), F32), pltpu.SemaphoreType.DMA(())],
        compiler_params=_params("arbitrary", "arbitrary"),
        name="ffn",
    )(h2, x1, norm_post, w_in, w_in, w_down)


def kernel(x, w_in, b_fox_f, hgrn_lb_logits, hgrn_norm_w, w_up_a, w_up_b, w_o, norm_mix_pre,
           norm_mix_post, norm_ffn_pre, norm_ffn_post, w_ffn_in, w_ffn_down):
    assert x.shape == (1, SEQ, D_MODEL) and w_in.shape[0] == 1
    w_t = w_in[0].T
    x2 = x[0]

    h, a_f, cx, nc = _prenorm(x2, norm_mix_pre, w_t, b_fox_f.reshape(N_HEADS, 1))
    proj = _inproj(h, w_t)
    y_a = _hgrn(proj, a_f, hgrn_lb_logits, hgrn_norm_w)
    y_b = _fox(proj, cx, nc)
    x1, h2 = _merge(y_a, y_b, proj, w_up_a[0].astype(BF16), w_up_b[0].astype(BF16), w_o[0].astype(BF16),
                    x2, norm_mix_post, norm_ffn_pre)
    out = _ffn(h2, x1, norm_ffn_post, w_ffn_in[0], w_ffn_down[0])
    return out[None]
```

```python
import functools

import numpy as np
import jax
import jax.numpy as jnp
from jax import lax
from jax.experimental import pallas as pl
from jax.experimental.pallas import tpu as pltpu

F32 = jnp.float32
BF16 = jnp.bfloat16

D_MODEL = 2048
SEQ = 8192
HEAD_DIM = 128
N_HEADS = 8
WIDTH = N_HEADS * HEAD_DIM
D_FF = 5632
RMS_EPS = 1e-6
N_PROJ = 3 * WIDTH + 2 * D_MODEL + 3 * WIDTH

VMEM_LIMIT_BYTES = 56 * 1024 * 1024

SRC_AQ, SRC_AF, SRC_AI, SRC_AG, SRC_BQ, SRC_BK, SRC_BV = 0, 1, 2, 3, 4, 5, 6
IN_NA = 7
BLK_AQ, BLK_AI, BLK_GA, BLK_GB, BLK_AG, BLK_BQ, BLK_BK, BLK_BV = 0, 1, 2, 4, 6, 7, 8, 9
HEADS_PER_BLK = WIDTH // HEAD_DIM

PRE_TM = 512
IN_TM, IN_TN = 2048, 512
HG_TS, HG_C, HG_HEADS, HG_SUB, HG_UNROLL = 1024, 64, 8, 1, 2
FOX_TQ, FOX_TK, FOX_HEADS = 512, 256, 2
LOG2E = 1.4426950408889634
FOX_SKIP_LOG2 = 160.0
FOX_NORM_MARGIN = 1.01
MERGE_TM = 256
FFN_TM, FFN_TF = 1024, 256


def _params(*sem):
    return pltpu.CompilerParams(dimension_semantics=sem, vmem_limit_bytes=VMEM_LIMIT_BYTES)


def _dot_nt(a, b):
    return lax.dot_general(a, b, (((1,), (1,)), ((), ())), preferred_element_type=F32)


def _dot_tn(a, b):
    return lax.dot_general(a, b, (((0,), (0,)), ((), ())), preferred_element_type=F32)


def _log_sigmoid(x):
    return jnp.minimum(x, 0.0) - jnp.log(1.0 + jnp.exp(-jnp.abs(x)))


def _sigmoid(x):
    return 1.0 / (1.0 + jnp.exp(-x))


def _split3(x):
    p0 = x.astype(BF16)
    r1 = x - p0.astype(F32)
    p1 = r1.astype(BF16)
    p2 = (r1 - p1.astype(F32)).astype(BF16)
    return p0, p1, p2


def _prenorm_kernel(x_ref, nw_ref, waf_ref, wf_ref, bf_ref, tri_ref, h_ref, af_ref, cx_ref, nc_ref,
                    wafb_ref, carry_ref):
    i = pl.program_id(0)
    tm = x_ref.shape[0]

    @pl.when(i == 0)
    def _():
        wafb_ref[...] = waf_ref[...].T.astype(BF16)
        carry_ref[...] = jnp.zeros_like(carry_ref)

    x = x_ref[...]
    ms = jnp.mean(x * x, axis=-1, keepdims=True)
    h_ref[...] = (x * lax.rsqrt(ms + RMS_EPS) * nw_ref[...]).astype(BF16)
    hb = h_ref[...]
    af_ref[...] = jnp.dot(hb, wafb_ref[...], preferred_element_type=F32)
    logit = _dot_nt(wf_ref[...].astype(BF16), hb) + bf_ref[...]
    nls = _log_sigmoid(logit) * (-LOG2E)
    tri = tri_ref[...]
    loc = sum(jnp.dot(p, tri, preferred_element_type=F32) for p in _split3(nls))
    nc = loc + carry_ref[:, 0:1]
    carry_ref[...] = jnp.broadcast_to(nc[:, tm - 1:tm], carry_ref.shape)
    pad = jnp.zeros((128 - N_HEADS, tm), F32)
    nc_ref[...] = jnp.concatenate([nc, pad], axis=0).T
    parts = [p.astype(F32) for p in _split3(nc)]
    pad3 = jnp.zeros((128 - 3 * N_HEADS, tm), F32)
    cx_ref[...] = jnp.concatenate(parts + [pad3], axis=0).T.astype(BF16)


def _prenorm(x2, norm_w, w, bf_col):
    tm = PRE_TM
    tri = jnp.asarray(np.triu(np.ones((tm, tm), np.float32)), BF16)
    once = pl.Buffered(1)
    return pl.pallas_call(
        _prenorm_kernel,
        grid=(SEQ // tm,),
        in_specs=[
            pl.BlockSpec((tm, D_MODEL), lambda i: (i, 0)),
            pl.BlockSpec((1, D_MODEL), lambda i: (0, 0)),
            pl.BlockSpec((WIDTH, D_MODEL), lambda i: (SRC_AF, 0), pipeline_mode=once),
            pl.BlockSpec((N_HEADS, D_MODEL), lambda i: (IN_NA * WIDTH // N_HEADS, 0), pipeline_mode=once),
            pl.BlockSpec((N_HEADS, 1), lambda i: (0, 0)),
            pl.BlockSpec((tm, tm), lambda i: (0, 0), pipeline_mode=once),
        ],
        out_specs=[
            pl.BlockSpec((tm, D_MODEL), lambda i: (i, 0)),
            pl.BlockSpec((tm, WIDTH), lambda i: (i, 0)),
            pl.BlockSpec((tm, 128), lambda i: (i, 0)),
            pl.BlockSpec((tm, 128), lambda i: (i, 0)),
        ],
        out_shape=[
            jax.ShapeDtypeStruct((SEQ, D_MODEL), BF16),
            jax.ShapeDtypeStruct((SEQ, WIDTH), F32),
            jax.ShapeDtypeStruct((SEQ, 128), BF16),
            jax.ShapeDtypeStruct((SEQ, 128), F32),
        ],
        scratch_shapes=[pltpu.VMEM((D_MODEL, WIDTH), BF16), pltpu.VMEM((N_HEADS, 128), F32)],
        compiler_params=_params("arbitrary"),
        name="prenorm",
    )(x2, norm_w, w, w, bf_col, tri)


def _inproj_kernel(h_ref, wa_ref, wb_ref, proj_ref, wc_ref):
    j = pl.program_id(1)
    i = pl.program_id(2)
    tm, tn = proj_ref.shape
    nb = WIDTH // tn
    src = _inproj_src_block(j, nb)

    @pl.when(i == 0)
    def _():
        @pl.when(src < IN_NA * nb)
        def _():
            scale = jnp.where(src // nb == SRC_BQ, LOG2E * HEAD_DIM ** -0.5, 1.0)
            wc_ref[...] = (wa_ref[...] * scale).T.astype(BF16)

        @pl.when(src >= IN_NA * nb)
        def _():
            wc_ref[...] = jnp.concatenate([wa_ref[N_HEADS:, :], wb_ref[:N_HEADS, :]], axis=0).T.astype(BF16)

    rows = pl.ds(pl.multiple_of(i * tm, tm), tm)
    proj_ref[...] = jnp.dot(h_ref[rows, :], wc_ref[...], preferred_element_type=F32).astype(BF16)


def _inproj_src_block(j, nb):
    return jnp.where(j < SRC_AF * nb, j, j + nb)


def _inproj_out_block(j, nb):
    src = _inproj_src_block(j, nb)
    g = src // nb
    g_out = jnp.int32(BLK_AQ)
    for g_src, g_dst in ((SRC_AI, BLK_AI), (SRC_AG, BLK_AG), (SRC_BQ, BLK_BQ), (SRC_BK, BLK_BK), (SRC_BV, BLK_BV)):
        g_out = jnp.where(g == g_src, g_dst, g_out)
    return jnp.where(g < IN_NA, g_out * nb + src % nb, src - IN_NA * nb + BLK_GA * nb)


def _inproj(h, w):
    tm, tn = IN_TM, IN_TN
    nb = WIDTH // tn
    half = SEQ // 2
    n_i = half // tm
    first_gate = IN_NA * nb
    grid = (2, N_PROJ // tn, n_i)
    return pl.pallas_call(
        _inproj_kernel,
        grid=grid,
        in_specs=[
            pl.BlockSpec((half, D_MODEL), lambda s, j, i: (s, 0), pipeline_mode=pl.Buffered(1)),
            pl.BlockSpec((tn, D_MODEL), lambda s, j, i: (_inproj_src_block(j, nb), 0)),
            pl.BlockSpec((tn, D_MODEL), lambda s, j, i: (jnp.maximum(_inproj_src_block(j, nb), first_gate) + 1, 0)),
        ],
        out_specs=pl.BlockSpec((tm, tn), lambda s, j, i: (s * n_i + i, _inproj_out_block(j, nb))),
        out_shape=jax.ShapeDtypeStruct((SEQ, N_PROJ), BF16),
        scratch_shapes=[pltpu.VMEM((D_MODEL, tn), BF16)],
        compiler_params=_params("arbitrary", "arbitrary", "arbitrary"),
        name="inproj",
    )(h, w, w)


def _hgrn_constants(c):
    n_lvl = int(np.log2(c))
    t = np.arange(c)[:, None]
    j = np.arange(c)[None, :]
    blocks = [(j <= t), (j > t)]
    level = np.full((c, c), -1, np.int32)
    level[np.arange(c), np.arange(c)] = 0
    for l in range(n_lvl):
        b = 2 << l
        mid = (t // b) * b + b // 2 - 1
        second = (t % b) >= b // 2
        m = np.where(second, (j > mid) & (j <= t), (j > t) & (j <= mid))
        blocks.append(m)
        s = np.arange(c)[None, :]
        own = (t // b == s // b) & second & ((s % b) < b // 2)
        level[own] = l + 1
    sums = np.concatenate(blocks, axis=0).astype(np.float32)
    sums2 = np.concatenate([sums, sums], axis=1)
    return jnp.asarray(sums2, BF16), jnp.asarray(level), n_lvl


def _hgrn_kernel(q_ref, z_ref, v_ref, g_ref, lbl_ref, nw_ref, sums_ref, lvl_ref, y_ref, st_ref, *,
                 c, n_lvl, n_sub):
    n_heads = st_ref.shape[0]
    d = HEAD_DIM

    @pl.when(pl.program_id(1) == 0)
    def _():
        st_ref[...] = jnp.zeros_like(st_ref)

    logits = lbl_ref[...]
    ex = jnp.exp(logits - jnp.max(logits, axis=0, keepdims=True))
    lb = ex[0:1, :] / jnp.sum(ex, axis=0, keepdims=True)
    one_m_lb = 1.0 - lb
    nw = nw_ref[...]
    sums = sums_ref[...]
    lvl = lvl_ref[...]
    n_chunks = q_ref.shape[0] // c

    heads = [slice(hh * d, (hh + 1) * d) for hh in range(n_heads)]

    def chunks(ci, carry):
        rows = [pl.ds(pl.multiple_of((ci * n_sub + s) * c, c), c) for s in range(n_sub)]
        k_all, dec_all = [], []
        for r in rows:
            z = z_ref[r, :]
            e = jnp.exp(-jnp.abs(z))
            inv = 1.0 / (1.0 + e)
            pos = z >= 0.0
            sig = jnp.where(pos, inv, e * inv)
            sig_n = jnp.where(pos, e * inv, inv)
            g = jnp.log2(lb + one_m_lb * sig)
            k_all.append(one_m_lb * sig_n)
            g_hi = g.astype(BF16)
            g_lo = (g - g_hi.astype(F32)).astype(BF16)
            expo = jnp.dot(sums, jnp.concatenate([g_hi, g_lo], axis=0), preferred_element_type=F32)
            dec_all.append(jnp.exp2(expo))
        scores = {}
        for s, r in enumerate(rows):
            for hh, cols in enumerate(heads):
                q = q_ref[r, cols].astype(F32)
                k = k_all[s][:, cols]
                sc = jnp.where(lvl == 0, _dot_nt(q.astype(BF16), k.astype(BF16)), 0.0)
                for l in range(n_lvl):
                    d_l = dec_all[s][(2 + l) * c:(3 + l) * c, cols]
                    s_l = _dot_nt((q * d_l).astype(BF16), (k * d_l).astype(BF16))
                    sc = jnp.where(lvl == l + 1, s_l, sc)
                scores[s, hh] = sc.astype(BF16)
        o_intra, inc, q_in = {}, {}, {}
        for s, r in enumerate(rows):
            for hh, cols in enumerate(heads):
                v = v_ref[r, cols]
                o_intra[s, hh] = jnp.dot(scores[s, hh], v, preferred_element_type=F32)
                k_out = (k_all[s][:, cols] * dec_all[s][c:2 * c, cols]).astype(BF16)
                inc[s, hh] = _dot_tn(v, k_out)
                q_in[s, hh] = (q_ref[r, cols].astype(F32) * dec_all[s][0:c, cols]).astype(BF16)
        for hh, cols in enumerate(heads):
            st = st_ref[hh]
            for s, r in enumerate(rows):
                o = o_intra[s, hh] + _dot_nt(q_in[s, hh], st.astype(BF16))
                st = st * dec_all[s][c - 1:c, cols] + inc[s, hh]
                o = o * lax.rsqrt(jnp.mean(o * o, axis=-1, keepdims=True) + RMS_EPS) * nw
                gt = g_ref[r, cols].astype(F32)
                y_ref[r, cols] = (o * gt * _sigmoid(gt)).astype(y_ref.dtype)
            st_ref[hh] = st
        return carry

    lax.fori_loop(0, n_chunks // n_sub, chunks, 0, unroll=HG_UNROLL)


def _hgrn(proj, a_f, lb_logits, norm_w):
    ts, c, hg = HG_TS, HG_C, HG_HEADS
    sums, lvl, n_lvl = _hgrn_constants(c)
    w = hg * HEAD_DIM
    per_blk = WIDTH // w
    grid = (N_HEADS // hg, SEQ // ts)
    return pl.pallas_call(
        functools.partial(_hgrn_kernel, c=c, n_lvl=n_lvl, n_sub=HG_SUB),
        grid=grid,
        in_specs=[
            pl.BlockSpec((ts, w), lambda h, i: (i, BLK_AQ * per_blk + h)),
            pl.BlockSpec((ts, w), lambda h, i: (i, h)),
            pl.BlockSpec((ts, w), lambda h, i: (i, BLK_AI * per_blk + h)),
            pl.BlockSpec((ts, w), lambda h, i: (i, BLK_AG * per_blk + h)),
            pl.BlockSpec((lb_logits.shape[0], w), lambda h, i: (0, h)),
            pl.BlockSpec((1, HEAD_DIM), lambda h, i: (0, 0)),
            pl.BlockSpec(sums.shape, lambda h, i: (0, 0)),
            pl.BlockSpec(lvl.shape, lambda h, i: (0, 0)),
        ],
        out_specs=pl.BlockSpec((ts, w), lambda h, i: (i, h)),
        out_shape=jax.ShapeDtypeStruct((SEQ, WIDTH), BF16),
        scratch_shapes=[pltpu.VMEM((hg, HEAD_DIM, HEAD_DIM), F32)],
        compiler_params=_params("arbitrary", "arbitrary"),
        name="hgrn",
    )(proj, a_f, proj, proj, lb_logits, norm_w, sums, lvl)


def _fox_kernel(q_ref, k_ref, v_ref, cx_ref, nc_ref, o_ref, vt_ref, nrm_ref, m_ref, acc_ref, *, tq, tk):
    g = pl.program_id(0)
    i = pl.program_id(1)
    n_heads = vt_ref.shape[0]
    d = HEAD_DIM
    s_len = k_ref.shape[0]
    n_sub = tq // tk
    n_kt = s_len // tk
    vrows = vt_ref.shape[2]
    heads = [(hh, slice(hh * d, (hh + 1) * d)) for hh in range(n_heads)]

    @pl.when(i == 0)
    def _():
        ones_row = jnp.where(lax.broadcasted_iota(jnp.int32, (vrows - d, tk), 0) == 0, 1.0, 0.0)
        wg = n_heads * d
        same_head = (lax.broadcasted_iota(jnp.int32, (wg, wg), 0) // d
                     == lax.broadcasted_iota(jnp.int32, (wg, wg), 1) // d)
        ones_blk = jnp.where(same_head, 1.0, 0.0).astype(BF16)

        def sq_norms(x):
            xf = x.astype(F32)
            return jnp.dot((xf * xf).astype(BF16), ones_blk, preferred_element_type=F32)

        kn = jnp.zeros((tk, wg), F32)
        qn = jnp.zeros((tk, wg), F32)
        for r in range(n_kt):
            rows = slice(r * tk, (r + 1) * tk)
            for hh, cols in heads:
                vt_ref[hh, r, 0:d, :] = v_ref[rows, cols].astype(F32).T.astype(BF16)
                vt_ref[hh, r, d:vrows, :] = ones_row.astype(BF16)
            kn = jnp.maximum(kn, sq_norms(k_ref[rows, :]))
            qn = jnp.maximum(qn, sq_norms(q_ref[rows, :]))
        nrm = 2.0 * FOX_NORM_MARGIN * jnp.sqrt(jnp.max(qn, axis=0, keepdims=True)
                                               * jnp.max(kn, axis=0, keepdims=True))
        for hh, cols in heads:
            nrm_ref[hh] = jnp.broadcast_to(nrm[:, cols], nrm_ref.shape[1:])

    q_rows = pl.ds(pl.multiple_of(i * tq, tq), tq)
    nc_q0 = nc_ref[pl.ds(pl.multiple_of(i * tq, tq), 1), :]
    ends = nc_ref[pl.ds(tk - 1, n_kt, stride=tk), :]
    lane = lax.broadcasted_iota(jnp.int32, ends.shape, 1)
    rsel = lax.broadcasted_iota(jnp.int32, (d, tq), 0)
    qa, r_lo = [], []
    for hh, cols in heads:
        h = g * n_heads + hh
        thr = nc_q0 - nrm_ref[hh, 0:1, :] - FOX_SKIP_LOG2
        r_lo.append(jnp.sum(jnp.where((ends < thr) & (lane == h), 1, 0)))
        sel = jnp.where((rsel < 3 * N_HEADS) & ((rsel & (N_HEADS - 1)) == h), 1.0, 0.0).astype(BF16)
        qa.append(jnp.concatenate([q_ref[q_rows, cols].astype(F32).T.astype(BF16), sel], axis=0))

    def scores(hh, cols, r, mask_off):
        ks = pl.ds(pl.multiple_of(r * tk, tk), tk)
        ka = jnp.concatenate([k_ref[ks, cols], cx_ref[ks, :]], axis=1)
        s = jnp.dot(ka, qa[hh], preferred_element_type=F32)
        if mask_off is not None:
            kid = lax.broadcasted_iota(jnp.int32, s.shape, 0) + mask_off
            qid = lax.broadcasted_iota(jnp.int32, s.shape, 1)
            s = jnp.where(kid <= qid, s, -jnp.inf)
        return s

    def first_step(tiles):
        ss = [[scores(hh, cols, r, off) for r, off in tiles] for hh, cols in heads]
        ms = []
        for hh, _ in heads:
            m = jnp.max(ss[hh][0], axis=0, keepdims=True)
            for s in ss[hh][1:]:
                m = jnp.maximum(m, jnp.max(s, axis=0, keepdims=True))
            ms.append(m)
        for hh, _ in heads:
            pv = None
            for (r, _), s in zip(tiles, ss[hh]):
                p = jnp.exp2(s - ms[hh]).astype(BF16)
                dd = jnp.dot(vt_ref[hh, r], p, preferred_element_type=F32)
                pv = dd if pv is None else pv + dd
            acc_ref[hh] = pv
            m_ref[hh] = ms[hh]

    def later_step(hh, cols, r):
        s = scores(hh, cols, r, None)
        m_prev = m_ref[hh]
        m_new = jnp.maximum(m_prev, jnp.max(s, axis=0, keepdims=True))
        pv = jnp.dot(vt_ref[hh, r], jnp.exp2(s - m_new).astype(BF16), preferred_element_type=F32)
        acc_ref[hh] = jnp.exp2(m_prev - m_new) * acc_ref[hh] + pv
        m_ref[hh] = m_new

    diag = [(i * n_sub + rr, rr * tk) for rr in range(n_sub)]

    @pl.when(i == 0)
    def _():
        first_step(diag)

    @pl.when(i > 0)
    def _():
        first_step([(i * n_sub - 1, None)] + diag)
        for hh, cols in heads:
            def off_diag(r, carry, hh=hh, cols=cols):
                later_step(hh, cols, r)
                return carry
            lax.fori_loop(r_lo[hh], i * n_sub - 1, off_diag, 0)

    for hh, cols in heads:
        acc = acc_ref[hh]
        o_ref[:, cols] = (acc[0:d] / acc[d:d + 1]).T.astype(o_ref.dtype)


def _fox(proj, cx, nc):
    tq, tk, hg = FOX_TQ, FOX_TK, FOX_HEADS
    w = hg * HEAD_DIM
    per_blk = WIDTH // w
    vrows = HEAD_DIM + 16
    return pl.pallas_call(
        functools.partial(_fox_kernel, tq=tq, tk=tk),
        grid=(N_HEADS // hg, SEQ // tq),
        in_specs=[
            pl.BlockSpec((SEQ, w), lambda g, i: (0, BLK_BQ * per_blk + g)),
            pl.BlockSpec((SEQ, w), lambda g, i: (0, BLK_BK * per_blk + g)),
            pl.BlockSpec((SEQ, w), lambda g, i: (0, BLK_BV * per_blk + g)),
            pl.BlockSpec((SEQ, 128), lambda g, i: (0, 0)),
            pl.BlockSpec((SEQ, 128), lambda g, i: (0, 0)),
        ],
        out_specs=pl.BlockSpec((tq, w), lambda g, i: (i, g)),
        out_shape=jax.ShapeDtypeStruct((SEQ, WIDTH), BF16),
        scratch_shapes=[
            pltpu.VMEM((hg, SEQ // tk, vrows, tk), BF16),
            pltpu.VMEM((hg, 8, 128), F32),
            pltpu.VMEM((hg, 1, tq), F32),
            pltpu.VMEM((hg, vrows, tq), F32),
        ],
        compiler_params=_params("arbitrary", "arbitrary"),
        name="fox",
    )(proj, proj, proj, cx, nc)


def _merge_kernel(oa_ref, ob_ref, ga_ref, gb_ref, wa_ref, wb_ref, wo_ref, x_ref, nw_ref, npre_ref,
                  out_ref, h2_ref):
    ya = jnp.dot(oa_ref[...], wa_ref[...], preferred_element_type=F32)
    yb = jnp.dot(ob_ref[...], wb_ref[...], preferred_element_type=F32)
    merged = _sigmoid(ga_ref[...].astype(F32)) * ya + _sigmoid(gb_ref[...].astype(F32)) * yb
    u = jnp.dot(merged.astype(BF16), wo_ref[...], preferred_element_type=F32)
    u = u * lax.rsqrt(jnp.mean(u * u, axis=-1, keepdims=True) + RMS_EPS) * nw_ref[...]
    x1 = x_ref[...] + u
    out_ref[...] = x1
    ms = jnp.mean(x1 * x1, axis=-1, keepdims=True)
    h2_ref[...] = (x1 * lax.rsqrt(ms + RMS_EPS) * npre_ref[...]).astype(BF16)


def _merge(y_a, y_b, proj, w_up_a, w_up_b, w_o, x2, norm_w, norm_ffn_pre):
    tm = MERGE_TM
    once = pl.Buffered(1)
    return pl.pallas_call(
        _merge_kernel,
        grid=(SEQ // tm,),
        in_specs=[
            pl.BlockSpec((tm, WIDTH), lambda i: (i, 0)),
            pl.BlockSpec((tm, WIDTH), lambda i: (i, 0)),
            pl.BlockSpec((tm, D_MODEL), lambda i: (i, BLK_GA // 2)),
            pl.BlockSpec((tm, D_MODEL), lambda i: (i, BLK_GB // 2)),
            pl.BlockSpec((WIDTH, D_MODEL), lambda i: (0, 0), pipeline_mode=once),
            pl.BlockSpec((WIDTH, D_MODEL), lambda i: (0, 0), pipeline_mode=once),
            pl.BlockSpec((D_MODEL, D_MODEL), lambda i: (0, 0), pipeline_mode=once),
            pl.BlockSpec((tm, D_MODEL), lambda i: (i, 0)),
            pl.BlockSpec((1, D_MODEL), lambda i: (0, 0)),
            pl.BlockSpec((1, D_MODEL), lambda i: (0, 0)),
        ],
        out_specs=[
            pl.BlockSpec((tm, D_MODEL), lambda i: (i, 0)),
            pl.BlockSpec((tm, D_MODEL), lambda i: (i, 0)),
        ],
        out_shape=[
            jax.ShapeDtypeStruct((SEQ, D_MODEL), F32),
            jax.ShapeDtypeStruct((SEQ, D_MODEL), BF16),
        ],
        compiler_params=_params("arbitrary"),
        name="merge",
    )(y_a, y_b, proj, proj, w_up_a, w_up_b, w_o, x2, norm_w, norm_ffn_pre)


def _ffn_residual_copy(x_hbm, xres_ref, sem, i):
    tm = xres_ref.shape[0]
    return pltpu.make_async_copy(x_hbm.at[pl.ds(pl.multiple_of(i * tm, tm), tm), :], xres_ref, sem)


def _ffn_weight_copies(win_hbm, wdn_hbm, wg_buf, wu_buf, wd_buf, sems, f, slot):
    tf = wg_buf.shape[2]
    off = pl.multiple_of(f * tf, tf)
    return (
        pltpu.make_async_copy(win_hbm.at[:, pl.ds(off, tf)], wg_buf.at[slot], sems.at[slot, 0]),
        pltpu.make_async_copy(win_hbm.at[:, pl.ds(D_FF + off, tf)], wu_buf.at[slot], sems.at[slot, 1]),
        pltpu.make_async_copy(wdn_hbm.at[pl.ds(off, tf), :], wd_buf.at[slot], sems.at[slot, 2]),
    )


def _ffn_kernel(h_ref, x_hbm, npost_ref, win_hbm, wdn_hbm, out_ref, xres_ref, wg_buf, wu_buf, wd_buf,
                sem_x, sems):
    i = pl.program_id(0)
    n_i = pl.num_programs(0)
    nf = D_FF // wg_buf.shape[2]
    assert nf % 2 == 0
    copies = functools.partial(_ffn_weight_copies, win_hbm, wdn_hbm, wg_buf, wu_buf, wd_buf, sems)

    @pl.when(i == 0)
    def _():
        for cp in copies(0, 0):
            cp.start()

    _ffn_residual_copy(x_hbm, xres_ref, sem_x, i).start()
    out_ref[...] = jnp.zeros_like(out_ref)

    def block_pair(j, carry):
        for slot in (0, 1):
            f = 2 * j + slot
            nxt = jnp.where(f + 1 < nf, f + 1, 0)

            @pl.when((f + 1 < nf) | (i + 1 < n_i))
            def _():
                for cp in copies(nxt, 1 - slot):
                    cp.start()

            for cp in copies(f, slot):
                cp.wait()
            h = h_ref[...]
            gate = jnp.dot(h, wg_buf[slot].astype(BF16), preferred_element_type=F32)
            up = jnp.dot(h, wu_buf[slot].astype(BF16), preferred_element_type=F32)
            act = (gate * _sigmoid(gate) * up).astype(BF16)
            out_ref[...] += jnp.dot(act, wd_buf[slot].astype(BF16), preferred_element_type=F32)
        return carry

    lax.fori_loop(0, nf // 2, block_pair, 0)

    _ffn_residual_copy(x_hbm, xres_ref, sem_x, i).wait()
    u = out_ref[...]
    u = u * lax.rsqrt(jnp.mean(u * u, axis=-1, keepdims=True) + RMS_EPS) * npost_ref[...]
    out_ref[...] = xres_ref[...] + u


def _ffn(h2, x1, norm_post, w_in, w_down):
    tm, tf = FFN_TM, FFN_TF
    return pl.pallas_call(
        _ffn_kernel,
        grid=(SEQ // tm,),
        in_specs=[
            pl.BlockSpec((tm, D_MODEL), lambda i: (i, 0)),
            pl.BlockSpec(memory_space=pl.ANY),
            pl.BlockSpec((1, D_MODEL), lambda i: (0, 0)),
            pl.BlockSpec(memory_space=pl.ANY),
            pl.BlockSpec(memory_space=pl.ANY),
        ],
        out_specs=pl.BlockSpec((tm, D_MODEL), lambda i: (i, 0)),
        out_shape=jax.ShapeDtypeStruct((SEQ, D_MODEL), F32),
        scratch_shapes=[
            pltpu.VMEM((tm, D_MODEL), F32),
            pltpu.VMEM((2, D_MODEL, tf), F32),
            pltpu.VMEM((2, D_MODEL, tf), F32),
            pltpu.VMEM((2, tf, D_MODEL), F32),
            pltpu.SemaphoreType.DMA(()),
            pltpu.SemaphoreType.DMA((2, 3)),
        ],
        compiler_params=_params("arbitrary"),
        name="ffn",
    )(h2, x1, norm_post, w_in, w_down)


def kernel(x, w_in, b_fox_f, hgrn_lb_logits, hgrn_norm_w, w_up_a, w_up_b, w_o, norm_mix_pre,
           norm_mix_post, norm_ffn_pre, norm_ffn_post, w_ffn_in, w_ffn_down):
    assert x.shape == (1, SEQ, D_MODEL) and w_in.shape[0] == 1
    w_t = w_in[0].T
    x2 = x[0]

    h, a_f, cx, nc = _prenorm(x2, norm_mix_pre, w_t, b_fox_f.reshape(N_HEADS, 1))
    proj = _inproj(h, w_t)
    y_a = _hgrn(proj, a_f, hgrn_lb_logits, hgrn_norm_w)
    y_b = _fox(proj, cx, nc)
    x1, h2 = _merge(y_a, y_b, proj, w_up_a[0].astype(BF16), w_up_b[0].astype(BF16), w_o[0].astype(BF16),
                    x2, norm_mix_post, norm_ffn_pre)
    out = _ffn(h2, x1, norm_ffn_post, w_ffn_in[0], w_ffn_down[0])
    return out[None]
```

```python
import functools

import numpy as np
import jax
import jax.numpy as jnp
from jax import lax
from jax.experimental import pallas as pl
from jax.experimental.pallas import tpu as pltpu

F32 = jnp.float32
BF16 = jnp.bfloat16

D_MODEL = 2048
SEQ = 8192
HEAD_DIM = 128
N_HEADS = 8
WIDTH = N_HEADS * HEAD_DIM
D_FF = 5632
RMS_EPS = 1e-6
N_PROJ = 3 * WIDTH + 2 * D_MODEL + 3 * WIDTH

VMEM_LIMIT_BYTES = 56 * 1024 * 1024

SRC_AQ, SRC_AF, SRC_AI, SRC_AG, SRC_BQ, SRC_BK, SRC_BV = 0, 1, 2, 3, 4, 5, 6
IN_NA = 7
BLK_AQ, BLK_AI, BLK_GA, BLK_GB, BLK_AG, BLK_BQ, BLK_BK, BLK_BV = 0, 1, 2, 4, 6, 7, 8, 9
HEADS_PER_BLK = WIDTH // HEAD_DIM

PRE_TM = 512
IN_TM, IN_TN = 2048, 512
HG_TS, HG_C, HG_HEADS = 1024, 64, 8
HG_FAST_LOG2 = 96.0
FOX_TQ, FOX_TK, FOX_HEADS = 512, 256, 2
LOG2E = 1.4426950408889634
FOX_SKIP_LOG2 = 160.0
FOX_NORM_MARGIN = 1.01
MERGE_TM = 256
FFN_TM, FFN_TF = 1024, 256


def _params(*sem):
    return pltpu.CompilerParams(dimension_semantics=sem, vmem_limit_bytes=VMEM_LIMIT_BYTES)


def _dot_nt(a, b):
    return lax.dot_general(a, b, (((1,), (1,)), ((), ())), preferred_element_type=F32)


def _dot_tn(a, b):
    return lax.dot_general(a, b, (((0,), (0,)), ((), ())), preferred_element_type=F32)


def _log_sigmoid(x):
    return jnp.minimum(x, 0.0) - jnp.log(1.0 + jnp.exp(-jnp.abs(x)))


def _sigmoid(x):
    return 1.0 / (1.0 + jnp.exp(-x))


def _split3(x):
    p0 = x.astype(BF16)
    r1 = x - p0.astype(F32)
    p1 = r1.astype(BF16)
    p2 = (r1 - p1.astype(F32)).astype(BF16)
    return p0, p1, p2


def _prenorm_kernel(x_ref, nw_ref, waf_ref, wf_ref, bf_ref, tri_ref, h_ref, af_ref, cx_ref, nc_ref,
                    wafb_ref, carry_ref):
    i = pl.program_id(0)
    tm = x_ref.shape[0]

    @pl.when(i == 0)
    def _():
        wafb_ref[...] = waf_ref[...].T.astype(BF16)
        carry_ref[...] = jnp.zeros_like(carry_ref)

    x = x_ref[...]
    ms = jnp.mean(x * x, axis=-1, keepdims=True)
    h_ref[...] = (x * lax.rsqrt(ms + RMS_EPS) * nw_ref[...]).astype(BF16)
    hb = h_ref[...]
    af_ref[...] = jnp.dot(hb, wafb_ref[...], preferred_element_type=F32)
    logit = _dot_nt(wf_ref[...].astype(BF16), hb) + bf_ref[...]
    nls = _log_sigmoid(logit) * (-LOG2E)
    tri = tri_ref[...]
    loc = sum(jnp.dot(p, tri, preferred_element_type=F32) for p in _split3(nls))
    nc = loc + carry_ref[:, 0:1]
    carry_ref[...] = jnp.broadcast_to(nc[:, tm - 1:tm], carry_ref.shape)
    pad = jnp.zeros((128 - N_HEADS, tm), F32)
    nc_ref[...] = jnp.concatenate([nc, pad], axis=0).T
    parts = [p.astype(F32) for p in _split3(nc)]
    pad3 = jnp.zeros((128 - 3 * N_HEADS, tm), F32)
    cx_ref[...] = jnp.concatenate(parts + [pad3], axis=0).T.astype(BF16)


def _prenorm(x2, norm_w, w, bf_col):
    tm = PRE_TM
    tri = jnp.asarray(np.triu(np.ones((tm, tm), np.float32)), BF16)
    once = pl.Buffered(1)
    return pl.pallas_call(
        _prenorm_kernel,
        grid=(SEQ // tm,),
        in_specs=[
            pl.BlockSpec((tm, D_MODEL), lambda i: (i, 0)),
            pl.BlockSpec((1, D_MODEL), lambda i: (0, 0)),
            pl.BlockSpec((WIDTH, D_MODEL), lambda i: (SRC_AF, 0), pipeline_mode=once),
            pl.BlockSpec((N_HEADS, D_MODEL), lambda i: (IN_NA * WIDTH // N_HEADS, 0), pipeline_mode=once),
            pl.BlockSpec((N_HEADS, 1), lambda i: (0, 0)),
            pl.BlockSpec((tm, tm), lambda i: (0, 0), pipeline_mode=once),
        ],
        out_specs=[
            pl.BlockSpec((tm, D_MODEL), lambda i: (i, 0)),
            pl.BlockSpec((tm, WIDTH), lambda i: (i, 0)),
            pl.BlockSpec((tm, 128), lambda i: (i, 0)),
            pl.BlockSpec((tm, 128), lambda i: (i, 0)),
        ],
        out_shape=[
            jax.ShapeDtypeStruct((SEQ, D_MODEL), BF16),
            jax.ShapeDtypeStruct((SEQ, WIDTH), F32),
            jax.ShapeDtypeStruct((SEQ, 128), BF16),
            jax.ShapeDtypeStruct((SEQ, 128), F32),
        ],
        scratch_shapes=[pltpu.VMEM((D_MODEL, WIDTH), BF16), pltpu.VMEM((N_HEADS, 128), F32)],
        compiler_params=_params("arbitrary"),
        name="prenorm",
    )(x2, norm_w, w, w, bf_col, tri)


def _inproj_kernel(h_ref, wa_ref, wb_ref, proj_ref, wc_ref):
    j = pl.program_id(1)
    i = pl.program_id(2)
    tm, tn = proj_ref.shape
    nb = WIDTH // tn
    src = _inproj_src_block(j, nb)

    @pl.when(i == 0)
    def _():
        @pl.when(src < IN_NA * nb)
        def _():
            scale = jnp.where(src // nb == SRC_BQ, LOG2E * HEAD_DIM ** -0.5, 1.0)
            wc_ref[...] = (wa_ref[...] * scale).T.astype(BF16)

        @pl.when(src >= IN_NA * nb)
        def _():
            wc_ref[...] = jnp.concatenate([wa_ref[N_HEADS:, :], wb_ref[:N_HEADS, :]], axis=0).T.astype(BF16)

    rows = pl.ds(pl.multiple_of(i * tm, tm), tm)
    proj_ref[...] = jnp.dot(h_ref[rows, :], wc_ref[...], preferred_element_type=F32).astype(BF16)


def _inproj_src_block(j, nb):
    return jnp.where(j < SRC_AF * nb, j, j + nb)


def _inproj_out_block(j, nb):
    src = _inproj_src_block(j, nb)
    g = src // nb
    g_out = jnp.int32(BLK_AQ)
    for g_src, g_dst in ((SRC_AI, BLK_AI), (SRC_AG, BLK_AG), (SRC_BQ, BLK_BQ), (SRC_BK, BLK_BK), (SRC_BV, BLK_BV)):
        g_out = jnp.where(g == g_src, g_dst, g_out)
    return jnp.where(g < IN_NA, g_out * nb + src % nb, src - IN_NA * nb + BLK_GA * nb)


def _inproj(h, w):
    tm, tn = IN_TM, IN_TN
    nb = WIDTH // tn
    half = SEQ // 2
    n_i = half // tm
    first_gate = IN_NA * nb
    grid = (2, N_PROJ // tn, n_i)
    return pl.pallas_call(
        _inproj_kernel,
        grid=grid,
        in_specs=[
            pl.BlockSpec((half, D_MODEL), lambda s, j, i: (s, 0), pipeline_mode=pl.Buffered(1)),
            pl.BlockSpec((tn, D_MODEL), lambda s, j, i: (_inproj_src_block(j, nb), 0)),
            pl.BlockSpec((tn, D_MODEL), lambda s, j, i: (jnp.maximum(_inproj_src_block(j, nb), first_gate) + 1, 0)),
        ],
        out_specs=pl.BlockSpec((tm, tn), lambda s, j, i: (s * n_i + i, _inproj_out_block(j, nb))),
        out_shape=jax.ShapeDtypeStruct((SEQ, N_PROJ), BF16),
        scratch_shapes=[pltpu.VMEM((D_MODEL, tn), BF16)],
        compiler_params=_params("arbitrary", "arbitrary", "arbitrary"),
        name="inproj",
    )(h, w, w)


def _hgrn_constants(c):
    n_lvl = int(np.log2(c))
    t = np.arange(c)[:, None]
    j = np.arange(c)[None, :]
    blocks = [(j <= t), (j > t)]
    level = np.full((c, c), -1, np.int32)
    level[np.arange(c), np.arange(c)] = 0
    for l in range(n_lvl):
        b = 2 << l
        mid = (t // b) * b + b // 2 - 1
        second = (t % b) >= b // 2
        m = np.where(second, (j > mid) & (j <= t), (j > t) & (j <= mid))
        blocks.append(m)
        s = np.arange(c)[None, :]
        own = (t // b == s // b) & second & ((s % b) < b // 2)
        level[own] = l + 1
    sums = np.concatenate(blocks, axis=0).astype(np.float32)
    sums2 = np.concatenate([sums, sums], axis=1)
    return jnp.asarray(sums2, BF16), jnp.asarray(level), n_lvl


def _hgrn_kernel(q_ref, z_ref, v_ref, g_ref, lbl_ref, nw_ref, sums_ref, lvl_ref, y_ref, st_ref, *,
                 c, n_lvl):
    n_heads = st_ref.shape[0]
    d = HEAD_DIM

    @pl.when(pl.program_id(1) == 0)
    def _():
        st_ref[...] = jnp.zeros_like(st_ref)

    logits = lbl_ref[...]
    ex = jnp.exp(logits - jnp.max(logits, axis=0, keepdims=True))
    lb = ex[0:1, :] / jnp.sum(ex, axis=0, keepdims=True)
    one_m_lb = 1.0 - lb
    nw = nw_ref[...]
    sums = sums_ref[...]
    lvl = lvl_ref[...]
    n_chunks = q_ref.shape[0] // c

    heads = [slice(hh * d, (hh + 1) * d) for hh in range(n_heads)]
    causal = lvl >= 0

    def finish(r, scores, q_in, k_all, dec_b, dec_r):
        o_intra, inc = [], []
        for hh, cols in enumerate(heads):
            v = v_ref[r, cols]
            o_intra.append(jnp.dot(scores[hh], v, preferred_element_type=F32))
            k_out = (k_all[:, cols] * dec_r[:, cols]).astype(BF16)
            inc.append(_dot_tn(v, k_out))
        for hh, cols in enumerate(heads):
            st = st_ref[hh]
            o = o_intra[hh] + _dot_nt(q_in[hh], st.astype(BF16))
            st_ref[hh] = st * dec_b[c - 1:c, cols] + inc[hh]
            o = o * lax.rsqrt(jnp.mean(o * o, axis=-1, keepdims=True) + RMS_EPS) * nw
            gt = g_ref[r, cols].astype(F32)
            y_ref[r, cols] = (o * gt * _sigmoid(gt)).astype(y_ref.dtype)

    def chunk(ci, carry):
        r = pl.ds(pl.multiple_of(ci * c, c), c)
        z = z_ref[r, :]
        e = jnp.exp(-jnp.abs(z))
        inv = 1.0 / (1.0 + e)
        pos = z >= 0.0
        sig = jnp.where(pos, inv, e * inv)
        sig_n = jnp.where(pos, e * inv, inv)
        g = jnp.log2(lb + one_m_lb * sig)
        k_all = one_m_lb * sig_n
        g_hi = g.astype(BF16)
        g_lo = (g - g_hi.astype(F32)).astype(BF16)
        g2 = jnp.concatenate([g_hi, g_lo], axis=0)
        e01 = jnp.dot(sums[0:2 * c], g2, preferred_element_type=F32)
        b = e01[0:c]
        dec_b = jnp.exp2(b)
        dec_r = jnp.exp2(e01[c:2 * c])
        q_in = [(q_ref[r, cols].astype(F32) * dec_b[:, cols]).astype(BF16) for cols in heads]
        b_min = jnp.min(b)

        @pl.when(b_min >= -HG_FAST_LOG2)
        def _():
            grow = jnp.exp2(-b)
            scores = []
            for hh, cols in enumerate(heads):
                k_up = (k_all[:, cols] * grow[:, cols]).astype(BF16)
                scores.append(jnp.where(causal, _dot_nt(q_in[hh], k_up), 0.0).astype(BF16))
            finish(r, scores, q_in, k_all, dec_b, dec_r)

        @pl.when(b_min < -HG_FAST_LOG2)
        def _():
            dec_l = jnp.exp2(jnp.dot(sums[2 * c:], g2, preferred_element_type=F32))
            scores = []
            for cols in heads:
                q = q_ref[r, cols].astype(F32)
                k = k_all[:, cols]
                sc = jnp.where(lvl == 0, _dot_nt(q.astype(BF16), k.astype(BF16)), 0.0)
                for l in range(n_lvl):
                    d_l = dec_l[l * c:(l + 1) * c, cols]
                    s_l = _dot_nt((q * d_l).astype(BF16), (k * d_l).astype(BF16))
                    sc = jnp.where(lvl == l + 1, s_l, sc)
                scores.append(sc.astype(BF16))
            finish(r, scores, q_in, k_all, dec_b, dec_r)

        return carry

    lax.fori_loop(0, n_chunks, chunk, 0)


def _hgrn(proj, a_f, lb_logits, norm_w):
    ts, c, hg = HG_TS, HG_C, HG_HEADS
    sums, lvl, n_lvl = _hgrn_constants(c)
    w = hg * HEAD_DIM
    per_blk = WIDTH // w
    grid = (N_HEADS // hg, SEQ // ts)
    return pl.pallas_call(
        functools.partial(_hgrn_kernel, c=c, n_lvl=n_lvl),
        grid=grid,
        in_specs=[
            pl.BlockSpec((ts, w), lambda h, i: (i, BLK_AQ * per_blk + h)),
            pl.BlockSpec((ts, w), lambda h, i: (i, h)),
            pl.BlockSpec((ts, w), lambda h, i: (i, BLK_AI * per_blk + h)),
            pl.BlockSpec((ts, w), lambda h, i: (i, BLK_AG * per_blk + h)),
            pl.BlockSpec((lb_logits.shape[0], w), lambda h, i: (0, h)),
            pl.BlockSpec((1, HEAD_DIM), lambda h, i: (0, 0)),
            pl.BlockSpec(sums.shape, lambda h, i: (0, 0)),
            pl.BlockSpec(lvl.shape, lambda h, i: (0, 0)),
        ],
        out_specs=pl.BlockSpec((ts, w), lambda h, i: (i, h)),
        out_shape=jax.ShapeDtypeStruct((SEQ, WIDTH), BF16),
        scratch_shapes=[pltpu.VMEM((hg, HEAD_DIM, HEAD_DIM), F32)],
        compiler_params=_params("arbitrary", "arbitrary"),
        name="hgrn",
    )(proj, a_f, proj, proj, lb_logits, norm_w, sums, lvl)


def _fox_kernel(q_ref, k_ref, v_ref, cx_ref, nc_ref, o_ref, vt_ref, nrm_ref, m_ref, acc_ref, *, tq, tk):
    g = pl.program_id(0)
    i = pl.program_id(1)
    n_heads = vt_ref.shape[0]
    d = HEAD_DIM
    s_len = k_ref.shape[0]
    n_sub = tq // tk
    n_kt = s_len // tk
    vrows = vt_ref.shape[2]
    heads = [(hh, slice(hh * d, (hh + 1) * d)) for hh in range(n_heads)]

    @pl.when(i == 0)
    def _():
        ones_row = jnp.where(lax.broadcasted_iota(jnp.int32, (vrows - d, tk), 0) == 0, 1.0, 0.0)
        wg = n_heads * d
        same_head = (lax.broadcasted_iota(jnp.int32, (wg, wg), 0) // d
                     == lax.broadcasted_iota(jnp.int32, (wg, wg), 1) // d)
        ones_blk = jnp.where(same_head, 1.0, 0.0).astype(BF16)

        def sq_norms(x):
            xf = x.astype(F32)
            return jnp.dot((xf * xf).astype(BF16), ones_blk, preferred_element_type=F32)

        kn = jnp.zeros((tk, wg), F32)
        qn = jnp.zeros((tk, wg), F32)
        for r in range(n_kt):
            rows = slice(r * tk, (r + 1) * tk)
            for hh, cols in heads:
                vt_ref[hh, r, 0:d, :] = v_ref[rows, cols].astype(F32).T.astype(BF16)
                vt_ref[hh, r, d:vrows, :] = ones_row.astype(BF16)
            kn = jnp.maximum(kn, sq_norms(k_ref[rows, :]))
            qn = jnp.maximum(qn, sq_norms(q_ref[rows, :]))
        nrm = 2.0 * FOX_NORM_MARGIN * jnp.sqrt(jnp.max(qn, axis=0, keepdims=True)
                                               * jnp.max(kn, axis=0, keepdims=True))
        for hh, cols in heads:
            nrm_ref[hh] = jnp.broadcast_to(nrm[:, cols], nrm_ref.shape[1:])

    q_rows = pl.ds(pl.multiple_of(i * tq, tq), tq)
    nc_q0 = nc_ref[pl.ds(pl.multiple_of(i * tq, tq), 1), :]
    ends = nc_ref[pl.ds(tk - 1, n_kt, stride=tk), :]
    lane = lax.broadcasted_iota(jnp.int32, ends.shape, 1)
    rsel = lax.broadcasted_iota(jnp.int32, (d, tq), 0)
    qa, r_lo = [], []
    for hh, cols in heads:
        h = g * n_heads + hh
        thr = nc_q0 - nrm_ref[hh, 0:1, :] - FOX_SKIP_LOG2
        r_lo.append(jnp.sum(jnp.where((ends < thr) & (lane == h), 1, 0)))
        sel = jnp.where((rsel < 3 * N_HEADS) & ((rsel & (N_HEADS - 1)) == h), 1.0, 0.0).astype(BF16)
        qa.append(jnp.concatenate([q_ref[q_rows, cols].astype(F32).T.astype(BF16), sel], axis=0))

    def scores(hh, cols, r, mask_off):
        ks = pl.ds(pl.multiple_of(r * tk, tk), tk)
        ka = jnp.concatenate([k_ref[ks, cols], cx_ref[ks, :]], axis=1)
        s = jnp.dot(ka, qa[hh], preferred_element_type=F32)
        if mask_off is not None:
            kid = lax.broadcasted_iota(jnp.int32, s.shape, 0) + mask_off
            qid = lax.broadcasted_iota(jnp.int32, s.shape, 1)
            s = jnp.where(kid <= qid, s, -jnp.inf)
        return s

    def first_step(tiles):
        ss = [[scores(hh, cols, r, off) for r, off in tiles] for hh, cols in heads]
        ms = []
        for hh, _ in heads:
            m = jnp.max(ss[hh][0], axis=0, keepdims=True)
            for s in ss[hh][1:]:
                m = jnp.maximum(m, jnp.max(s, axis=0, keepdims=True))
            ms.append(m)
        for hh, _ in heads:
            pv = None
            for (r, _), s in zip(tiles, ss[hh]):
                p = jnp.exp2(s - ms[hh]).astype(BF16)
                dd = jnp.dot(vt_ref[hh, r], p, preferred_element_type=F32)
                pv = dd if pv is None else pv + dd
            acc_ref[hh] = pv
            m_ref[hh] = ms[hh]

    def later_step(hh, cols, r):
        s = scores(hh, cols, r, None)
        m_prev = m_ref[hh]
        m_new = jnp.maximum(m_prev, jnp.max(s, axis=0, keepdims=True))
        pv = jnp.dot(vt_ref[hh, r], jnp.exp2(s - m_new).astype(BF16), preferred_element_type=F32)
        acc_ref[hh] = jnp.exp2(m_prev - m_new) * acc_ref[hh] + pv
        m_ref[hh] = m_new

    diag = [(i * n_sub + rr, rr * tk) for rr in range(n_sub)]

    @pl.when(i == 0)
    def _():
        first_step(diag)

    @pl.when(i > 0)
    def _():
        first_step([(i * n_sub - 1, None)] + diag)
        for hh, cols in heads:
            def off_diag(r, carry, hh=hh, cols=cols):
                later_step(hh, cols, r)
                return carry
            lax.fori_loop(r_lo[hh], i * n_sub - 1, off_diag, 0)

    for hh, cols in heads:
        acc = acc_ref[hh]
        o_ref[:, cols] = (acc[0:d] / acc[d:d + 1]).T.astype(o_ref.dtype)


def _fox(proj, cx, nc):
    tq, tk, hg = FOX_TQ, FOX_TK, FOX_HEADS
    w = hg * HEAD_DIM
    per_blk = WIDTH // w
    vrows = HEAD_DIM + 16
    return pl.pallas_call(
        functools.partial(_fox_kernel, tq=tq, tk=tk),
        grid=(N_HEADS // hg, SEQ // tq),
        in_specs=[
            pl.BlockSpec((SEQ, w), lambda g, i: (0, BLK_BQ * per_blk + g)),
            pl.BlockSpec((SEQ, w), lambda g, i: (0, BLK_BK * per_blk + g)),
            pl.BlockSpec((SEQ, w), lambda g, i: (0, BLK_BV * per_blk + g)),
            pl.BlockSpec((SEQ, 128), lambda g, i: (0, 0)),
            pl.BlockSpec((SEQ, 128), lambda g, i: (0, 0)),
        ],
        out_specs=pl.BlockSpec((tq, w), lambda g, i: (i, g)),
        out_shape=jax.ShapeDtypeStruct((SEQ, WIDTH), BF16),
        scratch_shapes=[
            pltpu.VMEM((hg, SEQ // tk, vrows, tk), BF16),
            pltpu.VMEM((hg, 8, 128), F32),
            pltpu.VMEM((hg, 1, tq), F32),
            pltpu.VMEM((hg, vrows, tq), F32),
        ],
        compiler_params=_params("arbitrary", "arbitrary"),
        name="fox",
    )(proj, proj, proj, cx, nc)


def _merge_kernel(oa_ref, ob_ref, ga_ref, gb_ref, wa_ref, wb_ref, wo_ref, x_ref, nw_ref, npre_ref,
                  out_ref, h2_ref):
    ya = jnp.dot(oa_ref[...], wa_ref[...], preferred_element_type=F32)
    yb = jnp.dot(ob_ref[...], wb_ref[...], preferred_element_type=F32)
    merged = _sigmoid(ga_ref[...].astype(F32)) * ya + _sigmoid(gb_ref[...].astype(F32)) * yb
    u = jnp.dot(merged.astype(BF16), wo_ref[...], preferred_element_type=F32)
    u = u * lax.rsqrt(jnp.mean(u * u, axis=-1, keepdims=True) + RMS_EPS) * nw_ref[...]
    x1 = x_ref[...] + u
    out_ref[...] = x1
    ms = jnp.mean(x1 * x1, axis=-1, keepdims=True)
    h2_ref[...] = (x1 * lax.rsqrt(ms + RMS_EPS) * npre_ref[...]).astype(BF16)


def _merge(y_a, y_b, proj, w_up_a, w_up_b, w_o, x2, norm_w, norm_ffn_pre):
    tm = MERGE_TM
    once = pl.Buffered(1)
    return pl.pallas_call(
        _merge_kernel,
        grid=(SEQ // tm,),
        in_specs=[
            pl.BlockSpec((tm, WIDTH), lambda i: (i, 0)),
            pl.BlockSpec((tm, WIDTH), lambda i: (i, 0)),
            pl.BlockSpec((tm, D_MODEL), lambda i: (i, BLK_GA // 2)),
            pl.BlockSpec((tm, D_MODEL), lambda i: (i, BLK_GB // 2)),
            pl.BlockSpec((WIDTH, D_MODEL), lambda i: (0, 0), pipeline_mode=once),
            pl.BlockSpec((WIDTH, D_MODEL), lambda i: (0, 0), pipeline_mode=once),
            pl.BlockSpec((D_MODEL, D_MODEL), lambda i: (0, 0), pipeline_mode=once),
            pl.BlockSpec((tm, D_MODEL), lambda i: (i, 0)),
            pl.BlockSpec((1, D_MODEL), lambda i: (0, 0)),
            pl.BlockSpec((1, D_MODEL), lambda i: (0, 0)),
        ],
        out_specs=[
            pl.BlockSpec((tm, D_MODEL), lambda i: (i, 0)),
            pl.BlockSpec((tm, D_MODEL), lambda i: (i, 0)),
        ],
        out_shape=[
            jax.ShapeDtypeStruct((SEQ, D_MODEL), F32),
            jax.ShapeDtypeStruct((SEQ, D_MODEL), BF16),
        ],
        compiler_params=_params("arbitrary"),
        name="merge",
    )(y_a, y_b, proj, proj, w_up_a, w_up_b, w_o, x2, norm_w, norm_ffn_pre)


def _ffn_residual_copy(x_hbm, xres_ref, sem, i):
    tm = xres_ref.shape[0]
    return pltpu.make_async_copy(x_hbm.at[pl.ds(pl.multiple_of(i * tm, tm), tm), :], xres_ref, sem)


def _ffn_kernel(h_ref, x_hbm, npost_ref, wg_ref, wu_ref, wd_ref, out_ref, xres_ref, sem):
    i = pl.program_id(0)
    f = pl.program_id(1)

    @pl.when(f == 0)
    def _():
        _ffn_residual_copy(x_hbm, xres_ref, sem, i).start()
        out_ref[...] = jnp.zeros_like(out_ref)

    h = h_ref[...]
    gate = jnp.dot(h, wg_ref[...].astype(BF16), preferred_element_type=F32)
    up = jnp.dot(h, wu_ref[...].astype(BF16), preferred_element_type=F32)
    act = (gate * _sigmoid(gate) * up).astype(BF16)
    out_ref[...] += jnp.dot(act, wd_ref[...].astype(BF16), preferred_element_type=F32)

    @pl.when(f == pl.num_programs(1) - 1)
    def _():
        _ffn_residual_copy(x_hbm, xres_ref, sem, i).wait()
        u = out_ref[...]
        u = u * lax.rsqrt(jnp.mean(u * u, axis=-1, keepdims=True) + RMS_EPS) * npost_ref[...]
        out_ref[...] = xres_ref[...] + u


def _ffn(h2, x1, norm_post, w_in, w_down):
    tm, tf = FFN_TM, FFN_TF
    nf = D_FF // tf
    return pl.pallas_call(
        _ffn_kernel,
        grid=(SEQ // tm, nf),
        in_specs=[
            pl.BlockSpec((tm, D_MODEL), lambda i, f: (i, 0)),
            pl.BlockSpec(memory_space=pl.ANY),
            pl.BlockSpec((1, D_MODEL), lambda i, f: (0, 0)),
            pl.BlockSpec((D_MODEL, tf), lambda i, f: (0, f)),
            pl.BlockSpec((D_MODEL, tf), lambda i, f: (0, f + nf)),
            pl.BlockSpec((tf, D_MODEL), lambda i, f: (f, 0)),
        ],
        out_specs=pl.BlockSpec((tm, D_MODEL), lambda i, f: (i, 0)),
        out_shape=jax.ShapeDtypeStruct((SEQ, D_MODEL), F32),
        scratch_shapes=[pltpu.VMEM((tm, D_MODEL), F32), pltpu.SemaphoreType.DMA(())],
        compiler_params=_params("arbitrary", "arbitrary"),
        name="ffn",
    )(h2, x1, norm_post, w_in, w_in, w_down)


def kernel(x, w_in, b_fox_f, hgrn_lb_logits, hgrn_norm_w, w_up_a, w_up_b, w_o, norm_mix_pre,
           norm_mix_post, norm_ffn_pre, norm_ffn_post, w_ffn_in, w_ffn_down):
    assert x.shape == (1, SEQ, D_MODEL) and w_in.shape[0] == 1
    w_t = w_in[0].T
    x2 = x[0]

    h, a_f, cx, nc = _prenorm(x2, norm_mix_pre, w_t, b_fox_f.reshape(N_HEADS, 1))
    proj = _inproj(h, w_t)
    y_a = _hgrn(proj, a_f, hgrn_lb_logits, hgrn_norm_w)
    y_b = _fox(proj, cx, nc)
    x1, h2 = _merge(y_a, y_b, proj, w_up_a[0].astype(BF16), w_up_b[0].astype(BF16), w_o[0].astype(BF16),
                    x2, norm_mix_post, norm_ffn_pre)
    out = _ffn(h2, x1, norm_ffn_post, w_ffn_in[0], w_ffn_down[0])
    return out[None]
```

```python
import functools

import numpy as np
import jax
import jax.numpy as jnp
from jax import lax
from jax.experimental import pallas as pl
from jax.experimental.pallas import tpu as pltpu

F32 = jnp.float32
BF16 = jnp.bfloat16

D_MODEL = 2048
SEQ = 8192
HEAD_DIM = 128
N_HEADS = 8
WIDTH = N_HEADS * HEAD_DIM
D_FF = 5632
RMS_EPS = 1e-6
N_PROJ = 3 * WIDTH + 2 * D_MODEL + 3 * WIDTH

VMEM_LIMIT_BYTES = 56 * 1024 * 1024

SRC_AQ, SRC_AF, SRC_AI, SRC_AG, SRC_BQ, SRC_BK, SRC_BV = 0, 1, 2, 3, 4, 5, 6
IN_NA = 7
BLK_AQ, BLK_AI, BLK_GA, BLK_GB, BLK_AG, BLK_BQ, BLK_BK, BLK_BV = 0, 1, 2, 4, 6, 7, 8, 9
HEADS_PER_BLK = WIDTH // HEAD_DIM

PRE_TM = 512
IN_TM, IN_TN = 2048, 512
HG_TS, HG_C, HG_HEADS = 1024, 64, 8
HG_FAST_LOG2 = 96.0
FOX_TQ, FOX_TK, FOX_HEADS = 512, 256, 2
LOG2E = 1.4426950408889634
FOX_SKIP_LOG2 = 160.0
FOX_NORM_MARGIN = 1.01
MERGE_TM = 256
FFN_TM, FFN_TF = 1024, 256


def _params(*sem):
    return pltpu.CompilerParams(dimension_semantics=sem, vmem_limit_bytes=VMEM_LIMIT_BYTES)


def _dot_nt(a, b):
    return lax.dot_general(a, b, (((1,), (1,)), ((), ())), preferred_element_type=F32)


def _dot_tn(a, b):
    return lax.dot_general(a, b, (((0,), (0,)), ((), ())), preferred_element_type=F32)


def _log_sigmoid(x):
    return jnp.minimum(x, 0.0) - jnp.log(1.0 + jnp.exp(-jnp.abs(x)))


def _sigmoid(x):
    return 1.0 / (1.0 + jnp.exp(-x))


def _split3(x):
    p0 = x.astype(BF16)
    r1 = x - p0.astype(F32)
    p1 = r1.astype(BF16)
    p2 = (r1 - p1.astype(F32)).astype(BF16)
    return p0, p1, p2


def _prenorm_kernel(x_ref, nw_ref, waf_ref, wf_ref, bf_ref, tri_ref, h_ref, af_ref, cx_ref, nc_ref,
                    wafb_ref, carry_ref):
    i = pl.program_id(0)
    tm = x_ref.shape[0]

    @pl.when(i == 0)
    def _():
        wafb_ref[...] = waf_ref[...].T.astype(BF16)
        carry_ref[...] = jnp.zeros_like(carry_ref)

    x = x_ref[...]
    ms = jnp.mean(x * x, axis=-1, keepdims=True)
    h_ref[...] = (x * lax.rsqrt(ms + RMS_EPS) * nw_ref[...]).astype(BF16)
    hb = h_ref[...]
    af_ref[...] = jnp.dot(hb, wafb_ref[...], preferred_element_type=F32)
    logit = _dot_nt(wf_ref[...].astype(BF16), hb) + bf_ref[...]
    nls = _log_sigmoid(logit) * (-LOG2E)
    tri = tri_ref[...]
    loc = sum(jnp.dot(p, tri, preferred_element_type=F32) for p in _split3(nls))
    nc = loc + carry_ref[:, 0:1]
    carry_ref[...] = jnp.broadcast_to(nc[:, tm - 1:tm], carry_ref.shape)
    pad = jnp.zeros((128 - N_HEADS, tm), F32)
    nc_ref[...] = jnp.concatenate([nc, pad], axis=0).T
    parts = [p.astype(F32) for p in _split3(nc)]
    pad3 = jnp.zeros((128 - 3 * N_HEADS, tm), F32)
    cx_ref[...] = jnp.concatenate(parts + [pad3], axis=0).T.astype(BF16)


def _prenorm(x2, norm_w, w, bf_col):
    tm = PRE_TM
    tri = jnp.asarray(np.triu(np.ones((tm, tm), np.float32)), BF16)
    once = pl.Buffered(1)
    return pl.pallas_call(
        _prenorm_kernel,
        grid=(SEQ // tm,),
        in_specs=[
            pl.BlockSpec((tm, D_MODEL), lambda i: (i, 0)),
            pl.BlockSpec((1, D_MODEL), lambda i: (0, 0)),
            pl.BlockSpec((WIDTH, D_MODEL), lambda i: (SRC_AF, 0), pipeline_mode=once),
            pl.BlockSpec((N_HEADS, D_MODEL), lambda i: (IN_NA * WIDTH // N_HEADS, 0), pipeline_mode=once),
            pl.BlockSpec((N_HEADS, 1), lambda i: (0, 0)),
            pl.BlockSpec((tm, tm), lambda i: (0, 0), pipeline_mode=once),
        ],
        out_specs=[
            pl.BlockSpec((tm, D_MODEL), lambda i: (i, 0)),
            pl.BlockSpec((tm, WIDTH), lambda i: (i, 0)),
            pl.BlockSpec((tm, 128), lambda i: (i, 0)),
            pl.BlockSpec((tm, 128), lambda i: (i, 0)),
        ],
        out_shape=[
            jax.ShapeDtypeStruct((SEQ, D_MODEL), BF16),
            jax.ShapeDtypeStruct((SEQ, WIDTH), F32),
            jax.ShapeDtypeStruct((SEQ, 128), BF16),
            jax.ShapeDtypeStruct((SEQ, 128), F32),
        ],
        scratch_shapes=[pltpu.VMEM((D_MODEL, WIDTH), BF16), pltpu.VMEM((N_HEADS, 128), F32)],
        compiler_params=_params("arbitrary"),
        name="prenorm",
    )(x2, norm_w, w, w, bf_col, tri)


def _inproj_kernel(h_ref, wa_ref, wb_ref, proj_ref, wc_ref):
    j = pl.program_id(1)
    i = pl.program_id(2)
    tm, tn = proj_ref.shape
    nb = WIDTH // tn
    src = _inproj_src_block(j, nb)

    @pl.when(i == 0)
    def _():
        @pl.when(src < IN_NA * nb)
        def _():
            scale = jnp.where(src // nb == SRC_BQ, LOG2E * HEAD_DIM ** -0.5, 1.0)
            wc_ref[...] = (wa_ref[...] * scale).T.astype(BF16)

        @pl.when(src >= IN_NA * nb)
        def _():
            wc_ref[...] = jnp.concatenate([wa_ref[N_HEADS:, :], wb_ref[:N_HEADS, :]], axis=0).T.astype(BF16)

    rows = pl.ds(pl.multiple_of(i * tm, tm), tm)
    proj_ref[...] = jnp.dot(h_ref[rows, :], wc_ref[...], preferred_element_type=F32).astype(BF16)


def _inproj_src_block(j, nb):
    return jnp.where(j < SRC_AF * nb, j, j + nb)


def _inproj_out_block(j, nb):
    src = _inproj_src_block(j, nb)
    g = src // nb
    g_out = jnp.int32(BLK_AQ)
    for g_src, g_dst in ((SRC_AI, BLK_AI), (SRC_AG, BLK_AG), (SRC_BQ, BLK_BQ), (SRC_BK, BLK_BK), (SRC_BV, BLK_BV)):
        g_out = jnp.where(g == g_src, g_dst, g_out)
    return jnp.where(g < IN_NA, g_out * nb + src % nb, src - IN_NA * nb + BLK_GA * nb)


def _inproj(h, w):
    tm, tn = IN_TM, IN_TN
    nb = WIDTH // tn
    half = SEQ // 2
    n_i = half // tm
    first_gate = IN_NA * nb
    grid = (2, N_PROJ // tn, n_i)
    return pl.pallas_call(
        _inproj_kernel,
        grid=grid,
        in_specs=[
            pl.BlockSpec((half, D_MODEL), lambda s, j, i: (s, 0), pipeline_mode=pl.Buffered(1)),
            pl.BlockSpec((tn, D_MODEL), lambda s, j, i: (_inproj_src_block(j, nb), 0)),
            pl.BlockSpec((tn, D_MODEL), lambda s, j, i: (jnp.maximum(_inproj_src_block(j, nb), first_gate) + 1, 0)),
        ],
        out_specs=pl.BlockSpec((tm, tn), lambda s, j, i: (s * n_i + i, _inproj_out_block(j, nb))),
        out_shape=jax.ShapeDtypeStruct((SEQ, N_PROJ), BF16),
        scratch_shapes=[pltpu.VMEM((D_MODEL, tn), BF16)],
        compiler_params=_params("arbitrary", "arbitrary", "arbitrary"),
        name="inproj",
    )(h, w, w)


def _hgrn_constants(c):
    n_lvl = int(np.log2(c))
    t = np.arange(c)[:, None]
    j = np.arange(c)[None, :]
    blocks = [(j <= t), (j > t)]
    level = np.full((c, c), -1, np.int32)
    level[np.arange(c), np.arange(c)] = 0
    for l in range(n_lvl):
        b = 2 << l
        mid = (t // b) * b + b // 2 - 1
        second = (t % b) >= b // 2
        m = np.where(second, (j > mid) & (j <= t), (j > t) & (j <= mid))
        blocks.append(m)
        s = np.arange(c)[None, :]
        own = (t // b == s // b) & second & ((s % b) < b // 2)
        level[own] = l + 1
    sums = np.concatenate(blocks, axis=0).astype(np.float32)
    sums2 = np.concatenate([sums, sums], axis=1)
    return jnp.asarray(sums2, BF16), jnp.asarray(level), n_lvl


def _hgrn_kernel(q_ref, z_ref, v_ref, g_ref, lbl_ref, nw_ref, sums_ref, lvl_ref, y_ref,
                 st_ref, k_ref, b_ref, kout_ref, qin_ref, g2_ref, dlast_ref, bmin_ref, *, c, n_lvl):
    n_heads = st_ref.shape[0]
    d = HEAD_DIM

    @pl.when(pl.program_id(1) == 0)
    def _():
        st_ref[...] = jnp.zeros_like(st_ref)

    logits = lbl_ref[...]
    ex = jnp.exp(logits - jnp.max(logits, axis=0, keepdims=True))
    lb = ex[0:1, :] / jnp.sum(ex, axis=0, keepdims=True)
    one_m_lb = 1.0 - lb
    nw = nw_ref[...]
    sums = sums_ref[...]
    lvl = lvl_ref[...]
    n_chunks = q_ref.shape[0] // c

    heads = [slice(hh * d, (hh + 1) * d) for hh in range(n_heads)]
    causal = lvl >= 0

    def chunk_rows(ci):
        return pl.ds(pl.multiple_of(ci * c, c), c)

    bmin_ref[...] = jnp.zeros_like(bmin_ref)

    def prepare(ci, carry):
        r = chunk_rows(ci)
        z = z_ref[r, :]
        e = jnp.exp(-jnp.abs(z))
        inv = 1.0 / (1.0 + e)
        pos = z >= 0.0
        sig = jnp.where(pos, inv, e * inv)
        sig_n = jnp.where(pos, e * inv, inv)
        g = jnp.log2(lb + one_m_lb * sig)
        k_all = one_m_lb * sig_n
        g_hi = g.astype(BF16)
        g_lo = (g - g_hi.astype(F32)).astype(BF16)
        g2 = jnp.concatenate([g_hi, g_lo], axis=0)
        e01 = jnp.dot(sums[0:2 * c], g2, preferred_element_type=F32)
        b = e01[0:c]
        dec_b = jnp.exp2(b)
        g2_ref[ci] = g2
        b_ref[r, :] = b
        k_ref[r, :] = k_all
        kout_ref[r, :] = (k_all * jnp.exp2(e01[c:2 * c])).astype(BF16)
        qin_ref[r, :] = (q_ref[r, :].astype(F32) * dec_b).astype(BF16)
        dlast_ref[ci] = jnp.broadcast_to(dec_b[c - 1:c, :], dlast_ref.shape[1:])
        bmin_ref[...] = jnp.minimum(bmin_ref[...], b)
        return carry

    lax.fori_loop(0, n_chunks, prepare, 0, unroll=2)
    b_min = jnp.min(bmin_ref[...])

    def finish(ci, scores):
        r = chunk_rows(ci)
        o_intra, inc = [], []
        for hh, cols in enumerate(heads):
            v = v_ref[r, cols]
            o_intra.append(jnp.dot(scores[hh], v, preferred_element_type=F32))
            inc.append(_dot_tn(v, kout_ref[r, cols]))
        for hh, cols in enumerate(heads):
            st = st_ref[hh]
            o = o_intra[hh] + _dot_nt(qin_ref[r, cols], st.astype(BF16))
            st_ref[hh] = st * dlast_ref[ci, 0:1, cols] + inc[hh]
            o = o * lax.rsqrt(jnp.mean(o * o, axis=-1, keepdims=True) + RMS_EPS) * nw
            gt = g_ref[r, cols].astype(F32)
            y_ref[r, cols] = (o * gt * _sigmoid(gt)).astype(y_ref.dtype)

    def fast_chunk(ci, carry):
        r = chunk_rows(ci)
        k_up = (k_ref[r, :] * jnp.exp2(-b_ref[r, :])).astype(BF16)
        scores = [jnp.where(causal, _dot_nt(qin_ref[r, cols], k_up[:, cols]), 0.0).astype(BF16)
                  for cols in heads]
        finish(ci, scores)
        return carry

    def safe_chunk(ci, carry):
        r = chunk_rows(ci)
        dec_l = jnp.exp2(jnp.dot(sums[2 * c:], g2_ref[ci], preferred_element_type=F32))
        scores = []
        for cols in heads:
            q = q_ref[r, cols].astype(F32)
            k = k_ref[r, cols]
            sc = jnp.where(lvl == 0, _dot_nt(q.astype(BF16), k.astype(BF16)), 0.0)
            for l in range(n_lvl):
                d_l = dec_l[l * c:(l + 1) * c, cols]
                s_l = _dot_nt((q * d_l).astype(BF16), (k * d_l).astype(BF16))
                sc = jnp.where(lvl == l + 1, s_l, sc)
            scores.append(sc.astype(BF16))
        finish(ci, scores)
        return carry

    @pl.when(b_min >= -HG_FAST_LOG2)
    def _():
        lax.fori_loop(0, n_chunks, fast_chunk, 0, unroll=2)

    @pl.when(b_min < -HG_FAST_LOG2)
    def _():
        lax.fori_loop(0, n_chunks, safe_chunk, 0)


def _hgrn(proj, a_f, lb_logits, norm_w):
    ts, c, hg = HG_TS, HG_C, HG_HEADS
    sums, lvl, n_lvl = _hgrn_constants(c)
    w = hg * HEAD_DIM
    per_blk = WIDTH // w
    grid = (N_HEADS // hg, SEQ // ts)
    return pl.pallas_call(
        functools.partial(_hgrn_kernel, c=c, n_lvl=n_lvl),
        grid=grid,
        in_specs=[
            pl.BlockSpec((ts, w), lambda h, i: (i, BLK_AQ * per_blk + h)),
            pl.BlockSpec((ts, w), lambda h, i: (i, h)),
            pl.BlockSpec((ts, w), lambda h, i: (i, BLK_AI * per_blk + h)),
            pl.BlockSpec((ts, w), lambda h, i: (i, BLK_AG * per_blk + h)),
            pl.BlockSpec((lb_logits.shape[0], w), lambda h, i: (0, h)),
            pl.BlockSpec((1, HEAD_DIM), lambda h, i: (0, 0)),
            pl.BlockSpec(sums.shape, lambda h, i: (0, 0)),
            pl.BlockSpec(lvl.shape, lambda h, i: (0, 0)),
        ],
        out_specs=pl.BlockSpec((ts, w), lambda h, i: (i, h)),
        out_shape=jax.ShapeDtypeStruct((SEQ, WIDTH), BF16),
        scratch_shapes=[
            pltpu.VMEM((hg, HEAD_DIM, HEAD_DIM), F32),
            pltpu.VMEM((ts, w), F32),
            pltpu.VMEM((ts, w), F32),
            pltpu.VMEM((ts, w), BF16),
            pltpu.VMEM((ts, w), BF16),
            pltpu.VMEM((ts // c, 2 * c, w), BF16),
            pltpu.VMEM((ts // c, 8, w), F32),
            pltpu.VMEM((c, w), F32),
        ],
        compiler_params=_params("arbitrary", "arbitrary"),
        name="hgrn",
    )(proj, a_f, proj, proj, lb_logits, norm_w, sums, lvl)


def _fox_kernel(q_ref, k_ref, v_ref, cx_ref, nc_ref, o_ref, vt_ref, nrm_ref, m_ref, acc_ref, *, tq, tk):
    g = pl.program_id(0)
    i = pl.program_id(1)
    n_heads = vt_ref.shape[0]
    d = HEAD_DIM
    s_len = k_ref.shape[0]
    n_sub = tq // tk
    n_kt = s_len // tk
    vrows = vt_ref.shape[2]
    heads = [(hh, slice(hh * d, (hh + 1) * d)) for hh in range(n_heads)]

    @pl.when(i == 0)
    def _():
        ones_row = jnp.where(lax.broadcasted_iota(jnp.int32, (vrows - d, tk), 0) == 0, 1.0, 0.0)
        wg = n_heads * d
        same_head = (lax.broadcasted_iota(jnp.int32, (wg, wg), 0) // d
                     == lax.broadcasted_iota(jnp.int32, (wg, wg), 1) // d)
        ones_blk = jnp.where(same_head, 1.0, 0.0).astype(BF16)

        def sq_norms(x):
            xf = x.astype(F32)
            return jnp.dot((xf * xf).astype(BF16), ones_blk, preferred_element_type=F32)

        kn = jnp.zeros((tk, wg), F32)
        qn = jnp.zeros((tk, wg), F32)
        for r in range(n_kt):
            rows = slice(r * tk, (r + 1) * tk)
            for hh, cols in heads:
                vt_ref[hh, r, 0:d, :] = v_ref[rows, cols].astype(F32).T.astype(BF16)
                vt_ref[hh, r, d:vrows, :] = ones_row.astype(BF16)
            kn = jnp.maximum(kn, sq_norms(k_ref[rows, :]))
            qn = jnp.maximum(qn, sq_norms(q_ref[rows, :]))
        nrm = 2.0 * FOX_NORM_MARGIN * jnp.sqrt(jnp.max(qn, axis=0, keepdims=True)
                                               * jnp.max(kn, axis=0, keepdims=True))
        for hh, cols in heads:
            nrm_ref[hh] = jnp.broadcast_to(nrm[:, cols], nrm_ref.shape[1:])

    q_rows = pl.ds(pl.multiple_of(i * tq, tq), tq)
    nc_q0 = nc_ref[pl.ds(pl.multiple_of(i * tq, tq), 1), :]
    ends = nc_ref[pl.ds(tk - 1, n_kt, stride=tk), :]
    lane = lax.broadcasted_iota(jnp.int32, ends.shape, 1)
    rsel = lax.broadcasted_iota(jnp.int32, (d, tq), 0)
    qa, r_lo = [], []
    for hh, cols in heads:
        h = g * n_heads + hh
        thr = nc_q0 - nrm_ref[hh, 0:1, :] - FOX_SKIP_LOG2
        r_lo.append(jnp.sum(jnp.where((ends < thr) & (lane == h), 1, 0)))
        sel = jnp.where((rsel < 3 * N_HEADS) & ((rsel & (N_HEADS - 1)) == h), 1.0, 0.0).astype(BF16)
        qa.append(jnp.concatenate([q_ref[q_rows, cols].astype(F32).T.astype(BF16), sel], axis=0))

    def scores(hh, cols, r, mask_off):
        ks = pl.ds(pl.multiple_of(r * tk, tk), tk)
        ka = jnp.concatenate([k_ref[ks, cols], cx_ref[ks, :]], axis=1)
        s = jnp.dot(ka, qa[hh], preferred_element_type=F32)
        if mask_off is not None:
            kid = lax.broadcasted_iota(jnp.int32, s.shape, 0) + mask_off
            qid = lax.broadcasted_iota(jnp.int32, s.shape, 1)
            s = jnp.where(kid <= qid, s, -jnp.inf)
        return s

    def first_step(tiles):
        ss = [[scores(hh, cols, r, off) for r, off in tiles] for hh, cols in heads]
        ms = []
        for hh, _ in heads:
            m = jnp.max(ss[hh][0], axis=0, keepdims=True)
            for s in ss[hh][1:]:
                m = jnp.maximum(m, jnp.max(s, axis=0, keepdims=True))
            ms.append(m)
        for hh, _ in heads:
            pv = None
            for (r, _), s in zip(tiles, ss[hh]):
                p = jnp.exp2(s - ms[hh]).astype(BF16)
                dd = jnp.dot(vt_ref[hh, r], p, preferred_element_type=F32)
                pv = dd if pv is None else pv + dd
            acc_ref[hh] = pv
            m_ref[hh] = ms[hh]

    def later_step(hh, cols, r):
        s = scores(hh, cols, r, None)
        m_prev = m_ref[hh]
        m_new = jnp.maximum(m_prev, jnp.max(s, axis=0, keepdims=True))
        pv = jnp.dot(vt_ref[hh, r], jnp.exp2(s - m_new).astype(BF16), preferred_element_type=F32)
        acc_ref[hh] = jnp.exp2(m_prev - m_new) * acc_ref[hh] + pv
        m_ref[hh] = m_new

    diag = [(i * n_sub + rr, rr * tk) for rr in range(n_sub)]

    @pl.when(i == 0)
    def _():
        first_step(diag)

    @pl.when(i > 0)
    def _():
        first_step([(i * n_sub - 1, None)] + diag)
        for hh, cols in heads:
            def off_diag(r, carry, hh=hh, cols=cols):
                later_step(hh, cols, r)
                return carry
            lax.fori_loop(r_lo[hh], i * n_sub - 1, off_diag, 0)

    for hh, cols in heads:
        acc = acc_ref[hh]
        o_ref[:, cols] = (acc[0:d] / acc[d:d + 1]).T.astype(o_ref.dtype)


def _fox(proj, cx, nc):
    tq, tk, hg = FOX_TQ, FOX_TK, FOX_HEADS
    w = hg * HEAD_DIM
    per_blk = WIDTH // w
    vrows = HEAD_DIM + 16
    return pl.pallas_call(
        functools.partial(_fox_kernel, tq=tq, tk=tk),
        grid=(N_HEADS // hg, SEQ // tq),
        in_specs=[
            pl.BlockSpec((SEQ, w), lambda g, i: (0, BLK_BQ * per_blk + g)),
            pl.BlockSpec((SEQ, w), lambda g, i: (0, BLK_BK * per_blk + g)),
            pl.BlockSpec((SEQ, w), lambda g, i: (0, BLK_BV * per_blk + g)),
            pl.BlockSpec((SEQ, 128), lambda g, i: (0, 0)),
            pl.BlockSpec((SEQ, 128), lambda g, i: (0, 0)),
        ],
        out_specs=pl.BlockSpec((tq, w), lambda g, i: (i, g)),
        out_shape=jax.ShapeDtypeStruct((SEQ, WIDTH), BF16),
        scratch_shapes=[
            pltpu.VMEM((hg, SEQ // tk, vrows, tk), BF16),
            pltpu.VMEM((hg, 8, 128), F32),
            pltpu.VMEM((hg, 1, tq), F32),
            pltpu.VMEM((hg, vrows, tq), F32),
        ],
        compiler_params=_params("arbitrary", "arbitrary"),
        name="fox",
    )(proj, proj, proj, cx, nc)


def _merge_kernel(oa_ref, ob_ref, ga_ref, gb_ref, wa_ref, wb_ref, wo_ref, x_ref, nw_ref, npre_ref,
                  out_ref, h2_ref):
    ya = jnp.dot(oa_ref[...], wa_ref[...], preferred_element_type=F32)
    yb = jnp.dot(ob_ref[...], wb_ref[...], preferred_element_type=F32)
    merged = _sigmoid(ga_ref[...].astype(F32)) * ya + _sigmoid(gb_ref[...].astype(F32)) * yb
    u = jnp.dot(merged.astype(BF16), wo_ref[...], preferred_element_type=F32)
    u = u * lax.rsqrt(jnp.mean(u * u, axis=-1, keepdims=True) + RMS_EPS) * nw_ref[...]
    x1 = x_ref[...] + u
    out_ref[...] = x1
    ms = jnp.mean(x1 * x1, axis=-1, keepdims=True)
    h2_ref[...] = (x1 * lax.rsqrt(ms + RMS_EPS) * npre_ref[...]).astype(BF16)


def _merge(y_a, y_b, proj, w_up_a, w_up_b, w_o, x2, norm_w, norm_ffn_pre):
    tm = MERGE_TM
    once = pl.Buffered(1)
    return pl.pallas_call(
        _merge_kernel,
        grid=(SEQ // tm,),
        in_specs=[
            pl.BlockSpec((tm, WIDTH), lambda i: (i, 0)),
            pl.BlockSpec((tm, WIDTH), lambda i: (i, 0)),
            pl.BlockSpec((tm, D_MODEL), lambda i: (i, BLK_GA // 2)),
            pl.BlockSpec((tm, D_MODEL), lambda i: (i, BLK_GB // 2)),
            pl.BlockSpec((WIDTH, D_MODEL), lambda i: (0, 0), pipeline_mode=once),
            pl.BlockSpec((WIDTH, D_MODEL), lambda i: (0, 0), pipeline_mode=once),
            pl.BlockSpec((D_MODEL, D_MODEL), lambda i: (0, 0), pipeline_mode=once),
            pl.BlockSpec((tm, D_MODEL), lambda i: (i, 0)),
            pl.BlockSpec((1, D_MODEL), lambda i: (0, 0)),
            pl.BlockSpec((1, D_MODEL), lambda i: (0, 0)),
        ],
        out_specs=[
            pl.BlockSpec((tm, D_MODEL), lambda i: (i, 0)),
            pl.BlockSpec((tm, D_MODEL), lambda i: (i, 0)),
        ],
        out_shape=[
            jax.ShapeDtypeStruct((SEQ, D_MODEL), F32),
            jax.ShapeDtypeStruct((SEQ, D_MODEL), BF16),
        ],
        compiler_params=_params("arbitrary"),
        name="merge",
    )(y_a, y_b, proj, proj, w_up_a, w_up_b, w_o, x2, norm_w, norm_ffn_pre)


def _ffn_residual_copy(x_hbm, xres_ref, sem, i):
    tm = xres_ref.shape[0]
    return pltpu.make_async_copy(x_hbm.at[pl.ds(pl.multiple_of(i * tm, tm), tm), :], xres_ref, sem)


def _ffn_kernel(h_ref, x_hbm, npost_ref, wg_ref, wu_ref, wd_ref, out_ref, xres_ref, sem):
    i = pl.program_id(0)
    f = pl.program_id(1)

    @pl.when(f == 0)
    def _():
        _ffn_residual_copy(x_hbm, xres_ref, sem, i).start()
        out_ref[...] = jnp.zeros_like(out_ref)

    h = h_ref[...]
    gate = jnp.dot(h, wg_ref[...].astype(BF16), preferred_element_type=F32)
    up = jnp.dot(h, wu_ref[...].astype(BF16), preferred_element_type=F32)
    act = (gate * _sigmoid(gate) * up).astype(BF16)
    out_ref[...] += jnp.dot(act, wd_ref[...].astype(BF16), preferred_element_type=F32)

    @pl.when(f == pl.num_programs(1) - 1)
    def _():
        _ffn_residual_copy(x_hbm, xres_ref, sem, i).wait()
        u = out_ref[...]
        u = u * lax.rsqrt(jnp.mean(u * u, axis=-1, keepdims=True) + RMS_EPS) * npost_ref[...]
        out_ref[...] = xres_ref[...] + u


def _ffn(h2, x1, norm_post, w_in, w_down):
    tm, tf = FFN_TM, FFN_TF
    nf = D_FF // tf
    return pl.pallas_call(
        _ffn_kernel,
        grid=(SEQ // tm, nf),
        in_specs=[
            pl.BlockSpec((tm, D_MODEL), lambda i, f: (i, 0)),
            pl.BlockSpec(memory_space=pl.ANY),
            pl.BlockSpec((1, D_MODEL), lambda i, f: (0, 0)),
            pl.BlockSpec((D_MODEL, tf), lambda i, f: (0, f)),
            pl.BlockSpec((D_MODEL, tf), lambda i, f: (0, f + nf)),
            pl.BlockSpec((tf, D_MODEL), lambda i, f: (f, 0)),
        ],
        out_specs=pl.BlockSpec((tm, D_MODEL), lambda i, f: (i, 0)),
        out_shape=jax.ShapeDtypeStruct((SEQ, D_MODEL), F32),
        scratch_shapes=[pltpu.VMEM((tm, D_MODEL), F32), pltpu.SemaphoreType.DMA(())],
        compiler_params=_params("arbitrary", "arbitrary"),
        name="ffn",
    )(h2, x1, norm_post, w_in, w_in, w_down)


def kernel(x, w_in, b_fox_f, hgrn_lb_logits, hgrn_norm_w, w_up_a, w_up_b, w_o, norm_mix_pre,
           norm_mix_post, norm_ffn_pre, norm_ffn_post, w_ffn_in, w_ffn_down):
    assert x.shape == (1, SEQ, D_MODEL) and w_in.shape[0] == 1
    w_t = w_in[0].T
    x2 = x[0]

    h, a_f, cx, nc = _prenorm(x2, norm_mix_pre, w_t, b_fox_f.reshape(N_HEADS, 1))
    proj = _inproj(h, w_t)
    y_a = _hgrn(proj, a_f, hgrn_lb_logits, hgrn_norm_w)
    y_b = _fox(proj, cx, nc)
    x1, h2 = _merge(y_a, y_b, proj, w_up_a[0].astype(BF16), w_up_b[0].astype(BF16), w_o[0].astype(BF16),
                    x2, norm_mix_post, norm_ffn_pre)
    out = _ffn(h2, x1, norm_ffn_post, w_ffn_in[0], w_ffn_down[0])
    return out[None]
```

```python
import functools

import numpy as np
import jax
import jax.numpy as jnp
from jax import lax
from jax.experimental import pallas as pl
from jax.experimental.pallas import tpu as pltpu

F32 = jnp.float32
BF16 = jnp.bfloat16

D_MODEL = 2048
SEQ = 8192
HEAD_DIM = 128
N_HEADS = 8
WIDTH = N_HEADS * HEAD_DIM
D_FF = 5632
RMS_EPS = 1e-6
N_PROJ = 3 * WIDTH + 2 * D_MODEL + 3 * WIDTH

VMEM_LIMIT_BYTES = 56 * 1024 * 1024

SRC_AQ, SRC_AF, SRC_AI, SRC_AG, SRC_BQ, SRC_BK, SRC_BV = 0, 1, 2, 3, 4, 5, 6
IN_NA = 7
BLK_AQ, BLK_AI, BLK_GA, BLK_GB, BLK_AG, BLK_BQ, BLK_BK, BLK_BV = 0, 1, 2, 4, 6, 7, 8, 9
HEADS_PER_BLK = WIDTH // HEAD_DIM

PRE_TM = 512
IN_TM, IN_TN = 2048, 512
HG_TS, HG_C, HG_HEADS = 1024, 64, 8
HG_FAST_LOG2 = 96.0
FOX_TQ, FOX_TK, FOX_HEADS = 512, 256, 2
LOG2E = 1.4426950408889634
FOX_SKIP_LOG2 = 160.0
FOX_NORM_MARGIN = 1.01
MERGE_TM = 256
FFN_TM, FFN_TF = 1024, 256


def _params(*sem):
    return pltpu.CompilerParams(dimension_semantics=sem, vmem_limit_bytes=VMEM_LIMIT_BYTES)


def _dot_nt(a, b):
    return lax.dot_general(a, b, (((1,), (1,)), ((), ())), preferred_element_type=F32)


def _dot_tn(a, b):
    return lax.dot_general(a, b, (((0,), (0,)), ((), ())), preferred_element_type=F32)


def _log_sigmoid(x):
    return jnp.minimum(x, 0.0) - jnp.log(1.0 + jnp.exp(-jnp.abs(x)))


def _sigmoid(x):
    return 1.0 / (1.0 + jnp.exp(-x))


def _split3(x):
    p0 = x.astype(BF16)
    r1 = x - p0.astype(F32)
    p1 = r1.astype(BF16)
    p2 = (r1 - p1.astype(F32)).astype(BF16)
    return p0, p1, p2


def _prenorm_kernel(x_ref, nw_ref, waf_ref, wf_ref, bf_ref, tri_ref, h_ref, af_ref, cx_ref, nc_ref,
                    wafb_ref, carry_ref):
    i = pl.program_id(0)
    tm = x_ref.shape[0]

    @pl.when(i == 0)
    def _():
        wafb_ref[...] = waf_ref[...].T.astype(BF16)
        carry_ref[...] = jnp.zeros_like(carry_ref)

    x = x_ref[...]
    ms = jnp.mean(x * x, axis=-1, keepdims=True)
    h_ref[...] = (x * lax.rsqrt(ms + RMS_EPS) * nw_ref[...]).astype(BF16)
    hb = h_ref[...]
    af_ref[...] = jnp.dot(hb, wafb_ref[...], preferred_element_type=F32)
    logit = _dot_nt(wf_ref[...].astype(BF16), hb) + bf_ref[...]
    nls = _log_sigmoid(logit) * (-LOG2E)
    tri = tri_ref[...]
    loc = sum(jnp.dot(p, tri, preferred_element_type=F32) for p in _split3(nls))
    nc = loc + carry_ref[:, 0:1]
    carry_ref[...] = jnp.broadcast_to(nc[:, tm - 1:tm], carry_ref.shape)
    pad = jnp.zeros((128 - N_HEADS, tm), F32)
    nc_ref[...] = jnp.concatenate([nc, pad], axis=0).T
    parts = [p.astype(F32) for p in _split3(nc)]
    pad3 = jnp.zeros((128 - 3 * N_HEADS, tm), F32)
    cx_ref[...] = jnp.concatenate(parts + [pad3], axis=0).T.astype(BF16)


def _prenorm(x2, norm_w, w, bf_col):
    tm = PRE_TM
    tri = jnp.asarray(np.triu(np.ones((tm, tm), np.float32)), BF16)
    once = pl.Buffered(1)
    return pl.pallas_call(
        _prenorm_kernel,
        grid=(SEQ // tm,),
        in_specs=[
            pl.BlockSpec((tm, D_MODEL), lambda i: (i, 0)),
            pl.BlockSpec((1, D_MODEL), lambda i: (0, 0)),
            pl.BlockSpec((WIDTH, D_MODEL), lambda i: (SRC_AF, 0), pipeline_mode=once),
            pl.BlockSpec((N_HEADS, D_MODEL), lambda i: (IN_NA * WIDTH // N_HEADS, 0), pipeline_mode=once),
            pl.BlockSpec((N_HEADS, 1), lambda i: (0, 0)),
            pl.BlockSpec((tm, tm), lambda i: (0, 0), pipeline_mode=once),
        ],
        out_specs=[
            pl.BlockSpec((tm, D_MODEL), lambda i: (i, 0)),
            pl.BlockSpec((tm, WIDTH), lambda i: (i, 0)),
            pl.BlockSpec((tm, 128), lambda i: (i, 0)),
            pl.BlockSpec((tm, 128), lambda i: (i, 0)),
        ],
        out_shape=[
            jax.ShapeDtypeStruct((SEQ, D_MODEL), BF16),
            jax.ShapeDtypeStruct((SEQ, WIDTH), F32),
            jax.ShapeDtypeStruct((SEQ, 128), BF16),
            jax.ShapeDtypeStruct((SEQ, 128), F32),
        ],
        scratch_shapes=[pltpu.VMEM((D_MODEL, WIDTH), BF16), pltpu.VMEM((N_HEADS, 128), F32)],
        compiler_params=_params("arbitrary"),
        name="prenorm",
    )(x2, norm_w, w, w, bf_col, tri)


def _inproj_kernel(h_ref, wa_ref, wb_ref, proj_ref, wc_ref):
    j = pl.program_id(1)
    i = pl.program_id(2)
    tm, tn = proj_ref.shape
    nb = WIDTH // tn
    src = _inproj_src_block(j, nb)

    @pl.when(i == 0)
    def _():
        @pl.when(src < IN_NA * nb)
        def _():
            scale = jnp.where(src // nb == SRC_BQ, LOG2E * HEAD_DIM ** -0.5, 1.0)
            wc_ref[...] = (wa_ref[...] * scale).astype(BF16)

        @pl.when(src >= IN_NA * nb)
        def _():
            wc_ref[...] = jnp.concatenate([wa_ref[N_HEADS:, :], wb_ref[:N_HEADS, :]], axis=0).astype(BF16)

    rows = pl.ds(pl.multiple_of(i * tm, tm), tm)
    proj_ref[...] = _dot_nt(h_ref[rows, :], wc_ref[...]).astype(BF16)


def _inproj_src_block(j, nb):
    return jnp.where(j < SRC_AF * nb, j, j + nb)


def _inproj_out_block(j, nb):
    src = _inproj_src_block(j, nb)
    g = src // nb
    g_out = jnp.int32(BLK_AQ)
    for g_src, g_dst in ((SRC_AI, BLK_AI), (SRC_AG, BLK_AG), (SRC_BQ, BLK_BQ), (SRC_BK, BLK_BK), (SRC_BV, BLK_BV)):
        g_out = jnp.where(g == g_src, g_dst, g_out)
    return jnp.where(g < IN_NA, g_out * nb + src % nb, src - IN_NA * nb + BLK_GA * nb)


def _inproj(h, w):
    tm, tn = IN_TM, IN_TN
    nb = WIDTH // tn
    half = SEQ // 2
    n_i = half // tm
    first_gate = IN_NA * nb
    grid = (2, N_PROJ // tn, n_i)
    return pl.pallas_call(
        _inproj_kernel,
        grid=grid,
        in_specs=[
            pl.BlockSpec((half, D_MODEL), lambda s, j, i: (s, 0), pipeline_mode=pl.Buffered(1)),
            pl.BlockSpec((tn, D_MODEL), lambda s, j, i: (_inproj_src_block(j, nb), 0)),
            pl.BlockSpec((tn, D_MODEL), lambda s, j, i: (jnp.maximum(_inproj_src_block(j, nb), first_gate) + 1, 0)),
        ],
        out_specs=pl.BlockSpec((tm, tn), lambda s, j, i: (s * n_i + i, _inproj_out_block(j, nb))),
        out_shape=jax.ShapeDtypeStruct((SEQ, N_PROJ), BF16),
        scratch_shapes=[pltpu.VMEM((tn, D_MODEL), BF16)],
        compiler_params=_params("arbitrary", "arbitrary", "arbitrary"),
        name="inproj",
    )(h, w, w)


def _hgrn_constants(c):
    n_lvl = int(np.log2(c))
    t = np.arange(c)[:, None]
    j = np.arange(c)[None, :]
    blocks = [(j <= t), (j > t)]
    level = np.full((c, c), -1, np.int32)
    level[np.arange(c), np.arange(c)] = 0
    for l in range(n_lvl):
        b = 2 << l
        mid = (t // b) * b + b // 2 - 1
        second = (t % b) >= b // 2
        m = np.where(second, (j > mid) & (j <= t), (j > t) & (j <= mid))
        blocks.append(m)
        s = np.arange(c)[None, :]
        own = (t // b == s // b) & second & ((s % b) < b // 2)
        level[own] = l + 1
    sums = np.concatenate(blocks, axis=0).astype(np.float32)
    sums2 = np.concatenate([sums, sums], axis=1)
    return jnp.asarray(sums2, BF16), jnp.asarray(level), n_lvl


def _hgrn_kernel(q_ref, z_ref, v_ref, g_ref, lbl_ref, nw_ref, sums_ref, lvl_ref, y_ref,
                 st_ref, k_ref, b_ref, kout_ref, qin_ref, g2_ref, dlast_ref, bmin_ref, *, c, n_lvl):
    n_heads = st_ref.shape[0]
    d = HEAD_DIM

    @pl.when(pl.program_id(1) == 0)
    def _():
        st_ref[...] = jnp.zeros_like(st_ref)

    logits = lbl_ref[...]
    ex = jnp.exp(logits - jnp.max(logits, axis=0, keepdims=True))
    lb = ex[0:1, :] / jnp.sum(ex, axis=0, keepdims=True)
    one_m_lb = 1.0 - lb
    nw = nw_ref[...]
    sums = sums_ref[...]
    lvl = lvl_ref[...]
    n_chunks = q_ref.shape[0] // c

    heads = [slice(hh * d, (hh + 1) * d) for hh in range(n_heads)]
    causal = lvl >= 0

    def chunk_rows(ci):
        return pl.ds(pl.multiple_of(ci * c, c), c)

    bmin_ref[...] = jnp.zeros_like(bmin_ref)

    def prepare(ci, carry):
        r = chunk_rows(ci)
        z = z_ref[r, :]
        e = jnp.exp(-jnp.abs(z))
        inv = 1.0 / (1.0 + e)
        pos = z >= 0.0
        sig = jnp.where(pos, inv, e * inv)
        sig_n = jnp.where(pos, e * inv, inv)
        g = jnp.log2(lb + one_m_lb * sig)
        k_all = one_m_lb * sig_n
        g_hi = g.astype(BF16)
        g_lo = (g - g_hi.astype(F32)).astype(BF16)
        g2 = jnp.concatenate([g_hi, g_lo], axis=0)
        e01 = jnp.dot(sums[0:2 * c], g2, preferred_element_type=F32)
        b = e01[0:c]
        dec_b = jnp.exp2(b)
        g2_ref[ci] = g2
        b_ref[r, :] = b
        k_ref[r, :] = k_all
        kout_ref[r, :] = (k_all * jnp.exp2(e01[c:2 * c])).astype(BF16)
        qin_ref[r, :] = (q_ref[r, :].astype(F32) * dec_b).astype(BF16)
        dlast_ref[ci] = jnp.broadcast_to(dec_b[c - 1:c, :], dlast_ref.shape[1:])
        bmin_ref[...] = jnp.minimum(bmin_ref[...], b)
        return carry

    lax.fori_loop(0, n_chunks, prepare, 0, unroll=2)
    b_min = jnp.min(bmin_ref[...])

    def finish(ci, scores):
        r = chunk_rows(ci)
        o_intra, inc = [], []
        for hh, cols in enumerate(heads):
            v = v_ref[r, cols]
            o_intra.append(jnp.dot(scores[hh], v, preferred_element_type=F32))
            inc.append(_dot_tn(v, kout_ref[r, cols]))
        for hh, cols in enumerate(heads):
            st = st_ref[hh]
            o = o_intra[hh] + _dot_nt(qin_ref[r, cols], st.astype(BF16))
            st_ref[hh] = st * dlast_ref[ci, 0:1, cols] + inc[hh]
            o = o * lax.rsqrt(jnp.mean(o * o, axis=-1, keepdims=True) + RMS_EPS) * nw
            gt = g_ref[r, cols].astype(F32)
            y_ref[r, cols] = (o * gt * _sigmoid(gt)).astype(y_ref.dtype)

    def fast_chunk(ci, carry):
        r = chunk_rows(ci)
        k_up = (k_ref[r, :] * jnp.exp2(-b_ref[r, :])).astype(BF16)
        scores = [jnp.where(causal, _dot_nt(qin_ref[r, cols], k_up[:, cols]), 0.0).astype(BF16)
                  for cols in heads]
        finish(ci, scores)
        return carry

    def safe_chunk(ci, carry):
        r = chunk_rows(ci)
        dec_l = jnp.exp2(jnp.dot(sums[2 * c:], g2_ref[ci], preferred_element_type=F32))
        scores = []
        for cols in heads:
            q = q_ref[r, cols].astype(F32)
            k = k_ref[r, cols]
            sc = jnp.where(lvl == 0, _dot_nt(q.astype(BF16), k.astype(BF16)), 0.0)
            for l in range(n_lvl):
                d_l = dec_l[l * c:(l + 1) * c, cols]
                s_l = _dot_nt((q * d_l).astype(BF16), (k * d_l).astype(BF16))
                sc = jnp.where(lvl == l + 1, s_l, sc)
            scores.append(sc.astype(BF16))
        finish(ci, scores)
        return carry

    @pl.when(b_min >= -HG_FAST_LOG2)
    def _():
        lax.fori_loop(0, n_chunks, fast_chunk, 0, unroll=2)

    @pl.when(b_min < -HG_FAST_LOG2)
    def _():
        lax.fori_loop(0, n_chunks, safe_chunk, 0)


def _hgrn(proj, a_f, lb_logits, norm_w):
    ts, c, hg = HG_TS, HG_C, HG_HEADS
    sums, lvl, n_lvl = _hgrn_constants(c)
    w = hg * HEAD_DIM
    per_blk = WIDTH // w
    grid = (N_HEADS // hg, SEQ // ts)
    return pl.pallas_call(
        functools.partial(_hgrn_kernel, c=c, n_lvl=n_lvl),
        grid=grid,
        in_specs=[
            pl.BlockSpec((ts, w), lambda h, i: (i, BLK_AQ * per_blk + h)),
            pl.BlockSpec((ts, w), lambda h, i: (i, h)),
            pl.BlockSpec((ts, w), lambda h, i: (i, BLK_AI * per_blk + h)),
            pl.BlockSpec((ts, w), lambda h, i: (i, BLK_AG * per_blk + h)),
            pl.BlockSpec((lb_logits.shape[0], w), lambda h, i: (0, h)),
            pl.BlockSpec((1, HEAD_DIM), lambda h, i: (0, 0)),
            pl.BlockSpec(sums.shape, lambda h, i: (0, 0)),
            pl.BlockSpec(lvl.shape, lambda h, i: (0, 0)),
        ],
        out_specs=pl.BlockSpec((ts, w), lambda h, i: (i, h)),
        out_shape=jax.ShapeDtypeStruct((SEQ, WIDTH), BF16),
        scratch_shapes=[
            pltpu.VMEM((hg, HEAD_DIM, HEAD_DIM), F32),
            pltpu.VMEM((ts, w), F32),
            pltpu.VMEM((ts, w), F32),
            pltpu.VMEM((ts, w), BF16),
            pltpu.VMEM((ts, w), BF16),
            pltpu.VMEM((ts // c, 2 * c, w), BF16),
            pltpu.VMEM((ts // c, 8, w), F32),
            pltpu.VMEM((c, w), F32),
        ],
        compiler_params=_params("arbitrary", "arbitrary"),
        name="hgrn",
    )(proj, a_f, proj, proj, lb_logits, norm_w, sums, lvl)


def _fox_kernel(q_ref, k_ref, v_ref, cx_ref, nc_ref, o_ref, vt_ref, nrm_ref, m_ref, acc_ref, *, tq, tk):
    g = pl.program_id(0)
    i = pl.program_id(1)
    n_heads = vt_ref.shape[0]
    d = HEAD_DIM
    s_len = k_ref.shape[0]
    n_sub = tq // tk
    n_kt = s_len // tk
    vrows = vt_ref.shape[2]
    heads = [(hh, slice(hh * d, (hh + 1) * d)) for hh in range(n_heads)]

    @pl.when(i == 0)
    def _():
        ones_row = jnp.where(lax.broadcasted_iota(jnp.int32, (vrows - d, tk), 0) == 0, 1.0, 0.0)
        wg = n_heads * d
        same_head = (lax.broadcasted_iota(jnp.int32, (wg, wg), 0) // d
                     == lax.broadcasted_iota(jnp.int32, (wg, wg), 1) // d)
        ones_blk = jnp.where(same_head, 1.0, 0.0).astype(BF16)

        def sq_norms(x):
            xf = x.astype(F32)
            return jnp.dot((xf * xf).astype(BF16), ones_blk, preferred_element_type=F32)

        kn = jnp.zeros((tk, wg), F32)
        qn = jnp.zeros((tk, wg), F32)
        for r in range(n_kt):
            rows = slice(r * tk, (r + 1) * tk)
            for hh, cols in heads:
                vt_ref[hh, r, 0:d, :] = v_ref[rows, cols].astype(F32).T.astype(BF16)
                vt_ref[hh, r, d:vrows, :] = ones_row.astype(BF16)
            kn = jnp.maximum(kn, sq_norms(k_ref[rows, :]))
            qn = jnp.maximum(qn, sq_norms(q_ref[rows, :]))
        nrm = 2.0 * FOX_NORM_MARGIN * jnp.sqrt(jnp.max(qn, axis=0, keepdims=True)
                                               * jnp.max(kn, axis=0, keepdims=True))
        for hh, cols in heads:
            nrm_ref[hh] = jnp.broadcast_to(nrm[:, cols], nrm_ref.shape[1:])

    q_rows = pl.ds(pl.multiple_of(i * tq, tq), tq)
    nc_q0 = nc_ref[pl.ds(pl.multiple_of(i * tq, tq), 1), :]
    ends = nc_ref[pl.ds(tk - 1, n_kt, stride=tk), :]
    lane = lax.broadcasted_iota(jnp.int32, ends.shape, 1)
    rsel = lax.broadcasted_iota(jnp.int32, (d, tq), 0)
    qa, r_lo = [], []
    for hh, cols in heads:
        h = g * n_heads + hh
        thr = nc_q0 - nrm_ref[hh, 0:1, :] - FOX_SKIP_LOG2
        r_lo.append(jnp.sum(jnp.where((ends < thr) & (lane == h), 1, 0)))
        sel = jnp.where((rsel < 3 * N_HEADS) & ((rsel & (N_HEADS - 1)) == h), 1.0, 0.0).astype(BF16)
        qa.append(jnp.concatenate([q_ref[q_rows, cols].astype(F32).T.astype(BF16), sel], axis=0))

    def scores(hh, cols, r, mask_off):
        ks = pl.ds(pl.multiple_of(r * tk, tk), tk)
        ka = jnp.concatenate([k_ref[ks, cols], cx_ref[ks, :]], axis=1)
        s = jnp.dot(ka, qa[hh], preferred_element_type=F32)
        if mask_off is not None:
            kid = lax.broadcasted_iota(jnp.int32, s.shape, 0) + mask_off
            qid = lax.broadcasted_iota(jnp.int32, s.shape, 1)
            s = jnp.where(kid <= qid, s, -jnp.inf)
        return s

    def first_step(tiles):
        ss = [[scores(hh, cols, r, off) for r, off in tiles] for hh, cols in heads]
        ms = []
        for hh, _ in heads:
            m = jnp.max(ss[hh][0], axis=0, keepdims=True)
            for s in ss[hh][1:]:
                m = jnp.maximum(m, jnp.max(s, axis=0, keepdims=True))
            ms.append(m)
        for hh, _ in heads:
            pv = None
            for (r, _), s in zip(tiles, ss[hh]):
                p = jnp.exp2(s - ms[hh]).astype(BF16)
                dd = jnp.dot(vt_ref[hh, r], p, preferred_element_type=F32)
                pv = dd if pv is None else pv + dd
            acc_ref[hh] = pv
            m_ref[hh] = ms[hh]

    def later_step(hh, cols, r):
        s = scores(hh, cols, r, None)
        m_prev = m_ref[hh]
        m_new = jnp.maximum(m_prev, jnp.max(s, axis=0, keepdims=True))
        pv = jnp.dot(vt_ref[hh, r], jnp.exp2(s - m_new).astype(BF16), preferred_element_type=F32)
        acc_ref[hh] = jnp.exp2(m_prev - m_new) * acc_ref[hh] + pv
        m_ref[hh] = m_new

    diag = [(i * n_sub + rr, rr * tk) for rr in range(n_sub)]

    @pl.when(i == 0)
    def _():
        first_step(diag)

    @pl.when(i > 0)
    def _():
        first_step([(i * n_sub - 1, None)] + diag)
        for hh, cols in heads:
            def off_diag(r, carry, hh=hh, cols=cols):
                later_step(hh, cols, r)
                return carry
            lax.fori_loop(r_lo[hh], i * n_sub - 1, off_diag, 0)

    for hh, cols in heads:
        acc = acc_ref[hh]
        o_ref[:, cols] = (acc[0:d] / acc[d:d + 1]).T.astype(o_ref.dtype)


def _fox(proj, cx, nc):
    tq, tk, hg = FOX_TQ, FOX_TK, FOX_HEADS
    w = hg * HEAD_DIM
    per_blk = WIDTH // w
    vrows = HEAD_DIM + 16
    return pl.pallas_call(
        functools.partial(_fox_kernel, tq=tq, tk=tk),
        grid=(N_HEADS // hg, SEQ // tq),
        in_specs=[
            pl.BlockSpec((SEQ, w), lambda g, i: (0, BLK_BQ * per_blk + g)),
            pl.BlockSpec((SEQ, w), lambda g, i: (0, BLK_BK * per_blk + g)),
            pl.BlockSpec((SEQ, w), lambda g, i: (0, BLK_BV * per_blk + g)),
            pl.BlockSpec((SEQ, 128), lambda g, i: (0, 0)),
            pl.BlockSpec((SEQ, 128), lambda g, i: (0, 0)),
        ],
        out_specs=pl.BlockSpec((tq, w), lambda g, i: (i, g)),
        out_shape=jax.ShapeDtypeStruct((SEQ, WIDTH), BF16),
        scratch_shapes=[
            pltpu.VMEM((hg, SEQ // tk, vrows, tk), BF16),
            pltpu.VMEM((hg, 8, 128), F32),
            pltpu.VMEM((hg, 1, tq), F32),
            pltpu.VMEM((hg, vrows, tq), F32),
        ],
        compiler_params=_params("arbitrary", "arbitrary"),
        name="fox",
    )(proj, proj, proj, cx, nc)


def _merge_kernel(oa_ref, ob_ref, ga_ref, gb_ref, wa_ref, wb_ref, wo_ref, x_ref, nw_ref, npre_ref,
                  out_ref, h2_ref):
    ya = jnp.dot(oa_ref[...], wa_ref[...], preferred_element_type=F32)
    yb = jnp.dot(ob_ref[...], wb_ref[...], preferred_element_type=F32)
    merged = _sigmoid(ga_ref[...].astype(F32)) * ya + _sigmoid(gb_ref[...].astype(F32)) * yb
    u = jnp.dot(merged.astype(BF16), wo_ref[...], preferred_element_type=F32)
    u = u * lax.rsqrt(jnp.mean(u * u, axis=-1, keepdims=True) + RMS_EPS) * nw_ref[...]
    x1 = x_ref[...] + u
    out_ref[...] = x1
    ms = jnp.mean(x1 * x1, axis=-1, keepdims=True)
    h2_ref[...] = (x1 * lax.rsqrt(ms + RMS_EPS) * npre_ref[...]).astype(BF16)


def _merge(y_a, y_b, proj, w_up_a, w_up_b, w_o, x2, norm_w, norm_ffn_pre):
    tm = MERGE_TM
    once = pl.Buffered(1)
    return pl.pallas_call(
        _merge_kernel,
        grid=(SEQ // tm,),
        in_specs=[
            pl.BlockSpec((tm, WIDTH), lambda i: (i, 0)),
            pl.BlockSpec((tm, WIDTH), lambda i: (i, 0)),
            pl.BlockSpec((tm, D_MODEL), lambda i: (i, BLK_GA // 2)),
            pl.BlockSpec((tm, D_MODEL), lambda i: (i, BLK_GB // 2)),
            pl.BlockSpec((WIDTH, D_MODEL), lambda i: (0, 0), pipeline_mode=once),
            pl.BlockSpec((WIDTH, D_MODEL), lambda i: (0, 0), pipeline_mode=once),
            pl.BlockSpec((D_MODEL, D_MODEL), lambda i: (0, 0), pipeline_mode=once),
            pl.BlockSpec((tm, D_MODEL), lambda i: (i, 0)),
            pl.BlockSpec((1, D_MODEL), lambda i: (0, 0)),
            pl.BlockSpec((1, D_MODEL), lambda i: (0, 0)),
        ],
        out_specs=[
            pl.BlockSpec((tm, D_MODEL), lambda i: (i, 0)),
            pl.BlockSpec((tm, D_MODEL), lambda i: (i, 0)),
        ],
        out_shape=[
            jax.ShapeDtypeStruct((SEQ, D_MODEL), F32),
            jax.ShapeDtypeStruct((SEQ, D_MODEL), BF16),
        ],
        compiler_params=_params("arbitrary"),
        name="merge",
    )(y_a, y_b, proj, proj, w_up_a, w_up_b, w_o, x2, norm_w, norm_ffn_pre)


def _ffn_residual_copy(x_hbm, xres_ref, sem, i):
    tm = xres_ref.shape[0]
    return pltpu.make_async_copy(x_hbm.at[pl.ds(pl.multiple_of(i * tm, tm), tm), :], xres_ref, sem)


def _ffn_kernel(h_ref, x_hbm, npost_ref, wg_ref, wu_ref, wd_ref, out_ref, xres_ref, sem):
    i = pl.program_id(0)
    f = pl.program_id(1)

    @pl.when(f == 0)
    def _():
        _ffn_residual_copy(x_hbm, xres_ref, sem, i).start()
        out_ref[...] = jnp.zeros_like(out_ref)

    h = h_ref[...]
    gate = jnp.dot(h, wg_ref[...].astype(BF16), preferred_element_type=F32)
    up = jnp.dot(h, wu_ref[...].astype(BF16), preferred_element_type=F32)
    act = (gate * _sigmoid(gate) * up).astype(BF16)
    out_ref[...] += jnp.dot(act, wd_ref[...].astype(BF16), preferred_element_type=F32)

    @pl.when(f == pl.num_programs(1) - 1)
    def _():
        _ffn_residual_copy(x_hbm, xres_ref, sem, i).wait()
        u = out_ref[...]
        u = u * lax.rsqrt(jnp.mean(u * u, axis=-1, keepdims=True) + RMS_EPS) * npost_ref[...]
        out_ref[...] = xres_ref[...] + u


def _ffn(h2, x1, norm_post, w_in, w_down):
    tm, tf = FFN_TM, FFN_TF
    nf = D_FF // tf
    return pl.pallas_call(
        _ffn_kernel,
        grid=(SEQ // tm, nf),
        in_specs=[
            pl.BlockSpec((tm, D_MODEL), lambda i, f: (i, 0)),
            pl.BlockSpec(memory_space=pl.ANY),
            pl.BlockSpec((1, D_MODEL), lambda i, f: (0, 0)),
            pl.BlockSpec((D_MODEL, tf), lambda i, f: (0, f)),
            pl.BlockSpec((D_MODEL, tf), lambda i, f: (0, f + nf)),
            pl.BlockSpec((tf, D_MODEL), lambda i, f: (f, 0)),
        ],
        out_specs=pl.BlockSpec((tm, D_MODEL), lambda i, f: (i, 0)),
        out_shape=jax.ShapeDtypeStruct((SEQ, D_MODEL), F32),
        scratch_shapes=[pltpu.VMEM((tm, D_MODEL), F32), pltpu.SemaphoreType.DMA(())],
        compiler_params=_params("arbitrary", "arbitrary"),
        name="ffn",
    )(h2, x1, norm_post, w_in, w_in, w_down)


def kernel(x, w_in, b_fox_f, hgrn_lb_logits, hgrn_norm_w, w_up_a, w_up_b, w_o, norm_mix_pre,
           norm_mix_post, norm_ffn_pre, norm_ffn_post, w_ffn_in, w_ffn_down):
    assert x.shape == (1, SEQ, D_MODEL) and w_in.shape[0] == 1
    w_t = w_in[0].T
    x2 = x[0]

    h, a_f, cx, nc = _prenorm(x2, norm_mix_pre, w_t, b_fox_f.reshape(N_HEADS, 1))
    proj = _inproj(h, w_t)
    y_a = _hgrn(proj, a_f, hgrn_lb_logits, hgrn_norm_w)
    y_b = _fox(proj, cx, nc)
    x1, h2 = _merge(y_a, y_b, proj, w_up_a[0].astype(BF16), w_up_b[0].astype(BF16), w_o[0].astype(BF16),
                    x2, norm_mix_post, norm_ffn_pre)
    out = _ffn(h2, x1, norm_ffn_post, w_ffn_in[0], w_ffn_down[0])
    return out[None]
```

```python
import functools

import numpy as np
import jax
import jax.numpy as jnp
from jax import lax
from jax.experimental import pallas as pl
from jax.experimental.pallas import tpu as pltpu

F32 = jnp.float32
BF16 = jnp.bfloat16

D_MODEL = 2048
SEQ = 8192
HEAD_DIM = 128
N_HEADS = 8
WIDTH = N_HEADS * HEAD_DIM
D_FF = 5632
RMS_EPS = 1e-6
N_PROJ = 3 * WIDTH + 2 * D_MODEL + 3 * WIDTH

VMEM_LIMIT_BYTES = 56 * 1024 * 1024
LANES = 128
F32_SUBLANES = 8
BF16_SUBLANES = 16

SRC_AQ, SRC_AF, SRC_AI, SRC_AG, SRC_BQ, SRC_BK, SRC_BV = 0, 1, 2, 3, 4, 5, 6
IN_NA = 7
BLK_AQ, BLK_AI, BLK_GA, BLK_GB, BLK_AG, BLK_BQ, BLK_BK, BLK_BV = 0, 1, 2, 4, 6, 7, 8, 9
HEADS_PER_BLK = WIDTH // HEAD_DIM

PRE_TM = 512
IN_TM, IN_TN = 2048, 512
HG_TS, HG_C, HG_HEADS = 1024, 64, 8
HG_FAST_LOG2 = 96.0
FOX_TQ, FOX_TK, FOX_HEADS = 512, 256, 2
LOG2E = 1.4426950408889634
FOX_SKIP_LOG2 = 160.0
FOX_NORM_MARGIN = 1.01
MERGE_TM = 256
MERGE_STAGE_ROWS = 256
FFN_TM, FFN_TF = 1024, 256


def _params(*sem):
    return pltpu.CompilerParams(dimension_semantics=sem, vmem_limit_bytes=VMEM_LIMIT_BYTES)


def _dot_nt(a, b):
    return lax.dot_general(a, b, (((1,), (1,)), ((), ())), preferred_element_type=F32)


def _dot_tn(a, b):
    return lax.dot_general(a, b, (((0,), (0,)), ((), ())), preferred_element_type=F32)


def _log_sigmoid(x):
    return jnp.minimum(x, 0.0) - jnp.log(1.0 + jnp.exp(-jnp.abs(x)))


def _sigmoid(x):
    return 1.0 / (1.0 + jnp.exp(-x))


def _split3(x):
    p0 = x.astype(BF16)
    r1 = x - p0.astype(F32)
    p1 = r1.astype(BF16)
    p2 = (r1 - p1.astype(F32)).astype(BF16)
    return p0, p1, p2


def _prenorm_kernel(x_ref, nw_ref, waf_ref, wf_ref, bf_ref, tri_ref, h_ref, af_ref, cx_ref, nc_ref,
                    wafb_ref, carry_ref):
    i = pl.program_id(0)
    tm = x_ref.shape[0]

    @pl.when(i == 0)
    def _():
        wafb_ref[...] = waf_ref[...].T.astype(BF16)
        carry_ref[...] = jnp.zeros_like(carry_ref)

    x = x_ref[...]
    ms = jnp.mean(x * x, axis=-1, keepdims=True)
    h_ref[...] = (x * lax.rsqrt(ms + RMS_EPS) * nw_ref[...]).astype(BF16)
    hb = h_ref[...]
    af_ref[...] = jnp.dot(hb, wafb_ref[...], preferred_element_type=F32)
    logit = _dot_nt(wf_ref[...].astype(BF16), hb) + bf_ref[...]
    nls = _log_sigmoid(logit) * (-LOG2E)
    tri = tri_ref[...]
    loc = sum(jnp.dot(p, tri, preferred_element_type=F32) for p in _split3(nls))
    nc = loc + carry_ref[:, 0:1]
    carry_ref[...] = jnp.broadcast_to(nc[:, tm - 1:tm], carry_ref.shape)
    pad = jnp.zeros((LANES - N_HEADS, tm), F32)
    nc_ref[...] = jnp.concatenate([nc, pad], axis=0).T
    parts = [p.astype(F32) for p in _split3(nc)]
    pad3 = jnp.zeros((LANES - 3 * N_HEADS, tm), F32)
    cx_ref[...] = jnp.concatenate(parts + [pad3], axis=0).T.astype(BF16)


def _prenorm(x2, norm_w, w, bf_col):
    tm = PRE_TM
    tri = jnp.asarray(np.triu(np.ones((tm, tm), np.float32)), BF16)
    once = pl.Buffered(1)
    return pl.pallas_call(
        _prenorm_kernel,
        grid=(SEQ // tm,),
        in_specs=[
            pl.BlockSpec((tm, D_MODEL), lambda i: (i, 0)),
            pl.BlockSpec((1, D_MODEL), lambda i: (0, 0)),
            pl.BlockSpec((WIDTH, D_MODEL), lambda i: (SRC_AF, 0), pipeline_mode=once),
            pl.BlockSpec((N_HEADS, D_MODEL), lambda i: (IN_NA * WIDTH // N_HEADS, 0), pipeline_mode=once),
            pl.BlockSpec((N_HEADS, 1), lambda i: (0, 0)),
            pl.BlockSpec((tm, tm), lambda i: (0, 0), pipeline_mode=once),
        ],
        out_specs=[
            pl.BlockSpec((tm, D_MODEL), lambda i: (i, 0)),
            pl.BlockSpec((tm, WIDTH), lambda i: (i, 0)),
            pl.BlockSpec((tm, LANES), lambda i: (i, 0)),
            pl.BlockSpec((tm, LANES), lambda i: (i, 0)),
        ],
        out_shape=[
            jax.ShapeDtypeStruct((SEQ, D_MODEL), BF16),
            jax.ShapeDtypeStruct((SEQ, WIDTH), F32),
            jax.ShapeDtypeStruct((SEQ, LANES), BF16),
            jax.ShapeDtypeStruct((SEQ, LANES), F32),
        ],
        scratch_shapes=[pltpu.VMEM((D_MODEL, WIDTH), BF16), pltpu.VMEM((N_HEADS, LANES), F32)],
        compiler_params=_params("arbitrary"),
        name="prenorm",
    )(x2, norm_w, w, w, bf_col, tri)


def _inproj_kernel(h_ref, wa_ref, wb_ref, proj_ref, wc_ref):
    j = pl.program_id(1)
    i = pl.program_id(2)
    tm, tn = proj_ref.shape
    nb = WIDTH // tn
    src = _inproj_src_block(j, nb)

    @pl.when(i == 0)
    def _():
        @pl.when(src < IN_NA * nb)
        def _():
            scale = jnp.where(src // nb == SRC_BQ, LOG2E * HEAD_DIM ** -0.5, 1.0)
            wc_ref[...] = (wa_ref[...] * scale).astype(BF16)

        @pl.when(src >= IN_NA * nb)
        def _():
            wc_ref[...] = jnp.concatenate([wa_ref[N_HEADS:, :], wb_ref[:N_HEADS, :]], axis=0).astype(BF16)

    rows = pl.ds(pl.multiple_of(i * tm, tm), tm)
    proj_ref[...] = _dot_nt(h_ref[rows, :], wc_ref[...]).astype(BF16)


def _inproj_src_block(j, nb):
    return jnp.where(j < SRC_AF * nb, j, j + nb)


def _inproj_out_block(j, nb):
    src = _inproj_src_block(j, nb)
    g = src // nb
    g_out = jnp.int32(BLK_AQ)
    for g_src, g_dst in ((SRC_AI, BLK_AI), (SRC_AG, BLK_AG), (SRC_BQ, BLK_BQ), (SRC_BK, BLK_BK), (SRC_BV, BLK_BV)):
        g_out = jnp.where(g == g_src, g_dst, g_out)
    return jnp.where(g < IN_NA, g_out * nb + src % nb, src - IN_NA * nb + BLK_GA * nb)


def _inproj(h, w):
    tm, tn = IN_TM, IN_TN
    nb = WIDTH // tn
    half = SEQ // 2
    n_i = half // tm
    first_gate = IN_NA * nb
    grid = (2, N_PROJ // tn, n_i)
    return pl.pallas_call(
        _inproj_kernel,
        grid=grid,
        in_specs=[
            pl.BlockSpec((half, D_MODEL), lambda s, j, i: (s, 0), pipeline_mode=pl.Buffered(1)),
            pl.BlockSpec((tn, D_MODEL), lambda s, j, i: (_inproj_src_block(j, nb), 0)),
            pl.BlockSpec((tn, D_MODEL), lambda s, j, i: (jnp.maximum(_inproj_src_block(j, nb), first_gate) + 1, 0)),
        ],
        out_specs=pl.BlockSpec((tm, tn), lambda s, j, i: (s * n_i + i, _inproj_out_block(j, nb))),
        out_shape=jax.ShapeDtypeStruct((SEQ, N_PROJ), BF16),
        scratch_shapes=[pltpu.VMEM((tn, D_MODEL), BF16)],
        compiler_params=_params("arbitrary", "arbitrary", "arbitrary"),
        name="inproj",
    )(h, w, w)


def _hgrn_constants(c):
    n_lvl = int(np.log2(c))
    t = np.arange(c)[:, None]
    j = np.arange(c)[None, :]
    blocks = [(j <= t), (j > t)]
    level = np.full((c, c), -1, np.int32)
    level[np.arange(c), np.arange(c)] = 0
    for l in range(n_lvl):
        b = 2 << l
        mid = (t // b) * b + b // 2 - 1
        second = (t % b) >= b // 2
        m = np.where(second, (j > mid) & (j <= t), (j > t) & (j <= mid))
        blocks.append(m)
        s = np.arange(c)[None, :]
        own = (t // b == s // b) & second & ((s % b) < b // 2)
        level[own] = l + 1
    sums = np.concatenate(blocks, axis=0).astype(np.float32)
    sums2 = np.concatenate([sums, sums], axis=1)
    return jnp.asarray(sums2, BF16), jnp.asarray(level), n_lvl


def _hgrn_kernel(q_ref, z_ref, v_ref, g_ref, lbl_ref, nw_ref, sums_ref, lvl_ref, y_ref,
                 st_ref, k_ref, b_ref, kout_ref, qin_ref, g2_ref, dlast_ref, bmin_ref, *, c, n_lvl):
    n_heads = st_ref.shape[0]
    d = HEAD_DIM

    @pl.when(pl.program_id(1) == 0)
    def _():
        st_ref[...] = jnp.zeros_like(st_ref)

    logits = lbl_ref[...]
    ex = jnp.exp(logits - jnp.max(logits, axis=0, keepdims=True))
    lb = ex[0:1, :] / jnp.sum(ex, axis=0, keepdims=True)
    one_m_lb = 1.0 - lb
    nw = nw_ref[...]
    sums = sums_ref[...]
    lvl = lvl_ref[...]
    n_chunks = q_ref.shape[0] // c

    heads = [slice(hh * d, (hh + 1) * d) for hh in range(n_heads)]
    causal = lvl >= 0

    def chunk_rows(ci):
        return pl.ds(pl.multiple_of(ci * c, c), c)

    bmin_ref[...] = jnp.zeros_like(bmin_ref)

    def prepare(ci, carry):
        r = chunk_rows(ci)
        z = z_ref[r, :]
        e = jnp.exp(-jnp.abs(z))
        inv = 1.0 / (1.0 + e)
        pos = z >= 0.0
        sig = jnp.where(pos, inv, e * inv)
        sig_n = jnp.where(pos, e * inv, inv)
        g = jnp.log2(lb + one_m_lb * sig)
        k_all = one_m_lb * sig_n
        g_hi = g.astype(BF16)
        g_lo = (g - g_hi.astype(F32)).astype(BF16)
        g2 = jnp.concatenate([g_hi, g_lo], axis=0)
        e01 = jnp.dot(sums[0:2 * c], g2, preferred_element_type=F32)
        b = e01[0:c]
        dec_b = jnp.exp2(b)
        g2_ref[ci] = g2
        b_ref[r, :] = b
        k_ref[r, :] = k_all
        kout_ref[r, :] = (k_all * jnp.exp2(e01[c:2 * c])).astype(BF16)
        qin_ref[r, :] = (q_ref[r, :].astype(F32) * dec_b).astype(BF16)
        dlast_ref[ci] = jnp.broadcast_to(dec_b[c - 1:c, :], dlast_ref.shape[1:])
        bmin_ref[...] = jnp.minimum(bmin_ref[...], b)
        return carry

    lax.fori_loop(0, n_chunks, prepare, 0, unroll=2)
    b_min = jnp.min(bmin_ref[...])

    def finish(ci, scores):
        r = chunk_rows(ci)
        o_intra, inc = [], []
        for hh, cols in enumerate(heads):
            v = v_ref[r, cols]
            o_intra.append(jnp.dot(scores[hh], v, preferred_element_type=F32))
            inc.append(_dot_tn(v, kout_ref[r, cols]))
        for hh, cols in enumerate(heads):
            st = st_ref[hh]
            o = o_intra[hh] + _dot_nt(qin_ref[r, cols], st.astype(BF16))
            st_ref[hh] = st * dlast_ref[ci, 0:1, cols] + inc[hh]
            o = o * lax.rsqrt(jnp.mean(o * o, axis=-1, keepdims=True) + RMS_EPS) * nw
            gt = g_ref[r, cols].astype(F32)
            y_ref[r, cols] = (o * gt * _sigmoid(gt)).astype(y_ref.dtype)

    def fast_chunk(ci, carry):
        r = chunk_rows(ci)
        k_up = (k_ref[r, :] * jnp.exp2(-b_ref[r, :])).astype(BF16)
        scores = [jnp.where(causal, _dot_nt(qin_ref[r, cols], k_up[:, cols]), 0.0).astype(BF16)
                  for cols in heads]
        finish(ci, scores)
        return carry

    def safe_chunk(ci, carry):
        r = chunk_rows(ci)
        dec_l = jnp.exp2(jnp.dot(sums[2 * c:], g2_ref[ci], preferred_element_type=F32))
        scores = []
        for cols in heads:
            q = q_ref[r, cols].astype(F32)
            k = k_ref[r, cols]
            sc = jnp.where(lvl == 0, _dot_nt(q.astype(BF16), k.astype(BF16)), 0.0)
            for l in range(n_lvl):
                d_l = dec_l[l * c:(l + 1) * c, cols]
                s_l = _dot_nt((q * d_l).astype(BF16), (k * d_l).astype(BF16))
                sc = jnp.where(lvl == l + 1, s_l, sc)
            scores.append(sc.astype(BF16))
        finish(ci, scores)
        return carry

    @pl.when(b_min >= -HG_FAST_LOG2)
    def _():
        lax.fori_loop(0, n_chunks, fast_chunk, 0, unroll=2)

    @pl.when(b_min < -HG_FAST_LOG2)
    def _():
        lax.fori_loop(0, n_chunks, safe_chunk, 0)


def _hgrn(proj, a_f, lb_logits, norm_w):
    ts, c, hg = HG_TS, HG_C, HG_HEADS
    sums, lvl, n_lvl = _hgrn_constants(c)
    w = hg * HEAD_DIM
    per_blk = WIDTH // w
    grid = (N_HEADS // hg, SEQ // ts)
    return pl.pallas_call(
        functools.partial(_hgrn_kernel, c=c, n_lvl=n_lvl),
        grid=grid,
        in_specs=[
            pl.BlockSpec((ts, w), lambda h, i: (i, BLK_AQ * per_blk + h)),
            pl.BlockSpec((ts, w), lambda h, i: (i, h)),
            pl.BlockSpec((ts, w), lambda h, i: (i, BLK_AI * per_blk + h)),
            pl.BlockSpec((ts, w), lambda h, i: (i, BLK_AG * per_blk + h)),
            pl.BlockSpec((lb_logits.shape[0], w), lambda h, i: (0, h)),
            pl.BlockSpec((1, HEAD_DIM), lambda h, i: (0, 0)),
            pl.BlockSpec(sums.shape, lambda h, i: (0, 0)),
            pl.BlockSpec(lvl.shape, lambda h, i: (0, 0)),
        ],
        out_specs=pl.BlockSpec((ts, w), lambda h, i: (i, h)),
        out_shape=jax.ShapeDtypeStruct((SEQ, WIDTH), BF16),
        scratch_shapes=[
            pltpu.VMEM((hg, HEAD_DIM, HEAD_DIM), F32),
            pltpu.VMEM((ts, w), F32),
            pltpu.VMEM((ts, w), F32),
            pltpu.VMEM((ts, w), BF16),
            pltpu.VMEM((ts, w), BF16),
            pltpu.VMEM((ts // c, 2 * c, w), BF16),
            pltpu.VMEM((ts // c, F32_SUBLANES, w), F32),
            pltpu.VMEM((c, w), F32),
        ],
        compiler_params=_params("arbitrary", "arbitrary"),
        name="hgrn",
    )(proj, a_f, proj, proj, lb_logits, norm_w, sums, lvl)


def _fox_kernel(q_ref, k_ref, v_ref, cx_ref, nc_ref, o_ref, vt_ref, nrm_ref, m_ref, acc_ref, *, tq, tk):
    g = pl.program_id(0)
    i = pl.program_id(1)
    n_heads = vt_ref.shape[0]
    d = HEAD_DIM
    s_len = k_ref.shape[0]
    n_sub = tq // tk
    n_kt = s_len // tk
    vrows = vt_ref.shape[2]
    heads = [(hh, slice(hh * d, (hh + 1) * d)) for hh in range(n_heads)]

    @pl.when(i == 0)
    def _():
        ones_row = jnp.where(lax.broadcasted_iota(jnp.int32, (vrows - d, tk), 0) == 0, 1.0, 0.0)
        wg = n_heads * d
        same_head = (lax.broadcasted_iota(jnp.int32, (wg, wg), 0) // d
                     == lax.broadcasted_iota(jnp.int32, (wg, wg), 1) // d)
        ones_blk = jnp.where(same_head, 1.0, 0.0).astype(BF16)

        def sq_norms(x):
            xf = x.astype(F32)
            return jnp.dot((xf * xf).astype(BF16), ones_blk, preferred_element_type=F32)

        kn = jnp.zeros((tk, wg), F32)
        qn = jnp.zeros((tk, wg), F32)
        for r in range(n_kt):
            rows = slice(r * tk, (r + 1) * tk)
            for hh, cols in heads:
                vt_ref[hh, r, 0:d, :] = v_ref[rows, cols].astype(F32).T.astype(BF16)
                vt_ref[hh, r, d:vrows, :] = ones_row.astype(BF16)
            kn = jnp.maximum(kn, sq_norms(k_ref[rows, :]))
            qn = jnp.maximum(qn, sq_norms(q_ref[rows, :]))
        nrm = 2.0 * FOX_NORM_MARGIN * jnp.sqrt(jnp.max(qn, axis=0, keepdims=True)
                                               * jnp.max(kn, axis=0, keepdims=True))
        for hh, cols in heads:
            nrm_ref[hh] = jnp.broadcast_to(nrm[:, cols], nrm_ref.shape[1:])

    q_rows = pl.ds(pl.multiple_of(i * tq, tq), tq)
    nc_q0 = nc_ref[pl.ds(pl.multiple_of(i * tq, tq), 1), :]
    ends = nc_ref[pl.ds(tk - 1, n_kt, stride=tk), :]
    lane = lax.broadcasted_iota(jnp.int32, ends.shape, 1)
    rsel = lax.broadcasted_iota(jnp.int32, (d, tq), 0)
    qa, r_lo = [], []
    for hh, cols in heads:
        h = g * n_heads + hh
        thr = nc_q0 - nrm_ref[hh, 0:1, :] - FOX_SKIP_LOG2
        r_lo.append(jnp.sum(jnp.where((ends < thr) & (lane == h), 1, 0)))
        sel = jnp.where((rsel < 3 * N_HEADS) & ((rsel & (N_HEADS - 1)) == h), 1.0, 0.0).astype(BF16)
        qa.append(jnp.concatenate([q_ref[q_rows, cols].astype(F32).T.astype(BF16), sel], axis=0))

    def scores(hh, cols, r, mask_off):
        ks = pl.ds(pl.multiple_of(r * tk, tk), tk)
        ka = jnp.concatenate([k_ref[ks, cols], cx_ref[ks, :]], axis=1)
        s = jnp.dot(ka, qa[hh], preferred_element_type=F32)
        if mask_off is not None:
            kid = lax.broadcasted_iota(jnp.int32, s.shape, 0) + mask_off
            qid = lax.broadcasted_iota(jnp.int32, s.shape, 1)
            s = jnp.where(kid <= qid, s, -jnp.inf)
        return s

    def first_step(tiles):
        ss = [[scores(hh, cols, r, off) for r, off in tiles] for hh, cols in heads]
        ms = []
        for hh, _ in heads:
            m = jnp.max(ss[hh][0], axis=0, keepdims=True)
            for s in ss[hh][1:]:
                m = jnp.maximum(m, jnp.max(s, axis=0, keepdims=True))
            ms.append(m)
        for hh, _ in heads:
            pv = None
            for (r, _), s in zip(tiles, ss[hh]):
                p = jnp.exp2(s - ms[hh]).astype(BF16)
                dd = jnp.dot(vt_ref[hh, r], p, preferred_element_type=F32)
                pv = dd if pv is None else pv + dd
            acc_ref[hh] = pv
            m_ref[hh] = ms[hh]

    def later_step(hh, cols, r):
        s = scores(hh, cols, r, None)
        m_prev = m_ref[hh]
        m_new = jnp.maximum(m_prev, jnp.max(s, axis=0, keepdims=True))
        pv = jnp.dot(vt_ref[hh, r], jnp.exp2(s - m_new).astype(BF16), preferred_element_type=F32)
        acc_ref[hh] = jnp.exp2(m_prev - m_new) * acc_ref[hh] + pv
        m_ref[hh] = m_new

    diag = [(i * n_sub + rr, rr * tk) for rr in range(n_sub)]

    @pl.when(i == 0)
    def _():
        first_step(diag)

    @pl.when(i > 0)
    def _():
        first_step([(i * n_sub - 1, None)] + diag)
        for hh, cols in heads:
            def off_diag(r, carry, hh=hh, cols=cols):
                later_step(hh, cols, r)
                return carry
            lax.fori_loop(r_lo[hh], i * n_sub - 1, off_diag, 0)

    for hh, cols in heads:
        acc = acc_ref[hh]
        o_ref[:, cols] = (acc[0:d] / acc[d:d + 1]).T.astype(o_ref.dtype)


def _fox(proj, cx, nc):
    tq, tk, hg = FOX_TQ, FOX_TK, FOX_HEADS
    w = hg * HEAD_DIM
    per_blk = WIDTH // w
    vrows = HEAD_DIM + BF16_SUBLANES
    return pl.pallas_call(
        functools.partial(_fox_kernel, tq=tq, tk=tk),
        grid=(N_HEADS // hg, SEQ // tq),
        in_specs=[
            pl.BlockSpec((SEQ, w), lambda g, i: (0, BLK_BQ * per_blk + g)),
            pl.BlockSpec((SEQ, w), lambda g, i: (0, BLK_BK * per_blk + g)),
            pl.BlockSpec((SEQ, w), lambda g, i: (0, BLK_BV * per_blk + g)),
            pl.BlockSpec((SEQ, LANES), lambda g, i: (0, 0)),
            pl.BlockSpec((SEQ, LANES), lambda g, i: (0, 0)),
        ],
        out_specs=pl.BlockSpec((tq, w), lambda g, i: (i, g)),
        out_shape=jax.ShapeDtypeStruct((SEQ, WIDTH), BF16),
        scratch_shapes=[
            pltpu.VMEM((hg, SEQ // tk, vrows, tk), BF16),
            pltpu.VMEM((hg, F32_SUBLANES, LANES), F32),
            pltpu.VMEM((hg, 1, tq), F32),
            pltpu.VMEM((hg, vrows, tq), F32),
        ],
        compiler_params=_params("arbitrary", "arbitrary"),
        name="fox",
    )(proj, proj, proj, cx, nc)


def _merge_load_weights(srcs, dsts, stage_ref, sems):
    rows = stage_ref.shape[1]
    chunks = [(src, dst, r0) for src, dst in zip(srcs, dsts) for r0 in range(0, src.shape[0], rows)]

    def copy(n):
        src, _, r0 = chunks[n]
        return pltpu.make_async_copy(src.at[pl.ds(r0, rows), :], stage_ref.at[n % 2], sems.at[n % 2])

    copy(0).start()
    for n, (_, dst, r0) in enumerate(chunks):
        if n + 1 < len(chunks):
            copy(n + 1).start()
        copy(n).wait()
        dst[r0:r0 + rows, :] = stage_ref[n % 2].astype(BF16)


def _merge_kernel(oa_ref, ob_ref, ga_ref, gb_ref, wa_hbm, wb_hbm, wo_hbm, x_ref, nw_ref, npre_ref,
                  out_ref, h2_ref, wa_ref, wb_ref, wo_ref, stage_ref, sems):
    @pl.when(pl.program_id(0) == 0)
    def _():
        _merge_load_weights((wa_hbm, wb_hbm, wo_hbm), (wa_ref, wb_ref, wo_ref), stage_ref, sems)

    ya = jnp.dot(oa_ref[...], wa_ref[...], preferred_element_type=F32)
    yb = jnp.dot(ob_ref[...], wb_ref[...], preferred_element_type=F32)
    merged = _sigmoid(ga_ref[...].astype(F32)) * ya + _sigmoid(gb_ref[...].astype(F32)) * yb
    u = jnp.dot(merged.astype(BF16), wo_ref[...], preferred_element_type=F32)
    u = u * lax.rsqrt(jnp.mean(u * u, axis=-1, keepdims=True) + RMS_EPS) * nw_ref[...]
    x1 = x_ref[...] + u
    out_ref[...] = x1
    ms = jnp.mean(x1 * x1, axis=-1, keepdims=True)
    h2_ref[...] = (x1 * lax.rsqrt(ms + RMS_EPS) * npre_ref[...]).astype(BF16)


def _merge(y_a, y_b, proj, w_up_a, w_up_b, w_o, x2, norm_w, norm_ffn_pre):
    tm = MERGE_TM
    hbm = pl.BlockSpec(memory_space=pl.ANY)
    return pl.pallas_call(
        _merge_kernel,
        grid=(SEQ // tm,),
        in_specs=[
            pl.BlockSpec((tm, WIDTH), lambda i: (i, 0)),
            pl.BlockSpec((tm, WIDTH), lambda i: (i, 0)),
            pl.BlockSpec((tm, D_MODEL), lambda i: (i, BLK_GA // 2)),
            pl.BlockSpec((tm, D_MODEL), lambda i: (i, BLK_GB // 2)),
            hbm,
            hbm,
            hbm,
            pl.BlockSpec((tm, D_MODEL), lambda i: (i, 0)),
            pl.BlockSpec((1, D_MODEL), lambda i: (0, 0)),
            pl.BlockSpec((1, D_MODEL), lambda i: (0, 0)),
        ],
        out_specs=[
            pl.BlockSpec((tm, D_MODEL), lambda i: (i, 0)),
            pl.BlockSpec((tm, D_MODEL), lambda i: (i, 0)),
        ],
        out_shape=[
            jax.ShapeDtypeStruct((SEQ, D_MODEL), F32),
            jax.ShapeDtypeStruct((SEQ, D_MODEL), BF16),
        ],
        scratch_shapes=[
            pltpu.VMEM((WIDTH, D_MODEL), BF16),
            pltpu.VMEM((WIDTH, D_MODEL), BF16),
            pltpu.VMEM((D_MODEL, D_MODEL), BF16),
            pltpu.VMEM((2, MERGE_STAGE_ROWS, D_MODEL), F32),
            pltpu.SemaphoreType.DMA((2,)),
        ],
        compiler_params=_params("arbitrary"),
        name="merge",
    )(y_a, y_b, proj, proj, w_up_a, w_up_b, w_o, x2, norm_w, norm_ffn_pre)


def _ffn_residual_copy(x_hbm, xres_ref, sem, i):
    tm = xres_ref.shape[0]
    return pltpu.make_async_copy(x_hbm.at[pl.ds(pl.multiple_of(i * tm, tm), tm), :], xres_ref, sem)


def _ffn_kernel(h_ref, x_hbm, npost_ref, wg_ref, wu_ref, wd_ref, out_ref, xres_ref, sem):
    i = pl.program_id(0)
    f = pl.program_id(1)

    @pl.when(f == 0)
    def _():
        _ffn_residual_copy(x_hbm, xres_ref, sem, i).start()
        out_ref[...] = jnp.zeros_like(out_ref)

    h = h_ref[...]
    gate = jnp.dot(h, wg_ref[...].astype(BF16), preferred_element_type=F32)
    up = jnp.dot(h, wu_ref[...].astype(BF16), preferred_element_type=F32)
    act = (gate * _sigmoid(gate) * up).astype(BF16)
    out_ref[...] += jnp.dot(act, wd_ref[...].astype(BF16), preferred_element_type=F32)

    @pl.when(f == pl.num_programs(1) - 1)
    def _():
        _ffn_residual_copy(x_hbm, xres_ref, sem, i).wait()
        u = out_ref[...]
        u = u * lax.rsqrt(jnp.mean(u * u, axis=-1, keepdims=True) + RMS_EPS) * npost_ref[...]
        out_ref[...] = xres_ref[...] + u


def _ffn(h2, x1, norm_post, w_in, w_down):
    tm, tf = FFN_TM, FFN_TF
    nf = D_FF // tf
    return pl.pallas_call(
        _ffn_kernel,
        grid=(SEQ // tm, nf),
        in_specs=[
            pl.BlockSpec((tm, D_MODEL), lambda i, f: (i, 0)),
            pl.BlockSpec(memory_space=pl.ANY),
            pl.BlockSpec((1, D_MODEL), lambda i, f: (0, 0)),
            pl.BlockSpec((D_MODEL, tf), lambda i, f: (0, f)),
            pl.BlockSpec((D_MODEL, tf), lambda i, f: (0, f + nf)),
            pl.BlockSpec((tf, D_MODEL), lambda i, f: (f, 0)),
        ],
        out_specs=pl.BlockSpec((tm, D_MODEL), lambda i, f: (i, 0)),
        out_shape=jax.ShapeDtypeStruct((SEQ, D_MODEL), F32),
        scratch_shapes=[pltpu.VMEM((tm, D_MODEL), F32), pltpu.SemaphoreType.DMA(())],
        compiler_params=_params("arbitrary", "arbitrary"),
        name="ffn",
    )(h2, x1, norm_post, w_in, w_in, w_down)


def kernel(x, w_in, b_fox_f, hgrn_lb_logits, hgrn_norm_w, w_up_a, w_up_b, w_o, norm_mix_pre,
           norm_mix_post, norm_ffn_pre, norm_ffn_post, w_ffn_in, w_ffn_down):
    assert x.shape == (1, SEQ, D_MODEL) and w_in.shape[0] == 1
    w_t = w_in[0].T
    x2 = x[0]

    h, a_f, cx, nc = _prenorm(x2, norm_mix_pre, w_t, b_fox_f.reshape(N_HEADS, 1))
    proj = _inproj(h, w_t)
    y_a = _hgrn(proj, a_f, hgrn_lb_logits, hgrn_norm_w)
    y_b = _fox(proj, cx, nc)
    x1, h2 = _merge(y_a, y_b, proj, w_up_a[0], w_up_b[0], w_o[0], x2, norm_mix_post, norm_ffn_pre)
    out = _ffn(h2, x1, norm_ffn_post, w_ffn_in[0], w_ffn_down[0])
    return out[None]
```

```python
import functools

import numpy as np
import jax
import jax.numpy as jnp
from jax import lax
from jax.experimental import pallas as pl
from jax.experimental.pallas import tpu as pltpu

F32 = jnp.float32
BF16 = jnp.bfloat16

D_MODEL = 2048
SEQ = 8192
HEAD_DIM = 128
N_HEADS = 8
WIDTH = N_HEADS * HEAD_DIM
D_FF = 5632
RMS_EPS = 1e-6
N_PROJ = 3 * WIDTH + 2 * D_MODEL + 3 * WIDTH

VMEM_LIMIT_BYTES = 56 * 1024 * 1024
LANES = 128
F32_SUBLANES = 8
BF16_SUBLANES = 16

SRC_AQ, SRC_AF, SRC_AI, SRC_AG, SRC_BQ, SRC_BK, SRC_BV = 0, 1, 2, 3, 4, 5, 6
IN_NA = 7
BLK_AQ, BLK_AI, BLK_GA, BLK_GB, BLK_AG, BLK_BQ, BLK_BK, BLK_BV = 0, 1, 2, 4, 6, 7, 8, 9

PRE_TM = 512
IN_TM, IN_TN = 2048, 512
HG_TS, HG_C, HG_HEADS = 1024, 64, 8
HG_FAST_LOG2 = 96.0
FOX_TQ, FOX_TK, FOX_HEADS = 512, 256, 2
LOG2E = 1.4426950408889634
FOX_SKIP_LOG2 = 160.0
FOX_NORM_MARGIN = 1.01
MERGE_TM = 256
MERGE_STAGE_ROWS = 256
FFN_TM, FFN_TF = 1024, 256


def _params(*sem):
    return pltpu.CompilerParams(dimension_semantics=sem, vmem_limit_bytes=VMEM_LIMIT_BYTES)


def _dot_nt(a, b):
    return lax.dot_general(a, b, (((1,), (1,)), ((), ())), preferred_element_type=F32)


def _dot_tn(a, b):
    return lax.dot_general(a, b, (((0,), (0,)), ((), ())), preferred_element_type=F32)


def _log_sigmoid(x):
    return jnp.minimum(x, 0.0) - jnp.log(1.0 + jnp.exp(-jnp.abs(x)))


def _sigmoid(x):
    return 1.0 / (1.0 + jnp.exp(-x))


def _split3(x):
    p0 = x.astype(BF16)
    r1 = x - p0.astype(F32)
    p1 = r1.astype(BF16)
    p2 = (r1 - p1.astype(F32)).astype(BF16)
    return p0, p1, p2


def _prenorm_kernel(x_ref, nw_ref, waf_ref, wf_ref, bf_ref, tri_ref, h_ref, af_ref, cx_ref, nc_ref,
                    wafb_ref, carry_ref):
    i = pl.program_id(0)
    tm = x_ref.shape[0]

    @pl.when(i == 0)
    def _():
        wafb_ref[...] = waf_ref[...].T.astype(BF16)
        carry_ref[...] = jnp.zeros_like(carry_ref)

    x = x_ref[...]
    ms = jnp.mean(x * x, axis=-1, keepdims=True)
    h_ref[...] = (x * lax.rsqrt(ms + RMS_EPS) * nw_ref[...]).astype(BF16)
    hb = h_ref[...]
    af_ref[...] = jnp.dot(hb, wafb_ref[...], preferred_element_type=F32)
    logit = _dot_nt(wf_ref[...].astype(BF16), hb) + bf_ref[...]
    nls = _log_sigmoid(logit) * (-LOG2E)
    tri = tri_ref[...]
    loc = sum(jnp.dot(p, tri, preferred_element_type=F32) for p in _split3(nls))
    nc = loc + carry_ref[:, 0:1]
    carry_ref[...] = jnp.broadcast_to(nc[:, tm - 1:tm], carry_ref.shape)
    pad = jnp.zeros((LANES - N_HEADS, tm), F32)
    nc_ref[...] = jnp.concatenate([nc, pad], axis=0).T
    parts = [p.astype(F32) for p in _split3(nc)]
    pad3 = jnp.zeros((LANES - 3 * N_HEADS, tm), F32)
    cx_ref[...] = jnp.concatenate(parts + [pad3], axis=0).T.astype(BF16)


def _prenorm(x2, norm_w, w, bf_col):
    tm = PRE_TM
    tri = jnp.asarray(np.triu(np.ones((tm, tm), np.float32)), BF16)
    once = pl.Buffered(1)
    return pl.pallas_call(
        _prenorm_kernel,
        grid=(SEQ // tm,),
        in_specs=[
            pl.BlockSpec((tm, D_MODEL), lambda i: (i, 0)),
            pl.BlockSpec((1, D_MODEL), lambda i: (0, 0)),
            pl.BlockSpec((WIDTH, D_MODEL), lambda i: (SRC_AF, 0), pipeline_mode=once),
            pl.BlockSpec((N_HEADS, D_MODEL), lambda i: (IN_NA * WIDTH // N_HEADS, 0), pipeline_mode=once),
            pl.BlockSpec((N_HEADS, 1), lambda i: (0, 0)),
            pl.BlockSpec((tm, tm), lambda i: (0, 0), pipeline_mode=once),
        ],
        out_specs=[
            pl.BlockSpec((tm, D_MODEL), lambda i: (i, 0)),
            pl.BlockSpec((tm, WIDTH), lambda i: (i, 0)),
            pl.BlockSpec((tm, LANES), lambda i: (i, 0)),
            pl.BlockSpec((tm, LANES), lambda i: (i, 0)),
        ],
        out_shape=[
            jax.ShapeDtypeStruct((SEQ, D_MODEL), BF16),
            jax.ShapeDtypeStruct((SEQ, WIDTH), F32),
            jax.ShapeDtypeStruct((SEQ, LANES), BF16),
            jax.ShapeDtypeStruct((SEQ, LANES), F32),
        ],
        scratch_shapes=[pltpu.VMEM((D_MODEL, WIDTH), BF16), pltpu.VMEM((N_HEADS, LANES), F32)],
        compiler_params=_params("arbitrary"),
        name="prenorm",
    )(x2, norm_w, w, w, bf_col, tri)


def _inproj_kernel(h_ref, wa_ref, wb_ref, proj_ref, wc_ref):
    j = pl.program_id(1)
    i = pl.program_id(2)
    tm, tn = proj_ref.shape
    nb = WIDTH // tn
    src = _inproj_src_block(j, nb)

    @pl.when(i == 0)
    def _():
        @pl.when(src < IN_NA * nb)
        def _():
            scale = jnp.where(src // nb == SRC_BQ, LOG2E * HEAD_DIM ** -0.5, 1.0)
            wc_ref[...] = (wa_ref[...] * scale).astype(BF16)

        @pl.when(src >= IN_NA * nb)
        def _():
            wc_ref[...] = jnp.concatenate([wa_ref[N_HEADS:, :], wb_ref[:N_HEADS, :]], axis=0).astype(BF16)

    rows = pl.ds(pl.multiple_of(i * tm, tm), tm)
    proj_ref[...] = _dot_nt(h_ref[rows, :], wc_ref[...]).astype(BF16)


def _inproj_src_block(j, nb):
    return jnp.where(j < SRC_AF * nb, j, j + nb)


def _inproj_out_block(j, nb):
    src = _inproj_src_block(j, nb)
    g = src // nb
    g_out = jnp.int32(BLK_AQ)
    for g_src, g_dst in ((SRC_AI, BLK_AI), (SRC_AG, BLK_AG), (SRC_BQ, BLK_BQ), (SRC_BK, BLK_BK), (SRC_BV, BLK_BV)):
        g_out = jnp.where(g == g_src, g_dst, g_out)
    return jnp.where(g < IN_NA, g_out * nb + src % nb, src - IN_NA * nb + BLK_GA * nb)


def _inproj(h, w):
    tm, tn = IN_TM, IN_TN
    nb = WIDTH // tn
    half = SEQ // 2
    n_i = half // tm
    first_gate = IN_NA * nb
    grid = (2, N_PROJ // tn, n_i)
    return pl.pallas_call(
        _inproj_kernel,
        grid=grid,
        in_specs=[
            pl.BlockSpec((half, D_MODEL), lambda s, j, i: (s, 0), pipeline_mode=pl.Buffered(1)),
            pl.BlockSpec((tn, D_MODEL), lambda s, j, i: (_inproj_src_block(j, nb), 0)),
            pl.BlockSpec((tn, D_MODEL), lambda s, j, i: (jnp.maximum(_inproj_src_block(j, nb), first_gate) + 1, 0)),
        ],
        out_specs=pl.BlockSpec((tm, tn), lambda s, j, i: (s * n_i + i, _inproj_out_block(j, nb))),
        out_shape=jax.ShapeDtypeStruct((SEQ, N_PROJ), BF16),
        scratch_shapes=[pltpu.VMEM((tn, D_MODEL), BF16)],
        compiler_params=_params("arbitrary", "arbitrary", "arbitrary"),
        name="inproj",
    )(h, w, w)


def _hgrn_constants(c):
    n_lvl = int(np.log2(c))
    t = np.arange(c)[:, None]
    j = np.arange(c)[None, :]
    blocks = [(j <= t), (j > t)]
    level = np.full((c, c), -1, np.int32)
    level[np.arange(c), np.arange(c)] = 0
    for l in range(n_lvl):
        b = 2 << l
        mid = (t // b) * b + b // 2 - 1
        second = (t % b) >= b // 2
        m = np.where(second, (j > mid) & (j <= t), (j > t) & (j <= mid))
        blocks.append(m)
        s = np.arange(c)[None, :]
        own = (t // b == s // b) & second & ((s % b) < b // 2)
        level[own] = l + 1
    sums = np.concatenate(blocks, axis=0).astype(np.float32)
    sums2 = np.concatenate([sums, sums], axis=1)
    return jnp.asarray(sums2, BF16), jnp.asarray(level), n_lvl


def _hgrn_kernel(q_ref, z_ref, v_ref, g_ref, lbl_ref, nw_ref, sums_ref, lvl_ref, y_ref,
                 st_ref, k_ref, b_ref, kout_ref, qin_ref, g2_ref, dlast_ref, bmin_ref, *, c, n_lvl):
    n_heads = st_ref.shape[0]
    d = HEAD_DIM

    @pl.when(pl.program_id(1) == 0)
    def _():
        st_ref[...] = jnp.zeros_like(st_ref)

    logits = lbl_ref[...]
    ex = jnp.exp(logits - jnp.max(logits, axis=0, keepdims=True))
    lb = ex[0:1, :] / jnp.sum(ex, axis=0, keepdims=True)
    one_m_lb = 1.0 - lb
    nw = nw_ref[...]
    sums = sums_ref[...]
    lvl = lvl_ref[...]
    n_chunks = q_ref.shape[0] // c

    heads = [slice(hh * d, (hh + 1) * d) for hh in range(n_heads)]
    causal = lvl >= 0

    def chunk_rows(ci):
        return pl.ds(pl.multiple_of(ci * c, c), c)

    bmin_ref[...] = jnp.zeros_like(bmin_ref)

    def prepare(ci, carry):
        r = chunk_rows(ci)
        z = z_ref[r, :]
        e = jnp.exp(-jnp.abs(z))
        inv = 1.0 / (1.0 + e)
        pos = z >= 0.0
        sig = jnp.where(pos, inv, e * inv)
        sig_n = jnp.where(pos, e * inv, inv)
        g = jnp.log2(lb + one_m_lb * sig)
        k_all = one_m_lb * sig_n
        g_hi = g.astype(BF16)
        g_lo = (g - g_hi.astype(F32)).astype(BF16)
        g2 = jnp.concatenate([g_hi, g_lo], axis=0)
        e01 = jnp.dot(sums[0:2 * c], g2, preferred_element_type=F32)
        b = e01[0:c]
        dec_b = jnp.exp2(b)
        g2_ref[ci] = g2
        b_ref[r, :] = b
        k_ref[r, :] = k_all
        kout_ref[r, :] = (k_all * jnp.exp2(e01[c:2 * c])).astype(BF16)
        qin_ref[r, :] = (q_ref[r, :].astype(F32) * dec_b).astype(BF16)
        dlast_ref[ci] = jnp.broadcast_to(dec_b[c - 1:c, :], dlast_ref.shape[1:])
        bmin_ref[...] = jnp.minimum(bmin_ref[...], b)
        return carry

    lax.fori_loop(0, n_chunks, prepare, 0, unroll=2)
    b_min = jnp.min(bmin_ref[...])

    def finish(ci, scores):
        r = chunk_rows(ci)
        o_intra, inc = [], []
        for hh, cols in enumerate(heads):
            v = v_ref[r, cols]
            o_intra.append(jnp.dot(scores[hh], v, preferred_element_type=F32))
            inc.append(_dot_tn(v, kout_ref[r, cols]))
        for hh, cols in enumerate(heads):
            st = st_ref[hh]
            o = o_intra[hh] + _dot_nt(qin_ref[r, cols], st.astype(BF16))
            st_ref[hh] = st * dlast_ref[ci, 0:1, cols] + inc[hh]
            o = o * lax.rsqrt(jnp.mean(o * o, axis=-1, keepdims=True) + RMS_EPS) * nw
            gt = g_ref[r, cols].astype(F32)
            y_ref[r, cols] = (o * gt * _sigmoid(gt)).astype(y_ref.dtype)

    def fast_chunk(ci, carry):
        r = chunk_rows(ci)
        k_up = (k_ref[r, :] * jnp.exp2(-b_ref[r, :])).astype(BF16)
        scores = [jnp.where(causal, _dot_nt(qin_ref[r, cols], k_up[:, cols]), 0.0).astype(BF16)
                  for cols in heads]
        finish(ci, scores)
        return carry

    def safe_chunk(ci, carry):
        r = chunk_rows(ci)
        dec_l = jnp.exp2(jnp.dot(sums[2 * c:], g2_ref[ci], preferred_element_type=F32))
        scores = []
        for cols in heads:
            q = q_ref[r, cols].astype(F32)
            k = k_ref[r, cols]
            sc = jnp.where(lvl == 0, _dot_nt(q.astype(BF16), k.astype(BF16)), 0.0)
            for l in range(n_lvl):
                d_l = dec_l[l * c:(l + 1) * c, cols]
                s_l = _dot_nt((q * d_l).astype(BF16), (k * d_l).astype(BF16))
                sc = jnp.where(lvl == l + 1, s_l, sc)
            scores.append(sc.astype(BF16))
        finish(ci, scores)
        return carry

    @pl.when(b_min >= -HG_FAST_LOG2)
    def _():
        lax.fori_loop(0, n_chunks, fast_chunk, 0, unroll=2)

    @pl.when(b_min < -HG_FAST_LOG2)
    def _():
        lax.fori_loop(0, n_chunks, safe_chunk, 0)


def _hgrn(proj, a_f, lb_logits, norm_w):
    ts, c, hg = HG_TS, HG_C, HG_HEADS
    sums, lvl, n_lvl = _hgrn_constants(c)
    w = hg * HEAD_DIM
    per_blk = WIDTH // w
    grid = (N_HEADS // hg, SEQ // ts)
    return pl.pallas_call(
        functools.partial(_hgrn_kernel, c=c, n_lvl=n_lvl),
        grid=grid,
        in_specs=[
            pl.BlockSpec((ts, w), lambda h, i: (i, BLK_AQ * per_blk + h)),
            pl.BlockSpec((ts, w), lambda h, i: (i, h)),
            pl.BlockSpec((ts, w), lambda h, i: (i, BLK_AI * per_blk + h)),
            pl.BlockSpec((ts, w), lambda h, i: (i, BLK_AG * per_blk + h)),
            pl.BlockSpec((lb_logits.shape[0], w), lambda h, i: (0, h)),
            pl.BlockSpec((1, HEAD_DIM), lambda h, i: (0, 0)),
            pl.BlockSpec(sums.shape, lambda h, i: (0, 0)),
            pl.BlockSpec(lvl.shape, lambda h, i: (0, 0)),
        ],
        out_specs=pl.BlockSpec((ts, w), lambda h, i: (i, h)),
        out_shape=jax.ShapeDtypeStruct((SEQ, WIDTH), BF16),
        scratch_shapes=[
            pltpu.VMEM((hg, HEAD_DIM, HEAD_DIM), F32),
            pltpu.VMEM((ts, w), F32),
            pltpu.VMEM((ts, w), F32),
            pltpu.VMEM((ts, w), BF16),
            pltpu.VMEM((ts, w), BF16),
            pltpu.VMEM((ts // c, 2 * c, w), BF16),
            pltpu.VMEM((ts // c, F32_SUBLANES, w), F32),
            pltpu.VMEM((c, w), F32),
        ],
        compiler_params=_params("arbitrary", "arbitrary"),
        name="hgrn",
    )(proj, a_f, proj, proj, lb_logits, norm_w, sums, lvl)


def _fox_kernel(q_ref, k_ref, v_ref, cx_ref, nc_ref, o_ref, vt_ref, nrm_ref, m_ref, acc_ref, *, tq, tk):
    g = pl.program_id(0)
    i = pl.program_id(1)
    n_heads = vt_ref.shape[0]
    d = HEAD_DIM
    s_len = k_ref.shape[0]
    n_sub = tq // tk
    n_kt = s_len // tk
    vrows = vt_ref.shape[2]
    heads = [(hh, slice(hh * d, (hh + 1) * d)) for hh in range(n_heads)]

    @pl.when(i == 0)
    def _():
        ones_row = jnp.where(lax.broadcasted_iota(jnp.int32, (vrows - d, tk), 0) == 0, 1.0, 0.0)
        wg = n_heads * d
        same_head = (lax.broadcasted_iota(jnp.int32, (wg, wg), 0) // d
                     == lax.broadcasted_iota(jnp.int32, (wg, wg), 1) // d)
        ones_blk = jnp.where(same_head, 1.0, 0.0).astype(BF16)

        def sq_norms(x):
            xf = x.astype(F32)
            return jnp.dot((xf * xf).astype(BF16), ones_blk, preferred_element_type=F32)

        kn = jnp.zeros((tk, wg), F32)
        qn = jnp.zeros((tk, wg), F32)
        for r in range(n_kt):
            rows = slice(r * tk, (r + 1) * tk)
            for hh, cols in heads:
                vt_ref[hh, r, 0:d, :] = v_ref[rows, cols].astype(F32).T.astype(BF16)
                vt_ref[hh, r, d:vrows, :] = ones_row.astype(BF16)
            kn = jnp.maximum(kn, sq_norms(k_ref[rows, :]))
            qn = jnp.maximum(qn, sq_norms(q_ref[rows, :]))
        nrm = 2.0 * FOX_NORM_MARGIN * jnp.sqrt(jnp.max(qn, axis=0, keepdims=True)
                                               * jnp.max(kn, axis=0, keepdims=True))
        for hh, cols in heads:
            nrm_ref[hh] = jnp.broadcast_to(nrm[:, cols], nrm_ref.shape[1:])

    q_rows = pl.ds(pl.multiple_of(i * tq, tq), tq)
    nc_q0 = nc_ref[pl.ds(pl.multiple_of(i * tq, tq), 1), :]
    ends = nc_ref[pl.ds(tk - 1, n_kt, stride=tk), :]
    lane = lax.broadcasted_iota(jnp.int32, ends.shape, 1)
    rsel = lax.broadcasted_iota(jnp.int32, (d, tq), 0)
    qa, r_lo = [], []
    for hh, cols in heads:
        h = g * n_heads + hh
        thr = nc_q0 - nrm_ref[hh, 0:1, :] - FOX_SKIP_LOG2
        r_lo.append(jnp.sum(jnp.where((ends < thr) & (lane == h), 1, 0)))
        sel = jnp.where((rsel < 3 * N_HEADS) & ((rsel & (N_HEADS - 1)) == h), 1.0, 0.0).astype(BF16)
        qa.append(jnp.concatenate([q_ref[q_rows, cols].astype(F32).T.astype(BF16), sel], axis=0))

    def scores(hh, cols, r, mask_off):
        ks = pl.ds(pl.multiple_of(r * tk, tk), tk)
        ka = jnp.concatenate([k_ref[ks, cols], cx_ref[ks, :]], axis=1)
        s = jnp.dot(ka, qa[hh], preferred_element_type=F32)
        if mask_off is not None:
            kid = lax.broadcasted_iota(jnp.int32, s.shape, 0) + mask_off
            qid = lax.broadcasted_iota(jnp.int32, s.shape, 1)
            s = jnp.where(kid <= qid, s, -jnp.inf)
        return s

    def first_step(tiles):
        ss = [[scores(hh, cols, r, off) for r, off in tiles] for hh, cols in heads]
        ms = []
        for hh, _ in heads:
            m = jnp.max(ss[hh][0], axis=0, keepdims=True)
            for s in ss[hh][1:]:
                m = jnp.maximum(m, jnp.max(s, axis=0, keepdims=True))
            ms.append(m)
        for hh, _ in heads:
            pv = None
            for (r, _), s in zip(tiles, ss[hh]):
                p = jnp.exp2(s - ms[hh]).astype(BF16)
                dd = jnp.dot(vt_ref[hh, r], p, preferred_element_type=F32)
                pv = dd if pv is None else pv + dd
            acc_ref[hh] = pv
            m_ref[hh] = ms[hh]

    def later_step(hh, cols, r):
        s = scores(hh, cols, r, None)
        m_prev = m_ref[hh]
        m_new = jnp.maximum(m_prev, jnp.max(s, axis=0, keepdims=True))
        pv = jnp.dot(vt_ref[hh, r], jnp.exp2(s - m_new).astype(BF16), preferred_element_type=F32)
        acc_ref[hh] = jnp.exp2(m_prev - m_new) * acc_ref[hh] + pv
        m_ref[hh] = m_new

    diag = [(i * n_sub + rr, rr * tk) for rr in range(n_sub)]

    @pl.when(i == 0)
    def _():
        first_step(diag)

    @pl.when(i > 0)
    def _():
        first_step([(i * n_sub - 1, None)] + diag)
        for hh, cols in heads:
            def off_diag(r, carry, hh=hh, cols=cols):
                later_step(hh, cols, r)
                return carry
            lax.fori_loop(r_lo[hh], i * n_sub - 1, off_diag, 0)

    for hh, cols in heads:
        acc = acc_ref[hh]
        o_ref[:, cols] = (acc[0:d] / acc[d:d + 1]).T.astype(o_ref.dtype)


def _fox(proj, cx, nc):
    tq, tk, hg = FOX_TQ, FOX_TK, FOX_HEADS
    w = hg * HEAD_DIM
    per_blk = WIDTH // w
    vrows = HEAD_DIM + BF16_SUBLANES
    return pl.pallas_call(
        functools.partial(_fox_kernel, tq=tq, tk=tk),
        grid=(N_HEADS // hg, SEQ // tq),
        in_specs=[
            pl.BlockSpec((SEQ, w), lambda g, i: (0, BLK_BQ * per_blk + g)),
            pl.BlockSpec((SEQ, w), lambda g, i: (0, BLK_BK * per_blk + g)),
            pl.BlockSpec((SEQ, w), lambda g, i: (0, BLK_BV * per_blk + g)),
            pl.BlockSpec((SEQ, LANES), lambda g, i: (0, 0)),
            pl.BlockSpec((SEQ, LANES), lambda g, i: (0, 0)),
        ],
        out_specs=pl.BlockSpec((tq, w), lambda g, i: (i, g)),
        out_shape=jax.ShapeDtypeStruct((SEQ, WIDTH), BF16),
        scratch_shapes=[
            pltpu.VMEM((hg, SEQ // tk, vrows, tk), BF16),
            pltpu.VMEM((hg, F32_SUBLANES, LANES), F32),
            pltpu.VMEM((hg, 1, tq), F32),
            pltpu.VMEM((hg, vrows, tq), F32),
        ],
        compiler_params=_params("arbitrary", "arbitrary"),
        name="fox",
    )(proj, proj, proj, cx, nc)


def _merge_load_weights(srcs, dsts, stage_ref, sems):
    rows = stage_ref.shape[1]
    chunks = [(src, dst, r0) for src, dst in zip(srcs, dsts) for r0 in range(0, src.shape[0], rows)]

    def copy(n):
        src, _, r0 = chunks[n]
        return pltpu.make_async_copy(src.at[pl.ds(r0, rows), :], stage_ref.at[n % 2], sems.at[n % 2])

    copy(0).start()
    for n, (_, dst, r0) in enumerate(chunks):
        if n + 1 < len(chunks):
            copy(n + 1).start()
        copy(n).wait()
        dst[r0:r0 + rows, :] = stage_ref[n % 2].astype(BF16)


def _merge_kernel(oa_ref, ob_ref, ga_ref, gb_ref, wa_hbm, wb_hbm, wo_hbm, x_ref, nw_ref, npre_ref,
                  out_ref, h2_ref, wa_ref, wb_ref, wo_ref, stage_ref, sems):
    @pl.when(pl.program_id(0) == 0)
    def _():
        _merge_load_weights((wa_hbm, wb_hbm, wo_hbm), (wa_ref, wb_ref, wo_ref), stage_ref, sems)

    ya = jnp.dot(oa_ref[...], wa_ref[...], preferred_element_type=F32)
    yb = jnp.dot(ob_ref[...], wb_ref[...], preferred_element_type=F32)
    merged = _sigmoid(ga_ref[...].astype(F32)) * ya + _sigmoid(gb_ref[...].astype(F32)) * yb
    u = jnp.dot(merged.astype(BF16), wo_ref[...], preferred_element_type=F32)
    u = u * lax.rsqrt(jnp.mean(u * u, axis=-1, keepdims=True) + RMS_EPS) * nw_ref[...]
    x1 = x_ref[...] + u
    out_ref[...] = x1
    ms = jnp.mean(x1 * x1, axis=-1, keepdims=True)
    h2_ref[...] = (x1 * lax.rsqrt(ms + RMS_EPS) * npre_ref[...]).astype(BF16)


def _merge(y_a, y_b, proj, w_up_a, w_up_b, w_o, x2, norm_w, norm_ffn_pre):
    tm = MERGE_TM
    hbm = pl.BlockSpec(memory_space=pl.ANY)
    return pl.pallas_call(
        _merge_kernel,
        grid=(SEQ // tm,),
        in_specs=[
            pl.BlockSpec((tm, WIDTH), lambda i: (i, 0)),
            pl.BlockSpec((tm, WIDTH), lambda i: (i, 0)),
            pl.BlockSpec((tm, D_MODEL), lambda i: (i, BLK_GA // 2)),
            pl.BlockSpec((tm, D_MODEL), lambda i: (i, BLK_GB // 2)),
            hbm,
            hbm,
            hbm,
            pl.BlockSpec((tm, D_MODEL), lambda i: (i, 0)),
            pl.BlockSpec((1, D_MODEL), lambda i: (0, 0)),
            pl.BlockSpec((1, D_MODEL), lambda i: (0, 0)),
        ],
        out_specs=[
            pl.BlockSpec((tm, D_MODEL), lambda i: (i, 0)),
            pl.BlockSpec((tm, D_MODEL), lambda i: (i, 0)),
        ],
        out_shape=[
            jax.ShapeDtypeStruct((SEQ, D_MODEL), F32),
            jax.ShapeDtypeStruct((SEQ, D_MODEL), BF16),
        ],
        scratch_shapes=[
            pltpu.VMEM((WIDTH, D_MODEL), BF16),
            pltpu.VMEM((WIDTH, D_MODEL), BF16),
            pltpu.VMEM((D_MODEL, D_MODEL), BF16),
            pltpu.VMEM((2, MERGE_STAGE_ROWS, D_MODEL), F32),
            pltpu.SemaphoreType.DMA((2,)),
        ],
        compiler_params=_params("arbitrary"),
        name="merge",
    )(y_a, y_b, proj, proj, w_up_a, w_up_b, w_o, x2, norm_w, norm_ffn_pre)


def _ffn_residual_copy(x_hbm, xres_ref, sem, i):
    tm = xres_ref.shape[0]
    return pltpu.make_async_copy(x_hbm.at[pl.ds(pl.multiple_of(i * tm, tm), tm), :], xres_ref, sem)


def _ffn_kernel(h_ref, x_hbm, npost_ref, wg_ref, wu_ref, wd_ref, out_ref, xres_ref, sem):
    i = pl.program_id(0)
    f = pl.program_id(1)

    @pl.when(f == 0)
    def _():
        _ffn_residual_copy(x_hbm, xres_ref, sem, i).start()
        out_ref[...] = jnp.zeros_like(out_ref)

    h = h_ref[...]
    gate = jnp.dot(h, wg_ref[...].astype(BF16), preferred_element_type=F32)
    up = jnp.dot(h, wu_ref[...].astype(BF16), preferred_element_type=F32)
    act = (gate * _sigmoid(gate) * up).astype(BF16)
    out_ref[...] += jnp.dot(act, wd_ref[...].astype(BF16), preferred_element_type=F32)

    @pl.when(f == pl.num_programs(1) - 1)
    def _():
        _ffn_residual_copy(x_hbm, xres_ref, sem, i).wait()
        u = out_ref[...]
        u = u * lax.rsqrt(jnp.mean(u * u, axis=-1, keepdims=True) + RMS_EPS) * npost_ref[...]
        out_ref[...] = xres_ref[...] + u


def _ffn(h2, x1, norm_post, w_in, w_down):
    tm, tf = FFN_TM, FFN_TF
    nf = D_FF // tf
    return pl.pallas_call(
        _ffn_kernel,
        grid=(SEQ // tm, nf),
        in_specs=[
            pl.BlockSpec((tm, D_MODEL), lambda i, f: (i, 0)),
            pl.BlockSpec(memory_space=pl.ANY),
            pl.BlockSpec((1, D_MODEL), lambda i, f: (0, 0)),
            pl.BlockSpec((D_MODEL, tf), lambda i, f: (0, f)),
            pl.BlockSpec((D_MODEL, tf), lambda i, f: (0, f + nf)),
            pl.BlockSpec((tf, D_MODEL), lambda i, f: (f, 0)),
        ],
        out_specs=pl.BlockSpec((tm, D_MODEL), lambda i, f: (i, 0)),
        out_shape=jax.ShapeDtypeStruct((SEQ, D_MODEL), F32),
        scratch_shapes=[pltpu.VMEM((tm, D_MODEL), F32), pltpu.SemaphoreType.DMA(())],
        compiler_params=_params("arbitrary", "arbitrary"),
        name="ffn",
    )(h2, x1, norm_post, w_in, w_in, w_down)


def kernel(x, w_in, b_fox_f, hgrn_lb_logits, hgrn_norm_w, w_up_a, w_up_b, w_o, norm_mix_pre,
           norm_mix_post, norm_ffn_pre, norm_ffn_post, w_ffn_in, w_ffn_down):
    assert x.shape == (1, SEQ, D_MODEL) and w_in.shape[0] == 1
    w_t = w_in[0].T
    x2 = x[0]

    h, a_f, cx, nc = _prenorm(x2, norm_mix_pre, w_t, b_fox_f.reshape(N_HEADS, 1))
    proj = _inproj(h, w_t)
    y_a = _hgrn(proj, a_f, hgrn_lb_logits, hgrn_norm_w)
    y_b = _fox(proj, cx, nc)
    x1, h2 = _merge(y_a, y_b, proj, w_up_a[0], w_up_b[0], w_o[0], x2, norm_mix_post, norm_ffn_pre)
    out = _ffn(h2, x1, norm_ffn_post, w_ffn_in[0], w_ffn_down[0])
    return out[None]
```

```python
import functools

import numpy as np
import jax
import jax.numpy as jnp
from jax import lax
from jax.experimental import pallas as pl
from jax.experimental.pallas import tpu as pltpu

F32 = jnp.float32
BF16 = jnp.bfloat16

D_MODEL = 2048
SEQ = 8192
HEAD_DIM = 128
N_HEADS = 8
WIDTH = N_HEADS * HEAD_DIM
D_FF = 5632
RMS_EPS = 1e-6
N_PROJ = 3 * WIDTH + 2 * D_MODEL + 3 * WIDTH

VMEM_LIMIT_BYTES = 56 * 1024 * 1024
LANES = 128
F32_SUBLANES = 8
BF16_SUBLANES = 16

SRC_AQ, SRC_AF, SRC_AI, SRC_AG, SRC_BQ, SRC_BK, SRC_BV = 0, 1, 2, 3, 4, 5, 6
IN_NA = 7
BLK_AQ, BLK_AI, BLK_GA, BLK_GB, BLK_AG, BLK_BQ, BLK_BK, BLK_BV = 0, 1, 2, 4, 6, 7, 8, 9

PRE_TM = 512
IN_TM, IN_TN = 2048, 512
HG_TS, HG_C, HG_HEADS = 1024, 64, 8
HG_FAST_LOG2 = 96.0
FOX_TQ, FOX_TK, FOX_HEADS, FOX_QBLOCKS = 512, 256, 2, 2
LOG2E = 1.4426950408889634
FOX_SKIP_LOG2 = 160.0
FOX_NORM_MARGIN = 1.01
MERGE_TM = 256
MERGE_STAGE_ROWS = 256
FFN_TM, FFN_TF = 1024, 256


def _params(*sem):
    return pltpu.CompilerParams(dimension_semantics=sem, vmem_limit_bytes=VMEM_LIMIT_BYTES)


def _dot_nt(a, b):
    return lax.dot_general(a, b, (((1,), (1,)), ((), ())), preferred_element_type=F32)


def _dot_tn(a, b):
    return lax.dot_general(a, b, (((0,), (0,)), ((), ())), preferred_element_type=F32)


def _log_sigmoid(x):
    return jnp.minimum(x, 0.0) - jnp.log(1.0 + jnp.exp(-jnp.abs(x)))


def _sigmoid(x):
    return 1.0 / (1.0 + jnp.exp(-x))


def _split3(x):
    p0 = x.astype(BF16)
    r1 = x - p0.astype(F32)
    p1 = r1.astype(BF16)
    p2 = (r1 - p1.astype(F32)).astype(BF16)
    return p0, p1, p2


def _prenorm_kernel(x_ref, nw_ref, waf_ref, wf_ref, bf_ref, tri_ref, h_ref, af_ref, cx_ref, nc_ref,
                    wafb_ref, carry_ref):
    i = pl.program_id(0)
    tm = x_ref.shape[0]

    @pl.when(i == 0)
    def _():
        wafb_ref[...] = waf_ref[...].T.astype(BF16)
        carry_ref[...] = jnp.zeros_like(carry_ref)

    x = x_ref[...]
    ms = jnp.mean(x * x, axis=-1, keepdims=True)
    h_ref[...] = (x * lax.rsqrt(ms + RMS_EPS) * nw_ref[...]).astype(BF16)
    hb = h_ref[...]
    af_ref[...] = jnp.dot(hb, wafb_ref[...], preferred_element_type=F32)
    logit = _dot_nt(wf_ref[...].astype(BF16), hb) + bf_ref[...]
    nls = _log_sigmoid(logit) * (-LOG2E)
    tri = tri_ref[...]
    loc = sum(jnp.dot(p, tri, preferred_element_type=F32) for p in _split3(nls))
    nc = loc + carry_ref[:, 0:1]
    carry_ref[...] = jnp.broadcast_to(nc[:, tm - 1:tm], carry_ref.shape)
    pad = jnp.zeros((LANES - N_HEADS, tm), F32)
    nc_ref[...] = jnp.concatenate([nc, pad], axis=0).T
    parts = [p.astype(F32) for p in _split3(nc)]
    pad3 = jnp.zeros((LANES - 3 * N_HEADS, tm), F32)
    cx_ref[...] = jnp.concatenate(parts + [pad3], axis=0).T.astype(BF16)


def _prenorm(x2, norm_w, w, bf_col):
    tm = PRE_TM
    tri = jnp.asarray(np.triu(np.ones((tm, tm), np.float32)), BF16)
    once = pl.Buffered(1)
    return pl.pallas_call(
        _prenorm_kernel,
        grid=(SEQ // tm,),
        in_specs=[
            pl.BlockSpec((tm, D_MODEL), lambda i: (i, 0)),
            pl.BlockSpec((1, D_MODEL), lambda i: (0, 0)),
            pl.BlockSpec((WIDTH, D_MODEL), lambda i: (SRC_AF, 0), pipeline_mode=once),
            pl.BlockSpec((N_HEADS, D_MODEL), lambda i: (IN_NA * WIDTH // N_HEADS, 0), pipeline_mode=once),
            pl.BlockSpec((N_HEADS, 1), lambda i: (0, 0)),
            pl.BlockSpec((tm, tm), lambda i: (0, 0), pipeline_mode=once),
        ],
        out_specs=[
            pl.BlockSpec((tm, D_MODEL), lambda i: (i, 0)),
            pl.BlockSpec((tm, WIDTH), lambda i: (i, 0)),
            pl.BlockSpec((tm, LANES), lambda i: (i, 0)),
            pl.BlockSpec((tm, LANES), lambda i: (i, 0)),
        ],
        out_shape=[
            jax.ShapeDtypeStruct((SEQ, D_MODEL), BF16),
            jax.ShapeDtypeStruct((SEQ, WIDTH), F32),
            jax.ShapeDtypeStruct((SEQ, LANES), BF16),
            jax.ShapeDtypeStruct((SEQ, LANES), F32),
        ],
        scratch_shapes=[pltpu.VMEM((D_MODEL, WIDTH), BF16), pltpu.VMEM((N_HEADS, LANES), F32)],
        compiler_params=_params("arbitrary"),
        name="prenorm",
    )(x2, norm_w, w, w, bf_col, tri)


def _inproj_kernel(h_ref, wa_ref, wb_ref, proj_ref, wc_ref):
    j = pl.program_id(1)
    i = pl.program_id(2)
    tm, tn = proj_ref.shape
    nb = WIDTH // tn
    src = _inproj_src_block(j, nb)

    @pl.when(i == 0)
    def _():
        @pl.when(src < IN_NA * nb)
        def _():
            scale = jnp.where(src // nb == SRC_BQ, LOG2E * HEAD_DIM ** -0.5, 1.0)
            wc_ref[...] = (wa_ref[...] * scale).astype(BF16)

        @pl.when(src >= IN_NA * nb)
        def _():
            wc_ref[...] = jnp.concatenate([wa_ref[N_HEADS:, :], wb_ref[:N_HEADS, :]], axis=0).astype(BF16)

    rows = pl.ds(pl.multiple_of(i * tm, tm), tm)
    proj_ref[...] = _dot_nt(h_ref[rows, :], wc_ref[...]).astype(BF16)


def _inproj_src_block(j, nb):
    return jnp.where(j < SRC_AF * nb, j, j + nb)


def _inproj_out_block(j, nb):
    src = _inproj_src_block(j, nb)
    g = src // nb
    g_out = jnp.int32(BLK_AQ)
    for g_src, g_dst in ((SRC_AI, BLK_AI), (SRC_AG, BLK_AG), (SRC_BQ, BLK_BQ), (SRC_BK, BLK_BK), (SRC_BV, BLK_BV)):
        g_out = jnp.where(g == g_src, g_dst, g_out)
    return jnp.where(g < IN_NA, g_out * nb + src % nb, src - IN_NA * nb + BLK_GA * nb)


def _inproj(h, w):
    tm, tn = IN_TM, IN_TN
    nb = WIDTH // tn
    half = SEQ // 2
    n_i = half // tm
    first_gate = IN_NA * nb
    grid = (2, N_PROJ // tn, n_i)
    return pl.pallas_call(
        _inproj_kernel,
        grid=grid,
        in_specs=[
            pl.BlockSpec((half, D_MODEL), lambda s, j, i: (s, 0), pipeline_mode=pl.Buffered(1)),
            pl.BlockSpec((tn, D_MODEL), lambda s, j, i: (_inproj_src_block(j, nb), 0)),
            pl.BlockSpec((tn, D_MODEL), lambda s, j, i: (jnp.maximum(_inproj_src_block(j, nb), first_gate) + 1, 0)),
        ],
        out_specs=pl.BlockSpec((tm, tn), lambda s, j, i: (s * n_i + i, _inproj_out_block(j, nb))),
        out_shape=jax.ShapeDtypeStruct((SEQ, N_PROJ), BF16),
        scratch_shapes=[pltpu.VMEM((tn, D_MODEL), BF16)],
        compiler_params=_params("arbitrary", "arbitrary", "arbitrary"),
        name="inproj",
    )(h, w, w)


def _hgrn_constants(c):
    n_lvl = int(np.log2(c))
    t = np.arange(c)[:, None]
    j = np.arange(c)[None, :]
    blocks = [(j <= t), (j > t)]
    level = np.full((c, c), -1, np.int32)
    level[np.arange(c), np.arange(c)] = 0
    for l in range(n_lvl):
        b = 2 << l
        mid = (t // b) * b + b // 2 - 1
        second = (t % b) >= b // 2
        m = np.where(second, (j > mid) & (j <= t), (j > t) & (j <= mid))
        blocks.append(m)
        s = np.arange(c)[None, :]
        own = (t // b == s // b) & second & ((s % b) < b // 2)
        level[own] = l + 1
    sums = np.concatenate(blocks, axis=0).astype(np.float32)
    sums2 = np.concatenate([sums, sums], axis=1)
    return jnp.asarray(sums2, BF16), jnp.asarray(level), n_lvl


def _hgrn_kernel(q_ref, z_ref, v_ref, g_ref, lbl_ref, nw_ref, sums_ref, lvl_ref, y_ref,
                 st_ref, k_ref, b_ref, kout_ref, qin_ref, g2_ref, dlast_ref, bmin_ref, *, c, n_lvl):
    n_heads = st_ref.shape[0]
    d = HEAD_DIM

    @pl.when(pl.program_id(1) == 0)
    def _():
        st_ref[...] = jnp.zeros_like(st_ref)

    logits = lbl_ref[...]
    ex = jnp.exp(logits - jnp.max(logits, axis=0, keepdims=True))
    lb = ex[0:1, :] / jnp.sum(ex, axis=0, keepdims=True)
    one_m_lb = 1.0 - lb
    nw = nw_ref[...]
    sums = sums_ref[...]
    lvl = lvl_ref[...]
    n_chunks = q_ref.shape[0] // c

    heads = [slice(hh * d, (hh + 1) * d) for hh in range(n_heads)]
    causal = lvl >= 0

    def chunk_rows(ci):
        return pl.ds(pl.multiple_of(ci * c, c), c)

    bmin_ref[...] = jnp.zeros_like(bmin_ref)

    def prepare(ci, carry):
        r = chunk_rows(ci)
        z = z_ref[r, :]
        e = jnp.exp(-jnp.abs(z))
        inv = 1.0 / (1.0 + e)
        pos = z >= 0.0
        sig = jnp.where(pos, inv, e * inv)
        sig_n = jnp.where(pos, e * inv, inv)
        g = jnp.log2(lb + one_m_lb * sig)
        k_all = one_m_lb * sig_n
        g_hi = g.astype(BF16)
        g_lo = (g - g_hi.astype(F32)).astype(BF16)
        g2 = jnp.concatenate([g_hi, g_lo], axis=0)
        e01 = jnp.dot(sums[0:2 * c], g2, preferred_element_type=F32)
        b = e01[0:c]
        dec_b = jnp.exp2(b)
        g2_ref[ci] = g2
        b_ref[r, :] = b
        k_ref[r, :] = k_all
        kout_ref[r, :] = (k_all * jnp.exp2(e01[c:2 * c])).astype(BF16)
        qin_ref[r, :] = (q_ref[r, :].astype(F32) * dec_b).astype(BF16)
        dlast_ref[ci] = jnp.broadcast_to(dec_b[c - 1:c, :], dlast_ref.shape[1:])
        bmin_ref[...] = jnp.minimum(bmin_ref[...], b)
        return carry

    lax.fori_loop(0, n_chunks, prepare, 0, unroll=2)
    b_min = jnp.min(bmin_ref[...])

    def finish(ci, scores):
        r = chunk_rows(ci)
        o_intra, inc = [], []
        for hh, cols in enumerate(heads):
            v = v_ref[r, cols]
            o_intra.append(jnp.dot(scores[hh], v, preferred_element_type=F32))
            inc.append(_dot_tn(v, kout_ref[r, cols]))
        for hh, cols in enumerate(heads):
            st = st_ref[hh]
            o = o_intra[hh] + _dot_nt(qin_ref[r, cols], st.astype(BF16))
            st_ref[hh] = st * dlast_ref[ci, 0:1, cols] + inc[hh]
            o = o * lax.rsqrt(jnp.mean(o * o, axis=-1, keepdims=True) + RMS_EPS) * nw
            gt = g_ref[r, cols].astype(F32)
            y_ref[r, cols] = (o * gt * _sigmoid(gt)).astype(y_ref.dtype)

    def fast_chunk(ci, carry):
        r = chunk_rows(ci)
        k_up = (k_ref[r, :] * jnp.exp2(-b_ref[r, :])).astype(BF16)
        scores = [jnp.where(causal, _dot_nt(qin_ref[r, cols], k_up[:, cols]), 0.0).astype(BF16)
                  for cols in heads]
        finish(ci, scores)
        return carry

    def safe_chunk(ci, carry):
        r = chunk_rows(ci)
        dec_l = jnp.exp2(jnp.dot(sums[2 * c:], g2_ref[ci], preferred_element_type=F32))
        scores = []
        for cols in heads:
            q = q_ref[r, cols].astype(F32)
            k = k_ref[r, cols]
            sc = jnp.where(lvl == 0, _dot_nt(q.astype(BF16), k.astype(BF16)), 0.0)
            for l in range(n_lvl):
                d_l = dec_l[l * c:(l + 1) * c, cols]
                s_l = _dot_nt((q * d_l).astype(BF16), (k * d_l).astype(BF16))
                sc = jnp.where(lvl == l + 1, s_l, sc)
            scores.append(sc.astype(BF16))
        finish(ci, scores)
        return carry

    @pl.when(b_min >= -HG_FAST_LOG2)
    def _():
        lax.fori_loop(0, n_chunks, fast_chunk, 0, unroll=8)

    @pl.when(b_min < -HG_FAST_LOG2)
    def _():
        lax.fori_loop(0, n_chunks, safe_chunk, 0)


def _hgrn(proj, a_f, lb_logits, norm_w):
    ts, c, hg = HG_TS, HG_C, HG_HEADS
    sums, lvl, n_lvl = _hgrn_constants(c)
    w = hg * HEAD_DIM
    per_blk = WIDTH // w
    grid = (N_HEADS // hg, SEQ // ts)
    return pl.pallas_call(
        functools.partial(_hgrn_kernel, c=c, n_lvl=n_lvl),
        grid=grid,
        in_specs=[
            pl.BlockSpec((ts, w), lambda h, i: (i, BLK_AQ * per_blk + h)),
            pl.BlockSpec((ts, w), lambda h, i: (i, h)),
            pl.BlockSpec((ts, w), lambda h, i: (i, BLK_AI * per_blk + h)),
            pl.BlockSpec((ts, w), lambda h, i: (i, BLK_AG * per_blk + h)),
            pl.BlockSpec((lb_logits.shape[0], w), lambda h, i: (0, h)),
            pl.BlockSpec((1, HEAD_DIM), lambda h, i: (0, 0)),
            pl.BlockSpec(sums.shape, lambda h, i: (0, 0)),
            pl.BlockSpec(lvl.shape, lambda h, i: (0, 0)),
        ],
        out_specs=pl.BlockSpec((ts, w), lambda h, i: (i, h)),
        out_shape=jax.ShapeDtypeStruct((SEQ, WIDTH), BF16),
        scratch_shapes=[
            pltpu.VMEM((hg, HEAD_DIM, HEAD_DIM), F32),
            pltpu.VMEM((ts, w), F32),
            pltpu.VMEM((ts, w), F32),
            pltpu.VMEM((ts, w), BF16),
            pltpu.VMEM((ts, w), BF16),
            pltpu.VMEM((ts // c, 2 * c, w), BF16),
            pltpu.VMEM((ts // c, F32_SUBLANES, w), F32),
            pltpu.VMEM((c, w), F32),
        ],
        compiler_params=_params("arbitrary", "arbitrary"),
        name="hgrn",
    )(proj, a_f, proj, proj, lb_logits, norm_w, sums, lvl)


def _fox_kernel(q_ref, k_ref, v_ref, cx_ref, nc_ref, o_ref, vt_ref, nrm_ref, m_ref, acc_ref, *, tq, tk):
    g = pl.program_id(0)
    i = pl.program_id(1)
    n_heads = vt_ref.shape[0]
    d = HEAD_DIM
    s_len = k_ref.shape[0]
    n_sub = tq // tk
    n_kt = s_len // tk
    vrows = vt_ref.shape[2]
    heads = [(hh, slice(hh * d, (hh + 1) * d)) for hh in range(n_heads)]

    @pl.when(i == 0)
    def _():
        ones_row = jnp.where(lax.broadcasted_iota(jnp.int32, (vrows - d, tk), 0) == 0, 1.0, 0.0)
        wg = n_heads * d
        same_head = (lax.broadcasted_iota(jnp.int32, (wg, wg), 0) // d
                     == lax.broadcasted_iota(jnp.int32, (wg, wg), 1) // d)
        ones_blk = jnp.where(same_head, 1.0, 0.0).astype(BF16)

        def sq_norms(x):
            xf = x.astype(F32)
            return jnp.dot((xf * xf).astype(BF16), ones_blk, preferred_element_type=F32)

        kn = jnp.zeros((tk, wg), F32)
        qn = jnp.zeros((tk, wg), F32)
        for r in range(n_kt):
            rows = slice(r * tk, (r + 1) * tk)
            for hh, cols in heads:
                vt_ref[hh, r, 0:d, :] = v_ref[rows, cols].astype(F32).T.astype(BF16)
                vt_ref[hh, r, d:vrows, :] = ones_row.astype(BF16)
            kn = jnp.maximum(kn, sq_norms(k_ref[rows, :]))
            qn = jnp.maximum(qn, sq_norms(q_ref[rows, :]))
        nrm = 2.0 * FOX_NORM_MARGIN * jnp.sqrt(jnp.max(qn, axis=0, keepdims=True)
                                               * jnp.max(kn, axis=0, keepdims=True))
        for hh, cols in heads:
            nrm_ref[hh] = jnp.broadcast_to(nrm[:, cols], nrm_ref.shape[1:])

    n_qb = o_ref.shape[0] // tq
    units = [(hh, cols, qb) for hh, cols in heads for qb in range(n_qb)]
    ends = nc_ref[pl.ds(tk - 1, n_kt, stride=tk), :]
    lane = lax.broadcasted_iota(jnp.int32, ends.shape, 1)
    rsel = lax.broadcasted_iota(jnp.int32, (d, tq), 0)
    qa, r_lo, first_tile = [], [], []
    for hh, cols, qb in units:
        h = g * n_heads + hh
        q0 = pl.multiple_of((i * n_qb + qb) * tq, tq)
        thr = nc_ref[pl.ds(q0, 1), :] - nrm_ref[hh, 0:1, :] - FOX_SKIP_LOG2
        r_lo.append(jnp.sum(jnp.where((ends < thr) & (lane == h), 1, 0)))
        first_tile.append((i * n_qb + qb) * n_sub)
        sel = jnp.where((rsel < 3 * N_HEADS) & ((rsel & (N_HEADS - 1)) == h), 1.0, 0.0).astype(BF16)
        qa.append(jnp.concatenate([q_ref[pl.ds(q0, tq), cols].astype(F32).T.astype(BF16), sel], axis=0))

    def scores(u, r, mask_off):
        hh, cols, _ = units[u]
        ks = pl.ds(pl.multiple_of(r * tk, tk), tk)
        ka = jnp.concatenate([k_ref[ks, cols], cx_ref[ks, :]], axis=1)
        s = jnp.dot(ka, qa[u], preferred_element_type=F32)
        if mask_off is not None:
            kid = lax.broadcasted_iota(jnp.int32, s.shape, 0) + mask_off
            qid = lax.broadcasted_iota(jnp.int32, s.shape, 1)
            s = jnp.where(kid <= qid, s, -jnp.inf)
        return s

    def first_step(with_prev):
        tiles = []
        for u in range(len(units)):
            t = [(first_tile[u] + rr, rr * tk) for rr in range(n_sub)]
            tiles.append(([(first_tile[u] - 1, None)] if with_prev[u] else []) + t)
        ss = [[scores(u, r, off) for r, off in tiles[u]] for u in range(len(units))]
        ms = []
        for u in range(len(units)):
            m = jnp.max(ss[u][0], axis=0, keepdims=True)
            for s in ss[u][1:]:
                m = jnp.maximum(m, jnp.max(s, axis=0, keepdims=True))
            ms.append(m)
        for u, (hh, _, _) in enumerate(units):
            pv = None
            for (r, _), s in zip(tiles[u], ss[u]):
                p = jnp.exp2(s - ms[u]).astype(BF16)
                dd = jnp.dot(vt_ref[hh, r], p, preferred_element_type=F32)
                pv = dd if pv is None else pv + dd
            acc_ref[u] = pv
            m_ref[u] = ms[u]

    def later_step(u, r):
        s = scores(u, r, None)
        m_prev = m_ref[u]
        m_new = jnp.maximum(m_prev, jnp.max(s, axis=0, keepdims=True))
        pv = jnp.dot(vt_ref[units[u][0], r], jnp.exp2(s - m_new).astype(BF16), preferred_element_type=F32)
        acc_ref[u] = jnp.exp2(m_prev - m_new) * acc_ref[u] + pv
        m_ref[u] = m_new

    @pl.when(i == 0)
    def _():
        first_step([qb > 0 for _, _, qb in units])

    @pl.when(i > 0)
    def _():
        first_step([True] * len(units))

    for u in range(len(units)):
        def off_diag(r, carry, u=u):
            later_step(u, r)
            return carry
        lax.fori_loop(r_lo[u], first_tile[u] - 1, off_diag, 0)

    for u, (hh, cols, qb) in enumerate(units):
        acc = acc_ref[u]
        o_ref[qb * tq:(qb + 1) * tq, cols] = (acc[0:d] / acc[d:d + 1]).T.astype(o_ref.dtype)


def _fox(proj, cx, nc):
    tq, tk, hg, n_qb = FOX_TQ, FOX_TK, FOX_HEADS, FOX_QBLOCKS
    w = hg * HEAD_DIM
    per_blk = WIDTH // w
    vrows = HEAD_DIM + BF16_SUBLANES
    return pl.pallas_call(
        functools.partial(_fox_kernel, tq=tq, tk=tk),
        grid=(N_HEADS // hg, SEQ // (n_qb * tq)),
        in_specs=[
            pl.BlockSpec((SEQ, w), lambda g, i: (0, BLK_BQ * per_blk + g)),
            pl.BlockSpec((SEQ, w), lambda g, i: (0, BLK_BK * per_blk + g)),
            pl.BlockSpec((SEQ, w), lambda g, i: (0, BLK_BV * per_blk + g)),
            pl.BlockSpec((SEQ, LANES), lambda g, i: (0, 0)),
            pl.BlockSpec((SEQ, LANES), lambda g, i: (0, 0)),
        ],
        out_specs=pl.BlockSpec((n_qb * tq, w), lambda g, i: (i, g)),
        out_shape=jax.ShapeDtypeStruct((SEQ, WIDTH), BF16),
        scratch_shapes=[
            pltpu.VMEM((hg, SEQ // tk, vrows, tk), BF16),
            pltpu.VMEM((hg, F32_SUBLANES, LANES), F32),
            pltpu.VMEM((hg * n_qb, 1, tq), F32),
            pltpu.VMEM((hg * n_qb, vrows, tq), F32),
        ],
        compiler_params=_params("arbitrary", "arbitrary"),
        name="fox",
    )(proj, proj, proj, cx, nc)


def _merge_load_weights(srcs, dsts, stage_ref, sems):
    rows = stage_ref.shape[1]
    chunks = [(src, dst, r0) for src, dst in zip(srcs, dsts) for r0 in range(0, src.shape[0], rows)]

    def copy(n):
        src, _, r0 = chunks[n]
        return pltpu.make_async_copy(src.at[pl.ds(r0, rows), :], stage_ref.at[n % 2], sems.at[n % 2])

    copy(0).start()
    for n, (_, dst, r0) in enumerate(chunks):
        if n + 1 < len(chunks):
            copy(n + 1).start()
        copy(n).wait()
        dst[r0:r0 + rows, :] = stage_ref[n % 2].astype(BF16)


def _merge_kernel(oa_ref, ob_ref, ga_ref, gb_ref, wa_hbm, wb_hbm, wo_hbm, x_ref, nw_ref, npre_ref,
                  out_ref, h2_ref, wa_ref, wb_ref, wo_ref, stage_ref, sems):
    @pl.when(pl.program_id(0) == 0)
    def _():
        _merge_load_weights((wa_hbm, wb_hbm, wo_hbm), (wa_ref, wb_ref, wo_ref), stage_ref, sems)

    ya = jnp.dot(oa_ref[...], wa_ref[...], preferred_element_type=F32)
    yb = jnp.dot(ob_ref[...], wb_ref[...], preferred_element_type=F32)
    merged = _sigmoid(ga_ref[...].astype(F32)) * ya + _sigmoid(gb_ref[...].astype(F32)) * yb
    u = jnp.dot(merged.astype(BF16), wo_ref[...], preferred_element_type=F32)
    u = u * lax.rsqrt(jnp.mean(u * u, axis=-1, keepdims=True) + RMS_EPS) * nw_ref[...]
    x1 = x_ref[...] + u
    out_ref[...] = x1
    ms = jnp.mean(x1 * x1, axis=-1, keepdims=True)
    h2_ref[...] = (x1 * lax.rsqrt(ms + RMS_EPS) * npre_ref[...]).astype(BF16)


def _merge(y_a, y_b, proj, w_up_a, w_up_b, w_o, x2, norm_w, norm_ffn_pre):
    tm = MERGE_TM
    hbm = pl.BlockSpec(memory_space=pl.ANY)
    return pl.pallas_call(
        _merge_kernel,
        grid=(SEQ // tm,),
        in_specs=[
            pl.BlockSpec((tm, WIDTH), lambda i: (i, 0)),
            pl.BlockSpec((tm, WIDTH), lambda i: (i, 0)),
            pl.BlockSpec((tm, D_MODEL), lambda i: (i, BLK_GA // 2)),
            pl.BlockSpec((tm, D_MODEL), lambda i: (i, BLK_GB // 2)),
            hbm,
            hbm,
            hbm,
            pl.BlockSpec((tm, D_MODEL), lambda i: (i, 0)),
            pl.BlockSpec((1, D_MODEL), lambda i: (0, 0)),
            pl.BlockSpec((1, D_MODEL), lambda i: (0, 0)),
        ],
        out_specs=[
            pl.BlockSpec((tm, D_MODEL), lambda i: (i, 0)),
            pl.BlockSpec((tm, D_MODEL), lambda i: (i, 0)),
        ],
        out_shape=[
            jax.ShapeDtypeStruct((SEQ, D_MODEL), F32),
            jax.ShapeDtypeStruct((SEQ, D_MODEL), BF16),
        ],
        scratch_shapes=[
            pltpu.VMEM((WIDTH, D_MODEL), BF16),
            pltpu.VMEM((WIDTH, D_MODEL), BF16),
            pltpu.VMEM((D_MODEL, D_MODEL), BF16),
            pltpu.VMEM((2, MERGE_STAGE_ROWS, D_MODEL), F32),
            pltpu.SemaphoreType.DMA((2,)),
        ],
        compiler_params=_params("arbitrary"),
        name="merge",
    )(y_a, y_b, proj, proj, w_up_a, w_up_b, w_o, x2, norm_w, norm_ffn_pre)


def _ffn_residual_copy(x_hbm, xres_ref, sem, i):
    tm = xres_ref.shape[0]
    return pltpu.make_async_copy(x_hbm.at[pl.ds(pl.multiple_of(i * tm, tm), tm), :], xres_ref, sem)


def _ffn_kernel(h_ref, x_hbm, npost_ref, wg_ref, wu_ref, wd_ref, out_ref, xres_ref, sem):
    i = pl.program_id(0)
    f = pl.program_id(1)

    @pl.when(f == 0)
    def _():
        _ffn_residual_copy(x_hbm, xres_ref, sem, i).start()
        out_ref[...] = jnp.zeros_like(out_ref)

    h = h_ref[...]
    gate = jnp.dot(h, wg_ref[...].astype(BF16), preferred_element_type=F32)
    up = jnp.dot(h, wu_ref[...].astype(BF16), preferred_element_type=F32)
    act = (gate * _sigmoid(gate) * up).astype(BF16)
    out_ref[...] += jnp.dot(act, wd_ref[...].astype(BF16), preferred_element_type=F32)

    @pl.when(f == pl.num_programs(1) - 1)
    def _():
        _ffn_residual_copy(x_hbm, xres_ref, sem, i).wait()
        u = out_ref[...]
        u = u * lax.rsqrt(jnp.mean(u * u, axis=-1, keepdims=True) + RMS_EPS) * npost_ref[...]
        out_ref[...] = xres_ref[...] + u


def _ffn(h2, x1, norm_post, w_in, w_down):
    tm, tf = FFN_TM, FFN_TF
    nf = D_FF // tf
    return pl.pallas_call(
        _ffn_kernel,
        grid=(SEQ // tm, nf),
        in_specs=[
            pl.BlockSpec((tm, D_MODEL), lambda i, f: (i, 0)),
            pl.BlockSpec(memory_space=pl.ANY),
            pl.BlockSpec((1, D_MODEL), lambda i, f: (0, 0)),
            pl.BlockSpec((D_MODEL, tf), lambda i, f: (0, f)),
            pl.BlockSpec((D_MODEL, tf), lambda i, f: (0, f + nf)),
            pl.BlockSpec((tf, D_MODEL), lambda i, f: (f, 0)),
        ],
        out_specs=pl.BlockSpec((tm, D_MODEL), lambda i, f: (i, 0)),
        out_shape=jax.ShapeDtypeStruct((SEQ, D_MODEL), F32),
        scratch_shapes=[pltpu.VMEM((tm, D_MODEL), F32), pltpu.SemaphoreType.DMA(())],
        compiler_params=_params("arbitrary", "arbitrary"),
        name="ffn",
    )(h2, x1, norm_post, w_in, w_in, w_down)


def kernel(x, w_in, b_fox_f, hgrn_lb_logits, hgrn_norm_w, w_up_a, w_up_b, w_o, norm_mix_pre,
           norm_mix_post, norm_ffn_pre, norm_ffn_post, w_ffn_in, w_ffn_down):
    assert x.shape == (1, SEQ, D_MODEL) and w_in.shape[0] == 1
    w_t = w_in[0].T
    x2 = x[0]

    h, a_f, cx, nc = _prenorm(x2, norm_mix_pre, w_t, b_fox_f.reshape(N_HEADS, 1))
    proj = _inproj(h, w_t)
    y_a = _hgrn(proj, a_f, hgrn_lb_logits, hgrn_norm_w)
    y_b = _fox(proj, cx, nc)
    x1, h2 = _merge(y_a, y_b, proj, w_up_a[0], w_up_b[0], w_o[0], x2, norm_mix_post, norm_ffn_pre)
    out = _ffn(h2, x1, norm_ffn_post, w_ffn_in[0], w_ffn_down[0])
    return out[None]
```

```python
import functools

import numpy as np
import jax
import jax.numpy as jnp
from jax import lax
from jax.experimental import pallas as pl
from jax.experimental.pallas import tpu as pltpu

F32 = jnp.float32
BF16 = jnp.bfloat16

D_MODEL = 2048
SEQ = 8192
HEAD_DIM = 128
N_HEADS = 8
WIDTH = N_HEADS * HEAD_DIM
D_FF = 5632
RMS_EPS = 1e-6
N_PROJ = 3 * WIDTH + 2 * D_MODEL + 3 * WIDTH

VMEM_LIMIT_BYTES = 56 * 1024 * 1024
LANES = 128
F32_SUBLANES = 8
BF16_SUBLANES = 16

SRC_AQ, SRC_AF, SRC_AI, SRC_AG, SRC_BQ, SRC_BK, SRC_BV = 0, 1, 2, 3, 4, 5, 6
IN_NA = 7
BLK_AQ, BLK_AI, BLK_GA, BLK_GB, BLK_AG, BLK_BQ, BLK_BK, BLK_BV = 0, 1, 2, 4, 6, 7, 8, 9

PRE_TM = 512
IN_TM, IN_TN = 2048, 512
HG_TS, HG_C, HG_HEADS = 1024, 64, 8
HG_FAST_LOG2 = 96.0
FOX_TQ, FOX_TK, FOX_HEADS, FOX_QBLOCKS = 512, 256, 2, 2
LOG2E = 1.4426950408889634
FOX_SKIP_LOG2 = 160.0
FOX_NORM_MARGIN = 1.01
MERGE_TM = 256
MERGE_STAGE_ROWS = 256
FFN_TM, FFN_TF = 1024, 256


def _params(*sem):
    return pltpu.CompilerParams(dimension_semantics=sem, vmem_limit_bytes=VMEM_LIMIT_BYTES)


def _dot_nt(a, b):
    return lax.dot_general(a, b, (((1,), (1,)), ((), ())), preferred_element_type=F32)


def _dot_tn(a, b):
    return lax.dot_general(a, b, (((0,), (0,)), ((), ())), preferred_element_type=F32)


def _log_sigmoid(x):
    return jnp.minimum(x, 0.0) - jnp.log(1.0 + jnp.exp(-jnp.abs(x)))


def _sigmoid(x):
    return 1.0 / (1.0 + jnp.exp(-x))


def _split3(x):
    p0 = x.astype(BF16)
    r1 = x - p0.astype(F32)
    p1 = r1.astype(BF16)
    p2 = (r1 - p1.astype(F32)).astype(BF16)
    return p0, p1, p2


def _prenorm_kernel(x_ref, nw_ref, waf_ref, wf_ref, bf_ref, tri_ref, h_ref, af_ref, cx_ref, nc_ref,
                    wafb_ref, carry_ref):
    i = pl.program_id(0)
    tm = x_ref.shape[0]

    @pl.when(i == 0)
    def _():
        wafb_ref[...] = waf_ref[...].T.astype(BF16)
        carry_ref[...] = jnp.zeros_like(carry_ref)

    x = x_ref[...]
    ms = jnp.mean(x * x, axis=-1, keepdims=True)
    h_ref[...] = (x * lax.rsqrt(ms + RMS_EPS) * nw_ref[...]).astype(BF16)
    hb = h_ref[...]
    af_ref[...] = jnp.dot(hb, wafb_ref[...], preferred_element_type=F32)
    logit = _dot_nt(wf_ref[...].astype(BF16), hb) + bf_ref[...]
    nls = _log_sigmoid(logit) * (-LOG2E)
    tri = tri_ref[...]
    loc = sum(jnp.dot(p, tri, preferred_element_type=F32) for p in _split3(nls))
    nc = loc + carry_ref[:, 0:1]
    carry_ref[...] = jnp.broadcast_to(nc[:, tm - 1:tm], carry_ref.shape)
    pad = jnp.zeros((LANES - N_HEADS, tm), F32)
    nc_ref[...] = jnp.concatenate([nc, pad], axis=0).T
    parts = [p.astype(F32) for p in _split3(nc)]
    pad3 = jnp.zeros((LANES - 3 * N_HEADS, tm), F32)
    cx_ref[...] = jnp.concatenate(parts + [pad3], axis=0).T.astype(BF16)


def _prenorm(x2, norm_w, w, bf_col):
    tm = PRE_TM
    tri = jnp.asarray(np.triu(np.ones((tm, tm), np.float32)), BF16)
    once = pl.Buffered(1)
    return pl.pallas_call(
        _prenorm_kernel,
        grid=(SEQ // tm,),
        in_specs=[
            pl.BlockSpec((tm, D_MODEL), lambda i: (i, 0)),
            pl.BlockSpec((1, D_MODEL), lambda i: (0, 0)),
            pl.BlockSpec((WIDTH, D_MODEL), lambda i: (SRC_AF, 0), pipeline_mode=once),
            pl.BlockSpec((N_HEADS, D_MODEL), lambda i: (IN_NA * WIDTH // N_HEADS, 0), pipeline_mode=once),
            pl.BlockSpec((N_HEADS, 1), lambda i: (0, 0)),
            pl.BlockSpec((tm, tm), lambda i: (0, 0), pipeline_mode=once),
        ],
        out_specs=[
            pl.BlockSpec((tm, D_MODEL), lambda i: (i, 0)),
            pl.BlockSpec((tm, WIDTH), lambda i: (i, 0)),
            pl.BlockSpec((tm, LANES), lambda i: (i, 0)),
            pl.BlockSpec((tm, LANES), lambda i: (i, 0)),
        ],
        out_shape=[
            jax.ShapeDtypeStruct((SEQ, D_MODEL), BF16),
            jax.ShapeDtypeStruct((SEQ, WIDTH), F32),
            jax.ShapeDtypeStruct((SEQ, LANES), BF16),
            jax.ShapeDtypeStruct((SEQ, LANES), F32),
        ],
        scratch_shapes=[pltpu.VMEM((D_MODEL, WIDTH), BF16), pltpu.VMEM((N_HEADS, LANES), F32)],
        compiler_params=_params("arbitrary"),
        name="prenorm",
    )(x2, norm_w, w, w, bf_col, tri)


def _inproj_kernel(h_ref, wa_ref, wb_ref, proj_ref, wc_ref):
    j = pl.program_id(1)
    i = pl.program_id(2)
    tm, tn = proj_ref.shape
    nb = WIDTH // tn
    src = _inproj_src_block(j, nb)

    @pl.when(i == 0)
    def _():
        @pl.when(src < IN_NA * nb)
        def _():
            scale = jnp.where(src // nb == SRC_BQ, LOG2E * HEAD_DIM ** -0.5, 1.0)
            wc_ref[...] = (wa_ref[...] * scale).astype(BF16)

        @pl.when(src >= IN_NA * nb)
        def _():
            wc_ref[...] = jnp.concatenate([wa_ref[N_HEADS:, :], wb_ref[:N_HEADS, :]], axis=0).astype(BF16)

    rows = pl.ds(pl.multiple_of(i * tm, tm), tm)
    proj_ref[...] = _dot_nt(h_ref[rows, :], wc_ref[...]).astype(BF16)


def _inproj_src_block(j, nb):
    return jnp.where(j < SRC_AF * nb, j, j + nb)


def _inproj_out_block(j, nb):
    src = _inproj_src_block(j, nb)
    g = src // nb
    g_out = jnp.int32(BLK_AQ)
    for g_src, g_dst in ((SRC_AI, BLK_AI), (SRC_AG, BLK_AG), (SRC_BQ, BLK_BQ), (SRC_BK, BLK_BK), (SRC_BV, BLK_BV)):
        g_out = jnp.where(g == g_src, g_dst, g_out)
    return jnp.where(g < IN_NA, g_out * nb + src % nb, src - IN_NA * nb + BLK_GA * nb)


def _inproj(h, w):
    tm, tn = IN_TM, IN_TN
    nb = WIDTH // tn
    half = SEQ // 2
    n_i = half // tm
    first_gate = IN_NA * nb
    grid = (2, N_PROJ // tn, n_i)
    return pl.pallas_call(
        _inproj_kernel,
        grid=grid,
        in_specs=[
            pl.BlockSpec((half, D_MODEL), lambda s, j, i: (s, 0), pipeline_mode=pl.Buffered(1)),
            pl.BlockSpec((tn, D_MODEL), lambda s, j, i: (_inproj_src_block(j, nb), 0)),
            pl.BlockSpec((tn, D_MODEL), lambda s, j, i: (jnp.maximum(_inproj_src_block(j, nb), first_gate) + 1, 0)),
        ],
        out_specs=pl.BlockSpec((tm, tn), lambda s, j, i: (s * n_i + i, _inproj_out_block(j, nb))),
        out_shape=jax.ShapeDtypeStruct((SEQ, N_PROJ), BF16),
        scratch_shapes=[pltpu.VMEM((tn, D_MODEL), BF16)],
        compiler_params=_params("arbitrary", "arbitrary", "arbitrary"),
        name="inproj",
    )(h, w, w)


def _hgrn_constants(c):
    n_lvl = int(np.log2(c))
    t = np.arange(c)[:, None]
    j = np.arange(c)[None, :]
    blocks = [(j <= t), (j > t)]
    level = np.full((c, c), -1, np.int32)
    level[np.arange(c), np.arange(c)] = 0
    for l in range(n_lvl):
        b = 2 << l
        mid = (t // b) * b + b // 2 - 1
        second = (t % b) >= b // 2
        m = np.where(second, (j > mid) & (j <= t), (j > t) & (j <= mid))
        blocks.append(m)
        s = np.arange(c)[None, :]
        own = (t // b == s // b) & second & ((s % b) < b // 2)
        level[own] = l + 1
    sums = np.concatenate(blocks, axis=0).astype(np.float32)
    sums2 = np.concatenate([sums, sums], axis=1)
    return jnp.asarray(sums2, BF16), jnp.asarray(level), n_lvl


def _hgrn_kernel(q_ref, z_ref, v_ref, g_ref, lbl_ref, nw_ref, sums_ref, lvl_ref, y_ref,
                 st_ref, k_ref, b_ref, kout_ref, qin_ref, g2_ref, dlast_ref, bmin_ref, *, c, n_lvl):
    n_heads = st_ref.shape[0]
    d = HEAD_DIM

    @pl.when(pl.program_id(1) == 0)
    def _():
        st_ref[...] = jnp.zeros_like(st_ref)

    logits = lbl_ref[...]
    ex = jnp.exp(logits - jnp.max(logits, axis=0, keepdims=True))
    lb = ex[0:1, :] / jnp.sum(ex, axis=0, keepdims=True)
    one_m_lb = 1.0 - lb
    nw = nw_ref[...]
    sums = sums_ref[...]
    lvl = lvl_ref[...]
    n_chunks = q_ref.shape[0] // c

    heads = [slice(hh * d, (hh + 1) * d) for hh in range(n_heads)]
    causal = lvl >= 0

    def chunk_rows(ci):
        return pl.ds(pl.multiple_of(ci * c, c), c)

    bmin_ref[...] = jnp.zeros_like(bmin_ref)

    def prepare(ci, carry):
        r = chunk_rows(ci)
        sig = _sigmoid(z_ref[r, :])
        g = jnp.log2(lb + one_m_lb * sig)
        k_all = one_m_lb * (1.0 - sig)
        g_hi = g.astype(BF16)
        g_lo = (g - g_hi.astype(F32)).astype(BF16)
        g2 = jnp.concatenate([g_hi, g_lo], axis=0)
        e01 = jnp.dot(sums[0:2 * c], g2, preferred_element_type=F32)
        b = e01[0:c]
        dec_b = jnp.exp2(b)
        g2_ref[ci] = g2
        b_ref[r, :] = b
        k_ref[r, :] = k_all
        kout_ref[r, :] = (k_all * jnp.exp2(e01[c:2 * c])).astype(BF16)
        qin_ref[r, :] = (q_ref[r, :].astype(F32) * dec_b).astype(BF16)
        dlast_ref[ci] = jnp.broadcast_to(dec_b[c - 1:c, :], dlast_ref.shape[1:])
        bmin_ref[...] = jnp.minimum(bmin_ref[...], b)
        return carry

    lax.fori_loop(0, n_chunks, prepare, 0, unroll=2)
    b_min = jnp.min(bmin_ref[...])

    def finish(ci, scores):
        r = chunk_rows(ci)
        o_intra, inc = [], []
        for hh, cols in enumerate(heads):
            v = v_ref[r, cols]
            o_intra.append(jnp.dot(scores[hh], v, preferred_element_type=F32))
            inc.append(_dot_tn(v, kout_ref[r, cols]))
        for hh, cols in enumerate(heads):
            st = st_ref[hh]
            o = o_intra[hh] + _dot_nt(qin_ref[r, cols], st.astype(BF16))
            st_ref[hh] = st * dlast_ref[ci, 0:1, cols] + inc[hh]
            o = o * lax.rsqrt(jnp.mean(o * o, axis=-1, keepdims=True) + RMS_EPS) * nw
            gt = g_ref[r, cols].astype(F32)
            y_ref[r, cols] = (o * gt * _sigmoid(gt)).astype(y_ref.dtype)

    def fast_chunk(ci, carry):
        r = chunk_rows(ci)
        k_up = (k_ref[r, :] * jnp.exp2(-b_ref[r, :])).astype(BF16)
        scores = [jnp.where(causal, _dot_nt(qin_ref[r, cols], k_up[:, cols]), 0.0).astype(BF16)
                  for cols in heads]
        finish(ci, scores)
        return carry

    def safe_chunk(ci, carry):
        r = chunk_rows(ci)
        dec_l = jnp.exp2(jnp.dot(sums[2 * c:], g2_ref[ci], preferred_element_type=F32))
        scores = []
        for cols in heads:
            q = q_ref[r, cols].astype(F32)
            k = k_ref[r, cols]
            sc = jnp.where(lvl == 0, _dot_nt(q.astype(BF16), k.astype(BF16)), 0.0)
            for l in range(n_lvl):
                d_l = dec_l[l * c:(l + 1) * c, cols]
                s_l = _dot_nt((q * d_l).astype(BF16), (k * d_l).astype(BF16))
                sc = jnp.where(lvl == l + 1, s_l, sc)
            scores.append(sc.astype(BF16))
        finish(ci, scores)
        return carry

    @pl.when(b_min >= -HG_FAST_LOG2)
    def _():
        lax.fori_loop(0, n_chunks, fast_chunk, 0, unroll=8)

    @pl.when(b_min < -HG_FAST_LOG2)
    def _():
        lax.fori_loop(0, n_chunks, safe_chunk, 0)


def _hgrn(proj, a_f, lb_logits, norm_w):
    ts, c, hg = HG_TS, HG_C, HG_HEADS
    sums, lvl, n_lvl = _hgrn_constants(c)
    w = hg * HEAD_DIM
    per_blk = WIDTH // w
    grid = (N_HEADS // hg, SEQ // ts)
    return pl.pallas_call(
        functools.partial(_hgrn_kernel, c=c, n_lvl=n_lvl),
        grid=grid,
        in_specs=[
            pl.BlockSpec((ts, w), lambda h, i: (i, BLK_AQ * per_blk + h)),
            pl.BlockSpec((ts, w), lambda h, i: (i, h)),
            pl.BlockSpec((ts, w), lambda h, i: (i, BLK_AI * per_blk + h)),
            pl.BlockSpec((ts, w), lambda h, i: (i, BLK_AG * per_blk + h)),
            pl.BlockSpec((lb_logits.shape[0], w), lambda h, i: (0, h)),
            pl.BlockSpec((1, HEAD_DIM), lambda h, i: (0, 0)),
            pl.BlockSpec(sums.shape, lambda h, i: (0, 0)),
            pl.BlockSpec(lvl.shape, lambda h, i: (0, 0)),
        ],
        out_specs=pl.BlockSpec((ts, w), lambda h, i: (i, h)),
        out_shape=jax.ShapeDtypeStruct((SEQ, WIDTH), BF16),
        scratch_shapes=[
            pltpu.VMEM((hg, HEAD_DIM, HEAD_DIM), F32),
            pltpu.VMEM((ts, w), F32),
            pltpu.VMEM((ts, w), F32),
            pltpu.VMEM((ts, w), BF16),
            pltpu.VMEM((ts, w), BF16),
            pltpu.VMEM((ts // c, 2 * c, w), BF16),
            pltpu.VMEM((ts // c, F32_SUBLANES, w), F32),
            pltpu.VMEM((c, w), F32),
        ],
        compiler_params=_params("arbitrary", "arbitrary"),
        name="hgrn",
    )(proj, a_f, proj, proj, lb_logits, norm_w, sums, lvl)


def _fox_kernel(q_ref, k_ref, v_ref, cx_ref, nc_ref, o_ref, vt_ref, nrm_ref, m_ref, acc_ref, *, tq, tk):
    g = pl.program_id(0)
    i = pl.program_id(1)
    n_heads = vt_ref.shape[0]
    d = HEAD_DIM
    s_len = k_ref.shape[0]
    n_sub = tq // tk
    n_kt = s_len // tk
    vrows = vt_ref.shape[2]
    heads = [(hh, slice(hh * d, (hh + 1) * d)) for hh in range(n_heads)]

    @pl.when(i == 0)
    def _():
        ones_row = jnp.where(lax.broadcasted_iota(jnp.int32, (vrows - d, tk), 0) == 0, 1.0, 0.0)
        wg = n_heads * d
        same_head = (lax.broadcasted_iota(jnp.int32, (wg, wg), 0) // d
                     == lax.broadcasted_iota(jnp.int32, (wg, wg), 1) // d)
        ones_blk = jnp.where(same_head, 1.0, 0.0).astype(BF16)

        def sq_norms(x):
            return jnp.dot(x * x, ones_blk, preferred_element_type=F32)

        kn = jnp.zeros((tk, wg), F32)
        qn = jnp.zeros((tk, wg), F32)
        for r in range(n_kt):
            rows = slice(r * tk, (r + 1) * tk)
            for hh, cols in heads:
                vt_ref[hh, r, 0:d, :] = v_ref[rows, cols].astype(F32).T.astype(BF16)
                vt_ref[hh, r, d:vrows, :] = ones_row.astype(BF16)
            kn = jnp.maximum(kn, sq_norms(k_ref[rows, :]))
            qn = jnp.maximum(qn, sq_norms(q_ref[rows, :]))
        nrm = 2.0 * FOX_NORM_MARGIN * jnp.sqrt(jnp.max(qn, axis=0, keepdims=True)
                                               * jnp.max(kn, axis=0, keepdims=True))
        for hh, cols in heads:
            nrm_ref[hh] = jnp.broadcast_to(nrm[:, cols], nrm_ref.shape[1:])

    n_qb = o_ref.shape[0] // tq
    units = [(hh, cols, qb) for hh, cols in heads for qb in range(n_qb)]
    ends = nc_ref[pl.ds(tk - 1, n_kt, stride=tk), :]
    lane = lax.broadcasted_iota(jnp.int32, ends.shape, 1)
    rsel = lax.broadcasted_iota(jnp.int32, (d, tq), 0)
    qa, r_lo, first_tile = [], [], []
    for hh, cols, qb in units:
        h = g * n_heads + hh
        q0 = pl.multiple_of((i * n_qb + qb) * tq, tq)
        thr = nc_ref[pl.ds(q0, 1), :] - nrm_ref[hh, 0:1, :] - FOX_SKIP_LOG2
        r_lo.append(jnp.sum(jnp.where((ends < thr) & (lane == h), 1, 0)))
        first_tile.append((i * n_qb + qb) * n_sub)
        sel = jnp.where((rsel < 3 * N_HEADS) & ((rsel & (N_HEADS - 1)) == h), 1.0, 0.0).astype(BF16)
        qa.append(jnp.concatenate([q_ref[pl.ds(q0, tq), cols].astype(F32).T.astype(BF16), sel], axis=0))

    def scores(u, r, mask_off):
        hh, cols, _ = units[u]
        ks = pl.ds(pl.multiple_of(r * tk, tk), tk)
        ka = jnp.concatenate([k_ref[ks, cols], cx_ref[ks, :]], axis=1)
        s = jnp.dot(ka, qa[u], preferred_element_type=F32)
        if mask_off is not None:
            kid = lax.broadcasted_iota(jnp.int32, s.shape, 0) + mask_off
            qid = lax.broadcasted_iota(jnp.int32, s.shape, 1)
            s = jnp.where(kid <= qid, s, -jnp.inf)
        return s

    def first_step(with_prev):
        tiles = []
        for u in range(len(units)):
            t = [(first_tile[u] + rr, rr * tk) for rr in range(n_sub)]
            tiles.append(([(first_tile[u] - 1, None)] if with_prev[u] else []) + t)
        ss = [[scores(u, r, off) for r, off in tiles[u]] for u in range(len(units))]
        ms = []
        for u in range(len(units)):
            m = jnp.max(ss[u][0], axis=0, keepdims=True)
            for s in ss[u][1:]:
                m = jnp.maximum(m, jnp.max(s, axis=0, keepdims=True))
            ms.append(m)
        for u, (hh, _, _) in enumerate(units):
            pv = None
            for (r, _), s in zip(tiles[u], ss[u]):
                p = jnp.exp2(s - ms[u]).astype(BF16)
                dd = jnp.dot(vt_ref[hh, r], p, preferred_element_type=F32)
                pv = dd if pv is None else pv + dd
            acc_ref[u] = pv
            m_ref[u] = ms[u]

    def later_step(u, r):
        s = scores(u, r, None)
        m_prev = m_ref[u]
        m_new = jnp.maximum(m_prev, jnp.max(s, axis=0, keepdims=True))
        pv = jnp.dot(vt_ref[units[u][0], r], jnp.exp2(s - m_new).astype(BF16), preferred_element_type=F32)
        acc_ref[u] = jnp.exp2(m_prev - m_new) * acc_ref[u] + pv
        m_ref[u] = m_new

    @pl.when(i == 0)
    def _():
        first_step([qb > 0 for _, _, qb in units])

    @pl.when(i > 0)
    def _():
        first_step([True] * len(units))

    for u in range(len(units)):
        def off_diag(r, carry, u=u):
            later_step(u, r)
            return carry
        lax.fori_loop(r_lo[u], first_tile[u] - 1, off_diag, 0)

    for u, (hh, cols, qb) in enumerate(units):
        acc = acc_ref[u]
        o_ref[qb * tq:(qb + 1) * tq, cols] = (acc[0:d] / acc[d:d + 1]).T.astype(o_ref.dtype)


def _fox(proj, cx, nc):
    tq, tk, hg, n_qb = FOX_TQ, FOX_TK, FOX_HEADS, FOX_QBLOCKS
    w = hg * HEAD_DIM
    per_blk = WIDTH // w
    vrows = HEAD_DIM + BF16_SUBLANES
    return pl.pallas_call(
        functools.partial(_fox_kernel, tq=tq, tk=tk),
        grid=(N_HEADS // hg, SEQ // (n_qb * tq)),
        in_specs=[
            pl.BlockSpec((SEQ, w), lambda g, i: (0, BLK_BQ * per_blk + g)),
            pl.BlockSpec((SEQ, w), lambda g, i: (0, BLK_BK * per_blk + g)),
            pl.BlockSpec((SEQ, w), lambda g, i: (0, BLK_BV * per_blk + g)),
            pl.BlockSpec((SEQ, LANES), lambda g, i: (0, 0)),
            pl.BlockSpec((SEQ, LANES), lambda g, i: (0, 0)),
        ],
        out_specs=pl.BlockSpec((n_qb * tq, w), lambda g, i: (i, g)),
        out_shape=jax.ShapeDtypeStruct((SEQ, WIDTH), BF16),
        scratch_shapes=[
            pltpu.VMEM((hg, SEQ // tk, vrows, tk), BF16),
            pltpu.VMEM((hg, F32_SUBLANES, LANES), F32),
            pltpu.VMEM((hg * n_qb, 1, tq), F32),
            pltpu.VMEM((hg * n_qb, vrows, tq), F32),
        ],
        compiler_params=_params("arbitrary", "arbitrary"),
        name="fox",
    )(proj, proj, proj, cx, nc)


def _merge_load_weights(srcs, dsts, stage_ref, sems):
    rows = stage_ref.shape[1]
    chunks = [(src, dst, r0) for src, dst in zip(srcs, dsts) for r0 in range(0, src.shape[0], rows)]

    def copy(n):
        src, _, r0 = chunks[n]
        return pltpu.make_async_copy(src.at[pl.ds(r0, rows), :], stage_ref.at[n % 2], sems.at[n % 2])

    copy(0).start()
    for n, (_, dst, r0) in enumerate(chunks):
        if n + 1 < len(chunks):
            copy(n + 1).start()
        copy(n).wait()
        dst[r0:r0 + rows, :] = stage_ref[n % 2].astype(BF16)


def _merge_kernel(oa_ref, ob_ref, ga_ref, gb_ref, wa_hbm, wb_hbm, wo_hbm, x_ref, nw_ref, npre_ref,
                  out_ref, h2_ref, wa_ref, wb_ref, wo_ref, stage_ref, sems):
    @pl.when(pl.program_id(0) == 0)
    def _():
        _merge_load_weights((wa_hbm, wb_hbm, wo_hbm), (wa_ref, wb_ref, wo_ref), stage_ref, sems)

    ya = jnp.dot(oa_ref[...], wa_ref[...], preferred_element_type=F32)
    yb = jnp.dot(ob_ref[...], wb_ref[...], preferred_element_type=F32)
    merged = _sigmoid(ga_ref[...].astype(F32)) * ya + _sigmoid(gb_ref[...].astype(F32)) * yb
    u = jnp.dot(merged.astype(BF16), wo_ref[...], preferred_element_type=F32)
    u = u * lax.rsqrt(jnp.mean(u * u, axis=-1, keepdims=True) + RMS_EPS) * nw_ref[...]
    x1 = x_ref[...] + u
    out_ref[...] = x1
    ms = jnp.mean(x1 * x1, axis=-1, keepdims=True)
    h2_ref[...] = (x1 * lax.rsqrt(ms + RMS_EPS) * npre_ref[...]).astype(BF16)


def _merge(y_a, y_b, proj, w_up_a, w_up_b, w_o, x2, norm_w, norm_ffn_pre):
    tm = MERGE_TM
    hbm = pl.BlockSpec(memory_space=pl.ANY)
    return pl.pallas_call(
        _merge_kernel,
        grid=(SEQ // tm,),
        in_specs=[
            pl.BlockSpec((tm, WIDTH), lambda i: (i, 0)),
            pl.BlockSpec((tm, WIDTH), lambda i: (i, 0)),
            pl.BlockSpec((tm, D_MODEL), lambda i: (i, BLK_GA // 2)),
            pl.BlockSpec((tm, D_MODEL), lambda i: (i, BLK_GB // 2)),
            hbm,
            hbm,
            hbm,
            pl.BlockSpec((tm, D_MODEL), lambda i: (i, 0)),
            pl.BlockSpec((1, D_MODEL), lambda i: (0, 0)),
            pl.BlockSpec((1, D_MODEL), lambda i: (0, 0)),
        ],
        out_specs=[
            pl.BlockSpec((tm, D_MODEL), lambda i: (i, 0)),
            pl.BlockSpec((tm, D_MODEL), lambda i: (i, 0)),
        ],
        out_shape=[
            jax.ShapeDtypeStruct((SEQ, D_MODEL), F32),
            jax.ShapeDtypeStruct((SEQ, D_MODEL), BF16),
        ],
        scratch_shapes=[
            pltpu.VMEM((WIDTH, D_MODEL), BF16),
            pltpu.VMEM((WIDTH, D_MODEL), BF16),
            pltpu.VMEM((D_MODEL, D_MODEL), BF16),
            pltpu.VMEM((2, MERGE_STAGE_ROWS, D_MODEL), F32),
            pltpu.SemaphoreType.DMA((2,)),
        ],
        compiler_params=_params("arbitrary"),
        name="merge",
    )(y_a, y_b, proj, proj, w_up_a, w_up_b, w_o, x2, norm_w, norm_ffn_pre)


def _ffn_residual_copy(x_hbm, xres_ref, sem, i):
    tm = xres_ref.shape[0]
    return pltpu.make_async_copy(x_hbm.at[pl.ds(pl.multiple_of(i * tm, tm), tm), :], xres_ref, sem)


def _ffn_kernel(h_ref, x_hbm, npost_ref, wg_ref, wu_ref, wd_ref, out_ref, xres_ref, sem):
    i = pl.program_id(0)
    f = pl.program_id(1)

    @pl.when(f == 0)
    def _():
        _ffn_residual_copy(x_hbm, xres_ref, sem, i).start()
        out_ref[...] = jnp.zeros_like(out_ref)

    h = h_ref[...]
    gate = jnp.dot(h, wg_ref[...].astype(BF16), preferred_element_type=F32)
    up = jnp.dot(h, wu_ref[...].astype(BF16), preferred_element_type=F32)
    act = (gate * _sigmoid(gate) * up).astype(BF16)
    out_ref[...] += jnp.dot(act, wd_ref[...].astype(BF16), preferred_element_type=F32)

    @pl.when(f == pl.num_programs(1) - 1)
    def _():
        _ffn_residual_copy(x_hbm, xres_ref, sem, i).wait()
        u = out_ref[...]
        u = u * lax.rsqrt(jnp.mean(u * u, axis=-1, keepdims=True) + RMS_EPS) * npost_ref[...]
        out_ref[...] = xres_ref[...] + u


def _ffn(h2, x1, norm_post, w_in, w_down):
    tm, tf = FFN_TM, FFN_TF
    nf = D_FF // tf
    return pl.pallas_call(
        _ffn_kernel,
        grid=(SEQ // tm, nf),
        in_specs=[
            pl.BlockSpec((tm, D_MODEL), lambda i, f: (i, 0)),
            pl.BlockSpec(memory_space=pl.ANY),
            pl.BlockSpec((1, D_MODEL), lambda i, f: (0, 0)),
            pl.BlockSpec((D_MODEL, tf), lambda i, f: (0, f)),
            pl.BlockSpec((D_MODEL, tf), lambda i, f: (0, f + nf)),
            pl.BlockSpec((tf, D_MODEL), lambda i, f: (f, 0)),
        ],
        out_specs=pl.BlockSpec((tm, D_MODEL), lambda i, f: (i, 0)),
        out_shape=jax.ShapeDtypeStruct((SEQ, D_MODEL), F32),
        scratch_shapes=[pltpu.VMEM((tm, D_MODEL), F32), pltpu.SemaphoreType.DMA(())],
        compiler_params=_params("arbitrary", "arbitrary"),
        name="ffn",
    )(h2, x1, norm_post, w_in, w_in, w_down)


def kernel(x, w_in, b_fox_f, hgrn_lb_logits, hgrn_norm_w, w_up_a, w_up_b, w_o, norm_mix_pre,
           norm_mix_post, norm_ffn_pre, norm_ffn_post, w_ffn_in, w_ffn_down):
    assert x.shape == (1, SEQ, D_MODEL) and w_in.shape[0] == 1
    w_t = w_in[0].T
    x2 = x[0]

    h, a_f, cx, nc = _prenorm(x2, norm_mix_pre, w_t, b_fox_f.reshape(N_HEADS, 1))
    proj = _inproj(h, w_t)
    y_a = _hgrn(proj, a_f, hgrn_lb_logits, hgrn_norm_w)
    y_b = _fox(proj, cx, nc)
    x1, h2 = _merge(y_a, y_b, proj, w_up_a[0], w_up_b[0], w_o[0], x2, norm_mix_post, norm_ffn_pre)
    out = _ffn(h2, x1, norm_ffn_post, w_ffn_in[0], w_ffn_down[0])
    return out[None]
```

```python
import functools

import numpy as np
import jax
import jax.numpy as jnp
from jax import lax
from jax.experimental import pallas as pl
from jax.experimental.pallas import tpu as pltpu

F32 = jnp.float32
BF16 = jnp.bfloat16

D_MODEL = 2048
SEQ = 8192
HEAD_DIM = 128
N_HEADS = 8
WIDTH = N_HEADS * HEAD_DIM
D_FF = 5632
RMS_EPS = 1e-6
N_PROJ = 3 * WIDTH + 2 * D_MODEL + 3 * WIDTH

VMEM_LIMIT_BYTES = 56 * 1024 * 1024
LANES = 128
F32_SUBLANES = 8
BF16_SUBLANES = 16

SRC_AQ, SRC_AF, SRC_AI, SRC_AG, SRC_BQ, SRC_BK, SRC_BV = 0, 1, 2, 3, 4, 5, 6
IN_NA = 7
BLK_AQ, BLK_AI, BLK_GA, BLK_GB, BLK_AG, BLK_BQ, BLK_BK, BLK_BV = 0, 1, 2, 4, 6, 7, 8, 9

PRE_TM = 512
IN_TM, IN_TN = 2048, 512
HG_TS, HG_C, HG_HEADS = 1024, 64, 8
HG_FAST_LOG2 = 96.0
FOX_TQ, FOX_TK, FOX_HEADS, FOX_QBLOCKS = 256, 256, 2, 4
LOG2E = 1.4426950408889634
FOX_SKIP_LOG2 = 160.0
FOX_NORM_MARGIN = 1.01
MERGE_TM = 256
MERGE_STAGE_ROWS = 256
FFN_TM, FFN_TF = 1024, 256


def _params(*sem):
    return pltpu.CompilerParams(dimension_semantics=sem, vmem_limit_bytes=VMEM_LIMIT_BYTES)


def _dot_nt(a, b):
    return lax.dot_general(a, b, (((1,), (1,)), ((), ())), preferred_element_type=F32)


def _dot_tn(a, b):
    return lax.dot_general(a, b, (((0,), (0,)), ((), ())), preferred_element_type=F32)


def _log_sigmoid(x):
    return jnp.minimum(x, 0.0) - jnp.log(1.0 + jnp.exp(-jnp.abs(x)))


def _sigmoid(x):
    return 1.0 / (1.0 + jnp.exp(-x))


def _split3(x):
    p0 = x.astype(BF16)
    r1 = x - p0.astype(F32)
    p1 = r1.astype(BF16)
    p2 = (r1 - p1.astype(F32)).astype(BF16)
    return p0, p1, p2


def _prenorm_kernel(x_ref, nw_ref, waf_ref, wf_ref, bf_ref, tri_ref, h_ref, af_ref, cx_ref, nc_ref,
                    wafb_ref, carry_ref):
    i = pl.program_id(0)
    tm = x_ref.shape[0]

    @pl.when(i == 0)
    def _():
        wafb_ref[...] = waf_ref[...].T.astype(BF16)
        carry_ref[...] = jnp.zeros_like(carry_ref)

    x = x_ref[...]
    ms = jnp.mean(x * x, axis=-1, keepdims=True)
    h_ref[...] = (x * lax.rsqrt(ms + RMS_EPS) * nw_ref[...]).astype(BF16)
    hb = h_ref[...]
    af_ref[...] = jnp.dot(hb, wafb_ref[...], preferred_element_type=F32)
    logit = _dot_nt(wf_ref[...].astype(BF16), hb) + bf_ref[...]
    nls = _log_sigmoid(logit) * (-LOG2E)
    tri = tri_ref[...]
    loc = sum(jnp.dot(p, tri, preferred_element_type=F32) for p in _split3(nls))
    nc = loc + carry_ref[:, 0:1]
    carry_ref[...] = jnp.broadcast_to(nc[:, tm - 1:tm], carry_ref.shape)
    pad = jnp.zeros((LANES - N_HEADS, tm), F32)
    nc_ref[...] = jnp.concatenate([nc, pad], axis=0).T
    parts = [p.astype(F32) for p in _split3(nc)]
    pad3 = jnp.zeros((LANES - 3 * N_HEADS, tm), F32)
    cx_ref[...] = jnp.concatenate(parts + [pad3], axis=0).T.astype(BF16)


def _prenorm(x2, norm_w, w, bf_col):
    tm = PRE_TM
    tri = jnp.asarray(np.triu(np.ones((tm, tm), np.float32)), BF16)
    once = pl.Buffered(1)
    return pl.pallas_call(
        _prenorm_kernel,
        grid=(SEQ // tm,),
        in_specs=[
            pl.BlockSpec((tm, D_MODEL), lambda i: (i, 0)),
            pl.BlockSpec((1, D_MODEL), lambda i: (0, 0)),
            pl.BlockSpec((WIDTH, D_MODEL), lambda i: (SRC_AF, 0), pipeline_mode=once),
            pl.BlockSpec((N_HEADS, D_MODEL), lambda i: (IN_NA * WIDTH // N_HEADS, 0), pipeline_mode=once),
            pl.BlockSpec((N_HEADS, 1), lambda i: (0, 0)),
            pl.BlockSpec((tm, tm), lambda i: (0, 0), pipeline_mode=once),
        ],
        out_specs=[
            pl.BlockSpec((tm, D_MODEL), lambda i: (i, 0)),
            pl.BlockSpec((tm, WIDTH), lambda i: (i, 0)),
            pl.BlockSpec((tm, LANES), lambda i: (i, 0)),
            pl.BlockSpec((tm, LANES), lambda i: (i, 0)),
        ],
        out_shape=[
            jax.ShapeDtypeStruct((SEQ, D_MODEL), BF16),
            jax.ShapeDtypeStruct((SEQ, WIDTH), F32),
            jax.ShapeDtypeStruct((SEQ, LANES), BF16),
            jax.ShapeDtypeStruct((SEQ, LANES), F32),
        ],
        scratch_shapes=[pltpu.VMEM((D_MODEL, WIDTH), BF16), pltpu.VMEM((N_HEADS, LANES), F32)],
        compiler_params=_params("arbitrary"),
        name="prenorm",
    )(x2, norm_w, w, w, bf_col, tri)


def _inproj_kernel(h_ref, wa_ref, wb_ref, proj_ref, wc_ref):
    j = pl.program_id(1)
    i = pl.program_id(2)
    tm, tn = proj_ref.shape
    nb = WIDTH // tn
    src = _inproj_src_block(j, nb)

    @pl.when(i == 0)
    def _():
        @pl.when(src < IN_NA * nb)
        def _():
            scale = jnp.where(src // nb == SRC_BQ, LOG2E * HEAD_DIM ** -0.5, 1.0)
            wc_ref[...] = (wa_ref[...] * scale).astype(BF16)

        @pl.when(src >= IN_NA * nb)
        def _():
            wc_ref[...] = jnp.concatenate([wa_ref[N_HEADS:, :], wb_ref[:N_HEADS, :]], axis=0).astype(BF16)

    rows = pl.ds(pl.multiple_of(i * tm, tm), tm)
    proj_ref[...] = _dot_nt(h_ref[rows, :], wc_ref[...]).astype(BF16)


def _inproj_src_block(j, nb):
    return jnp.where(j < SRC_AF * nb, j, j + nb)


def _inproj_out_block(j, nb):
    src = _inproj_src_block(j, nb)
    g = src // nb
    g_out = jnp.int32(BLK_AQ)
    for g_src, g_dst in ((SRC_AI, BLK_AI), (SRC_AG, BLK_AG), (SRC_BQ, BLK_BQ), (SRC_BK, BLK_BK), (SRC_BV, BLK_BV)):
        g_out = jnp.where(g == g_src, g_dst, g_out)
    return jnp.where(g < IN_NA, g_out * nb + src % nb, src - IN_NA * nb + BLK_GA * nb)


def _inproj(h, w):
    tm, tn = IN_TM, IN_TN
    nb = WIDTH // tn
    half = SEQ // 2
    n_i = half // tm
    first_gate = IN_NA * nb
    grid = (2, N_PROJ // tn, n_i)
    return pl.pallas_call(
        _inproj_kernel,
        grid=grid,
        in_specs=[
            pl.BlockSpec((half, D_MODEL), lambda s, j, i: (s, 0), pipeline_mode=pl.Buffered(1)),
            pl.BlockSpec((tn, D_MODEL), lambda s, j, i: (_inproj_src_block(j, nb), 0)),
            pl.BlockSpec((tn, D_MODEL), lambda s, j, i: (jnp.maximum(_inproj_src_block(j, nb), first_gate) + 1, 0)),
        ],
        out_specs=pl.BlockSpec((tm, tn), lambda s, j, i: (s * n_i + i, _inproj_out_block(j, nb))),
        out_shape=jax.ShapeDtypeStruct((SEQ, N_PROJ), BF16),
        scratch_shapes=[pltpu.VMEM((tn, D_MODEL), BF16)],
        compiler_params=_params("arbitrary", "arbitrary", "arbitrary"),
        name="inproj",
    )(h, w, w)


def _hgrn_constants(c):
    n_lvl = int(np.log2(c))
    t = np.arange(c)[:, None]
    j = np.arange(c)[None, :]
    blocks = [(j <= t), (j > t)]
    level = np.full((c, c), -1, np.int32)
    level[np.arange(c), np.arange(c)] = 0
    for l in range(n_lvl):
        b = 2 << l
        mid = (t // b) * b + b // 2 - 1
        second = (t % b) >= b // 2
        m = np.where(second, (j > mid) & (j <= t), (j > t) & (j <= mid))
        blocks.append(m)
        s = np.arange(c)[None, :]
        own = (t // b == s // b) & second & ((s % b) < b // 2)
        level[own] = l + 1
    sums = np.concatenate(blocks, axis=0).astype(np.float32)
    sums2 = np.concatenate([sums, sums], axis=1)
    return jnp.asarray(sums2, BF16), jnp.asarray(level), n_lvl


def _hgrn_kernel(q_ref, z_ref, v_ref, g_ref, lbl_ref, nw_ref, sums_ref, lvl_ref, y_ref,
                 st_ref, k_ref, b_ref, kout_ref, qin_ref, g2_ref, dlast_ref, bmin_ref, *, c, n_lvl):
    n_heads = st_ref.shape[0]
    d = HEAD_DIM

    @pl.when(pl.program_id(1) == 0)
    def _():
        st_ref[...] = jnp.zeros_like(st_ref)

    logits = lbl_ref[...]
    ex = jnp.exp(logits - jnp.max(logits, axis=0, keepdims=True))
    lb = ex[0:1, :] / jnp.sum(ex, axis=0, keepdims=True)
    one_m_lb = 1.0 - lb
    nw = nw_ref[...]
    sums = sums_ref[...]
    lvl = lvl_ref[...]
    n_chunks = q_ref.shape[0] // c

    heads = [slice(hh * d, (hh + 1) * d) for hh in range(n_heads)]
    causal = lvl >= 0

    def chunk_rows(ci):
        return pl.ds(pl.multiple_of(ci * c, c), c)

    bmin_ref[...] = jnp.zeros_like(bmin_ref)

    def prepare(ci, carry):
        r = chunk_rows(ci)
        sig = _sigmoid(z_ref[r, :])
        g = jnp.log2(lb + one_m_lb * sig)
        k_all = one_m_lb * (1.0 - sig)
        g_hi = g.astype(BF16)
        g_lo = (g - g_hi.astype(F32)).astype(BF16)
        g2 = jnp.concatenate([g_hi, g_lo], axis=0)
        e01 = jnp.dot(sums[0:2 * c], g2, preferred_element_type=F32)
        b = e01[0:c]
        dec_b = jnp.exp2(b)
        g2_ref[ci] = g2
        b_ref[r, :] = b
        k_ref[r, :] = k_all
        kout_ref[r, :] = (k_all * jnp.exp2(e01[c:2 * c])).astype(BF16)
        qin_ref[r, :] = (q_ref[r, :].astype(F32) * dec_b).astype(BF16)
        dlast_ref[ci] = jnp.broadcast_to(dec_b[c - 1:c, :], dlast_ref.shape[1:])
        bmin_ref[...] = jnp.minimum(bmin_ref[...], b)
        return carry

    lax.fori_loop(0, n_chunks, prepare, 0, unroll=2)
    b_min = jnp.min(bmin_ref[...])

    def finish(ci, scores):
        r = chunk_rows(ci)
        o_intra, inc = [], []
        for hh, cols in enumerate(heads):
            v = v_ref[r, cols]
            o_intra.append(jnp.dot(scores[hh], v, preferred_element_type=F32))
            inc.append(_dot_tn(v, kout_ref[r, cols]))
        for hh, cols in enumerate(heads):
            st = st_ref[hh]
            o = o_intra[hh] + _dot_nt(qin_ref[r, cols], st.astype(BF16))
            st_ref[hh] = st * dlast_ref[ci, 0:1, cols] + inc[hh]
            o = o * lax.rsqrt(jnp.mean(o * o, axis=-1, keepdims=True) + RMS_EPS) * nw
            gt = g_ref[r, cols].astype(F32)
            y_ref[r, cols] = (o * gt * _sigmoid(gt)).astype(y_ref.dtype)

    def fast_chunk(ci, carry):
        r = chunk_rows(ci)
        k_up = (k_ref[r, :] * jnp.exp2(-b_ref[r, :])).astype(BF16)
        scores = [jnp.where(causal, _dot_nt(qin_ref[r, cols], k_up[:, cols]), 0.0).astype(BF16)
                  for cols in heads]
        finish(ci, scores)
        return carry

    def safe_chunk(ci, carry):
        r = chunk_rows(ci)
        dec_l = jnp.exp2(jnp.dot(sums[2 * c:], g2_ref[ci], preferred_element_type=F32))
        scores = []
        for cols in heads:
            q = q_ref[r, cols].astype(F32)
            k = k_ref[r, cols]
            sc = jnp.where(lvl == 0, _dot_nt(q.astype(BF16), k.astype(BF16)), 0.0)
            for l in range(n_lvl):
                d_l = dec_l[l * c:(l + 1) * c, cols]
                s_l = _dot_nt((q * d_l).astype(BF16), (k * d_l).astype(BF16))
                sc = jnp.where(lvl == l + 1, s_l, sc)
            scores.append(sc.astype(BF16))
        finish(ci, scores)
        return carry

    @pl.when(b_min >= -HG_FAST_LOG2)
    def _():
        lax.fori_loop(0, n_chunks, fast_chunk, 0, unroll=8)

    @pl.when(b_min < -HG_FAST_LOG2)
    def _():
        lax.fori_loop(0, n_chunks, safe_chunk, 0)


def _hgrn(proj, a_f, lb_logits, norm_w):
    ts, c, hg = HG_TS, HG_C, HG_HEADS
    sums, lvl, n_lvl = _hgrn_constants(c)
    w = hg * HEAD_DIM
    per_blk = WIDTH // w
    grid = (N_HEADS // hg, SEQ // ts)
    return pl.pallas_call(
        functools.partial(_hgrn_kernel, c=c, n_lvl=n_lvl),
        grid=grid,
        in_specs=[
            pl.BlockSpec((ts, w), lambda h, i: (i, BLK_AQ * per_blk + h)),
            pl.BlockSpec((ts, w), lambda h, i: (i, h)),
            pl.BlockSpec((ts, w), lambda h, i: (i, BLK_AI * per_blk + h)),
            pl.BlockSpec((ts, w), lambda h, i: (i, BLK_AG * per_blk + h)),
            pl.BlockSpec((lb_logits.shape[0], w), lambda h, i: (0, h)),
            pl.BlockSpec((1, HEAD_DIM), lambda h, i: (0, 0)),
            pl.BlockSpec(sums.shape, lambda h, i: (0, 0)),
            pl.BlockSpec(lvl.shape, lambda h, i: (0, 0)),
        ],
        out_specs=pl.BlockSpec((ts, w), lambda h, i: (i, h)),
        out_shape=jax.ShapeDtypeStruct((SEQ, WIDTH), BF16),
        scratch_shapes=[
            pltpu.VMEM((hg, HEAD_DIM, HEAD_DIM), F32),
            pltpu.VMEM((ts, w), F32),
            pltpu.VMEM((ts, w), F32),
            pltpu.VMEM((ts, w), BF16),
            pltpu.VMEM((ts, w), BF16),
            pltpu.VMEM((ts // c, 2 * c, w), BF16),
            pltpu.VMEM((ts // c, F32_SUBLANES, w), F32),
            pltpu.VMEM((c, w), F32),
        ],
        compiler_params=_params("arbitrary", "arbitrary"),
        name="hgrn",
    )(proj, a_f, proj, proj, lb_logits, norm_w, sums, lvl)


def _fox_kernel(q_ref, k_ref, v_ref, cx_ref, nc_ref, o_ref, vt_ref, nrm_ref, m_ref, acc_ref, *, tq, tk):
    g = pl.program_id(0)
    i = pl.program_id(1)
    n_heads = vt_ref.shape[0]
    d = HEAD_DIM
    s_len = k_ref.shape[0]
    n_sub = tq // tk
    n_kt = s_len // tk
    vrows = vt_ref.shape[2]
    heads = [(hh, slice(hh * d, (hh + 1) * d)) for hh in range(n_heads)]

    @pl.when(i == 0)
    def _():
        ones_row = jnp.where(lax.broadcasted_iota(jnp.int32, (vrows - d, tk), 0) == 0, 1.0, 0.0)
        wg = n_heads * d
        same_head = (lax.broadcasted_iota(jnp.int32, (wg, wg), 0) // d
                     == lax.broadcasted_iota(jnp.int32, (wg, wg), 1) // d)
        ones_blk = jnp.where(same_head, 1.0, 0.0).astype(BF16)

        def sq_norms(x):
            return jnp.dot(x * x, ones_blk, preferred_element_type=F32)

        kn = jnp.zeros((tk, wg), F32)
        qn = jnp.zeros((tk, wg), F32)
        for r in range(n_kt):
            rows = slice(r * tk, (r + 1) * tk)
            for hh, cols in heads:
                vt_ref[hh, r, 0:d, :] = v_ref[rows, cols].astype(F32).T.astype(BF16)
                vt_ref[hh, r, d:vrows, :] = ones_row.astype(BF16)
            kn = jnp.maximum(kn, sq_norms(k_ref[rows, :]))
            qn = jnp.maximum(qn, sq_norms(q_ref[rows, :]))
        nrm = 2.0 * FOX_NORM_MARGIN * jnp.sqrt(jnp.max(qn, axis=0, keepdims=True)
                                               * jnp.max(kn, axis=0, keepdims=True))
        for hh, cols in heads:
            nrm_ref[hh] = jnp.broadcast_to(nrm[:, cols], nrm_ref.shape[1:])

    n_qb = o_ref.shape[0] // tq
    units = [(hh, cols, qb) for hh, cols in heads for qb in range(n_qb)]
    ends = nc_ref[pl.ds(tk - 1, n_kt, stride=tk), :]
    lane = lax.broadcasted_iota(jnp.int32, ends.shape, 1)
    rsel = lax.broadcasted_iota(jnp.int32, (d, tq), 0)
    qa, r_lo, first_tile = [], [], []
    for hh, cols, qb in units:
        h = g * n_heads + hh
        q0 = pl.multiple_of((i * n_qb + qb) * tq, tq)
        thr = nc_ref[pl.ds(q0, 1), :] - nrm_ref[hh, 0:1, :] - FOX_SKIP_LOG2
        r_lo.append(jnp.sum(jnp.where((ends < thr) & (lane == h), 1, 0)))
        first_tile.append((i * n_qb + qb) * n_sub)
        sel = jnp.where((rsel < 3 * N_HEADS) & ((rsel & (N_HEADS - 1)) == h), 1.0, 0.0).astype(BF16)
        qa.append(jnp.concatenate([q_ref[pl.ds(q0, tq), cols].astype(F32).T.astype(BF16), sel], axis=0))

    def scores(u, r, mask_off):
        hh, cols, _ = units[u]
        ks = pl.ds(pl.multiple_of(r * tk, tk), tk)
        ka = jnp.concatenate([k_ref[ks, cols], cx_ref[ks, :]], axis=1)
        s = jnp.dot(ka, qa[u], preferred_element_type=F32)
        if mask_off is not None:
            kid = lax.broadcasted_iota(jnp.int32, s.shape, 0) + mask_off
            qid = lax.broadcasted_iota(jnp.int32, s.shape, 1)
            s = jnp.where(kid <= qid, s, -jnp.inf)
        return s

    def first_step(with_prev):
        tiles = []
        for u in range(len(units)):
            t = [(first_tile[u] + rr, rr * tk) for rr in range(n_sub)]
            tiles.append(([(first_tile[u] - 1, None)] if with_prev[u] else []) + t)
        ss = [[scores(u, r, off) for r, off in tiles[u]] for u in range(len(units))]
        ms = []
        for u in range(len(units)):
            m = jnp.max(ss[u][0], axis=0, keepdims=True)
            for s in ss[u][1:]:
                m = jnp.maximum(m, jnp.max(s, axis=0, keepdims=True))
            ms.append(m)
        for u, (hh, _, _) in enumerate(units):
            pv = None
            for (r, _), s in zip(tiles[u], ss[u]):
                p = jnp.exp2(s - ms[u]).astype(BF16)
                dd = jnp.dot(vt_ref[hh, r], p, preferred_element_type=F32)
                pv = dd if pv is None else pv + dd
            acc_ref[u] = pv
            m_ref[u] = ms[u]

    def later_step(u, r):
        s = scores(u, r, None)
        m_prev = m_ref[u]
        m_new = jnp.maximum(m_prev, jnp.max(s, axis=0, keepdims=True))
        pv = jnp.dot(vt_ref[units[u][0], r], jnp.exp2(s - m_new).astype(BF16), preferred_element_type=F32)
        acc_ref[u] = jnp.exp2(m_prev - m_new) * acc_ref[u] + pv
        m_ref[u] = m_new

    @pl.when(i == 0)
    def _():
        first_step([qb > 0 for _, _, qb in units])

    @pl.when(i > 0)
    def _():
        first_step([True] * len(units))

    for u in range(len(units)):
        def off_diag(r, carry, u=u):
            later_step(u, r)
            return carry
        lax.fori_loop(r_lo[u], first_tile[u] - 1, off_diag, 0)

    for u, (hh, cols, qb) in enumerate(units):
        acc = acc_ref[u]
        o_ref[qb * tq:(qb + 1) * tq, cols] = (acc[0:d] / acc[d:d + 1]).T.astype(o_ref.dtype)


def _fox(proj, cx, nc):
    tq, tk, hg, n_qb = FOX_TQ, FOX_TK, FOX_HEADS, FOX_QBLOCKS
    w = hg * HEAD_DIM
    per_blk = WIDTH // w
    vrows = HEAD_DIM + BF16_SUBLANES
    return pl.pallas_call(
        functools.partial(_fox_kernel, tq=tq, tk=tk),
        grid=(N_HEADS // hg, SEQ // (n_qb * tq)),
        in_specs=[
            pl.BlockSpec((SEQ, w), lambda g, i: (0, BLK_BQ * per_blk + g)),
            pl.BlockSpec((SEQ, w), lambda g, i: (0, BLK_BK * per_blk + g)),
            pl.BlockSpec((SEQ, w), lambda g, i: (0, BLK_BV * per_blk + g)),
            pl.BlockSpec((SEQ, LANES), lambda g, i: (0, 0)),
            pl.BlockSpec((SEQ, LANES), lambda g, i: (0, 0)),
        ],
        out_specs=pl.BlockSpec((n_qb * tq, w), lambda g, i: (i, g)),
        out_shape=jax.ShapeDtypeStruct((SEQ, WIDTH), BF16),
        scratch_shapes=[
            pltpu.VMEM((hg, SEQ // tk, vrows, tk), BF16),
            pltpu.VMEM((hg, F32_SUBLANES, LANES), F32),
            pltpu.VMEM((hg * n_qb, 1, tq), F32),
            pltpu.VMEM((hg * n_qb, vrows, tq), F32),
        ],
        compiler_params=_params("arbitrary", "arbitrary"),
        name="fox",
    )(proj, proj, proj, cx, nc)


def _merge_load_weights(srcs, dsts, stage_ref, sems):
    rows = stage_ref.shape[1]
    chunks = [(src, dst, r0) for src, dst in zip(srcs, dsts) for r0 in range(0, src.shape[0], rows)]

    def copy(n):
        src, _, r0 = chunks[n]
        return pltpu.make_async_copy(src.at[pl.ds(r0, rows), :], stage_ref.at[n % 2], sems.at[n % 2])

    copy(0).start()
    for n, (_, dst, r0) in enumerate(chunks):
        if n + 1 < len(chunks):
            copy(n + 1).start()
        copy(n).wait()
        dst[r0:r0 + rows, :] = stage_ref[n % 2].astype(BF16)


def _merge_kernel(oa_ref, ob_ref, ga_ref, gb_ref, wa_hbm, wb_hbm, wo_hbm, x_ref, nw_ref, npre_ref,
                  out_ref, h2_ref, wa_ref, wb_ref, wo_ref, stage_ref, sems):
    @pl.when(pl.program_id(0) == 0)
    def _():
        _merge_load_weights((wa_hbm, wb_hbm, wo_hbm), (wa_ref, wb_ref, wo_ref), stage_ref, sems)

    ya = jnp.dot(oa_ref[...], wa_ref[...], preferred_element_type=F32)
    yb = jnp.dot(ob_ref[...], wb_ref[...], preferred_element_type=F32)
    merged = _sigmoid(ga_ref[...].astype(F32)) * ya + _sigmoid(gb_ref[...].astype(F32)) * yb
    u = jnp.dot(merged.astype(BF16), wo_ref[...], preferred_element_type=F32)
    u = u * lax.rsqrt(jnp.mean(u * u, axis=-1, keepdims=True) + RMS_EPS) * nw_ref[...]
    x1 = x_ref[...] + u
    out_ref[...] = x1
    ms = jnp.mean(x1 * x1, axis=-1, keepdims=True)
    h2_ref[...] = (x1 * lax.rsqrt(ms + RMS_EPS) * npre_ref[...]).astype(BF16)


def _merge(y_a, y_b, proj, w_up_a, w_up_b, w_o, x2, norm_w, norm_ffn_pre):
    tm = MERGE_TM
    hbm = pl.BlockSpec(memory_space=pl.ANY)
    return pl.pallas_call(
        _merge_kernel,
        grid=(SEQ // tm,),
        in_specs=[
            pl.BlockSpec((tm, WIDTH), lambda i: (i, 0)),
            pl.BlockSpec((tm, WIDTH), lambda i: (i, 0)),
            pl.BlockSpec((tm, D_MODEL), lambda i: (i, BLK_GA // 2)),
            pl.BlockSpec((tm, D_MODEL), lambda i: (i, BLK_GB // 2)),
            hbm,
            hbm,
            hbm,
            pl.BlockSpec((tm, D_MODEL), lambda i: (i, 0)),
            pl.BlockSpec((1, D_MODEL), lambda i: (0, 0)),
            pl.BlockSpec((1, D_MODEL), lambda i: (0, 0)),
        ],
        out_specs=[
            pl.BlockSpec((tm, D_MODEL), lambda i: (i, 0)),
            pl.BlockSpec((tm, D_MODEL), lambda i: (i, 0)),
        ],
        out_shape=[
            jax.ShapeDtypeStruct((SEQ, D_MODEL), F32),
            jax.ShapeDtypeStruct((SEQ, D_MODEL), BF16),
        ],
        scratch_shapes=[
            pltpu.VMEM((WIDTH, D_MODEL), BF16),
            pltpu.VMEM((WIDTH, D_MODEL), BF16),
            pltpu.VMEM((D_MODEL, D_MODEL), BF16),
            pltpu.VMEM((2, MERGE_STAGE_ROWS, D_MODEL), F32),
            pltpu.SemaphoreType.DMA((2,)),
        ],
        compiler_params=_params("arbitrary"),
        name="merge",
    )(y_a, y_b, proj, proj, w_up_a, w_up_b, w_o, x2, norm_w, norm_ffn_pre)


def _ffn_residual_copy(x_hbm, xres_ref, sem, i):
    tm = xres_ref.shape[0]
    return pltpu.make_async_copy(x_hbm.at[pl.ds(pl.multiple_of(i * tm, tm), tm), :], xres_ref, sem)


def _ffn_kernel(h_ref, x_hbm, npost_ref, wg_ref, wu_ref, wd_ref, out_ref, xres_ref, sem):
    i = pl.program_id(0)
    f = pl.program_id(1)

    @pl.when(f == 0)
    def _():
        _ffn_residual_copy(x_hbm, xres_ref, sem, i).start()
        out_ref[...] = jnp.zeros_like(out_ref)

    h = h_ref[...]
    gate = jnp.dot(h, wg_ref[...].astype(BF16), preferred_element_type=F32)
    up = jnp.dot(h, wu_ref[...].astype(BF16), preferred_element_type=F32)
    act = (gate * _sigmoid(gate) * up).astype(BF16)
    out_ref[...] += jnp.dot(act, wd_ref[...].astype(BF16), preferred_element_type=F32)

    @pl.when(f == pl.num_programs(1) - 1)
    def _():
        _ffn_residual_copy(x_hbm, xres_ref, sem, i).wait()
        u = out_ref[...]
        u = u * lax.rsqrt(jnp.mean(u * u, axis=-1, keepdims=True) + RMS_EPS) * npost_ref[...]
        out_ref[...] = xres_ref[...] + u


def _ffn(h2, x1, norm_post, w_in, w_down):
    tm, tf = FFN_TM, FFN_TF
    nf = D_FF // tf
    return pl.pallas_call(
        _ffn_kernel,
        grid=(SEQ // tm, nf),
        in_specs=[
            pl.BlockSpec((tm, D_MODEL), lambda i, f: (i, 0)),
            pl.BlockSpec(memory_space=pl.ANY),
            pl.BlockSpec((1, D_MODEL), lambda i, f: (0, 0)),
            pl.BlockSpec((D_MODEL, tf), lambda i, f: (0, f)),
            pl.BlockSpec((D_MODEL, tf), lambda i, f: (0, f + nf)),
            pl.BlockSpec((tf, D_MODEL), lambda i, f: (f, 0)),
        ],
        out_specs=pl.BlockSpec((tm, D_MODEL), lambda i, f: (i, 0)),
        out_shape=jax.ShapeDtypeStruct((SEQ, D_MODEL), F32),
        scratch_shapes=[pltpu.VMEM((tm, D_MODEL), F32), pltpu.SemaphoreType.DMA(())],
        compiler_params=_params("arbitrary", "arbitrary"),
        name="ffn",
    )(h2, x1, norm_post, w_in, w_in, w_down)


def kernel(x, w_in, b_fox_f, hgrn_lb_logits, hgrn_norm_w, w_up_a, w_up_b, w_o, norm_mix_pre,
           norm_mix_post, norm_ffn_pre, norm_ffn_post, w_ffn_in, w_ffn_down):
    assert x.shape == (1, SEQ, D_MODEL) and w_in.shape[0] == 1
    w_t = w_in[0].T
    x2 = x[0]

    h, a_f, cx, nc = _prenorm(x2, norm_mix_pre, w_t, b_fox_f.reshape(N_HEADS, 1))
    proj = _inproj(h, w_t)
    y_a = _hgrn(proj, a_f, hgrn_lb_logits, hgrn_norm_w)
    y_b = _fox(proj, cx, nc)
    x1, h2 = _merge(y_a, y_b, proj, w_up_a[0], w_up_b[0], w_o[0], x2, norm_mix_post, norm_ffn_pre)
    out = _ffn(h2, x1, norm_ffn_post, w_ffn_in[0], w_ffn_down[0])
    return out[None]
```

```python
import functools

import numpy as np
import jax
import jax.numpy as jnp
from jax import lax
from jax.experimental import pallas as pl
from jax.experimental.pallas import tpu as pltpu

F32 = jnp.float32
BF16 = jnp.bfloat16

D_MODEL = 2048
SEQ = 8192
HEAD_DIM = 128
N_HEADS = 8
WIDTH = N_HEADS * HEAD_DIM
D_FF = 5632
RMS_EPS = 1e-6
N_PROJ = 3 * WIDTH + 2 * D_MODEL + 3 * WIDTH

VMEM_LIMIT_BYTES = 56 * 1024 * 1024
LANES = 128
F32_SUBLANES = 8
BF16_SUBLANES = 16

SRC_AQ, SRC_AF, SRC_AI, SRC_AG, SRC_BQ, SRC_BK, SRC_BV = 0, 1, 2, 3, 4, 5, 6
IN_NA = 7
BLK_AQ, BLK_AI, BLK_GA, BLK_GB, BLK_AG, BLK_BQ, BLK_BK, BLK_BV = 0, 1, 2, 4, 6, 7, 8, 9

PRE_TM = 512
IN_TM, IN_TN = 2048, 512
HG_TS, HG_C, HG_HEADS = 1024, 64, 8
HG_FAST_LOG2 = 96.0
FOX_TQ, FOX_TK, FOX_HEADS, FOX_QBLOCKS = 256, 256, 2, 4
LOG2E = 1.4426950408889634
FOX_SKIP_LOG2 = 160.0
FOX_NORM_MARGIN = 1.01
MERGE_TM, MERGE_SUB_ROWS = 512, 256
MERGE_STAGE_ROWS = 64
FFN_TM, FFN_TF = 1024, 256


def _params(*sem):
    return pltpu.CompilerParams(dimension_semantics=sem, vmem_limit_bytes=VMEM_LIMIT_BYTES)


def _dot_nt(a, b):
    return lax.dot_general(a, b, (((1,), (1,)), ((), ())), preferred_element_type=F32)


def _dot_tn(a, b):
    return lax.dot_general(a, b, (((0,), (0,)), ((), ())), preferred_element_type=F32)


def _log_sigmoid(x):
    return jnp.minimum(x, 0.0) - jnp.log(1.0 + jnp.exp(-jnp.abs(x)))


def _sigmoid(x):
    return 1.0 / (1.0 + jnp.exp(-x))


def _split3(x):
    p0 = x.astype(BF16)
    r1 = x - p0.astype(F32)
    p1 = r1.astype(BF16)
    p2 = (r1 - p1.astype(F32)).astype(BF16)
    return p0, p1, p2


def _prenorm_kernel(x_ref, nw_ref, waf_ref, wf_ref, bf_ref, tri_ref, h_ref, af_ref, cx_ref, nc_ref,
                    wafb_ref, carry_ref):
    i = pl.program_id(0)
    tm = x_ref.shape[0]

    @pl.when(i == 0)
    def _():
        wafb_ref[...] = waf_ref[...].T.astype(BF16)
        carry_ref[...] = jnp.zeros_like(carry_ref)

    x = x_ref[...]
    ms = jnp.mean(x * x, axis=-1, keepdims=True)
    h_ref[...] = (x * lax.rsqrt(ms + RMS_EPS) * nw_ref[...]).astype(BF16)
    hb = h_ref[...]
    af_ref[...] = jnp.dot(hb, wafb_ref[...], preferred_element_type=F32)
    logit = _dot_nt(wf_ref[...].astype(BF16), hb) + bf_ref[...]
    nls = _log_sigmoid(logit) * (-LOG2E)
    tri = tri_ref[...]
    loc = sum(jnp.dot(p, tri, preferred_element_type=F32) for p in _split3(nls))
    nc = loc + carry_ref[:, 0:1]
    carry_ref[...] = jnp.broadcast_to(nc[:, tm - 1:tm], carry_ref.shape)
    pad = jnp.zeros((LANES - N_HEADS, tm), F32)
    nc_ref[...] = jnp.concatenate([nc, pad], axis=0).T
    parts = [p.astype(F32) for p in _split3(nc)]
    pad3 = jnp.zeros((LANES - 3 * N_HEADS, tm), F32)
    cx_ref[...] = jnp.concatenate(parts + [pad3], axis=0).T.astype(BF16)


def _prenorm(x2, norm_w, w, bf_col):
    tm = PRE_TM
    tri = jnp.asarray(np.triu(np.ones((tm, tm), np.float32)), BF16)
    once = pl.Buffered(1)
    return pl.pallas_call(
        _prenorm_kernel,
        grid=(SEQ // tm,),
        in_specs=[
            pl.BlockSpec((tm, D_MODEL), lambda i: (i, 0)),
            pl.BlockSpec((1, D_MODEL), lambda i: (0, 0)),
            pl.BlockSpec((WIDTH, D_MODEL), lambda i: (SRC_AF, 0), pipeline_mode=once),
            pl.BlockSpec((N_HEADS, D_MODEL), lambda i: (IN_NA * WIDTH // N_HEADS, 0), pipeline_mode=once),
            pl.BlockSpec((N_HEADS, 1), lambda i: (0, 0)),
            pl.BlockSpec((tm, tm), lambda i: (0, 0), pipeline_mode=once),
        ],
        out_specs=[
            pl.BlockSpec((tm, D_MODEL), lambda i: (i, 0)),
            pl.BlockSpec((tm, WIDTH), lambda i: (i, 0)),
            pl.BlockSpec((tm, LANES), lambda i: (i, 0)),
            pl.BlockSpec((tm, LANES), lambda i: (i, 0)),
        ],
        out_shape=[
            jax.ShapeDtypeStruct((SEQ, D_MODEL), BF16),
            jax.ShapeDtypeStruct((SEQ, WIDTH), F32),
            jax.ShapeDtypeStruct((SEQ, LANES), BF16),
            jax.ShapeDtypeStruct((SEQ, LANES), F32),
        ],
        scratch_shapes=[pltpu.VMEM((D_MODEL, WIDTH), BF16), pltpu.VMEM((N_HEADS, LANES), F32)],
        compiler_params=_params("arbitrary"),
        name="prenorm",
    )(x2, norm_w, w, w, bf_col, tri)


def _inproj_kernel(h_ref, wa_ref, wb_ref, proj_ref, wc_ref):
    j = pl.program_id(1)
    i = pl.program_id(2)
    tm, tn = proj_ref.shape
    nb = WIDTH // tn
    src = _inproj_src_block(j, nb)

    @pl.when(i == 0)
    def _():
        @pl.when(src < IN_NA * nb)
        def _():
            scale = jnp.where(src // nb == SRC_BQ, LOG2E * HEAD_DIM ** -0.5, 1.0)
            wc_ref[...] = (wa_ref[...] * scale).astype(BF16)

        @pl.when(src >= IN_NA * nb)
        def _():
            wc_ref[...] = jnp.concatenate([wa_ref[N_HEADS:, :], wb_ref[:N_HEADS, :]], axis=0).astype(BF16)

    rows = pl.ds(pl.multiple_of(i * tm, tm), tm)
    proj_ref[...] = _dot_nt(h_ref[rows, :], wc_ref[...]).astype(BF16)


def _inproj_src_block(j, nb):
    return jnp.where(j < SRC_AF * nb, j, j + nb)


def _inproj_out_block(j, nb):
    src = _inproj_src_block(j, nb)
    g = src // nb
    g_out = jnp.int32(BLK_AQ)
    for g_src, g_dst in ((SRC_AI, BLK_AI), (SRC_AG, BLK_AG), (SRC_BQ, BLK_BQ), (SRC_BK, BLK_BK), (SRC_BV, BLK_BV)):
        g_out = jnp.where(g == g_src, g_dst, g_out)
    return jnp.where(g < IN_NA, g_out * nb + src % nb, src - IN_NA * nb + BLK_GA * nb)


def _inproj(h, w):
    tm, tn = IN_TM, IN_TN
    nb = WIDTH // tn
    half = SEQ // 2
    n_i = half // tm
    first_gate = IN_NA * nb
    grid = (2, N_PROJ // tn, n_i)
    return pl.pallas_call(
        _inproj_kernel,
        grid=grid,
        in_specs=[
            pl.BlockSpec((half, D_MODEL), lambda s, j, i: (s, 0), pipeline_mode=pl.Buffered(1)),
            pl.BlockSpec((tn, D_MODEL), lambda s, j, i: (_inproj_src_block(j, nb), 0)),
            pl.BlockSpec((tn, D_MODEL), lambda s, j, i: (jnp.maximum(_inproj_src_block(j, nb), first_gate) + 1, 0)),
        ],
        out_specs=pl.BlockSpec((tm, tn), lambda s, j, i: (s * n_i + i, _inproj_out_block(j, nb))),
        out_shape=jax.ShapeDtypeStruct((SEQ, N_PROJ), BF16),
        scratch_shapes=[pltpu.VMEM((tn, D_MODEL), BF16)],
        compiler_params=_params("arbitrary", "arbitrary", "arbitrary"),
        name="inproj",
    )(h, w, w)


def _hgrn_constants(c):
    n_lvl = int(np.log2(c))
    t = np.arange(c)[:, None]
    j = np.arange(c)[None, :]
    blocks = [(j <= t), (j > t)]
    level = np.full((c, c), -1, np.int32)
    level[np.arange(c), np.arange(c)] = 0
    for l in range(n_lvl):
        b = 2 << l
        mid = (t // b) * b + b // 2 - 1
        second = (t % b) >= b // 2
        m = np.where(second, (j > mid) & (j <= t), (j > t) & (j <= mid))
        blocks.append(m)
        s = np.arange(c)[None, :]
        own = (t // b == s // b) & second & ((s % b) < b // 2)
        level[own] = l + 1
    sums = np.concatenate(blocks, axis=0).astype(np.float32)
    sums2 = np.concatenate([sums, sums], axis=1)
    return jnp.asarray(sums2, BF16), jnp.asarray(level), n_lvl


def _hgrn_kernel(q_ref, z_ref, v_ref, g_ref, lbl_ref, nw_ref, sums_ref, lvl_ref, y_ref,
                 st_ref, k_ref, b_ref, kout_ref, qin_ref, g2_ref, dlast_ref, bmin_ref, *, c, n_lvl):
    n_heads = st_ref.shape[0]
    d = HEAD_DIM

    @pl.when(pl.program_id(1) == 0)
    def _():
        st_ref[...] = jnp.zeros_like(st_ref)

    logits = lbl_ref[...]
    ex = jnp.exp(logits - jnp.max(logits, axis=0, keepdims=True))
    lb = ex[0:1, :] / jnp.sum(ex, axis=0, keepdims=True)
    one_m_lb = 1.0 - lb
    nw = nw_ref[...]
    sums = sums_ref[...]
    lvl = lvl_ref[...]
    n_chunks = q_ref.shape[0] // c

    heads = [slice(hh * d, (hh + 1) * d) for hh in range(n_heads)]
    causal = lvl >= 0

    def chunk_rows(ci):
        return pl.ds(pl.multiple_of(ci * c, c), c)

    bmin_ref[...] = jnp.zeros_like(bmin_ref)

    def prepare(ci, carry):
        r = chunk_rows(ci)
        sig = _sigmoid(z_ref[r, :])
        g = jnp.log2(lb + one_m_lb * sig)
        k_all = one_m_lb * (1.0 - sig)
        g_hi = g.astype(BF16)
        g_lo = (g - g_hi.astype(F32)).astype(BF16)
        g2 = jnp.concatenate([g_hi, g_lo], axis=0)
        e01 = jnp.dot(sums[0:2 * c], g2, preferred_element_type=F32)
        b = e01[0:c]
        dec_b = jnp.exp2(b)
        g2_ref[ci] = g2
        b_ref[r, :] = b
        k_ref[r, :] = k_all
        kout_ref[r, :] = (k_all * jnp.exp2(e01[c:2 * c])).astype(BF16)
        qin_ref[r, :] = (q_ref[r, :].astype(F32) * dec_b).astype(BF16)
        dlast_ref[ci] = jnp.broadcast_to(dec_b[c - 1:c, :], dlast_ref.shape[1:])
        bmin_ref[...] = jnp.minimum(bmin_ref[...], b)
        return carry

    lax.fori_loop(0, n_chunks, prepare, 0, unroll=2)
    b_min = jnp.min(bmin_ref[...])

    def finish(ci, scores):
        r = chunk_rows(ci)
        o_intra, inc = [], []
        for hh, cols in enumerate(heads):
            v = v_ref[r, cols]
            o_intra.append(jnp.dot(scores[hh], v, preferred_element_type=F32))
            inc.append(_dot_tn(v, kout_ref[r, cols]))
        for hh, cols in enumerate(heads):
            st = st_ref[hh]
            o = o_intra[hh] + _dot_nt(qin_ref[r, cols], st.astype(BF16))
            st_ref[hh] = st * dlast_ref[ci, 0:1, cols] + inc[hh]
            o = o * lax.rsqrt(jnp.mean(o * o, axis=-1, keepdims=True) + RMS_EPS) * nw
            gt = g_ref[r, cols].astype(F32)
            y_ref[r, cols] = (o * gt * _sigmoid(gt)).astype(y_ref.dtype)

    def fast_chunk(ci, carry):
        r = chunk_rows(ci)
        k_up = (k_ref[r, :] * jnp.exp2(-b_ref[r, :])).astype(BF16)
        scores = [jnp.where(causal, _dot_nt(qin_ref[r, cols], k_up[:, cols]), 0.0).astype(BF16)
                  for cols in heads]
        finish(ci, scores)
        return carry

    def safe_chunk(ci, carry):
        r = chunk_rows(ci)
        dec_l = jnp.exp2(jnp.dot(sums[2 * c:], g2_ref[ci], preferred_element_type=F32))
        scores = []
        for cols in heads:
            q = q_ref[r, cols].astype(F32)
            k = k_ref[r, cols]
            sc = jnp.where(lvl == 0, _dot_nt(q.astype(BF16), k.astype(BF16)), 0.0)
            for l in range(n_lvl):
                d_l = dec_l[l * c:(l + 1) * c, cols]
                s_l = _dot_nt((q * d_l).astype(BF16), (k * d_l).astype(BF16))
                sc = jnp.where(lvl == l + 1, s_l, sc)
            scores.append(sc.astype(BF16))
        finish(ci, scores)
        return carry

    @pl.when(b_min >= -HG_FAST_LOG2)
    def _():
        lax.fori_loop(0, n_chunks, fast_chunk, 0, unroll=8)

    @pl.when(b_min < -HG_FAST_LOG2)
    def _():
        lax.fori_loop(0, n_chunks, safe_chunk, 0)


def _hgrn(proj, a_f, lb_logits, norm_w):
    ts, c, hg = HG_TS, HG_C, HG_HEADS
    sums, lvl, n_lvl = _hgrn_constants(c)
    w = hg * HEAD_DIM
    per_blk = WIDTH // w
    grid = (N_HEADS // hg, SEQ // ts)
    return pl.pallas_call(
        functools.partial(_hgrn_kernel, c=c, n_lvl=n_lvl),
        grid=grid,
        in_specs=[
            pl.BlockSpec((ts, w), lambda h, i: (i, BLK_AQ * per_blk + h)),
            pl.BlockSpec((ts, w), lambda h, i: (i, h)),
            pl.BlockSpec((ts, w), lambda h, i: (i, BLK_AI * per_blk + h)),
            pl.BlockSpec((ts, w), lambda h, i: (i, BLK_AG * per_blk + h)),
            pl.BlockSpec((lb_logits.shape[0], w), lambda h, i: (0, h)),
            pl.BlockSpec((1, HEAD_DIM), lambda h, i: (0, 0)),
            pl.BlockSpec(sums.shape, lambda h, i: (0, 0)),
            pl.BlockSpec(lvl.shape, lambda h, i: (0, 0)),
        ],
        out_specs=pl.BlockSpec((ts, w), lambda h, i: (i, h)),
        out_shape=jax.ShapeDtypeStruct((SEQ, WIDTH), BF16),
        scratch_shapes=[
            pltpu.VMEM((hg, HEAD_DIM, HEAD_DIM), F32),
            pltpu.VMEM((ts, w), F32),
            pltpu.VMEM((ts, w), F32),
            pltpu.VMEM((ts, w), BF16),
            pltpu.VMEM((ts, w), BF16),
            pltpu.VMEM((ts // c, 2 * c, w), BF16),
            pltpu.VMEM((ts // c, F32_SUBLANES, w), F32),
            pltpu.VMEM((c, w), F32),
        ],
        compiler_params=_params("arbitrary", "arbitrary"),
        name="hgrn",
    )(proj, a_f, proj, proj, lb_logits, norm_w, sums, lvl)


def _fox_kernel(q_ref, k_ref, v_ref, cx_ref, nc_ref, o_ref, vt_ref, nrm_ref, m_ref, acc_ref, *, tq, tk):
    g = pl.program_id(0)
    i = pl.program_id(1)
    n_heads = vt_ref.shape[0]
    d = HEAD_DIM
    s_len = k_ref.shape[0]
    n_sub = tq // tk
    n_kt = s_len // tk
    vrows = vt_ref.shape[2]
    heads = [(hh, slice(hh * d, (hh + 1) * d)) for hh in range(n_heads)]

    @pl.when(i == 0)
    def _():
        ones_row = jnp.where(lax.broadcasted_iota(jnp.int32, (vrows - d, tk), 0) == 0, 1.0, 0.0)
        wg = n_heads * d
        same_head = (lax.broadcasted_iota(jnp.int32, (wg, wg), 0) // d
                     == lax.broadcasted_iota(jnp.int32, (wg, wg), 1) // d)
        ones_blk = jnp.where(same_head, 1.0, 0.0).astype(BF16)

        def sq_norms(x):
            return jnp.dot(x * x, ones_blk, preferred_element_type=F32)

        kn = jnp.zeros((tk, wg), F32)
        qn = jnp.zeros((tk, wg), F32)
        for r in range(n_kt):
            rows = slice(r * tk, (r + 1) * tk)
            for hh, cols in heads:
                vt_ref[hh, r, 0:d, :] = v_ref[rows, cols].astype(F32).T.astype(BF16)
                vt_ref[hh, r, d:vrows, :] = ones_row.astype(BF16)
            kn = jnp.maximum(kn, sq_norms(k_ref[rows, :]))
            qn = jnp.maximum(qn, sq_norms(q_ref[rows, :]))
        nrm = 2.0 * FOX_NORM_MARGIN * jnp.sqrt(jnp.max(qn, axis=0, keepdims=True)
                                               * jnp.max(kn, axis=0, keepdims=True))
        for hh, cols in heads:
            nrm_ref[hh] = jnp.broadcast_to(nrm[:, cols], nrm_ref.shape[1:])

    n_qb = o_ref.shape[0] // tq
    units = [(hh, cols, qb) for hh, cols in heads for qb in range(n_qb)]
    ends = nc_ref[pl.ds(tk - 1, n_kt, stride=tk), :]
    lane = lax.broadcasted_iota(jnp.int32, ends.shape, 1)
    rsel = lax.broadcasted_iota(jnp.int32, (d, tq), 0)
    qa, r_lo, first_tile = [], [], []
    for hh, cols, qb in units:
        h = g * n_heads + hh
        q0 = pl.multiple_of((i * n_qb + qb) * tq, tq)
        thr = nc_ref[pl.ds(q0, 1), :] - nrm_ref[hh, 0:1, :] - FOX_SKIP_LOG2
        r_lo.append(jnp.sum(jnp.where((ends < thr) & (lane == h), 1, 0)))
        first_tile.append((i * n_qb + qb) * n_sub)
        sel = jnp.where((rsel < 3 * N_HEADS) & ((rsel & (N_HEADS - 1)) == h), 1.0, 0.0).astype(BF16)
        qa.append(jnp.concatenate([q_ref[pl.ds(q0, tq), cols].astype(F32).T.astype(BF16), sel], axis=0))

    def scores(u, r, mask_off):
        hh, cols, _ = units[u]
        ks = pl.ds(pl.multiple_of(r * tk, tk), tk)
        ka = jnp.concatenate([k_ref[ks, cols], cx_ref[ks, :]], axis=1)
        s = jnp.dot(ka, qa[u], preferred_element_type=F32)
        if mask_off is not None:
            kid = lax.broadcasted_iota(jnp.int32, s.shape, 0) + mask_off
            qid = lax.broadcasted_iota(jnp.int32, s.shape, 1)
            s = jnp.where(kid <= qid, s, -jnp.inf)
        return s

    def first_step(with_prev):
        tiles = []
        for u in range(len(units)):
            t = [(first_tile[u] + rr, rr * tk) for rr in range(n_sub)]
            tiles.append(([(first_tile[u] - 1, None)] if with_prev[u] else []) + t)
        ss = [[scores(u, r, off) for r, off in tiles[u]] for u in range(len(units))]
        ms = []
        for u in range(len(units)):
            m = jnp.max(ss[u][0], axis=0, keepdims=True)
            for s in ss[u][1:]:
                m = jnp.maximum(m, jnp.max(s, axis=0, keepdims=True))
            ms.append(m)
        for u, (hh, _, _) in enumerate(units):
            pv = None
            for (r, _), s in zip(tiles[u], ss[u]):
                p = jnp.exp2(s - ms[u]).astype(BF16)
                dd = jnp.dot(vt_ref[hh, r], p, preferred_element_type=F32)
                pv = dd if pv is None else pv + dd
            acc_ref[u] = pv
            m_ref[u] = ms[u]

    def later_step(u, r):
        s = scores(u, r, None)
        m_prev = m_ref[u]
        m_new = jnp.maximum(m_prev, jnp.max(s, axis=0, keepdims=True))
        pv = jnp.dot(vt_ref[units[u][0], r], jnp.exp2(s - m_new).astype(BF16), preferred_element_type=F32)
        acc_ref[u] = jnp.exp2(m_prev - m_new) * acc_ref[u] + pv
        m_ref[u] = m_new

    @pl.when(i == 0)
    def _():
        first_step([qb > 0 for _, _, qb in units])

    @pl.when(i > 0)
    def _():
        first_step([True] * len(units))

    for u in range(len(units)):
        def off_diag(r, carry, u=u):
            later_step(u, r)
            return carry
        lax.fori_loop(r_lo[u], first_tile[u] - 1, off_diag, 0)

    for u, (hh, cols, qb) in enumerate(units):
        acc = acc_ref[u]
        o_ref[qb * tq:(qb + 1) * tq, cols] = (acc[0:d] / acc[d:d + 1]).T.astype(o_ref.dtype)


def _fox(proj, cx, nc):
    tq, tk, hg, n_qb = FOX_TQ, FOX_TK, FOX_HEADS, FOX_QBLOCKS
    w = hg * HEAD_DIM
    per_blk = WIDTH // w
    vrows = HEAD_DIM + BF16_SUBLANES
    return pl.pallas_call(
        functools.partial(_fox_kernel, tq=tq, tk=tk),
        grid=(N_HEADS // hg, SEQ // (n_qb * tq)),
        in_specs=[
            pl.BlockSpec((SEQ, w), lambda g, i: (0, BLK_BQ * per_blk + g)),
            pl.BlockSpec((SEQ, w), lambda g, i: (0, BLK_BK * per_blk + g)),
            pl.BlockSpec((SEQ, w), lambda g, i: (0, BLK_BV * per_blk + g)),
            pl.BlockSpec((SEQ, LANES), lambda g, i: (0, 0)),
            pl.BlockSpec((SEQ, LANES), lambda g, i: (0, 0)),
        ],
        out_specs=pl.BlockSpec((n_qb * tq, w), lambda g, i: (i, g)),
        out_shape=jax.ShapeDtypeStruct((SEQ, WIDTH), BF16),
        scratch_shapes=[
            pltpu.VMEM((hg, SEQ // tk, vrows, tk), BF16),
            pltpu.VMEM((hg, F32_SUBLANES, LANES), F32),
            pltpu.VMEM((hg * n_qb, 1, tq), F32),
            pltpu.VMEM((hg * n_qb, vrows, tq), F32),
        ],
        compiler_params=_params("arbitrary", "arbitrary"),
        name="fox",
    )(proj, proj, proj, cx, nc)


def _merge_load_weights(srcs, dsts, stage_ref, sems):
    rows = stage_ref.shape[1]
    chunks = [(src, dst, r0) for src, dst in zip(srcs, dsts) for r0 in range(0, src.shape[0], rows)]

    def copy(n):
        src, _, r0 = chunks[n]
        return pltpu.make_async_copy(src.at[pl.ds(r0, rows), :], stage_ref.at[n % 2], sems.at[n % 2])

    copy(0).start()
    for n, (_, dst, r0) in enumerate(chunks):
        if n + 1 < len(chunks):
            copy(n + 1).start()
        copy(n).wait()
        dst[r0:r0 + rows, :] = stage_ref[n % 2].astype(BF16)


def _merge_kernel(oa_ref, ob_ref, ga_ref, gb_ref, wa_hbm, wb_hbm, wo_hbm, x_hbm, nw_ref, npre_ref,
                  out_ref, h2_ref, wa_ref, wb_ref, wo_ref, stage_ref, sems, xres_ref, sem_x):
    i = pl.program_id(0)

    @pl.when(i == 0)
    def _():
        _merge_load_weights((wa_hbm, wb_hbm, wo_hbm), (wa_ref, wb_ref, wo_ref), stage_ref, sems)

    _ffn_residual_copy(x_hbm, xres_ref, sem_x, i).start()
    sub = MERGE_SUB_ROWS
    tiles = [slice(r0, r0 + sub) for r0 in range(0, out_ref.shape[0], sub)]
    merged = []
    for r in tiles:
        ya = jnp.dot(oa_ref[r, :], wa_ref[...], preferred_element_type=F32)
        yb = jnp.dot(ob_ref[r, :], wb_ref[...], preferred_element_type=F32)
        merged.append((_sigmoid(ga_ref[r, :].astype(F32)) * ya
                       + _sigmoid(gb_ref[r, :].astype(F32)) * yb).astype(BF16))
    _ffn_residual_copy(x_hbm, xres_ref, sem_x, i).wait()
    us = [jnp.dot(m, wo_ref[...], preferred_element_type=F32) for m in merged]
    for r, u in zip(tiles, us):
        u = u * lax.rsqrt(jnp.mean(u * u, axis=-1, keepdims=True) + RMS_EPS) * nw_ref[...]
        x1 = xres_ref[r, :] + u
        out_ref[r, :] = x1
        ms = jnp.mean(x1 * x1, axis=-1, keepdims=True)
        h2_ref[r, :] = (x1 * lax.rsqrt(ms + RMS_EPS) * npre_ref[...]).astype(BF16)


def _merge(y_a, y_b, proj, w_up_a, w_up_b, w_o, x2, norm_w, norm_ffn_pre):
    tm = MERGE_TM
    hbm = pl.BlockSpec(memory_space=pl.ANY)
    return pl.pallas_call(
        _merge_kernel,
        grid=(SEQ // tm,),
        in_specs=[
            pl.BlockSpec((tm, WIDTH), lambda i: (i, 0)),
            pl.BlockSpec((tm, WIDTH), lambda i: (i, 0)),
            pl.BlockSpec((tm, D_MODEL), lambda i: (i, BLK_GA // 2)),
            pl.BlockSpec((tm, D_MODEL), lambda i: (i, BLK_GB // 2)),
            hbm,
            hbm,
            hbm,
            hbm,
            pl.BlockSpec((1, D_MODEL), lambda i: (0, 0)),
            pl.BlockSpec((1, D_MODEL), lambda i: (0, 0)),
        ],
        out_specs=[
            pl.BlockSpec((tm, D_MODEL), lambda i: (i, 0)),
            pl.BlockSpec((tm, D_MODEL), lambda i: (i, 0)),
        ],
        out_shape=[
            jax.ShapeDtypeStruct((SEQ, D_MODEL), F32),
            jax.ShapeDtypeStruct((SEQ, D_MODEL), BF16),
        ],
        scratch_shapes=[
            pltpu.VMEM((WIDTH, D_MODEL), BF16),
            pltpu.VMEM((WIDTH, D_MODEL), BF16),
            pltpu.VMEM((D_MODEL, D_MODEL), BF16),
            pltpu.VMEM((2, MERGE_STAGE_ROWS, D_MODEL), F32),
            pltpu.SemaphoreType.DMA((2,)),
            pltpu.VMEM((tm, D_MODEL), F32),
            pltpu.SemaphoreType.DMA(()),
        ],
        compiler_params=_params("arbitrary"),
        name="merge",
    )(y_a, y_b, proj, proj, w_up_a, w_up_b, w_o, x2, norm_w, norm_ffn_pre)


def _ffn_residual_copy(x_hbm, xres_ref, sem, i):
    tm = xres_ref.shape[0]
    return pltpu.make_async_copy(x_hbm.at[pl.ds(pl.multiple_of(i * tm, tm), tm), :], xres_ref, sem)


def _ffn_kernel(h_ref, x_hbm, npost_ref, wg_ref, wu_ref, wd_ref, out_ref, xres_ref, sem):
    i = pl.program_id(0)
    f = pl.program_id(1)

    @pl.when(f == 0)
    def _():
        _ffn_residual_copy(x_hbm, xres_ref, sem, i).start()
        out_ref[...] = jnp.zeros_like(out_ref)

    h = h_ref[...]
    gate = jnp.dot(h, wg_ref[...].astype(BF16), preferred_element_type=F32)
    up = jnp.dot(h, wu_ref[...].astype(BF16), preferred_element_type=F32)
    act = (gate * _sigmoid(gate) * up).astype(BF16)
    out_ref[...] += jnp.dot(act, wd_ref[...].astype(BF16), preferred_element_type=F32)

    @pl.when(f == pl.num_programs(1) - 1)
    def _():
        _ffn_residual_copy(x_hbm, xres_ref, sem, i).wait()
        u = out_ref[...]
        u = u * lax.rsqrt(jnp.mean(u * u, axis=-1, keepdims=True) + RMS_EPS) * npost_ref[...]
        out_ref[...] = xres_ref[...] + u


def _ffn(h2, x1, norm_post, w_in, w_down):
    tm, tf = FFN_TM, FFN_TF
    nf = D_FF // tf
    return pl.pallas_call(
        _ffn_kernel,
        grid=(SEQ // tm, nf),
        in_specs=[
            pl.BlockSpec((tm, D_MODEL), lambda i, f: (i, 0)),
            pl.BlockSpec(memory_space=pl.ANY),
            pl.BlockSpec((1, D_MODEL), lambda i, f: (0, 0)),
            pl.BlockSpec((D_MODEL, tf), lambda i, f: (0, f)),
            pl.BlockSpec((D_MODEL, tf), lambda i, f: (0, f + nf)),
            pl.BlockSpec((tf, D_MODEL), lambda i, f: (f, 0)),
        ],
        out_specs=pl.BlockSpec((tm, D_MODEL), lambda i, f: (i, 0)),
        out_shape=jax.ShapeDtypeStruct((SEQ, D_MODEL), F32),
        scratch_shapes=[pltpu.VMEM((tm, D_MODEL), F32), pltpu.SemaphoreType.DMA(())],
        compiler_params=_params("arbitrary", "arbitrary"),
        name="ffn",
    )(h2, x1, norm_post, w_in, w_in, w_down)


def kernel(x, w_in, b_fox_f, hgrn_lb_logits, hgrn_norm_w, w_up_a, w_up_b, w_o, norm_mix_pre,
           norm_mix_post, norm_ffn_pre, norm_ffn_post, w_ffn_in, w_ffn_down):
    assert x.shape == (1, SEQ, D_MODEL) and w_in.shape[0] == 1
    w_t = w_in[0].T
    x2 = x[0]

    h, a_f, cx, nc = _prenorm(x2, norm_mix_pre, w_t, b_fox_f.reshape(N_HEADS, 1))
    proj = _inproj(h, w_t)
    y_a = _hgrn(proj, a_f, hgrn_lb_logits, hgrn_norm_w)
    y_b = _fox(proj, cx, nc)
    x1, h2 = _merge(y_a, y_b, proj, w_up_a[0], w_up_b[0], w_o[0], x2, norm_mix_post, norm_ffn_pre)
    out = _ffn(h2, x1, norm_ffn_post, w_ffn_in[0], w_ffn_down[0])
    return out[None]
```

```python
import functools

import numpy as np
import jax
import jax.numpy as jnp
from jax import lax
from jax.experimental import pallas as pl
from jax.experimental.pallas import tpu as pltpu

F32 = jnp.float32
BF16 = jnp.bfloat16

D_MODEL = 2048
SEQ = 8192
HEAD_DIM = 128
N_HEADS = 8
WIDTH = N_HEADS * HEAD_DIM
D_FF = 5632
RMS_EPS = 1e-6
N_PROJ = 3 * WIDTH + 2 * D_MODEL + 3 * WIDTH

VMEM_LIMIT_BYTES = 56 * 1024 * 1024
LANES = 128
F32_SUBLANES = 8
BF16_SUBLANES = 16

SRC_AQ, SRC_AF, SRC_AI, SRC_AG, SRC_BQ, SRC_BK, SRC_BV = 0, 1, 2, 3, 4, 5, 6
IN_NA = 7
BLK_AQ, BLK_AI, BLK_GA, BLK_GB, BLK_AG, BLK_BQ, BLK_BK, BLK_BV = 0, 1, 2, 4, 6, 7, 8, 9

PRE_TM = 512
IN_TM, IN_TN = 2048, 512
HG_TS, HG_C, HG_HEADS = 1024, 64, 8
HG_FAST_LOG2 = 96.0
FOX_TQ, FOX_TK, FOX_HEADS, FOX_QBLOCKS = 256, 256, 2, 4
LOG2E = 1.4426950408889634
FOX_SKIP_LOG2 = 160.0
FOX_NORM_MARGIN = 1.01
MERGE_TM, MERGE_SUB_ROWS = 512, 256
MERGE_VMEM_LIMIT_BYTES = 61 * 1024 * 1024
MERGE_STAGE_ROWS = 64
FFN_TM, FFN_TF = 1024, 256


def _params(*sem):
    return pltpu.CompilerParams(dimension_semantics=sem, vmem_limit_bytes=VMEM_LIMIT_BYTES)


def _dot_nt(a, b):
    return lax.dot_general(a, b, (((1,), (1,)), ((), ())), preferred_element_type=F32)


def _dot_tn(a, b):
    return lax.dot_general(a, b, (((0,), (0,)), ((), ())), preferred_element_type=F32)


def _log_sigmoid(x):
    return jnp.minimum(x, 0.0) - jnp.log(1.0 + jnp.exp(-jnp.abs(x)))


def _sigmoid(x):
    return 1.0 / (1.0 + jnp.exp(-x))


def _split3(x):
    p0 = x.astype(BF16)
    r1 = x - p0.astype(F32)
    p1 = r1.astype(BF16)
    p2 = (r1 - p1.astype(F32)).astype(BF16)
    return p0, p1, p2


def _prenorm_kernel(x_ref, nw_ref, waf_ref, wf_ref, bf_ref, tri_ref, h_ref, af_ref, cx_ref, nc_ref,
                    wafb_ref, carry_ref):
    i = pl.program_id(0)
    tm = x_ref.shape[0]

    @pl.when(i == 0)
    def _():
        wafb_ref[...] = waf_ref[...].T.astype(BF16)
        carry_ref[...] = jnp.zeros_like(carry_ref)

    x = x_ref[...]
    ms = jnp.mean(x * x, axis=-1, keepdims=True)
    h_ref[...] = (x * lax.rsqrt(ms + RMS_EPS) * nw_ref[...]).astype(BF16)
    hb = h_ref[...]
    af_ref[...] = jnp.dot(hb, wafb_ref[...], preferred_element_type=F32)
    logit = _dot_nt(wf_ref[...].astype(BF16), hb) + bf_ref[...]
    nls = _log_sigmoid(logit) * (-LOG2E)
    tri = tri_ref[...]
    loc = sum(jnp.dot(p, tri, preferred_element_type=F32) for p in _split3(nls))
    nc = loc + carry_ref[:, 0:1]
    carry_ref[...] = jnp.broadcast_to(nc[:, tm - 1:tm], carry_ref.shape)
    pad = jnp.zeros((LANES - N_HEADS, tm), F32)
    nc_ref[...] = jnp.concatenate([nc, pad], axis=0).T
    parts = [p.astype(F32) for p in _split3(nc)]
    pad3 = jnp.zeros((LANES - 3 * N_HEADS, tm), F32)
    cx_ref[...] = jnp.concatenate(parts + [pad3], axis=0).T.astype(BF16)


def _prenorm(x2, norm_w, w, bf_col):
    tm = PRE_TM
    tri = jnp.asarray(np.triu(np.ones((tm, tm), np.float32)), BF16)
    once = pl.Buffered(1)
    return pl.pallas_call(
        _prenorm_kernel,
        grid=(SEQ // tm,),
        in_specs=[
            pl.BlockSpec((tm, D_MODEL), lambda i: (i, 0)),
            pl.BlockSpec((1, D_MODEL), lambda i: (0, 0)),
            pl.BlockSpec((WIDTH, D_MODEL), lambda i: (SRC_AF, 0), pipeline_mode=once),
            pl.BlockSpec((N_HEADS, D_MODEL), lambda i: (IN_NA * WIDTH // N_HEADS, 0), pipeline_mode=once),
            pl.BlockSpec((N_HEADS, 1), lambda i: (0, 0)),
            pl.BlockSpec((tm, tm), lambda i: (0, 0), pipeline_mode=once),
        ],
        out_specs=[
            pl.BlockSpec((tm, D_MODEL), lambda i: (i, 0)),
            pl.BlockSpec((tm, WIDTH), lambda i: (i, 0)),
            pl.BlockSpec((tm, LANES), lambda i: (i, 0)),
            pl.BlockSpec((tm, LANES), lambda i: (i, 0)),
        ],
        out_shape=[
            jax.ShapeDtypeStruct((SEQ, D_MODEL), BF16),
            jax.ShapeDtypeStruct((SEQ, WIDTH), F32),
            jax.ShapeDtypeStruct((SEQ, LANES), BF16),
            jax.ShapeDtypeStruct((SEQ, LANES), F32),
        ],
        scratch_shapes=[pltpu.VMEM((D_MODEL, WIDTH), BF16), pltpu.VMEM((N_HEADS, LANES), F32)],
        compiler_params=_params("arbitrary"),
        name="prenorm",
    )(x2, norm_w, w, w, bf_col, tri)


def _inproj_kernel(h_ref, wa_ref, wb_ref, proj_ref, wc_ref):
    j = pl.program_id(1)
    i = pl.program_id(2)
    tm, tn = proj_ref.shape
    nb = WIDTH // tn
    src = _inproj_src_block(j, nb)

    @pl.when(i == 0)
    def _():
        @pl.when(src < IN_NA * nb)
        def _():
            scale = jnp.where(src // nb == SRC_BQ, LOG2E * HEAD_DIM ** -0.5, 1.0)
            wc_ref[...] = (wa_ref[...] * scale).astype(BF16)

        @pl.when(src >= IN_NA * nb)
        def _():
            wc_ref[...] = jnp.concatenate([wa_ref[N_HEADS:, :], wb_ref[:N_HEADS, :]], axis=0).astype(BF16)

    rows = pl.ds(pl.multiple_of(i * tm, tm), tm)
    proj_ref[...] = _dot_nt(h_ref[rows, :], wc_ref[...]).astype(BF16)


def _inproj_src_block(j, nb):
    return jnp.where(j < SRC_AF * nb, j, j + nb)


def _inproj_out_block(j, nb):
    src = _inproj_src_block(j, nb)
    g = src // nb
    g_out = jnp.int32(BLK_AQ)
    for g_src, g_dst in ((SRC_AI, BLK_AI), (SRC_AG, BLK_AG), (SRC_BQ, BLK_BQ), (SRC_BK, BLK_BK), (SRC_BV, BLK_BV)):
        g_out = jnp.where(g == g_src, g_dst, g_out)
    return jnp.where(g < IN_NA, g_out * nb + src % nb, src - IN_NA * nb + BLK_GA * nb)


def _inproj(h, w):
    tm, tn = IN_TM, IN_TN
    nb = WIDTH // tn
    half = SEQ // 2
    n_i = half // tm
    first_gate = IN_NA * nb
    grid = (2, N_PROJ // tn, n_i)
    return pl.pallas_call(
        _inproj_kernel,
        grid=grid,
        in_specs=[
            pl.BlockSpec((half, D_MODEL), lambda s, j, i: (s, 0), pipeline_mode=pl.Buffered(1)),
            pl.BlockSpec((tn, D_MODEL), lambda s, j, i: (_inproj_src_block(j, nb), 0)),
            pl.BlockSpec((tn, D_MODEL), lambda s, j, i: (jnp.maximum(_inproj_src_block(j, nb), first_gate) + 1, 0)),
        ],
        out_specs=pl.BlockSpec((tm, tn), lambda s, j, i: (s * n_i + i, _inproj_out_block(j, nb))),
        out_shape=jax.ShapeDtypeStruct((SEQ, N_PROJ), BF16),
        scratch_shapes=[pltpu.VMEM((tn, D_MODEL), BF16)],
        compiler_params=_params("arbitrary", "arbitrary", "arbitrary"),
        name="inproj",
    )(h, w, w)


def _hgrn_constants(c):
    n_lvl = int(np.log2(c))
    t = np.arange(c)[:, None]
    j = np.arange(c)[None, :]
    blocks = [(j <= t), (j > t)]
    level = np.full((c, c), -1, np.int32)
    level[np.arange(c), np.arange(c)] = 0
    for l in range(n_lvl):
        b = 2 << l
        mid = (t // b) * b + b // 2 - 1
        second = (t % b) >= b // 2
        m = np.where(second, (j > mid) & (j <= t), (j > t) & (j <= mid))
        blocks.append(m)
        s = np.arange(c)[None, :]
        own = (t // b == s // b) & second & ((s % b) < b // 2)
        level[own] = l + 1
    sums = np.concatenate(blocks, axis=0).astype(np.float32)
    sums2 = np.concatenate([sums, sums], axis=1)
    return jnp.asarray(sums2, BF16), jnp.asarray(level), n_lvl


def _hgrn_kernel(q_ref, z_ref, v_ref, g_ref, lbl_ref, nw_ref, sums_ref, lvl_ref, y_ref,
                 st_ref, k_ref, b_ref, kout_ref, qin_ref, g2_ref, dlast_ref, bmin_ref, *, c, n_lvl):
    n_heads = st_ref.shape[0]
    d = HEAD_DIM

    @pl.when(pl.program_id(1) == 0)
    def _():
        st_ref[...] = jnp.zeros_like(st_ref)

    logits = lbl_ref[...]
    ex = jnp.exp(logits - jnp.max(logits, axis=0, keepdims=True))
    lb = ex[0:1, :] / jnp.sum(ex, axis=0, keepdims=True)
    one_m_lb = 1.0 - lb
    nw = nw_ref[...]
    sums = sums_ref[...]
    lvl = lvl_ref[...]
    n_chunks = q_ref.shape[0] // c

    heads = [slice(hh * d, (hh + 1) * d) for hh in range(n_heads)]
    causal = lvl >= 0

    def chunk_rows(ci):
        return pl.ds(pl.multiple_of(ci * c, c), c)

    bmin_ref[...] = jnp.zeros_like(bmin_ref)

    def prepare(ci, carry):
        r = chunk_rows(ci)
        sig = _sigmoid(z_ref[r, :])
        g = jnp.log2(lb + one_m_lb * sig)
        k_all = one_m_lb * (1.0 - sig)
        g_hi = g.astype(BF16)
        g_lo = (g - g_hi.astype(F32)).astype(BF16)
        g2 = jnp.concatenate([g_hi, g_lo], axis=0)
        e01 = jnp.dot(sums[0:2 * c], g2, preferred_element_type=F32)
        b = e01[0:c]
        dec_b = jnp.exp2(b)
        g2_ref[ci] = g2
        b_ref[r, :] = b
        k_ref[r, :] = k_all
        kout_ref[r, :] = (k_all * jnp.exp2(e01[c:2 * c])).astype(BF16)
        qin_ref[r, :] = (q_ref[r, :].astype(F32) * dec_b).astype(BF16)
        dlast_ref[ci] = jnp.broadcast_to(dec_b[c - 1:c, :], dlast_ref.shape[1:])
        bmin_ref[...] = jnp.minimum(bmin_ref[...], b)
        return carry

    lax.fori_loop(0, n_chunks, prepare, 0, unroll=2)
    b_min = jnp.min(bmin_ref[...])

    def finish(ci, scores):
        r = chunk_rows(ci)
        o_intra, inc = [], []
        for hh, cols in enumerate(heads):
            v = v_ref[r, cols]
            o_intra.append(jnp.dot(scores[hh], v, preferred_element_type=F32))
            inc.append(_dot_tn(v, kout_ref[r, cols]))
        for hh, cols in enumerate(heads):
            st = st_ref[hh]
            o = o_intra[hh] + _dot_nt(qin_ref[r, cols], st.astype(BF16))
            st_ref[hh] = st * dlast_ref[ci, 0:1, cols] + inc[hh]
            o = o * lax.rsqrt(jnp.mean(o * o, axis=-1, keepdims=True) + RMS_EPS) * nw
            gt = g_ref[r, cols].astype(F32)
            y_ref[r, cols] = (o * gt * _sigmoid(gt)).astype(y_ref.dtype)

    def fast_chunk(ci, carry):
        r = chunk_rows(ci)
        k_up = (k_ref[r, :] * jnp.exp2(-b_ref[r, :])).astype(BF16)
        scores = [jnp.where(causal, _dot_nt(qin_ref[r, cols], k_up[:, cols]), 0.0).astype(BF16)
                  for cols in heads]
        finish(ci, scores)
        return carry

    def safe_chunk(ci, carry):
        r = chunk_rows(ci)
        dec_l = jnp.exp2(jnp.dot(sums[2 * c:], g2_ref[ci], preferred_element_type=F32))
        scores = []
        for cols in heads:
            q = q_ref[r, cols].astype(F32)
            k = k_ref[r, cols]
            sc = jnp.where(lvl == 0, _dot_nt(q.astype(BF16), k.astype(BF16)), 0.0)
            for l in range(n_lvl):
                d_l = dec_l[l * c:(l + 1) * c, cols]
                s_l = _dot_nt((q * d_l).astype(BF16), (k * d_l).astype(BF16))
                sc = jnp.where(lvl == l + 1, s_l, sc)
            scores.append(sc.astype(BF16))
        finish(ci, scores)
        return carry

    @pl.when(b_min >= -HG_FAST_LOG2)
    def _():
        lax.fori_loop(0, n_chunks, fast_chunk, 0, unroll=8)

    @pl.when(b_min < -HG_FAST_LOG2)
    def _():
        lax.fori_loop(0, n_chunks, safe_chunk, 0)


def _hgrn(proj, a_f, lb_logits, norm_w):
    ts, c, hg = HG_TS, HG_C, HG_HEADS
    sums, lvl, n_lvl = _hgrn_constants(c)
    w = hg * HEAD_DIM
    per_blk = WIDTH // w
    grid = (N_HEADS // hg, SEQ // ts)
    return pl.pallas_call(
        functools.partial(_hgrn_kernel, c=c, n_lvl=n_lvl),
        grid=grid,
        in_specs=[
            pl.BlockSpec((ts, w), lambda h, i: (i, BLK_AQ * per_blk + h)),
            pl.BlockSpec((ts, w), lambda h, i: (i, h)),
            pl.BlockSpec((ts, w), lambda h, i: (i, BLK_AI * per_blk + h)),
            pl.BlockSpec((ts, w), lambda h, i: (i, BLK_AG * per_blk + h)),
            pl.BlockSpec((lb_logits.shape[0], w), lambda h, i: (0, h)),
            pl.BlockSpec((1, HEAD_DIM), lambda h, i: (0, 0)),
            pl.BlockSpec(sums.shape, lambda h, i: (0, 0)),
            pl.BlockSpec(lvl.shape, lambda h, i: (0, 0)),
        ],
        out_specs=pl.BlockSpec((ts, w), lambda h, i: (i, h)),
        out_shape=jax.ShapeDtypeStruct((SEQ, WIDTH), BF16),
        scratch_shapes=[
            pltpu.VMEM((hg, HEAD_DIM, HEAD_DIM), F32),
            pltpu.VMEM((ts, w), F32),
            pltpu.VMEM((ts, w), F32),
            pltpu.VMEM((ts, w), BF16),
            pltpu.VMEM((ts, w), BF16),
            pltpu.VMEM((ts // c, 2 * c, w), BF16),
            pltpu.VMEM((ts // c, F32_SUBLANES, w), F32),
            pltpu.VMEM((c, w), F32),
        ],
        compiler_params=_params("arbitrary", "arbitrary"),
        name="hgrn",
    )(proj, a_f, proj, proj, lb_logits, norm_w, sums, lvl)


def _fox_kernel(q_ref, k_ref, v_ref, cx_ref, nc_ref, o_ref, vt_ref, nrm_ref, m_ref, acc_ref, *, tq, tk):
    g = pl.program_id(0)
    i = pl.program_id(1)
    n_heads = vt_ref.shape[0]
    d = HEAD_DIM
    s_len = k_ref.shape[0]
    n_sub = tq // tk
    n_kt = s_len // tk
    vrows = vt_ref.shape[2]
    heads = [(hh, slice(hh * d, (hh + 1) * d)) for hh in range(n_heads)]

    @pl.when(i == 0)
    def _():
        ones_row = jnp.where(lax.broadcasted_iota(jnp.int32, (vrows - d, tk), 0) == 0, 1.0, 0.0)
        wg = n_heads * d
        same_head = (lax.broadcasted_iota(jnp.int32, (wg, wg), 0) // d
                     == lax.broadcasted_iota(jnp.int32, (wg, wg), 1) // d)
        ones_blk = jnp.where(same_head, 1.0, 0.0).astype(BF16)

        def sq_norms(x):
            return jnp.dot(x * x, ones_blk, preferred_element_type=F32)

        kn = jnp.zeros((tk, wg), F32)
        qn = jnp.zeros((tk, wg), F32)
        for r in range(n_kt):
            rows = slice(r * tk, (r + 1) * tk)
            for hh, cols in heads:
                vt_ref[hh, r, 0:d, :] = v_ref[rows, cols].astype(F32).T.astype(BF16)
                vt_ref[hh, r, d:vrows, :] = ones_row.astype(BF16)
            kn = jnp.maximum(kn, sq_norms(k_ref[rows, :]))
            qn = jnp.maximum(qn, sq_norms(q_ref[rows, :]))
        nrm = 2.0 * FOX_NORM_MARGIN * jnp.sqrt(jnp.max(qn, axis=0, keepdims=True)
                                               * jnp.max(kn, axis=0, keepdims=True))
        for hh, cols in heads:
            nrm_ref[hh] = jnp.broadcast_to(nrm[:, cols], nrm_ref.shape[1:])

    n_qb = o_ref.shape[0] // tq
    units = [(hh, cols, qb) for hh, cols in heads for qb in range(n_qb)]
    ends = nc_ref[pl.ds(tk - 1, n_kt, stride=tk), :]
    lane = lax.broadcasted_iota(jnp.int32, ends.shape, 1)
    rsel = lax.broadcasted_iota(jnp.int32, (d, tq), 0)
    qa, r_lo, first_tile = [], [], []
    for hh, cols, qb in units:
        h = g * n_heads + hh
        q0 = pl.multiple_of((i * n_qb + qb) * tq, tq)
        thr = nc_ref[pl.ds(q0, 1), :] - nrm_ref[hh, 0:1, :] - FOX_SKIP_LOG2
        r_lo.append(jnp.sum(jnp.where((ends < thr) & (lane == h), 1, 0)))
        first_tile.append((i * n_qb + qb) * n_sub)
        sel = jnp.where((rsel < 3 * N_HEADS) & ((rsel & (N_HEADS - 1)) == h), 1.0, 0.0).astype(BF16)
        qa.append(jnp.concatenate([q_ref[pl.ds(q0, tq), cols].astype(F32).T.astype(BF16), sel], axis=0))

    def scores(u, r, mask_off):
        hh, cols, _ = units[u]
        ks = pl.ds(pl.multiple_of(r * tk, tk), tk)
        ka = jnp.concatenate([k_ref[ks, cols], cx_ref[ks, :]], axis=1)
        s = jnp.dot(ka, qa[u], preferred_element_type=F32)
        if mask_off is not None:
            kid = lax.broadcasted_iota(jnp.int32, s.shape, 0) + mask_off
            qid = lax.broadcasted_iota(jnp.int32, s.shape, 1)
            s = jnp.where(kid <= qid, s, -jnp.inf)
        return s

    def first_step(with_prev):
        tiles = []
        for u in range(len(units)):
            t = [(first_tile[u] + rr, rr * tk) for rr in range(n_sub)]
            tiles.append(([(first_tile[u] - 1, None)] if with_prev[u] else []) + t)
        ss = [[scores(u, r, off) for r, off in tiles[u]] for u in range(len(units))]
        ms = []
        for u in range(len(units)):
            m = jnp.max(ss[u][0], axis=0, keepdims=True)
            for s in ss[u][1:]:
                m = jnp.maximum(m, jnp.max(s, axis=0, keepdims=True))
            ms.append(m)
        for u, (hh, _, _) in enumerate(units):
            pv = None
            for (r, _), s in zip(tiles[u], ss[u]):
                p = jnp.exp2(s - ms[u]).astype(BF16)
                dd = jnp.dot(vt_ref[hh, r], p, preferred_element_type=F32)
                pv = dd if pv is None else pv + dd
            acc_ref[u] = pv
            m_ref[u] = ms[u]

    def later_step(u, r):
        s = scores(u, r, None)
        m_prev = m_ref[u]
        m_new = jnp.maximum(m_prev, jnp.max(s, axis=0, keepdims=True))
        pv = jnp.dot(vt_ref[units[u][0], r], jnp.exp2(s - m_new).astype(BF16), preferred_element_type=F32)
        acc_ref[u] = jnp.exp2(m_prev - m_new) * acc_ref[u] + pv
        m_ref[u] = m_new

    @pl.when(i == 0)
    def _():
        first_step([qb > 0 for _, _, qb in units])

    @pl.when(i > 0)
    def _():
        first_step([True] * len(units))

    for u in range(len(units)):
        def off_diag(r, carry, u=u):
            later_step(u, r)
            return carry
        lax.fori_loop(r_lo[u], first_tile[u] - 1, off_diag, 0)

    for u, (hh, cols, qb) in enumerate(units):
        acc = acc_ref[u]
        o_ref[qb * tq:(qb + 1) * tq, cols] = (acc[0:d] / acc[d:d + 1]).T.astype(o_ref.dtype)


def _fox(proj, cx, nc):
    tq, tk, hg, n_qb = FOX_TQ, FOX_TK, FOX_HEADS, FOX_QBLOCKS
    w = hg * HEAD_DIM
    per_blk = WIDTH // w
    vrows = HEAD_DIM + BF16_SUBLANES
    return pl.pallas_call(
        functools.partial(_fox_kernel, tq=tq, tk=tk),
        grid=(N_HEADS // hg, SEQ // (n_qb * tq)),
        in_specs=[
            pl.BlockSpec((SEQ, w), lambda g, i: (0, BLK_BQ * per_blk + g)),
            pl.BlockSpec((SEQ, w), lambda g, i: (0, BLK_BK * per_blk + g)),
            pl.BlockSpec((SEQ, w), lambda g, i: (0, BLK_BV * per_blk + g)),
            pl.BlockSpec((SEQ, LANES), lambda g, i: (0, 0)),
            pl.BlockSpec((SEQ, LANES), lambda g, i: (0, 0)),
        ],
        out_specs=pl.BlockSpec((n_qb * tq, w), lambda g, i: (i, g)),
        out_shape=jax.ShapeDtypeStruct((SEQ, WIDTH), BF16),
        scratch_shapes=[
            pltpu.VMEM((hg, SEQ // tk, vrows, tk), BF16),
            pltpu.VMEM((hg, F32_SUBLANES, LANES), F32),
            pltpu.VMEM((hg * n_qb, 1, tq), F32),
            pltpu.VMEM((hg * n_qb, vrows, tq), F32),
        ],
        compiler_params=_params("arbitrary", "arbitrary"),
        name="fox",
    )(proj, proj, proj, cx, nc)


def _merge_load_weights(srcs, dsts, stage_ref, sems):
    rows = stage_ref.shape[1]
    chunks = [(src, dst, r0) for src, dst in zip(srcs, dsts) for r0 in range(0, src.shape[0], rows)]

    def copy(n):
        src, _, r0 = chunks[n]
        return pltpu.make_async_copy(src.at[pl.ds(r0, rows), :], stage_ref.at[n % 2], sems.at[n % 2])

    copy(0).start()
    for n, (_, dst, r0) in enumerate(chunks):
        if n + 1 < len(chunks):
            copy(n + 1).start()
        copy(n).wait()
        dst[r0:r0 + rows, :] = stage_ref[n % 2].astype(BF16)


def _merge_kernel(oa_ref, ob_ref, ga_ref, gb_ref, wa_hbm, wb_hbm, wo_hbm, x_ref, nw_ref, npre_ref,
                  out_ref, h2_ref, wa_ref, wb_ref, wo_ref, stage_ref, sems):
    @pl.when(pl.program_id(0) == 0)
    def _():
        _merge_load_weights((wa_hbm, wb_hbm, wo_hbm), (wa_ref, wb_ref, wo_ref), stage_ref, sems)

    sub = MERGE_SUB_ROWS
    tiles = [slice(r0, r0 + sub) for r0 in range(0, out_ref.shape[0], sub)]
    merged = []
    for r in tiles:
        ya = jnp.dot(oa_ref[r, :], wa_ref[...], preferred_element_type=F32)
        yb = jnp.dot(ob_ref[r, :], wb_ref[...], preferred_element_type=F32)
        merged.append((_sigmoid(ga_ref[r, :].astype(F32)) * ya
                       + _sigmoid(gb_ref[r, :].astype(F32)) * yb).astype(BF16))
    us = [jnp.dot(m, wo_ref[...], preferred_element_type=F32) for m in merged]
    for r, u in zip(tiles, us):
        u = u * lax.rsqrt(jnp.mean(u * u, axis=-1, keepdims=True) + RMS_EPS) * nw_ref[...]
        x1 = x_ref[r, :] + u
        out_ref[r, :] = x1
        ms = jnp.mean(x1 * x1, axis=-1, keepdims=True)
        h2_ref[r, :] = (x1 * lax.rsqrt(ms + RMS_EPS) * npre_ref[...]).astype(BF16)


def _merge(y_a, y_b, proj, w_up_a, w_up_b, w_o, x2, norm_w, norm_ffn_pre):
    tm = MERGE_TM
    hbm = pl.BlockSpec(memory_space=pl.ANY)
    return pl.pallas_call(
        _merge_kernel,
        grid=(SEQ // tm,),
        in_specs=[
            pl.BlockSpec((tm, WIDTH), lambda i: (i, 0)),
            pl.BlockSpec((tm, WIDTH), lambda i: (i, 0)),
            pl.BlockSpec((tm, D_MODEL), lambda i: (i, BLK_GA // 2)),
            pl.BlockSpec((tm, D_MODEL), lambda i: (i, BLK_GB // 2)),
            hbm,
            hbm,
            hbm,
            pl.BlockSpec((tm, D_MODEL), lambda i: (i, 0)),
            pl.BlockSpec((1, D_MODEL), lambda i: (0, 0)),
            pl.BlockSpec((1, D_MODEL), lambda i: (0, 0)),
        ],
        out_specs=[
            pl.BlockSpec((tm, D_MODEL), lambda i: (i, 0)),
            pl.BlockSpec((tm, D_MODEL), lambda i: (i, 0)),
        ],
        out_shape=[
            jax.ShapeDtypeStruct((SEQ, D_MODEL), F32),
            jax.ShapeDtypeStruct((SEQ, D_MODEL), BF16),
        ],
        scratch_shapes=[
            pltpu.VMEM((WIDTH, D_MODEL), BF16),
            pltpu.VMEM((WIDTH, D_MODEL), BF16),
            pltpu.VMEM((D_MODEL, D_MODEL), BF16),
            pltpu.VMEM((2, MERGE_STAGE_ROWS, D_MODEL), F32),
            pltpu.SemaphoreType.DMA((2,)),
        ],
        compiler_params=pltpu.CompilerParams(dimension_semantics=("arbitrary",),
                                             vmem_limit_bytes=MERGE_VMEM_LIMIT_BYTES),
        name="merge",
    )(y_a, y_b, proj, proj, w_up_a, w_up_b, w_o, x2, norm_w, norm_ffn_pre)


def _ffn_residual_copy(x_hbm, xres_ref, sem, i):
    tm = xres_ref.shape[0]
    return pltpu.make_async_copy(x_hbm.at[pl.ds(pl.multiple_of(i * tm, tm), tm), :], xres_ref, sem)


def _ffn_kernel(h_ref, x_hbm, npost_ref, wg_ref, wu_ref, wd_ref, out_ref, xres_ref, sem):
    i = pl.program_id(0)
    f = pl.program_id(1)

    @pl.when(f == 0)
    def _():
        _ffn_residual_copy(x_hbm, xres_ref, sem, i).start()
        out_ref[...] = jnp.zeros_like(out_ref)

    h = h_ref[...]
    gate = jnp.dot(h, wg_ref[...].astype(BF16), preferred_element_type=F32)
    up = jnp.dot(h, wu_ref[...].astype(BF16), preferred_element_type=F32)
    act = (gate * _sigmoid(gate) * up).astype(BF16)
    out_ref[...] += jnp.dot(act, wd_ref[...].astype(BF16), preferred_element_type=F32)

    @pl.when(f == pl.num_programs(1) - 1)
    def _():
        _ffn_residual_copy(x_hbm, xres_ref, sem, i).wait()
        u = out_ref[...]
        u = u * lax.rsqrt(jnp.mean(u * u, axis=-1, keepdims=True) + RMS_EPS) * npost_ref[...]
        out_ref[...] = xres_ref[...] + u


def _ffn(h2, x1, norm_post, w_in, w_down):
    tm, tf = FFN_TM, FFN_TF
    nf = D_FF // tf
    return pl.pallas_call(
        _ffn_kernel,
        grid=(SEQ // tm, nf),
        in_specs=[
            pl.BlockSpec((tm, D_MODEL), lambda i, f: (i, 0)),
            pl.BlockSpec(memory_space=pl.ANY),
            pl.BlockSpec((1, D_MODEL), lambda i, f: (0, 0)),
            pl.BlockSpec((D_MODEL, tf), lambda i, f: (0, f)),
            pl.BlockSpec((D_MODEL, tf), lambda i, f: (0, f + nf)),
            pl.BlockSpec((tf, D_MODEL), lambda i, f: (f, 0)),
        ],
        out_specs=pl.BlockSpec((tm, D_MODEL), lambda i, f: (i, 0)),
        out_shape=jax.ShapeDtypeStruct((SEQ, D_MODEL), F32),
        scratch_shapes=[pltpu.VMEM((tm, D_MODEL), F32), pltpu.SemaphoreType.DMA(())],
        compiler_params=_params("arbitrary", "arbitrary"),
        name="ffn",
    )(h2, x1, norm_post, w_in, w_in, w_down)


def kernel(x, w_in, b_fox_f, hgrn_lb_logits, hgrn_norm_w, w_up_a, w_up_b, w_o, norm_mix_pre,
           norm_mix_post, norm_ffn_pre, norm_ffn_post, w_ffn_in, w_ffn_down):
    assert x.shape == (1, SEQ, D_MODEL) and w_in.shape[0] == 1
    w_t = w_in[0].T
    x2 = x[0]

    h, a_f, cx, nc = _prenorm(x2, norm_mix_pre, w_t, b_fox_f.reshape(N_HEADS, 1))
    proj = _inproj(h, w_t)
    y_a = _hgrn(proj, a_f, hgrn_lb_logits, hgrn_norm_w)
    y_b = _fox(proj, cx, nc)
    x1, h2 = _merge(y_a, y_b, proj, w_up_a[0], w_up_b[0], w_o[0], x2, norm_mix_post, norm_ffn_pre)
    out = _ffn(h2, x1, norm_ffn_post, w_ffn_in[0], w_ffn_down[0])
    return out[None]
```

```python
import functools

import numpy as np
import jax
import jax.numpy as jnp
from jax import lax
from jax.experimental import pallas as pl
from jax.experimental.pallas import tpu as pltpu

F32 = jnp.float32
BF16 = jnp.bfloat16

D_MODEL = 2048
SEQ = 8192
HEAD_DIM = 128
N_HEADS = 8
WIDTH = N_HEADS * HEAD_DIM
D_FF = 5632
RMS_EPS = 1e-6
N_PROJ = 3 * WIDTH + 2 * D_MODEL + 3 * WIDTH

VMEM_LIMIT_BYTES = 56 * 1024 * 1024
LANES = 128
F32_SUBLANES = 8
BF16_SUBLANES = 16

SRC_AQ, SRC_AF, SRC_AI, SRC_AG, SRC_BQ, SRC_BK, SRC_BV = 0, 1, 2, 3, 4, 5, 6
IN_NA = 7
BLK_AQ, BLK_AI, BLK_GA, BLK_GB, BLK_AG, BLK_BQ, BLK_BK, BLK_BV = 0, 1, 2, 4, 6, 7, 8, 9

PRE_TM = 512
IN_TM, IN_TN = 2048, 512
HG_TS, HG_C, HG_HEADS = 1024, 64, 8
HG_FAST_LOG2 = 96.0
FOX_TQ, FOX_TK, FOX_HEADS, FOX_QBLOCKS = 256, 256, 2, 8
LOG2E = 1.4426950408889634
FOX_SKIP_LOG2 = 160.0
FOX_NORM_MARGIN = 1.01
MERGE_TM = 256
MERGE_STAGE_ROWS = 256
FFN_TM, FFN_TF = 1024, 256


def _params(*sem):
    return pltpu.CompilerParams(dimension_semantics=sem, vmem_limit_bytes=VMEM_LIMIT_BYTES)


def _dot_nt(a, b):
    return lax.dot_general(a, b, (((1,), (1,)), ((), ())), preferred_element_type=F32)


def _dot_tn(a, b):
    return lax.dot_general(a, b, (((0,), (0,)), ((), ())), preferred_element_type=F32)


def _log_sigmoid(x):
    return jnp.minimum(x, 0.0) - jnp.log(1.0 + jnp.exp(-jnp.abs(x)))


def _sigmoid(x):
    return 1.0 / (1.0 + jnp.exp(-x))


def _split3(x):
    p0 = x.astype(BF16)
    r1 = x - p0.astype(F32)
    p1 = r1.astype(BF16)
    p2 = (r1 - p1.astype(F32)).astype(BF16)
    return p0, p1, p2


def _prenorm_kernel(x_ref, nw_ref, waf_ref, wf_ref, bf_ref, tri_ref, h_ref, af_ref, cx_ref, nc_ref,
                    wafb_ref, carry_ref):
    i = pl.program_id(0)
    tm = x_ref.shape[0]

    @pl.when(i == 0)
    def _():
        wafb_ref[...] = waf_ref[...].T.astype(BF16)
        carry_ref[...] = jnp.zeros_like(carry_ref)

    x = x_ref[...]
    ms = jnp.mean(x * x, axis=-1, keepdims=True)
    h_ref[...] = (x * lax.rsqrt(ms + RMS_EPS) * nw_ref[...]).astype(BF16)
    hb = h_ref[...]
    af_ref[...] = jnp.dot(hb, wafb_ref[...], preferred_element_type=F32)
    logit = _dot_nt(wf_ref[...].astype(BF16), hb) + bf_ref[...]
    nls = _log_sigmoid(logit) * (-LOG2E)
    tri = tri_ref[...]
    loc = sum(jnp.dot(p, tri, preferred_element_type=F32) for p in _split3(nls))
    nc = loc + carry_ref[:, 0:1]
    carry_ref[...] = jnp.broadcast_to(nc[:, tm - 1:tm], carry_ref.shape)
    pad = jnp.zeros((LANES - N_HEADS, tm), F32)
    nc_ref[...] = jnp.concatenate([nc, pad], axis=0).T
    parts = [p.astype(F32) for p in _split3(nc)]
    pad3 = jnp.zeros((LANES - 3 * N_HEADS, tm), F32)
    cx_ref[...] = jnp.concatenate(parts + [pad3], axis=0).T.astype(BF16)


def _prenorm(x2, norm_w, w, bf_col):
    tm = PRE_TM
    tri = jnp.asarray(np.triu(np.ones((tm, tm), np.float32)), BF16)
    once = pl.Buffered(1)
    return pl.pallas_call(
        _prenorm_kernel,
        grid=(SEQ // tm,),
        in_specs=[
            pl.BlockSpec((tm, D_MODEL), lambda i: (i, 0)),
            pl.BlockSpec((1, D_MODEL), lambda i: (0, 0)),
            pl.BlockSpec((WIDTH, D_MODEL), lambda i: (SRC_AF, 0), pipeline_mode=once),
            pl.BlockSpec((N_HEADS, D_MODEL), lambda i: (IN_NA * WIDTH // N_HEADS, 0), pipeline_mode=once),
            pl.BlockSpec((N_HEADS, 1), lambda i: (0, 0)),
            pl.BlockSpec((tm, tm), lambda i: (0, 0), pipeline_mode=once),
        ],
        out_specs=[
            pl.BlockSpec((tm, D_MODEL), lambda i: (i, 0)),
            pl.BlockSpec((tm, WIDTH), lambda i: (i, 0)),
            pl.BlockSpec((tm, LANES), lambda i: (i, 0)),
            pl.BlockSpec((tm, LANES), lambda i: (i, 0)),
        ],
        out_shape=[
            jax.ShapeDtypeStruct((SEQ, D_MODEL), BF16),
            jax.ShapeDtypeStruct((SEQ, WIDTH), F32),
            jax.ShapeDtypeStruct((SEQ, LANES), BF16),
            jax.ShapeDtypeStruct((SEQ, LANES), F32),
        ],
        scratch_shapes=[pltpu.VMEM((D_MODEL, WIDTH), BF16), pltpu.VMEM((N_HEADS, LANES), F32)],
        compiler_params=_params("arbitrary"),
        name="prenorm",
    )(x2, norm_w, w, w, bf_col, tri)


def _inproj_kernel(h_ref, wa_ref, wb_ref, proj_ref, wc_ref):
    j = pl.program_id(1)
    i = pl.program_id(2)
    tm, tn = proj_ref.shape
    nb = WIDTH // tn
    src = _inproj_src_block(j, nb)

    @pl.when(i == 0)
    def _():
        @pl.when(src < IN_NA * nb)
        def _():
            scale = jnp.where(src // nb == SRC_BQ, LOG2E * HEAD_DIM ** -0.5, 1.0)
            wc_ref[...] = (wa_ref[...] * scale).astype(BF16)

        @pl.when(src >= IN_NA * nb)
        def _():
            wc_ref[...] = jnp.concatenate([wa_ref[N_HEADS:, :], wb_ref[:N_HEADS, :]], axis=0).astype(BF16)

    rows = pl.ds(pl.multiple_of(i * tm, tm), tm)
    proj_ref[...] = _dot_nt(h_ref[rows, :], wc_ref[...]).astype(BF16)


def _inproj_src_block(j, nb):
    return jnp.where(j < SRC_AF * nb, j, j + nb)


def _inproj_out_block(j, nb):
    src = _inproj_src_block(j, nb)
    g = src // nb
    g_out = jnp.int32(BLK_AQ)
    for g_src, g_dst in ((SRC_AI, BLK_AI), (SRC_AG, BLK_AG), (SRC_BQ, BLK_BQ), (SRC_BK, BLK_BK), (SRC_BV, BLK_BV)):
        g_out = jnp.where(g == g_src, g_dst, g_out)
    return jnp.where(g < IN_NA, g_out * nb + src % nb, src - IN_NA * nb + BLK_GA * nb)


def _inproj(h, w):
    tm, tn = IN_TM, IN_TN
    nb = WIDTH // tn
    half = SEQ // 2
    n_i = half // tm
    first_gate = IN_NA * nb
    grid = (2, N_PROJ // tn, n_i)
    return pl.pallas_call(
        _inproj_kernel,
        grid=grid,
        in_specs=[
            pl.BlockSpec((half, D_MODEL), lambda s, j, i: (s, 0), pipeline_mode=pl.Buffered(1)),
            pl.BlockSpec((tn, D_MODEL), lambda s, j, i: (_inproj_src_block(j, nb), 0)),
            pl.BlockSpec((tn, D_MODEL), lambda s, j, i: (jnp.maximum(_inproj_src_block(j, nb), first_gate) + 1, 0)),
        ],
        out_specs=pl.BlockSpec((tm, tn), lambda s, j, i: (s * n_i + i, _inproj_out_block(j, nb))),
        out_shape=jax.ShapeDtypeStruct((SEQ, N_PROJ), BF16),
        scratch_shapes=[pltpu.VMEM((tn, D_MODEL), BF16)],
        compiler_params=_params("arbitrary", "arbitrary", "arbitrary"),
        name="inproj",
    )(h, w, w)


def _hgrn_constants(c):
    n_lvl = int(np.log2(c))
    t = np.arange(c)[:, None]
    j = np.arange(c)[None, :]
    blocks = [(j <= t), (j > t)]
    level = np.full((c, c), -1, np.int32)
    level[np.arange(c), np.arange(c)] = 0
    for l in range(n_lvl):
        b = 2 << l
        mid = (t // b) * b + b // 2 - 1
        second = (t % b) >= b // 2
        m = np.where(second, (j > mid) & (j <= t), (j > t) & (j <= mid))
        blocks.append(m)
        s = np.arange(c)[None, :]
        own = (t // b == s // b) & second & ((s % b) < b // 2)
        level[own] = l + 1
    sums = np.concatenate(blocks, axis=0).astype(np.float32)
    sums2 = np.concatenate([sums, sums], axis=1)
    return jnp.asarray(sums2, BF16), jnp.asarray(level), n_lvl


def _hgrn_kernel(q_ref, z_ref, v_ref, g_ref, lbl_ref, nw_ref, sums_ref, lvl_ref, y_ref,
                 st_ref, k_ref, b_ref, kout_ref, qin_ref, g2_ref, dlast_ref, bmin_ref, *, c, n_lvl):
    n_heads = st_ref.shape[0]
    d = HEAD_DIM

    @pl.when(pl.program_id(1) == 0)
    def _():
        st_ref[...] = jnp.zeros_like(st_ref)

    logits = lbl_ref[...]
    ex = jnp.exp(logits - jnp.max(logits, axis=0, keepdims=True))
    lb = ex[0:1, :] / jnp.sum(ex, axis=0, keepdims=True)
    one_m_lb = 1.0 - lb
    nw = nw_ref[...]
    sums = sums_ref[...]
    lvl = lvl_ref[...]
    n_chunks = q_ref.shape[0] // c

    heads = [slice(hh * d, (hh + 1) * d) for hh in range(n_heads)]
    causal = lvl >= 0

    def chunk_rows(ci):
        return pl.ds(pl.multiple_of(ci * c, c), c)

    bmin_ref[...] = jnp.zeros_like(bmin_ref)

    def prepare(ci, carry):
        r = chunk_rows(ci)
        sig = _sigmoid(z_ref[r, :])
        g = jnp.log2(lb + one_m_lb * sig)
        k_all = one_m_lb * (1.0 - sig)
        g_hi = g.astype(BF16)
        g_lo = (g - g_hi.astype(F32)).astype(BF16)
        g2 = jnp.concatenate([g_hi, g_lo], axis=0)
        e01 = jnp.dot(sums[0:2 * c], g2, preferred_element_type=F32)
        b = e01[0:c]
        dec_b = jnp.exp2(b)
        g2_ref[ci] = g2
        b_ref[r, :] = b
        k_ref[r, :] = k_all
        kout_ref[r, :] = (k_all * jnp.exp2(e01[c:2 * c])).astype(BF16)
        qin_ref[r, :] = (q_ref[r, :].astype(F32) * dec_b).astype(BF16)
        dlast_ref[ci] = jnp.broadcast_to(dec_b[c - 1:c, :], dlast_ref.shape[1:])
        bmin_ref[...] = jnp.minimum(bmin_ref[...], b)
        return carry

    lax.fori_loop(0, n_chunks, prepare, 0, unroll=2)
    b_min = jnp.min(bmin_ref[...])

    def finish(ci, scores):
        r = chunk_rows(ci)
        o_intra, inc = [], []
        for hh, cols in enumerate(heads):
            v = v_ref[r, cols]
            o_intra.append(jnp.dot(scores[hh], v, preferred_element_type=F32))
            inc.append(_dot_tn(v, kout_ref[r, cols]))
        for hh, cols in enumerate(heads):
            st = st_ref[hh]
            o = o_intra[hh] + _dot_nt(qin_ref[r, cols], st.astype(BF16))
            st_ref[hh] = st * dlast_ref[ci, 0:1, cols] + inc[hh]
            o = o * lax.rsqrt(jnp.mean(o * o, axis=-1, keepdims=True) + RMS_EPS) * nw
            gt = g_ref[r, cols].astype(F32)
            y_ref[r, cols] = (o * gt * _sigmoid(gt)).astype(y_ref.dtype)

    def fast_chunk(ci, carry):
        r = chunk_rows(ci)
        k_up = (k_ref[r, :] * jnp.exp2(-b_ref[r, :])).astype(BF16)
        scores = [jnp.where(causal, _dot_nt(qin_ref[r, cols], k_up[:, cols]), 0.0).astype(BF16)
                  for cols in heads]
        finish(ci, scores)
        return carry

    def safe_chunk(ci, carry):
        r = chunk_rows(ci)
        dec_l = jnp.exp2(jnp.dot(sums[2 * c:], g2_ref[ci], preferred_element_type=F32))
        scores = []
        for cols in heads:
            q = q_ref[r, cols].astype(F32)
            k = k_ref[r, cols]
            sc = jnp.where(lvl == 0, _dot_nt(q.astype(BF16), k.astype(BF16)), 0.0)
            for l in range(n_lvl):
                d_l = dec_l[l * c:(l + 1) * c, cols]
                s_l = _dot_nt((q * d_l).astype(BF16), (k * d_l).astype(BF16))
                sc = jnp.where(lvl == l + 1, s_l, sc)
            scores.append(sc.astype(BF16))
        finish(ci, scores)
        return carry

    @pl.when(b_min >= -HG_FAST_LOG2)
    def _():
        lax.fori_loop(0, n_chunks, fast_chunk, 0, unroll=8)

    @pl.when(b_min < -HG_FAST_LOG2)
    def _():
        lax.fori_loop(0, n_chunks, safe_chunk, 0)


def _hgrn(proj, a_f, lb_logits, norm_w):
    ts, c, hg = HG_TS, HG_C, HG_HEADS
    sums, lvl, n_lvl = _hgrn_constants(c)
    w = hg * HEAD_DIM
    per_blk = WIDTH // w
    grid = (N_HEADS // hg, SEQ // ts)
    return pl.pallas_call(
        functools.partial(_hgrn_kernel, c=c, n_lvl=n_lvl),
        grid=grid,
        in_specs=[
            pl.BlockSpec((ts, w), lambda h, i: (i, BLK_AQ * per_blk + h)),
            pl.BlockSpec((ts, w), lambda h, i: (i, h)),
            pl.BlockSpec((ts, w), lambda h, i: (i, BLK_AI * per_blk + h)),
            pl.BlockSpec((ts, w), lambda h, i: (i, BLK_AG * per_blk + h)),
            pl.BlockSpec((lb_logits.shape[0], w), lambda h, i: (0, h)),
            pl.BlockSpec((1, HEAD_DIM), lambda h, i: (0, 0)),
            pl.BlockSpec(sums.shape, lambda h, i: (0, 0)),
            pl.BlockSpec(lvl.shape, lambda h, i: (0, 0)),
        ],
        out_specs=pl.BlockSpec((ts, w), lambda h, i: (i, h)),
        out_shape=jax.ShapeDtypeStruct((SEQ, WIDTH), BF16),
        scratch_shapes=[
            pltpu.VMEM((hg, HEAD_DIM, HEAD_DIM), F32),
            pltpu.VMEM((ts, w), F32),
            pltpu.VMEM((ts, w), F32),
            pltpu.VMEM((ts, w), BF16),
            pltpu.VMEM((ts, w), BF16),
            pltpu.VMEM((ts // c, 2 * c, w), BF16),
            pltpu.VMEM((ts // c, F32_SUBLANES, w), F32),
            pltpu.VMEM((c, w), F32),
        ],
        compiler_params=_params("arbitrary", "arbitrary"),
        name="hgrn",
    )(proj, a_f, proj, proj, lb_logits, norm_w, sums, lvl)


def _fox_kernel(q_ref, k_ref, v_ref, cx_ref, nc_ref, o_ref, vt_ref, nrm_ref, m_ref, acc_ref, *, tq, tk):
    g = pl.program_id(0)
    i = pl.program_id(1)
    n_heads = vt_ref.shape[0]
    d = HEAD_DIM
    s_len = k_ref.shape[0]
    n_sub = tq // tk
    n_kt = s_len // tk
    vrows = vt_ref.shape[2]
    heads = [(hh, slice(hh * d, (hh + 1) * d)) for hh in range(n_heads)]

    @pl.when(i == 0)
    def _():
        ones_row = jnp.where(lax.broadcasted_iota(jnp.int32, (vrows - d, tk), 0) == 0, 1.0, 0.0)
        wg = n_heads * d
        same_head = (lax.broadcasted_iota(jnp.int32, (wg, wg), 0) // d
                     == lax.broadcasted_iota(jnp.int32, (wg, wg), 1) // d)
        ones_blk = jnp.where(same_head, 1.0, 0.0).astype(BF16)

        def sq_norms(x):
            return jnp.dot(x * x, ones_blk, preferred_element_type=F32)

        kn = jnp.zeros((tk, wg), F32)
        qn = jnp.zeros((tk, wg), F32)
        for r in range(n_kt):
            rows = slice(r * tk, (r + 1) * tk)
            for hh, cols in heads:
                vt_ref[hh, r, 0:d, :] = v_ref[rows, cols].astype(F32).T.astype(BF16)
                vt_ref[hh, r, d:vrows, :] = ones_row.astype(BF16)
            kn = jnp.maximum(kn, sq_norms(k_ref[rows, :]))
            qn = jnp.maximum(qn, sq_norms(q_ref[rows, :]))
        nrm = 2.0 * FOX_NORM_MARGIN * jnp.sqrt(jnp.max(qn, axis=0, keepdims=True)
                                               * jnp.max(kn, axis=0, keepdims=True))
        for hh, cols in heads:
            nrm_ref[hh] = jnp.broadcast_to(nrm[:, cols], nrm_ref.shape[1:])

    n_qb = o_ref.shape[0] // tq
    units = [(hh, cols, qb) for hh, cols in heads for qb in range(n_qb)]
    ends = nc_ref[pl.ds(tk - 1, n_kt, stride=tk), :]
    lane = lax.broadcasted_iota(jnp.int32, ends.shape, 1)
    rsel = lax.broadcasted_iota(jnp.int32, (d, tq), 0)
    qa, r_lo, first_tile = [], [], []
    for hh, cols, qb in units:
        h = g * n_heads + hh
        q0 = pl.multiple_of((i * n_qb + qb) * tq, tq)
        thr = nc_ref[pl.ds(q0, 1), :] - nrm_ref[hh, 0:1, :] - FOX_SKIP_LOG2
        r_lo.append(jnp.sum(jnp.where((ends < thr) & (lane == h), 1, 0)))
        first_tile.append((i * n_qb + qb) * n_sub)
        sel = jnp.where((rsel < 3 * N_HEADS) & ((rsel & (N_HEADS - 1)) == h), 1.0, 0.0).astype(BF16)
        qa.append(jnp.concatenate([q_ref[pl.ds(q0, tq), cols].astype(F32).T.astype(BF16), sel], axis=0))

    def scores(u, r, mask_off):
        hh, cols, _ = units[u]
        ks = pl.ds(pl.multiple_of(r * tk, tk), tk)
        ka = jnp.concatenate([k_ref[ks, cols], cx_ref[ks, :]], axis=1)
        s = jnp.dot(ka, qa[u], preferred_element_type=F32)
        if mask_off is not None:
            kid = lax.broadcasted_iota(jnp.int32, s.shape, 0) + mask_off
            qid = lax.broadcasted_iota(jnp.int32, s.shape, 1)
            s = jnp.where(kid <= qid, s, -jnp.inf)
        return s

    def first_step(with_prev):
        tiles = []
        for u in range(len(units)):
            t = [(first_tile[u] + rr, rr * tk) for rr in range(n_sub)]
            tiles.append(([(first_tile[u] - 1, None)] if with_prev[u] else []) + t)
        ss = [[scores(u, r, off) for r, off in tiles[u]] for u in range(len(units))]
        ms = []
        for u in range(len(units)):
            m = jnp.max(ss[u][0], axis=0, keepdims=True)
            for s in ss[u][1:]:
                m = jnp.maximum(m, jnp.max(s, axis=0, keepdims=True))
            ms.append(m)
        for u, (hh, _, _) in enumerate(units):
            pv = None
            for (r, _), s in zip(tiles[u], ss[u]):
                p = jnp.exp2(s - ms[u]).astype(BF16)
                dd = jnp.dot(vt_ref[hh, r], p, preferred_element_type=F32)
                pv = dd if pv is None else pv + dd
            acc_ref[u] = pv
            m_ref[u] = ms[u]

    def later_step(u, r):
        s = scores(u, r, None)
        m_prev = m_ref[u]
        m_new = jnp.maximum(m_prev, jnp.max(s, axis=0, keepdims=True))
        pv = jnp.dot(vt_ref[units[u][0], r], jnp.exp2(s - m_new).astype(BF16), preferred_element_type=F32)
        acc_ref[u] = jnp.exp2(m_prev - m_new) * acc_ref[u] + pv
        m_ref[u] = m_new

    @pl.when(i == 0)
    def _():
        first_step([qb > 0 for _, _, qb in units])

    @pl.when(i > 0)
    def _():
        first_step([True] * len(units))

    for u in range(len(units)):
        def off_diag(r, carry, u=u):
            later_step(u, r)
            return carry
        lax.fori_loop(r_lo[u], first_tile[u] - 1, off_diag, 0)

    for u, (hh, cols, qb) in enumerate(units):
        acc = acc_ref[u]
        o_ref[qb * tq:(qb + 1) * tq, cols] = (acc[0:d] / acc[d:d + 1]).T.astype(o_ref.dtype)


def _fox(proj, cx, nc):
    tq, tk, hg, n_qb = FOX_TQ, FOX_TK, FOX_HEADS, FOX_QBLOCKS
    w = hg * HEAD_DIM
    per_blk = WIDTH // w
    vrows = HEAD_DIM + BF16_SUBLANES
    return pl.pallas_call(
        functools.partial(_fox_kernel, tq=tq, tk=tk),
        grid=(N_HEADS // hg, SEQ // (n_qb * tq)),
        in_specs=[
            pl.BlockSpec((SEQ, w), lambda g, i: (0, BLK_BQ * per_blk + g)),
            pl.BlockSpec((SEQ, w), lambda g, i: (0, BLK_BK * per_blk + g)),
            pl.BlockSpec((SEQ, w), lambda g, i: (0, BLK_BV * per_blk + g)),
            pl.BlockSpec((SEQ, LANES), lambda g, i: (0, 0)),
            pl.BlockSpec((SEQ, LANES), lambda g, i: (0, 0)),
        ],
        out_specs=pl.BlockSpec((n_qb * tq, w), lambda g, i: (i, g)),
        out_shape=jax.ShapeDtypeStruct((SEQ, WIDTH), BF16),
        scratch_shapes=[
            pltpu.VMEM((hg, SEQ // tk, vrows, tk), BF16),
            pltpu.VMEM((hg, F32_SUBLANES, LANES), F32),
            pltpu.VMEM((hg * n_qb, 1, tq), F32),
            pltpu.VMEM((hg * n_qb, vrows, tq), F32),
        ],
        compiler_params=_params("arbitrary", "arbitrary"),
        name="fox",
    )(proj, proj, proj, cx, nc)


def _merge_load_weights(srcs, dsts, stage_ref, sems):
    rows = stage_ref.shape[1]
    chunks = [(src, dst, r0) for src, dst in zip(srcs, dsts) for r0 in range(0, src.shape[0], rows)]

    def copy(n):
        src, _, r0 = chunks[n]
        return pltpu.make_async_copy(src.at[pl.ds(r0, rows), :], stage_ref.at[n % 2], sems.at[n % 2])

    copy(0).start()
    for n, (_, dst, r0) in enumerate(chunks):
        if n + 1 < len(chunks):
            copy(n + 1).start()
        copy(n).wait()
        dst[r0:r0 + rows, :] = stage_ref[n % 2].astype(BF16)


def _merge_kernel(oa_ref, ob_ref, ga_ref, gb_ref, wa_hbm, wb_hbm, wo_hbm, x_ref, nw_ref, npre_ref,
                  out_ref, h2_ref, wa_ref, wb_ref, wo_ref, stage_ref, sems):
    @pl.when(pl.program_id(0) == 0)
    def _():
        _merge_load_weights((wa_hbm, wb_hbm, wo_hbm), (wa_ref, wb_ref, wo_ref), stage_ref, sems)

    ya = jnp.dot(oa_ref[...], wa_ref[...], preferred_element_type=F32)
    yb = jnp.dot(ob_ref[...], wb_ref[...], preferred_element_type=F32)
    merged = _sigmoid(ga_ref[...].astype(F32)) * ya + _sigmoid(gb_ref[...].astype(F32)) * yb
    u = jnp.dot(merged.astype(BF16), wo_ref[...], preferred_element_type=F32)
    u = u * lax.rsqrt(jnp.mean(u * u, axis=-1, keepdims=True) + RMS_EPS) * nw_ref[...]
    x1 = x_ref[...] + u
    out_ref[...] = x1
    ms = jnp.mean(x1 * x1, axis=-1, keepdims=True)
    h2_ref[...] = (x1 * lax.rsqrt(ms + RMS_EPS) * npre_ref[...]).astype(BF16)


def _merge(y_a, y_b, proj, w_up_a, w_up_b, w_o, x2, norm_w, norm_ffn_pre):
    tm = MERGE_TM
    hbm = pl.BlockSpec(memory_space=pl.ANY)
    return pl.pallas_call(
        _merge_kernel,
        grid=(SEQ // tm,),
        in_specs=[
            pl.BlockSpec((tm, WIDTH), lambda i: (i, 0)),
            pl.BlockSpec((tm, WIDTH), lambda i: (i, 0)),
            pl.BlockSpec((tm, D_MODEL), lambda i: (i, BLK_GA // 2)),
            pl.BlockSpec((tm, D_MODEL), lambda i: (i, BLK_GB // 2)),
            hbm,
            hbm,
            hbm,
            pl.BlockSpec((tm, D_MODEL), lambda i: (i, 0)),
            pl.BlockSpec((1, D_MODEL), lambda i: (0, 0)),
            pl.BlockSpec((1, D_MODEL), lambda i: (0, 0)),
        ],
        out_specs=[
            pl.BlockSpec((tm, D_MODEL), lambda i: (i, 0)),
            pl.BlockSpec((tm, D_MODEL), lambda i: (i, 0)),
        ],
        out_shape=[
            jax.ShapeDtypeStruct((SEQ, D_MODEL), F32),
            jax.ShapeDtypeStruct((SEQ, D_MODEL), BF16),
        ],
        scratch_shapes=[
            pltpu.VMEM((WIDTH, D_MODEL), BF16),
            pltpu.VMEM((WIDTH, D_MODEL), BF16),
            pltpu.VMEM((D_MODEL, D_MODEL), BF16),
            pltpu.VMEM((2, MERGE_STAGE_ROWS, D_MODEL), F32),
            pltpu.SemaphoreType.DMA((2,)),
        ],
        compiler_params=_params("arbitrary"),
        name="merge",
    )(y_a, y_b, proj, proj, w_up_a, w_up_b, w_o, x2, norm_w, norm_ffn_pre)


def _ffn_residual_copy(x_hbm, xres_ref, sem, i):
    tm = xres_ref.shape[0]
    return pltpu.make_async_copy(x_hbm.at[pl.ds(pl.multiple_of(i * tm, tm), tm), :], xres_ref, sem)


def _ffn_kernel(h_ref, x_hbm, npost_ref, wg_ref, wu_ref, wd_ref, out_ref, xres_ref, sem):
    i = pl.program_id(0)
    f = pl.program_id(1)

    @pl.when(f == 0)
    def _():
        _ffn_residual_copy(x_hbm, xres_ref, sem, i).start()
        out_ref[...] = jnp.zeros_like(out_ref)

    h = h_ref[...]
    gate = jnp.dot(h, wg_ref[...].astype(BF16), preferred_element_type=F32)
    up = jnp.dot(h, wu_ref[...].astype(BF16), preferred_element_type=F32)
    act = (gate * _sigmoid(gate) * up).astype(BF16)
    out_ref[...] += jnp.dot(act, wd_ref[...].astype(BF16), preferred_element_type=F32)

    @pl.when(f == pl.num_programs(1) - 1)
    def _():
        _ffn_residual_copy(x_hbm, xres_ref, sem, i).wait()
        u = out_ref[...]
        u = u * lax.rsqrt(jnp.mean(u * u, axis=-1, keepdims=True) + RMS_EPS) * npost_ref[...]
        out_ref[...] = xres_ref[...] + u


def _ffn(h2, x1, norm_post, w_in, w_down):
    tm, tf = FFN_TM, FFN_TF
    nf = D_FF // tf
    return pl.pallas_call(
        _ffn_kernel,
        grid=(SEQ // tm, nf),
        in_specs=[
            pl.BlockSpec((tm, D_MODEL), lambda i, f: (i, 0)),
            pl.BlockSpec(memory_space=pl.ANY),
            pl.BlockSpec((1, D_MODEL), lambda i, f: (0, 0)),
            pl.BlockSpec((D_MODEL, tf), lambda i, f: (0, f)),
            pl.BlockSpec((D_MODEL, tf), lambda i, f: (0, f + nf)),
            pl.BlockSpec((tf, D_MODEL), lambda i, f: (f, 0)),
        ],
        out_specs=pl.BlockSpec((tm, D_MODEL), lambda i, f: (i, 0)),
        out_shape=jax.ShapeDtypeStruct((SEQ, D_MODEL), F32),
        scratch_shapes=[pltpu.VMEM((tm, D_MODEL), F32), pltpu.SemaphoreType.DMA(())],
        compiler_params=_params("arbitrary", "arbitrary"),
        name="ffn",
    )(h2, x1, norm_post, w_in, w_in, w_down)


def kernel(x, w_in, b_fox_f, hgrn_lb_logits, hgrn_norm_w, w_up_a, w_up_b, w_o, norm_mix_pre,
           norm_mix_post, norm_ffn_pre, norm_ffn_post, w_ffn_in, w_ffn_down):
    assert x.shape == (1, SEQ, D_MODEL) and w_in.shape[0] == 1
    w_t = w_in[0].T
    x2 = x[0]

    h, a_f, cx, nc = _prenorm(x2, norm_mix_pre, w_t, b_fox_f.reshape(N_HEADS, 1))
    proj = _inproj(h, w_t)
    y_a = _hgrn(proj, a_f, hgrn_lb_logits, hgrn_norm_w)
    y_b = _fox(proj, cx, nc)
    x1, h2 = _merge(y_a, y_b, proj, w_up_a[0], w_up_b[0], w_o[0], x2, norm_mix_post, norm_ffn_pre)
    out = _ffn(h2, x1, norm_ffn_post, w_ffn_in[0], w_ffn_down[0])
    return out[None]
```

```python
import functools

import numpy as np
import jax
import jax.numpy as jnp
from jax import lax
from jax.experimental import pallas as pl
from jax.experimental.pallas import tpu as pltpu

F32 = jnp.float32
BF16 = jnp.bfloat16

D_MODEL = 2048
SEQ = 8192
HEAD_DIM = 128
N_HEADS = 8
WIDTH = N_HEADS * HEAD_DIM
D_FF = 5632
RMS_EPS = 1e-6
N_PROJ = 3 * WIDTH + 2 * D_MODEL + 3 * WIDTH

VMEM_LIMIT_BYTES = 56 * 1024 * 1024
LANES = 128
F32_SUBLANES = 8
BF16_SUBLANES = 16

SRC_AQ, SRC_AF, SRC_AI, SRC_AG, SRC_BQ, SRC_BK, SRC_BV = 0, 1, 2, 3, 4, 5, 6
IN_NA = 7
BLK_AQ, BLK_AI, BLK_GA, BLK_GB, BLK_AG, BLK_BQ, BLK_BK, BLK_BV = 0, 1, 2, 4, 6, 7, 8, 9

PRE_TM = 512
IN_TM, IN_TN = 2048, 512
HG_TS, HG_C, HG_HEADS = 1024, 64, 8
HG_FAST_LOG2 = 96.0
FOX_TQ, FOX_TK, FOX_HEADS, FOX_QBLOCKS = 256, 256, 2, 8
LOG2E = 1.4426950408889634
FOX_SKIP_LOG2 = 160.0
FOX_NORM_MARGIN = 1.01
MERGE_TM = 256
MERGE_STAGE_ROWS = 256
FFN_TM, FFN_TF = 1024, 256


def _params(*sem):
    return pltpu.CompilerParams(dimension_semantics=sem, vmem_limit_bytes=VMEM_LIMIT_BYTES)


def _dot_nt(a, b):
    return lax.dot_general(a, b, (((1,), (1,)), ((), ())), preferred_element_type=F32)


def _dot_tn(a, b):
    return lax.dot_general(a, b, (((0,), (0,)), ((), ())), preferred_element_type=F32)


def _log_sigmoid(x):
    return jnp.minimum(x, 0.0) - jnp.log(1.0 + jnp.exp(-jnp.abs(x)))


def _sigmoid(x):
    return 1.0 / (1.0 + jnp.exp(-x))


def _split3(x):
    p0 = x.astype(BF16)
    r1 = x - p0.astype(F32)
    p1 = r1.astype(BF16)
    p2 = (r1 - p1.astype(F32)).astype(BF16)
    return p0, p1, p2


def _prenorm_kernel(x_ref, nw_ref, waf_ref, wf_ref, bf_ref, tri_ref, h_ref, af_ref, cx_ref, nc_ref,
                    wafb_ref, carry_ref):
    i = pl.program_id(0)
    tm = x_ref.shape[0]

    @pl.when(i == 0)
    def _():
        wafb_ref[...] = waf_ref[...].T.astype(BF16)
        carry_ref[...] = jnp.zeros_like(carry_ref)

    x = x_ref[...]
    ms = jnp.mean(x * x, axis=-1, keepdims=True)
    h_ref[...] = (x * lax.rsqrt(ms + RMS_EPS) * nw_ref[...]).astype(BF16)
    hb = h_ref[...]
    af_ref[...] = jnp.dot(hb, wafb_ref[...], preferred_element_type=F32)
    logit = _dot_nt(wf_ref[...].astype(BF16), hb) + bf_ref[...]
    nls = _log_sigmoid(logit) * (-LOG2E)
    tri = tri_ref[...]
    loc = sum(jnp.dot(p, tri, preferred_element_type=F32) for p in _split3(nls))
    nc = loc + carry_ref[:, 0:1]
    carry_ref[...] = jnp.broadcast_to(nc[:, tm - 1:tm], carry_ref.shape)
    pad = jnp.zeros((LANES - N_HEADS, tm), F32)
    nc_ref[...] = jnp.concatenate([nc, pad], axis=0).T
    parts = [p.astype(F32) for p in _split3(nc)]
    pad3 = jnp.zeros((LANES - 3 * N_HEADS, tm), F32)
    cx_ref[...] = jnp.concatenate(parts + [pad3], axis=0).T.astype(BF16)


def _prenorm(x2, norm_w, w, bf_col):
    tm = PRE_TM
    tri = jnp.asarray(np.triu(np.ones((tm, tm), np.float32)), BF16)
    once = pl.Buffered(1)
    return pl.pallas_call(
        _prenorm_kernel,
        grid=(SEQ // tm,),
        in_specs=[
            pl.BlockSpec((tm, D_MODEL), lambda i: (i, 0)),
            pl.BlockSpec((1, D_MODEL), lambda i: (0, 0)),
            pl.BlockSpec((WIDTH, D_MODEL), lambda i: (SRC_AF, 0), pipeline_mode=once),
            pl.BlockSpec((N_HEADS, D_MODEL), lambda i: (IN_NA * WIDTH // N_HEADS, 0), pipeline_mode=once),
            pl.BlockSpec((N_HEADS, 1), lambda i: (0, 0)),
            pl.BlockSpec((tm, tm), lambda i: (0, 0), pipeline_mode=once),
        ],
        out_specs=[
            pl.BlockSpec((tm, D_MODEL), lambda i: (i, 0)),
            pl.BlockSpec((tm, WIDTH), lambda i: (i, 0)),
            pl.BlockSpec((tm, LANES), lambda i: (i, 0)),
            pl.BlockSpec((tm, LANES), lambda i: (i, 0)),
        ],
        out_shape=[
            jax.ShapeDtypeStruct((SEQ, D_MODEL), BF16),
            jax.ShapeDtypeStruct((SEQ, WIDTH), F32),
            jax.ShapeDtypeStruct((SEQ, LANES), BF16),
            jax.ShapeDtypeStruct((SEQ, LANES), F32),
        ],
        scratch_shapes=[pltpu.VMEM((D_MODEL, WIDTH), BF16), pltpu.VMEM((N_HEADS, LANES), F32)],
        compiler_params=_params("arbitrary"),
        name="prenorm",
    )(x2, norm_w, w, w, bf_col, tri)


def _inproj_kernel(h_ref, wa_ref, wb_ref, proj_ref, wc_ref):
    j = pl.program_id(1)
    i = pl.program_id(2)
    tm, tn = proj_ref.shape
    nb = WIDTH // tn
    src = _inproj_src_block(j, nb)

    @pl.when(i == 0)
    def _():
        @pl.when(src < IN_NA * nb)
        def _():
            scale = jnp.where(src // nb == SRC_BQ, LOG2E * HEAD_DIM ** -0.5, 1.0)
            wc_ref[...] = (wa_ref[...] * scale).astype(BF16)

        @pl.when(src >= IN_NA * nb)
        def _():
            wc_ref[...] = jnp.concatenate([wa_ref[N_HEADS:, :], wb_ref[:N_HEADS, :]], axis=0).astype(BF16)

    rows = pl.ds(pl.multiple_of(i * tm, tm), tm)
    proj_ref[...] = _dot_nt(h_ref[rows, :], wc_ref[...]).astype(BF16)


def _inproj_src_block(j, nb):
    return jnp.where(j < SRC_AF * nb, j, j + nb)


def _inproj_out_block(j, nb):
    src = _inproj_src_block(j, nb)
    g = src // nb
    g_out = jnp.int32(BLK_AQ)
    for g_src, g_dst in ((SRC_AI, BLK_AI), (SRC_AG, BLK_AG), (SRC_BQ, BLK_BQ), (SRC_BK, BLK_BK), (SRC_BV, BLK_BV)):
        g_out = jnp.where(g == g_src, g_dst, g_out)
    return jnp.where(g < IN_NA, g_out * nb + src % nb, src - IN_NA * nb + BLK_GA * nb)


def _inproj(h, w):
    tm, tn = IN_TM, IN_TN
    nb = WIDTH // tn
    half = SEQ // 2
    n_i = half // tm
    first_gate = IN_NA * nb
    grid = (2, N_PROJ // tn, n_i)
    return pl.pallas_call(
        _inproj_kernel,
        grid=grid,
        in_specs=[
            pl.BlockSpec((half, D_MODEL), lambda s, j, i: (s, 0), pipeline_mode=pl.Buffered(1)),
            pl.BlockSpec((tn, D_MODEL), lambda s, j, i: (_inproj_src_block(j, nb), 0)),
            pl.BlockSpec((tn, D_MODEL), lambda s, j, i: (jnp.maximum(_inproj_src_block(j, nb), first_gate) + 1, 0)),
        ],
        out_specs=pl.BlockSpec((tm, tn), lambda s, j, i: (s * n_i + i, _inproj_out_block(j, nb))),
        out_shape=jax.ShapeDtypeStruct((SEQ, N_PROJ), BF16),
        scratch_shapes=[pltpu.VMEM((tn, D_MODEL), BF16)],
        compiler_params=_params("arbitrary", "arbitrary", "arbitrary"),
        name="inproj",
    )(h, w, w)


def _hgrn_constants(c):
    n_lvl = int(np.log2(c))
    t = np.arange(c)[:, None]
    j = np.arange(c)[None, :]
    blocks = [(j <= t), (j > t)]
    level = np.full((c, c), -1, np.int32)
    level[np.arange(c), np.arange(c)] = 0
    for l in range(n_lvl):
        b = 2 << l
        mid = (t // b) * b + b // 2 - 1
        second = (t % b) >= b // 2
        m = np.where(second, (j > mid) & (j <= t), (j > t) & (j <= mid))
        blocks.append(m)
        s = np.arange(c)[None, :]
        own = (t // b == s // b) & second & ((s % b) < b // 2)
        level[own] = l + 1
    sums = np.concatenate(blocks, axis=0).astype(np.float32)
    sums2 = np.concatenate([sums, sums], axis=1)
    return jnp.asarray(sums2, BF16), jnp.asarray(level), n_lvl


def _hgrn_kernel(q_ref, z_ref, v_ref, g_ref, lbl_ref, nw_ref, sums_ref, lvl_ref, y_ref,
                 st_ref, k_ref, b_ref, kout_ref, qin_ref, g2_ref, dlast_ref, bmin_ref, *, c, n_lvl):
    n_heads = st_ref.shape[0]
    d = HEAD_DIM

    @pl.when(pl.program_id(1) == 0)
    def _():
        st_ref[...] = jnp.zeros_like(st_ref)

    logits = lbl_ref[...]
    ex = jnp.exp(logits - jnp.max(logits, axis=0, keepdims=True))
    lb = ex[0:1, :] / jnp.sum(ex, axis=0, keepdims=True)
    one_m_lb = 1.0 - lb
    nw = nw_ref[...]
    sums = sums_ref[...]
    lvl = lvl_ref[...]
    n_chunks = q_ref.shape[0] // c

    heads = [slice(hh * d, (hh + 1) * d) for hh in range(n_heads)]
    causal = lvl >= 0

    def chunk_rows(ci):
        return pl.ds(pl.multiple_of(ci * c, c), c)

    bmin_ref[...] = jnp.zeros_like(bmin_ref)

    def prepare(ci, carry):
        r = chunk_rows(ci)
        sig = _sigmoid(z_ref[r, :])
        g = jnp.log2(lb + one_m_lb * sig)
        k_all = one_m_lb * (1.0 - sig)
        g_hi = g.astype(BF16)
        g_lo = (g - g_hi.astype(F32)).astype(BF16)
        g2 = jnp.concatenate([g_hi, g_lo], axis=0)
        e01 = jnp.dot(sums[0:2 * c], g2, preferred_element_type=F32)
        b = e01[0:c]
        dec_b = jnp.exp2(b)
        g2_ref[ci] = g2
        b_ref[r, :] = b
        k_ref[r, :] = k_all
        kout_ref[r, :] = (k_all * jnp.exp2(e01[c:2 * c])).astype(BF16)
        qin_ref[r, :] = (q_ref[r, :].astype(F32) * dec_b).astype(BF16)
        dlast_ref[ci] = jnp.broadcast_to(dec_b[c - 1:c, :], dlast_ref.shape[1:])
        bmin_ref[...] = jnp.minimum(bmin_ref[...], b)
        return carry

    lax.fori_loop(0, n_chunks, prepare, 0, unroll=2)
    b_min = jnp.min(bmin_ref[...])

    def finish(ci, scores):
        r = chunk_rows(ci)
        o_intra, inc = [], []
        for hh, cols in enumerate(heads):
            v = v_ref[r, cols]
            o_intra.append(jnp.dot(scores[hh], v, preferred_element_type=F32))
            inc.append(_dot_tn(v, kout_ref[r, cols]))
        for hh, cols in enumerate(heads):
            st = st_ref[hh]
            o = o_intra[hh] + _dot_nt(qin_ref[r, cols], st.astype(BF16))
            st_ref[hh] = st * dlast_ref[ci, 0:1, cols] + inc[hh]
            o = o * lax.rsqrt(jnp.mean(o * o, axis=-1, keepdims=True) + RMS_EPS) * nw
            gt = g_ref[r, cols].astype(F32)
            y_ref[r, cols] = (o * gt * _sigmoid(gt)).astype(y_ref.dtype)

    def fast_chunk(ci, carry):
        r = chunk_rows(ci)
        k_up = (k_ref[r, :] * jnp.exp2(-b_ref[r, :])).astype(BF16)
        scores = [jnp.where(causal, _dot_nt(qin_ref[r, cols], k_up[:, cols]), 0.0).astype(BF16)
                  for cols in heads]
        finish(ci, scores)
        return carry

    def safe_chunk(ci, carry):
        r = chunk_rows(ci)
        dec_l = jnp.exp2(jnp.dot(sums[2 * c:], g2_ref[ci], preferred_element_type=F32))
        scores = []
        for cols in heads:
            q = q_ref[r, cols].astype(F32)
            k = k_ref[r, cols]
            sc = jnp.where(lvl == 0, _dot_nt(q.astype(BF16), k.astype(BF16)), 0.0)
            for l in range(n_lvl):
                d_l = dec_l[l * c:(l + 1) * c, cols]
                s_l = _dot_nt((q * d_l).astype(BF16), (k * d_l).astype(BF16))
                sc = jnp.where(lvl == l + 1, s_l, sc)
            scores.append(sc.astype(BF16))
        finish(ci, scores)
        return carry

    @pl.when(b_min >= -HG_FAST_LOG2)
    def _():
        lax.fori_loop(0, n_chunks, fast_chunk, 0, unroll=8)

    @pl.when(b_min < -HG_FAST_LOG2)
    def _():
        lax.fori_loop(0, n_chunks, safe_chunk, 0)


def _hgrn(proj, a_f, lb_logits, norm_w):
    ts, c, hg = HG_TS, HG_C, HG_HEADS
    sums, lvl, n_lvl = _hgrn_constants(c)
    w = hg * HEAD_DIM
    per_blk = WIDTH // w
    grid = (N_HEADS // hg, SEQ // ts)
    return pl.pallas_call(
        functools.partial(_hgrn_kernel, c=c, n_lvl=n_lvl),
        grid=grid,
        in_specs=[
            pl.BlockSpec((ts, w), lambda h, i: (i, BLK_AQ * per_blk + h)),
            pl.BlockSpec((ts, w), lambda h, i: (i, h)),
            pl.BlockSpec((ts, w), lambda h, i: (i, BLK_AI * per_blk + h)),
            pl.BlockSpec((ts, w), lambda h, i: (i, BLK_AG * per_blk + h)),
            pl.BlockSpec((lb_logits.shape[0], w), lambda h, i: (0, h)),
            pl.BlockSpec((1, HEAD_DIM), lambda h, i: (0, 0)),
            pl.BlockSpec(sums.shape, lambda h, i: (0, 0)),
            pl.BlockSpec(lvl.shape, lambda h, i: (0, 0)),
        ],
        out_specs=pl.BlockSpec((ts, w), lambda h, i: (i, h)),
        out_shape=jax.ShapeDtypeStruct((SEQ, WIDTH), BF16),
        scratch_shapes=[
            pltpu.VMEM((hg, HEAD_DIM, HEAD_DIM), F32),
            pltpu.VMEM((ts, w), F32),
            pltpu.VMEM((ts, w), F32),
            pltpu.VMEM((ts, w), BF16),
            pltpu.VMEM((ts, w), BF16),
            pltpu.VMEM((ts // c, 2 * c, w), BF16),
            pltpu.VMEM((ts // c, F32_SUBLANES, w), F32),
            pltpu.VMEM((c, w), F32),
        ],
        compiler_params=_params("arbitrary", "arbitrary"),
        name="hgrn",
    )(proj, a_f, proj, proj, lb_logits, norm_w, sums, lvl)


def _fox_kernel(q_ref, k_ref, v_ref, cx_ref, nc_ref, o_ref, vt_ref, nrm_ref, m_ref, acc_ref, *, tq, tk):
    g = pl.program_id(0)
    i = pl.program_id(1)
    n_heads = vt_ref.shape[0]
    d = HEAD_DIM
    s_len = k_ref.shape[0]
    n_sub = tq // tk
    n_kt = s_len // tk
    vrows = vt_ref.shape[2]
    heads = [(hh, slice(hh * d, (hh + 1) * d)) for hh in range(n_heads)]

    @pl.when(i == 0)
    def _():
        ones_row = jnp.where(lax.broadcasted_iota(jnp.int32, (vrows - d, tk), 0) == 0, 1.0, 0.0)
        wg = n_heads * d
        same_head = (lax.broadcasted_iota(jnp.int32, (wg, wg), 0) // d
                     == lax.broadcasted_iota(jnp.int32, (wg, wg), 1) // d)
        ones_blk = jnp.where(same_head, 1.0, 0.0).astype(BF16)

        def sq_norms(x):
            return jnp.dot(x * x, ones_blk, preferred_element_type=F32)

        kn = jnp.zeros((tk, wg), F32)
        qn = jnp.zeros((tk, wg), F32)
        for r in range(n_kt):
            rows = slice(r * tk, (r + 1) * tk)
            for hh, cols in heads:
                vt_ref[hh, r, 0:d, :] = v_ref[rows, cols].astype(F32).T.astype(BF16)
                vt_ref[hh, r, d:vrows, :] = ones_row.astype(BF16)
            kn = jnp.maximum(kn, sq_norms(k_ref[rows, :]))
            qn = jnp.maximum(qn, sq_norms(q_ref[rows, :]))
        nrm = 2.0 * FOX_NORM_MARGIN * jnp.sqrt(jnp.max(qn, axis=0, keepdims=True)
                                               * jnp.max(kn, axis=0, keepdims=True))
        for hh, cols in heads:
            nrm_ref[hh] = jnp.broadcast_to(nrm[:, cols], nrm_ref.shape[1:])

    n_qb = o_ref.shape[0] // tq
    units = [(hh, cols, qb) for hh, cols in heads for qb in range(n_qb)]
    ends = nc_ref[pl.ds(tk - 1, n_kt, stride=tk), :]
    lane = lax.broadcasted_iota(jnp.int32, ends.shape, 1)
    rsel = lax.broadcasted_iota(jnp.int32, (d, tq), 0)
    qa, r_lo, first_tile = [], [], []
    for hh, cols, qb in units:
        h = g * n_heads + hh
        q0 = pl.multiple_of((i * n_qb + qb) * tq, tq)
        thr = nc_ref[pl.ds(q0, 1), :] - nrm_ref[hh, 0:1, :] - FOX_SKIP_LOG2
        r_lo.append(jnp.sum(jnp.where((ends < thr) & (lane == h), 1, 0)))
        first_tile.append((i * n_qb + qb) * n_sub)
        sel = jnp.where((rsel < 3 * N_HEADS) & ((rsel & (N_HEADS - 1)) == h), 1.0, 0.0).astype(BF16)
        qa.append(jnp.concatenate([q_ref[pl.ds(q0, tq), cols].astype(F32).T.astype(BF16), sel], axis=0))

    def scores(u, r, mask_off):
        hh, cols, _ = units[u]
        ks = pl.ds(pl.multiple_of(r * tk, tk), tk)
        ka = jnp.concatenate([k_ref[ks, cols], cx_ref[ks, :]], axis=1)
        s = jnp.dot(ka, qa[u], preferred_element_type=F32)
        if mask_off is not None:
            kid = lax.broadcasted_iota(jnp.int32, s.shape, 0) + mask_off
            qid = lax.broadcasted_iota(jnp.int32, s.shape, 1)
            s = jnp.where(kid <= qid, s, -jnp.inf)
        return s

    def first_step(with_prev):
        tiles = []
        for u in range(len(units)):
            t = [(first_tile[u] + rr, rr * tk) for rr in range(n_sub)]
            tiles.append(([(first_tile[u] - 1, None)] if with_prev[u] else []) + t)
        ss = [[scores(u, r, off) for r, off in tiles[u]] for u in range(len(units))]
        ms = []
        for u in range(len(units)):
            m = jnp.max(ss[u][0], axis=0, keepdims=True)
            for s in ss[u][1:]:
                m = jnp.maximum(m, jnp.max(s, axis=0, keepdims=True))
            ms.append(m)
        for u, (hh, _, _) in enumerate(units):
            pv = None
            for (r, _), s in zip(tiles[u], ss[u]):
                p = jnp.exp2(s - ms[u]).astype(BF16)
                dd = jnp.dot(vt_ref[hh, r], p, preferred_element_type=F32)
                pv = dd if pv is None else pv + dd
            acc_ref[u] = pv
            m_ref[u] = ms[u]

    def later_step(u, r):
        s = scores(u, r, None)
        m_prev = m_ref[u]
        m_new = jnp.maximum(m_prev, jnp.max(s, axis=0, keepdims=True))
        pv = jnp.dot(vt_ref[units[u][0], r], jnp.exp2(s - m_new).astype(BF16), preferred_element_type=F32)
        acc_ref[u] = jnp.exp2(m_prev - m_new) * acc_ref[u] + pv
        m_ref[u] = m_new

    @pl.when(i == 0)
    def _():
        first_step([qb > 0 for _, _, qb in units])

    @pl.when(i > 0)
    def _():
        first_step([True] * len(units))

    for u in range(len(units)):
        def off_diag(r, carry, u=u):
            later_step(u, r)
            return carry
        lax.fori_loop(r_lo[u], first_tile[u] - 1, off_diag, 0)

    for u, (hh, cols, qb) in enumerate(units):
        acc = acc_ref[u]
        o_ref[qb * tq:(qb + 1) * tq, cols] = (acc[0:d] / acc[d:d + 1]).T.astype(o_ref.dtype)


def _fox(proj, cx, nc):
    tq, tk, hg, n_qb = FOX_TQ, FOX_TK, FOX_HEADS, FOX_QBLOCKS
    w = hg * HEAD_DIM
    per_blk = WIDTH // w
    vrows = HEAD_DIM + BF16_SUBLANES
    return pl.pallas_call(
        functools.partial(_fox_kernel, tq=tq, tk=tk),
        grid=(N_HEADS // hg, SEQ // (n_qb * tq)),
        in_specs=[
            pl.BlockSpec((SEQ, w), lambda g, i: (0, BLK_BQ * per_blk + g)),
            pl.BlockSpec((SEQ, w), lambda g, i: (0, BLK_BK * per_blk + g)),
            pl.BlockSpec((SEQ, w), lambda g, i: (0, BLK_BV * per_blk + g)),
            pl.BlockSpec((SEQ, LANES), lambda g, i: (0, 0)),
            pl.BlockSpec((SEQ, LANES), lambda g, i: (0, 0)),
        ],
        out_specs=pl.BlockSpec((n_qb * tq, w), lambda g, i: (i, g)),
        out_shape=jax.ShapeDtypeStruct((SEQ, WIDTH), BF16),
        scratch_shapes=[
            pltpu.VMEM((hg, SEQ // tk, vrows, tk), BF16),
            pltpu.VMEM((hg, F32_SUBLANES, LANES), F32),
            pltpu.VMEM((hg * n_qb, 1, tq), F32),
            pltpu.VMEM((hg * n_qb, vrows, tq), F32),
        ],
        compiler_params=_params("arbitrary", "arbitrary"),
        name="fox",
    )(proj, proj, proj, cx, nc)


def _merge_load_weights(srcs, dsts, stage_ref, sems):
    rows = stage_ref.shape[1]
    chunks = [(src, dst, r0) for src, dst in zip(srcs, dsts) for r0 in range(0, src.shape[0], rows)]

    def copy(n):
        src, _, r0 = chunks[n]
        return pltpu.make_async_copy(src.at[pl.ds(r0, rows), :], stage_ref.at[n % 2], sems.at[n % 2])

    copy(0).start()
    for n, (_, dst, r0) in enumerate(chunks):
        if n + 1 < len(chunks):
            copy(n + 1).start()
        copy(n).wait()
        dst[r0:r0 + rows, :] = stage_ref[n % 2].astype(BF16)


def _merge_kernel(oa_ref, ob_ref, ga_ref, gb_ref, wa_hbm, wb_hbm, wo_hbm, x_ref, nw_ref, npre_ref,
                  out_ref, h2_ref, wa_ref, wb_ref, wo_ref, stage_ref, sems):
    @pl.when(pl.program_id(0) == 0)
    def _():
        _merge_load_weights((wa_hbm, wb_hbm, wo_hbm), (wa_ref, wb_ref, wo_ref), stage_ref, sems)

    ya = jnp.dot(oa_ref[...], wa_ref[...], preferred_element_type=F32)
    yb = jnp.dot(ob_ref[...], wb_ref[...], preferred_element_type=F32)
    merged = (_sigmoid(ga_ref[...].astype(F32)) * ya + _sigmoid(gb_ref[...].astype(F32)) * yb).astype(BF16)
    half = out_ref.shape[0] // 2
    us = [jnp.dot(merged[r0:r0 + half], wo_ref[...], preferred_element_type=F32) for r0 in (0, half)]
    for r0, u in zip((0, half), us):
        r = slice(r0, r0 + half)
        u = u * lax.rsqrt(jnp.mean(u * u, axis=-1, keepdims=True) + RMS_EPS) * nw_ref[...]
        x1 = x_ref[r, :] + u
        out_ref[r, :] = x1
        ms = jnp.mean(x1 * x1, axis=-1, keepdims=True)
        h2_ref[r, :] = (x1 * lax.rsqrt(ms + RMS_EPS) * npre_ref[...]).astype(BF16)


def _merge(y_a, y_b, proj, w_up_a, w_up_b, w_o, x2, norm_w, norm_ffn_pre):
    tm = MERGE_TM
    hbm = pl.BlockSpec(memory_space=pl.ANY)
    return pl.pallas_call(
        _merge_kernel,
        grid=(SEQ // tm,),
        in_specs=[
            pl.BlockSpec((tm, WIDTH), lambda i: (i, 0)),
            pl.BlockSpec((tm, WIDTH), lambda i: (i, 0)),
            pl.BlockSpec((tm, D_MODEL), lambda i: (i, BLK_GA // 2)),
            pl.BlockSpec((tm, D_MODEL), lambda i: (i, BLK_GB // 2)),
            hbm,
            hbm,
            hbm,
            pl.BlockSpec((tm, D_MODEL), lambda i: (i, 0)),
            pl.BlockSpec((1, D_MODEL), lambda i: (0, 0)),
            pl.BlockSpec((1, D_MODEL), lambda i: (0, 0)),
        ],
        out_specs=[
            pl.BlockSpec((tm, D_MODEL), lambda i: (i, 0)),
            pl.BlockSpec((tm, D_MODEL), lambda i: (i, 0)),
        ],
        out_shape=[
            jax.ShapeDtypeStruct((SEQ, D_MODEL), F32),
            jax.ShapeDtypeStruct((SEQ, D_MODEL), BF16),
        ],
        scratch_shapes=[
            pltpu.VMEM((WIDTH, D_MODEL), BF16),
            pltpu.VMEM((WIDTH, D_MODEL), BF16),
            pltpu.VMEM((D_MODEL, D_MODEL), BF16),
            pltpu.VMEM((2, MERGE_STAGE_ROWS, D_MODEL), F32),
            pltpu.SemaphoreType.DMA((2,)),
        ],
        compiler_params=_params("arbitrary"),
        name="merge",
    )(y_a, y_b, proj, proj, w_up_a, w_up_b, w_o, x2, norm_w, norm_ffn_pre)


def _ffn_residual_copy(x_hbm, xres_ref, sem, i):
    tm = xres_ref.shape[0]
    return pltpu.make_async_copy(x_hbm.at[pl.ds(pl.multiple_of(i * tm, tm), tm), :], xres_ref, sem)


def _ffn_kernel(h_ref, x_hbm, npost_ref, wg_ref, wu_ref, wd_ref, out_ref, xres_ref, sem):
    i = pl.program_id(0)
    f = pl.program_id(1)

    @pl.when(f == 0)
    def _():
        _ffn_residual_copy(x_hbm, xres_ref, sem, i).start()
        out_ref[...] = jnp.zeros_like(out_ref)

    h = h_ref[...]
    gate = jnp.dot(h, wg_ref[...].astype(BF16), preferred_element_type=F32)
    up = jnp.dot(h, wu_ref[...].astype(BF16), preferred_element_type=F32)
    act = (gate * _sigmoid(gate) * up).astype(BF16)
    out_ref[...] += jnp.dot(act, wd_ref[...].astype(BF16), preferred_element_type=F32)

    @pl.when(f == pl.num_programs(1) - 1)
    def _():
        _ffn_residual_copy(x_hbm, xres_ref, sem, i).wait()
        u = out_ref[...]
        u = u * lax.rsqrt(jnp.mean(u * u, axis=-1, keepdims=True) + RMS_EPS) * npost_ref[...]
        out_ref[...] = xres_ref[...] + u


def _ffn(h2, x1, norm_post, w_in, w_down):
    tm, tf = FFN_TM, FFN_TF
    nf = D_FF // tf
    return pl.pallas_call(
        _ffn_kernel,
        grid=(SEQ // tm, nf),
        in_specs=[
            pl.BlockSpec((tm, D_MODEL), lambda i, f: (i, 0)),
            pl.BlockSpec(memory_space=pl.ANY),
            pl.BlockSpec((1, D_MODEL), lambda i, f: (0, 0)),
            pl.BlockSpec((D_MODEL, tf), lambda i, f: (0, f)),
            pl.BlockSpec((D_MODEL, tf), lambda i, f: (0, f + nf)),
            pl.BlockSpec((tf, D_MODEL), lambda i, f: (f, 0)),
        ],
        out_specs=pl.BlockSpec((tm, D_MODEL), lambda i, f: (i, 0)),
        out_shape=jax.ShapeDtypeStruct((SEQ, D_MODEL), F32),
        scratch_shapes=[pltpu.VMEM((tm, D_MODEL), F32), pltpu.SemaphoreType.DMA(())],
        compiler_params=_params("arbitrary", "arbitrary"),
        name="ffn",
    )(h2, x1, norm_post, w_in, w_in, w_down)


def kernel(x, w_in, b_fox_f, hgrn_lb_logits, hgrn_norm_w, w_up_a, w_up_b, w_o, norm_mix_pre,
           norm_mix_post, norm_ffn_pre, norm_ffn_post, w_ffn_in, w_ffn_down):
    assert x.shape == (1, SEQ, D_MODEL) and w_in.shape[0] == 1
    w_t = w_in[0].T
    x2 = x[0]

    h, a_f, cx, nc = _prenorm(x2, norm_mix_pre, w_t, b_fox_f.reshape(N_HEADS, 1))
    proj = _inproj(h, w_t)
    y_a = _hgrn(proj, a_f, hgrn_lb_logits, hgrn_norm_w)
    y_b = _fox(proj, cx, nc)
    x1, h2 = _merge(y_a, y_b, proj, w_up_a[0], w_up_b[0], w_o[0], x2, norm_mix_post, norm_ffn_pre)
    out = _ffn(h2, x1, norm_ffn_post, w_ffn_in[0], w_ffn_down[0])
    return out[None]
```
